```python
import jax, jax.numpy as jnp
from jax import lax
import numpy as np

D_MODEL = 2048
BATCH = 2
SEQ = 4096
DEPTH = 1

HEAD_DIM = 128
N_HEADS = D_MODEL // HEAD_DIM
NSA_HEADS = N_HEADS // 2
NSA_KV_HEADS = 2
NSA_GROUP = NSA_HEADS // NSA_KV_HEADS
MOBA_HEADS = N_HEADS - NSA_HEADS
CMP_BLOCK = 32
CMP_STRIDE = 16
SLC_BLOCK = 64
SLC_TOPK = 16
WINDOW = 512
FORCE_BONUS = 1e4
Q_BLOCK = 128
MOBA_BLOCK = 256
MOBA_TOPK = 3
MOBA_Q_CHUNK = 32
ROPE_THETA = 500000.0
ROPE_DIMS = HEAD_DIM // 4
FFN_HIDDEN = -(-8 * D_MODEL // (3 * 256)) * 256

NSA_Q_W = NSA_HEADS * HEAD_DIM
NSA_KV_W = NSA_KV_HEADS * HEAD_DIM
NSA_GATE_W = NSA_HEADS * 3
MOBA_W = MOBA_HEADS * HEAD_DIM
IN_SIZES = (NSA_Q_W,) + (NSA_KV_W,) * 6 + (NSA_GATE_W,) + (MOBA_W,) * 3
IN_WIDTH = sum(IN_SIZES)
IN_OFFSETS = tuple(int(v) for v in np.cumsum(IN_SIZES)[:-1])

kernel_name = 'hymba_nsa_moba_adaln_block'


def rms_norm(x, eps=1e-6):
    xf = x.astype(jnp.float32)
    return (xf * lax.rsqrt(jnp.mean(xf * xf, axis=-1, keepdims=True) + eps)).astype(x.dtype)


def head_rms(x, g):
    return rms_norm(x) * g


def split_heads(x, n):
    b, t, _ = x.shape
    return x.reshape(b, t, n, HEAD_DIM).transpose(0, 2, 1, 3)


def rope_tables(pos):
    inv = ROPE_THETA ** (-jnp.arange(0, ROPE_DIMS, 2, dtype=jnp.float32) / ROPE_DIMS)
    ang = pos.astype(jnp.float32)[:, None] * inv[None, :]
    return jnp.cos(ang), jnp.sin(ang)


def apply_rope(x, cos, sin):
    half = ROPE_DIMS // 2
    x1, x2, rest = x[..., :half], x[..., half:ROPE_DIMS], x[..., ROPE_DIMS:]
    c, s = cos.astype(x.dtype), sin.astype(x.dtype)
    return jnp.concatenate([x1 * c - x2 * s, x2 * c + x1 * s, rest], axis=-1)


def masked_softmax(s, mask):
    s = jnp.where(mask, s.astype(jnp.float32), -1e30)
    p = jax.nn.softmax(s, axis=-1)
    return jnp.where(mask, p, 0.0)


def gather_blocks(blocks, idx):
    return jax.vmap(jax.vmap(lambda b_, i_: b_[i_]))(blocks, idx)


def seq_chunks(a, axis, size):
    shp = a.shape
    a = a.reshape(shp[:axis] + (shp[axis] // size, size) + shp[axis + 1:])
    return jnp.moveaxis(a, axis, 0)


def compress(kv, pe, w1, w2):
    t = kv.shape[2]
    nc = (t - CMP_BLOCK) // CMP_STRIDE + 1
    idx = jnp.arange(nc)[:, None] * CMP_STRIDE + jnp.arange(CMP_BLOCK)[None, :]
    blocks = kv[:, :, idx] + pe
    flat = blocks.reshape(blocks.shape[:3] + (CMP_BLOCK * HEAD_DIM,))
    return jax.nn.silu(flat @ w1) @ w2


def nsa_attention(q, k_cmp, v_cmp, k_slc, v_slc, k_win, v_win, gate_logits, q_g, k_g,
                  cmp_pe_k, cmp_w1_k, cmp_w2_k, cmp_pe_v, cmp_w1_v, cmp_w2_v, cos, sin):
    B, T = q.shape[:2]
    G, R = NSA_KV_HEADS, NSA_GROUP
    scale = HEAD_DIM ** -0.5
    pos = jnp.arange(T)
    q = apply_rope(head_rms(split_heads(q, NSA_HEADS), q_g), cos, sin)
    q = q.reshape(B, G, R, T, HEAD_DIM)

    kc = compress(split_heads(k_cmp, G), cmp_pe_k, cmp_w1_k, cmp_w2_k)
    vc = compress(split_heads(v_cmp, G), cmp_pe_v, cmp_w1_v, cmp_w2_v)
    nc = kc.shape[2]
    cmp_end = jnp.arange(nc) * CMP_STRIDE + CMP_BLOCK - 1
    kc = apply_rope(head_rms(kc, k_g[0]), cos[cmp_end], sin[cmp_end])
    s = jnp.einsum('bgrtd,bgcd->bgrtc', q, kc) * scale
    p_cmp = masked_softmax(s, cmp_end[None, :] <= pos[:, None])
    o_cmp = jnp.einsum('bgrtc,bgcd->bgrtd', p_cmp.astype(vc.dtype), vc)

    nsb = T // SLC_BLOCK
    cs = jnp.arange(nc) * CMP_STRIDE
    sb = jnp.arange(nsb) * SLC_BLOCK
    overlap = ((cs[:, None] < sb[None, :] + SLC_BLOCK) &
               (cs[:, None] + CMP_BLOCK > sb[None, :])).astype(jnp.float32)
    imp = jnp.einsum('bgtc,cn->bgtn', p_cmp.sum(axis=2), overlap)
    cur = pos // SLC_BLOCK
    blk = jnp.arange(nsb)[None, :]
    forced = (blk == 0) | (blk == cur[:, None]) | (blk == cur[:, None] - 1)
    valid = blk <= cur[:, None]
    score = jnp.where(valid, imp + FORCE_BONUS * forced, -1e30)
    n_sel = min(SLC_TOPK, nsb)
    _, sel_idx = lax.top_k(score, n_sel)
    sel_valid = sel_idx <= cur[:, None]

    ks = apply_rope(head_rms(split_heads(k_slc, G), k_g[1]), cos, sin)
    vs = split_heads(v_slc, G)
    ks_blocks = ks.reshape(B, G, nsb, SLC_BLOCK, HEAD_DIM)
    vs_blocks = vs.reshape(B, G, nsb, SLC_BLOCK, HEAD_DIM)
    n_keys = n_sel * SLC_BLOCK

    def sel_chunk(args):
        qc, ic, vld, tc = args
        kg = gather_blocks(ks_blocks, ic).reshape(B, G, Q_BLOCK, n_keys, HEAD_DIM)
        vg = gather_blocks(vs_blocks, ic).reshape(B, G, Q_BLOCK, n_keys, HEAD_DIM)
        kpos = (ic[..., None] * SLC_BLOCK + jnp.arange(SLC_BLOCK)).reshape(B, G, Q_BLOCK, n_keys)
        mask = jnp.repeat(vld, SLC_BLOCK, axis=-1) & (kpos <= tc[:, None])
        sc = jnp.einsum('bgrqd,bgqnd->bgrqn', qc, kg) * scale
        p = masked_softmax(sc, mask[:, :, None]).astype(vg.dtype)
        return jnp.einsum('bgrqn,bgqnd->bgrqd', p, vg)

    o_slc = lax.map(sel_chunk, (seq_chunks(q, 3, Q_BLOCK), seq_chunks(sel_idx, 2, Q_BLOCK),
                                seq_chunks(sel_valid, 2, Q_BLOCK), pos.reshape(-1, Q_BLOCK)))
    o_slc = jnp.moveaxis(o_slc, 0, 3).reshape(B, G, R, T, HEAD_DIM)

    kw = apply_rope(head_rms(split_heads(k_win, G), k_g[2]), cos, sin)
    vw = split_heads(v_win, G)
    nqb = T // Q_BLOCK
    span = WINDOW + Q_BLOCK
    win_idx = jnp.arange(nqb)[:, None] * Q_BLOCK + jnp.arange(span)[None, :]
    pad = ((0, 0), (0, 0), (WINDOW, 0), (0, 0))
    kwb = jnp.pad(kw, pad)[:, :, win_idx]
    vwb = jnp.pad(vw, pad)[:, :, win_idx]
    key_pos = win_idx - WINDOW
    q_pos = pos.reshape(nqb, Q_BLOCK)
    diff = q_pos[:, :, None] - key_pos[:, None, :]
    wmask = (key_pos[:, None, :] >= 0) & (diff >= 0) & (diff < WINDOW)
    qb = q.reshape(B, G, R, nqb, Q_BLOCK, HEAD_DIM)
    sw = jnp.einsum('bgrnqd,bgnkd->bgrnqk', qb, kwb) * scale
    pw = masked_softmax(sw, wmask).astype(vwb.dtype)
    o_win = jnp.einsum('bgrnqk,bgnkd->bgrnqd', pw, vwb).reshape(B, G, R, T, HEAD_DIM)

    g = jax.nn.sigmoid(gate_logits.astype(jnp.float32)).astype(q.dtype)
    g = g.reshape(B, T, G, R, 3).transpose(4, 0, 2, 3, 1)[..., None]
    o = g[0] * o_cmp + g[1] * o_slc + g[2] * o_win
    return o.reshape(B, NSA_HEADS, T, HEAD_DIM)


def moba_attention(q, k, v, q_g, k_g, cos, sin):
    B, T = q.shape[:2]
    H = MOBA_HEADS
    scale = HEAD_DIM ** -0.5
    q = apply_rope(head_rms(split_heads(q, H), q_g), cos, sin)
    k = apply_rope(head_rms(split_heads(k, H), k_g), cos, sin)
    v = split_heads(v, H)
    tp = -(-T // MOBA_BLOCK) * MOBA_BLOCK
    pad = ((0, 0), (0, 0), (0, tp - T), (0, 0))
    q, k, v = jnp.pad(q, pad), jnp.pad(k, pad), jnp.pad(v, pad)
    nb = tp // MOBA_BLOCK
    k_blocks = k.reshape(B, H, nb, MOBA_BLOCK, HEAD_DIM)
    v_blocks = v.reshape(B, H, nb, MOBA_BLOCK, HEAD_DIM)
    pos = jnp.arange(tp)
    cur = pos // MOBA_BLOCK

    gate = jnp.einsum('bhtd,bhnd->bhtn', q, jnp.mean(k_blocks, axis=3)).astype(jnp.float32)
    past = jnp.arange(nb)[None, :] < cur[:, None]
    n_sel = min(MOBA_TOPK, nb)
    _, sel_idx = lax.top_k(jnp.where(past, gate, -1e30), n_sel)
    sel_valid = sel_idx < cur[:, None]
    n_keys = n_sel * MOBA_BLOCK
    n_ch = tp // MOBA_Q_CHUNK
    own = (jnp.arange(n_ch) * MOBA_Q_CHUNK) // MOBA_BLOCK

    def chunk(args):
        qc, ic, vld, tc, ob = args
        kg = gather_blocks(k_blocks, ic).reshape(B, H, MOBA_Q_CHUNK, n_keys, HEAD_DIM)
        vg = gather_blocks(v_blocks, ic).reshape(B, H, MOBA_Q_CHUNK, n_keys, HEAD_DIM)
        k_own = lax.dynamic_index_in_dim(k_blocks, ob, axis=2, keepdims=False)
        v_own = lax.dynamic_index_in_dim(v_blocks, ob, axis=2, keepdims=False)
        s_sel = jnp.einsum('bhqd,bhqnd->bhqn', qc, kg)
        s_own = jnp.einsum('bhqd,bhmd->bhqm', qc, k_own)
        m_sel = jnp.repeat(vld, MOBA_BLOCK, axis=-1)
        m_own = jnp.broadcast_to(ob * MOBA_BLOCK + jnp.arange(MOBA_BLOCK) <= tc[:, None], s_own.shape)
        p = masked_softmax(jnp.concatenate([s_sel, s_own], axis=-1) * scale,
                           jnp.concatenate([m_sel, m_own], axis=-1)).astype(v_own.dtype)
        return (jnp.einsum('bhqn,bhqnd->bhqd', p[..., :n_keys], vg) +
                jnp.einsum('bhqm,bhmd->bhqd', p[..., n_keys:], v_own))

    o = lax.map(chunk, (seq_chunks(q, 2, MOBA_Q_CHUNK), seq_chunks(sel_idx, 2, MOBA_Q_CHUNK),
                        seq_chunks(sel_valid, 2, MOBA_Q_CHUNK), pos.reshape(n_ch, MOBA_Q_CHUNK), own))
    o = jnp.moveaxis(o, 0, 2).reshape(B, H, tp, HEAD_DIM)
    return o[:, :, :T]


def setup_inputs(seed: int = 0) -> dict:
    key = jax.random.key(seed)
    ks = jax.random.split(key, 20)
    f32 = jnp.float32
    L = DEPTH
    d_cmp = CMP_BLOCK * HEAD_DIM

    def nrm(k, shape, scale):
        return jax.random.normal(k, shape, f32) * scale

    def gain(k, shape):
        return 1.0 + 0.1 * jax.random.normal(k, shape, f32)

    return {
        'x': nrm(ks[0], (BATCH, SEQ, D_MODEL), 1.0),
        'c': nrm(ks[1], (BATCH, D_MODEL), 1.0),
        'w_ada': nrm(ks[2], (L, D_MODEL, 6 * D_MODEL), D_MODEL ** -0.5),
        'b_ada': nrm(ks[3], (L, 6 * D_MODEL), 0.02),
        'w_in': nrm(ks[4], (L, D_MODEL, IN_WIDTH), D_MODEL ** -0.5),
        'nsa_q_norm': gain(ks[5], (L, HEAD_DIM)),
        'nsa_k_norm': gain(ks[6], (L, 3, HEAD_DIM)),
        'moba_q_norm': gain(ks[7], (L, HEAD_DIM)),
        'moba_k_norm': gain(ks[8], (L, HEAD_DIM)),
        'cmp_pe_k': nrm(ks[9], (L, CMP_BLOCK, HEAD_DIM), 0.1),
        'cmp_w1_k': nrm(ks[10], (L, d_cmp, HEAD_DIM), d_cmp ** -0.5),
        'cmp_w2_k': nrm(ks[11], (L, HEAD_DIM, HEAD_DIM), HEAD_DIM ** -0.5),
        'cmp_pe_v': nrm(ks[12], (L, CMP_BLOCK, HEAD_DIM), 0.1),
        'cmp_w1_v': nrm(ks[13], (L, d_cmp, HEAD_DIM), d_cmp ** -0.5),
        'cmp_w2_v': nrm(ks[14], (L, HEAD_DIM, HEAD_DIM), HEAD_DIM ** -0.5),
        'out_norm': gain(ks[15], (L, D_MODEL)),
        'w_out': nrm(ks[16], (L, D_MODEL, D_MODEL), D_MODEL ** -0.5),
        'w_ffn_in': nrm(ks[17], (L, D_MODEL, 2 * FFN_HIDDEN), D_MODEL ** -0.5),
        'w_ffn_out': nrm(ks[18], (L, FFN_HIDDEN, D_MODEL), FFN_HIDDEN ** -0.5),
    }


def reference(x, c, w_ada, b_ada, w_in, nsa_q_norm, nsa_k_norm, moba_q_norm, moba_k_norm,
              cmp_pe_k, cmp_w1_k, cmp_w2_k, cmp_pe_v, cmp_w1_v, cmp_w2_v,
              out_norm, w_out, w_ffn_in, w_ffn_out):
    B, T, _ = x.shape
    cos, sin = rope_tables(jnp.arange(T))
    for l in range(DEPTH):
        mod = jax.nn.silu(c) @ w_ada[l] + b_ada[l]
        sh_a, sc_a, g_a, sh_f, sc_f, g_f = [m[:, None, :] for m in jnp.split(mod, 6, axis=-1)]

        h = rms_norm(x) * (1.0 + sc_a) + sh_a
        parts = jnp.split(h @ w_in[l], IN_OFFSETS, axis=-1)
        o_nsa = nsa_attention(parts[0], parts[1], parts[2], parts[3], parts[4], parts[5], parts[6],
                              parts[7], nsa_q_norm[l], nsa_k_norm[l],
                              cmp_pe_k[l], cmp_w1_k[l], cmp_w2_k[l],
                              cmp_pe_v[l], cmp_w1_v[l], cmp_w2_v[l], cos, sin)
        o_moba = moba_attention(parts[8], parts[9], parts[10], moba_q_norm[l], moba_k_norm[l], cos, sin)
        o = jnp.concatenate([o_nsa, o_moba], axis=1)
        o = rms_norm(o) * out_norm[l].reshape(N_HEADS, 1, HEAD_DIM)
        o = o.transpose(0, 2, 1, 3).reshape(B, T, D_MODEL)
        x = x + g_a * (o @ w_out[l])

        h = rms_norm(x) * (1.0 + sc_f) + sh_f
        gate, up = jnp.split(h @ w_ffn_in[l], 2, axis=-1)
        x = x + g_f * ((jax.nn.silu(gate) * up) @ w_ffn_out[l])
    return x
```

```python
import functools

import jax
import jax.numpy as jnp
from jax import lax
from jax.experimental import pallas as pl
from jax.experimental.pallas import tpu as pltpu

F32 = jnp.float32
BF16 = jnp.bfloat16

HEAD_DIM = 128
NSA_HEADS = 8
NSA_KV_HEADS = 2
NSA_GROUP = NSA_HEADS // NSA_KV_HEADS
MOBA_HEADS = 8
N_HEADS = NSA_HEADS + MOBA_HEADS
CMP_BLOCK = 32
CMP_STRIDE = 16
SLC_BLOCK = 64
SLC_TOPK = 16
WINDOW = 512
FORCE_BONUS = 1e4
MOBA_BLOCK = 256
MOBA_TOPK = 3
ROPE_THETA = 500000.0
ROPE_DIMS = HEAD_DIM // 4
ROPE_HALF = ROPE_DIMS // 2
EPS = 1e-6
NEG = -1e30

V7X_VMEM_BYTES = 64 * 1024 * 1024
VMEM_LIMIT = V7X_VMEM_BYTES - 8 * 1024 * 1024

H_NSA_Q = 0
H_KSLC = 8
H_KWIN = 10
H_MOBA_Q = 12
H_MOBA_K = 20
N_TREATED = 28
H_KCMP = 28
H_VCMP = 30
H_VSLC = 32
H_VWIN = 34
H_MOBA_V = 36
N_PROJ_HEADS = 44
HEADS_PER_TILE = 4
GATE_LANES = 128

NSA_TQ = 128
SLC_TK = 256
WIN_TK = 128
MOBA_TQ = 256
MOBA_TK = MOBA_BLOCK


def _cparams(sem):
    return pltpu.CompilerParams(dimension_semantics=sem, vmem_limit_bytes=VMEM_LIMIT)


def _split3(a):
    hi = a.astype(BF16)
    r1 = a - hi.astype(F32)
    mid = r1.astype(BF16)
    lo = (r1 - mid.astype(F32)).astype(BF16)
    return hi, mid, lo


def _adaln_kernel(c_ref, w_ref, b_ref, o_ref):
    cv = c_ref[...]
    s = cv * jax.nn.sigmoid(cv)
    w = w_ref[...]
    acc = jnp.zeros(o_ref.shape, F32)
    for part in _split3(s):
        for wpart in _split3(w)[:2]:
            acc = acc + jnp.dot(part, wpart, preferred_element_type=F32)
    o_ref[...] = acc + b_ref[...]


def _adaln(c, w_ada, b_ada):
    B, D = c.shape
    N = w_ada.shape[1]
    tn = 1024
    c8 = jnp.zeros((8, D), F32).at[:B].set(c)
    out = pl.pallas_call(
        _adaln_kernel,
        grid=(N // tn,),
        in_specs=[pl.BlockSpec((8, D), lambda j: (0, 0)),
                  pl.BlockSpec((D, tn), lambda j: (0, j)),
                  pl.BlockSpec((1, tn), lambda j: (0, j))],
        out_specs=pl.BlockSpec((8, tn), lambda j: (0, j)),
        out_shape=jax.ShapeDtypeStruct((8, N), F32),
        compiler_params=_cparams(("arbitrary",)),
        name="adaln",
    )(c8, w_ada, b_ada.reshape(1, N))
    return out[:B].reshape(B, 6, D)


def _rope(y, cc, sa, sb):
    return (y * cc + pltpu.roll(y, HEAD_DIM - ROPE_HALF, 1) * sa + pltpu.roll(y, ROPE_HALF, 1) * sb)


def _inproj_kernel(x_ref, mod_ref, w_ref, wg_ref, gain_ref, cc_ref, sa_ref, sb_ref,
                   y_ref, g_ref, h_scr, *, n_treated_tiles):
    j = pl.program_id(1)

    @pl.when(j == 0)
    def _():
        x = x_ref[...]
        ms = jnp.mean(x * x, axis=-1, keepdims=True)
        h = x * lax.rsqrt(ms + EPS) * (1.0 + mod_ref[0, 1:2, :]) + mod_ref[0, 0:1, :]
        hb = h.astype(BF16)
        h_scr[...] = hb
        g_ref[...] = jax.nn.sigmoid(jnp.dot(hb, wg_ref[...], preferred_element_type=F32))

    acc = jnp.dot(h_scr[...], w_ref[...], preferred_element_type=F32)

    @pl.when(j < n_treated_tiles)
    def _():
        cc, sa, sb = cc_ref[...], sa_ref[...], sb_ref[...]
        for hh in range(HEADS_PER_TILE):
            yh = acc[:, hh * HEAD_DIM:(hh + 1) * HEAD_DIM]
            ms = jnp.mean(yh * yh, axis=-1, keepdims=True)
            yn = yh * lax.rsqrt(ms + EPS) * gain_ref[0, hh:hh + 1, :]
            y_ref[hh] = _rope(yn, cc, sa, sb).astype(BF16)

    @pl.when(j >= n_treated_tiles)
    def _():
        for hh in range(HEADS_PER_TILE):
            y_ref[hh] = acc[:, hh * HEAD_DIM:(hh + 1) * HEAD_DIM].astype(BF16)


def _in_proj(x2, mod3, w_main, w_gate, gains, cc, sa, sb, T):
    BT, D = x2.shape
    tm = min(1024, T)
    tn = HEADS_PER_TILE * HEAD_DIM
    n_tiles = N_PROJ_HEADS // HEADS_PER_TILE
    tpb = T // tm
    kern = functools.partial(_inproj_kernel, n_treated_tiles=N_TREATED // HEADS_PER_TILE)
    y, g = pl.pallas_call(
        kern,
        grid=(BT // tm, n_tiles),
        in_specs=[pl.BlockSpec((tm, D), lambda i, j: (i, 0)),
                  pl.BlockSpec((1, 6, D), lambda i, j: (i // tpb, 0, 0)),
                  pl.BlockSpec((D, tn), lambda i, j: (0, j)),
                  pl.BlockSpec((D, NSA_KV_HEADS * GATE_LANES), lambda i, j: (0, 0)),
                  pl.BlockSpec((1, HEADS_PER_TILE, HEAD_DIM), lambda i, j: (j, 0, 0)),
                  pl.BlockSpec((tm, HEAD_DIM), lambda i, j: (i % tpb, 0)),
                  pl.BlockSpec((tm, HEAD_DIM), lambda i, j: (i % tpb, 0)),
                  pl.BlockSpec((tm, HEAD_DIM), lambda i, j: (i % tpb, 0))],
        out_specs=[pl.BlockSpec((HEADS_PER_TILE, tm, HEAD_DIM), lambda i, j: (j, i, 0)),
                   pl.BlockSpec((tm, NSA_KV_HEADS * GATE_LANES), lambda i, j: (i, 0))],
        out_shape=[jax.ShapeDtypeStruct((N_PROJ_HEADS, BT, HEAD_DIM), BF16),
                   jax.ShapeDtypeStruct((BT, NSA_KV_HEADS * GATE_LANES), F32)],
        scratch_shapes=[pltpu.VMEM((tm, D), BF16)],
        compiler_params=_cparams(("parallel", "arbitrary")),
        name="in_proj",
    )(x2, mod3, w_main, w_gate, gains, cc, sa, sb)
    return y, g


def _compress_kernel(h_ref, w1c_ref, pe_ref, w1_ref, w2_ref, gain_ref, cc_ref, sa_ref, sb_ref, o_ref):
    a = pl.program_id(0)
    ncp = h_ref.shape[2]
    z = jnp.dot(h_ref[0, 0], w1c_ref[0], preferred_element_type=F32)
    top = z[:, :HEAD_DIM]
    bot = pltpu.roll(z[:, HEAD_DIM:], ncp - 1, 0)
    pe_term = jnp.dot(pe_ref[0], w1_ref[0], preferred_element_type=F32)[0:1, :]
    pre = top + bot + pe_term
    act = pre * jax.nn.sigmoid(pre)
    out = jnp.dot(act.astype(BF16), w2_ref[0], preferred_element_type=F32)
    live = lax.broadcasted_iota(jnp.int32, out.shape, 0) < ncp - 1
    out = jnp.where(live, out, 0.0)

    @pl.when(a < NSA_KV_HEADS)
    def _():
        ms = jnp.mean(out * out, axis=-1, keepdims=True)
        yn = out * lax.rsqrt(ms + EPS) * gain_ref[...]
        o_ref[0, 0] = _rope(yn, cc_ref[...], sa_ref[...], sb_ref[...]).astype(BF16)

    @pl.when(a >= NSA_KV_HEADS)
    def _():
        o_ref[0, 0] = out.astype(BF16)


def _compress(hc, w1cat, pe8, w1, w2, gain, cc, sa, sb):
    A, B, ncp, K = hc.shape
    G = NSA_KV_HEADS
    return pl.pallas_call(
        _compress_kernel,
        grid=(A, B),
        in_specs=[pl.BlockSpec((1, 1, ncp, K), lambda a, b: (a, b, 0, 0)),
                  pl.BlockSpec((1, K, 2 * HEAD_DIM), lambda a, b: (a // G, 0, 0)),
                  pl.BlockSpec((1, 8, 2 * K), lambda a, b: (a // G, 0, 0)),
                  pl.BlockSpec((1, 2 * K, HEAD_DIM), lambda a, b: (a // G, 0, 0)),
                  pl.BlockSpec((1, HEAD_DIM, HEAD_DIM), lambda a, b: (a // G, 0, 0)),
                  pl.BlockSpec((1, HEAD_DIM), lambda a, b: (0, 0)),
                  pl.BlockSpec((ncp, HEAD_DIM), lambda a, b: (0, 0)),
                  pl.BlockSpec((ncp, HEAD_DIM), lambda a, b: (0, 0)),
                  pl.BlockSpec((ncp, HEAD_DIM), lambda a, b: (0, 0))],
        out_specs=pl.BlockSpec((1, 1, ncp, HEAD_DIM), lambda a, b: (a, b, 0, 0)),
        out_shape=jax.ShapeDtypeStruct((A, B, ncp, HEAD_DIM), BF16),
        compiler_params=_cparams(("arbitrary", "arbitrary")),
        name="compress",
    )(hc, w1cat, pe8, w1, w2, gain, cc, sa, sb)


def _reset(m_scr, l_scr, acc_scr):
    m_scr[...] = jnp.full(m_scr.shape, NEG, F32)
    l_scr[...] = jnp.zeros(l_scr.shape, F32)
    acc_scr[...] = jnp.zeros(acc_scr.shape, F32)


def _online_step(s, vT, m_scr, l_scr, acc_scr):
    m_prev = m_scr[...]
    m_new = jnp.maximum(m_prev, jnp.max(s, axis=0, keepdims=True))
    alpha = jnp.exp(m_prev - m_new)
    p = jnp.exp(s - m_new)
    l_scr[...] = alpha * l_scr[...] + jnp.sum(p, axis=0, keepdims=True)
    acc_scr[...] = alpha * acc_scr[...] + jnp.dot(vT, p.astype(BF16), preferred_element_type=F32)
    m_scr[...] = m_new


def _rank_select(score, n_idx, k):
    rank = jnp.zeros(score.shape, F32)
    for m in range(score.shape[0]):
        sm = score[m:m + 1, :]
        gt = jnp.where(sm > score, 1.0, 0.0)
        ge = jnp.where(sm >= score, 1.0, 0.0)
        rank = rank + jnp.where(n_idx > m, ge, gt)
    return rank < k


def _nsa_kernel(qT_ref, kc_ref, vcT_ref, ks_ref, vsT_ref, kw_ref, vwT_ref, gT_ref, ovT_ref, gain_ref,
                o_ref, m_scr, l_scr, acc_scr, comb_scr, bias_scr, *, T):
    R, tq = NSA_GROUP, NSA_TQ
    i = pl.program_id(2)
    t0 = i * tq
    ncp = T // CMP_STRIDE
    nsb = T // SLC_BLOCK
    qT = jnp.concatenate([qT_ref[r] for r in range(R)], axis=1)
    gT = gT_ref[0]

    def per_head(row0):
        return jnp.concatenate([gT[row0 + 3 * r:row0 + 3 * r + 1, :] for r in range(R)], axis=1)

    def lanes_x_heads(a):
        return jnp.concatenate([a] * R, axis=1)

    s = jnp.dot(kc_ref[0, 0], qT, preferred_element_type=F32)
    c_idx = lax.broadcasted_iota(jnp.int32, (ncp, tq), 0)
    t_c = t0 + lax.broadcasted_iota(jnp.int32, (ncp, tq), 1)
    vis = lanes_x_heads((c_idx * CMP_STRIDE + (CMP_BLOCK - 1) <= t_c) & (c_idx < ncp - 1))
    s = jnp.where(vis, s, NEG)
    m = jnp.max(s, axis=0, keepdims=True)
    p = jnp.where(vis, jnp.exp(s - m), 0.0)
    l = jnp.sum(p, axis=0, keepdims=True)
    p = p / jnp.where(l > 0.0, l, 1.0)
    o_cmp = jnp.dot(vcT_ref[0, 0], p.astype(BF16), preferred_element_type=F32)
    comb_scr[...] = o_cmp * per_head(0)

    psum = p[:, 0:tq]
    for r in range(1, R):
        psum = psum + p[:, r * tq:(r + 1) * tq]
    ov = ovT_ref[...]
    imp = jnp.zeros((nsb, tq), F32)
    for part in _split3(psum):
        imp = imp + jnp.dot(ov, part, preferred_element_type=F32)
    n_idx = lax.broadcasted_iota(jnp.int32, (nsb, tq), 0)
    cur = (t0 + lax.broadcasted_iota(jnp.int32, (nsb, tq), 1)) // SLC_BLOCK
    forced = (n_idx == 0) | (n_idx == cur) | (n_idx == cur - 1)
    valid = n_idx <= cur
    score = jnp.where(valid, imp + jnp.where(forced, FORCE_BONUS, 0.0), NEG)
    sel = _rank_select(score, n_idx, min(SLC_TOPK, nsb)) & valid
    bias = jnp.where(sel, 0.0, NEG)
    for n in range(nsb):
        bias_scr[n] = bias[n:n + 1, :]

    bpt = SLC_TK // SLC_BLOCK

    def slc_scores(j):
        k_j = ks_ref[0, pl.ds(pl.multiple_of(j * SLC_TK, SLC_TK), SLC_TK), :]
        sj = jnp.dot(k_j, qT, preferred_element_type=F32)
        b = jnp.concatenate([jnp.broadcast_to(bias_scr[j * bpt + a], (SLC_BLOCK, tq))
                             for a in range(bpt)], axis=0)
        return sj + lanes_x_heads(b)

    _reset(m_scr, l_scr, acc_scr)

    def slc_body(j, carry):
        _online_step(slc_scores(j), vsT_ref[0, j], m_scr, l_scr, acc_scr)
        return carry

    j_last = t0 // SLC_TK
    lax.fori_loop(0, j_last, slc_body, 0)
    key = j_last * SLC_TK + lax.broadcasted_iota(jnp.int32, (SLC_TK, tq), 0)
    t_k = t0 + lax.broadcasted_iota(jnp.int32, (SLC_TK, tq), 1)
    s_last = jnp.where(lanes_x_heads(key <= t_k), slc_scores(j_last), NEG)
    _online_step(s_last, vsT_ref[0, j_last], m_scr, l_scr, acc_scr)
    comb_scr[...] += acc_scr[...] / l_scr[...] * per_head(1)

    _reset(m_scr, l_scr, acc_scr)

    def win_body(j, carry):
        k_j = kw_ref[0, pl.ds(pl.multiple_of(j * WIN_TK, WIN_TK), WIN_TK), :]
        sj = jnp.dot(k_j, qT, preferred_element_type=F32)
        kpos = j * WIN_TK + lax.broadcasted_iota(jnp.int32, (WIN_TK, tq), 0)
        tpos = t0 + lax.broadcasted_iota(jnp.int32, (WIN_TK, tq), 1)
        ok = (kpos <= tpos) & (tpos - kpos < WINDOW)
        _online_step(jnp.where(lanes_x_heads(ok), sj, NEG), vwT_ref[0, j], m_scr, l_scr, acc_scr)
        return carry

    j_hi = (t0 + tq - 1) // WIN_TK
    j_lo = jnp.maximum(t0 - (WINDOW - 1), 0) // WIN_TK
    lax.fori_loop(j_lo, j_hi + 1, win_body, 0)
    comb = comb_scr[...] + acc_scr[...] / l_scr[...] * per_head(2)

    for r in range(R):
        oT = comb[:, r * tq:(r + 1) * tq]
        ms = jnp.mean(oT * oT, axis=0, keepdims=True)
        on = (oT * lax.rsqrt(ms + EPS)).T * gain_ref[0, r:r + 1, :]
        o_ref[:, r * HEAD_DIM:(r + 1) * HEAD_DIM] = on.astype(BF16)


def _nsa(qT, kcvc, vcT, y, vsT, vwT, gT, ovT, gains, B, T):
    G, R, tq = NSA_KV_HEADS, NSA_GROUP, NSA_TQ
    nq = T // tq
    ncp = T // CMP_STRIDE
    nsb = T // SLC_BLOCK
    kern = functools.partial(_nsa_kernel, T=T)
    return pl.pallas_call(
        kern,
        grid=(B, G, nq),
        in_specs=[pl.BlockSpec((R, HEAD_DIM, tq), lambda b, g, i: (g, 0, b * nq + i)),
                  pl.BlockSpec((1, 1, ncp, HEAD_DIM), lambda b, g, i: (g, b, 0, 0)),
                  pl.BlockSpec((1, 1, HEAD_DIM, ncp), lambda b, g, i: (g, b, 0, 0)),
                  pl.BlockSpec((1, T, HEAD_DIM), lambda b, g, i: (H_KSLC + g, b, 0)),
                  pl.BlockSpec((1, T // SLC_TK, HEAD_DIM, SLC_TK), lambda b, g, i: (g * B + b, 0, 0, 0)),
                  pl.BlockSpec((1, T, HEAD_DIM), lambda b, g, i: (H_KWIN + g, b, 0)),
                  pl.BlockSpec((1, T // WIN_TK, HEAD_DIM, WIN_TK), lambda b, g, i: (g * B + b, 0, 0, 0)),
                  pl.BlockSpec((1, 16, tq), lambda b, g, i: (g, 0, b * nq + i)),
                  pl.BlockSpec((nsb, ncp), lambda b, g, i: (0, 0)),
                  pl.BlockSpec((1, R, HEAD_DIM), lambda b, g, i: (g, 0, 0))],
        out_specs=pl.BlockSpec((tq, R * HEAD_DIM), lambda b, g, i: (b * nq + i, g)),
        out_shape=jax.ShapeDtypeStruct((B * T, NSA_HEADS * HEAD_DIM), BF16),
        scratch_shapes=[pltpu.VMEM((1, R * tq), F32), pltpu.VMEM((1, R * tq), F32),
                        pltpu.VMEM((HEAD_DIM, R * tq), F32), pltpu.VMEM((HEAD_DIM, R * tq), F32),
                        pltpu.VMEM((nsb, 1, tq), F32)],
        compiler_params=_cparams(("parallel", "parallel", "arbitrary")),
        name="nsa",
    )(qT, kcvc, vcT, y, vsT, y, vwT, gT, ovT, gains)


def _moba_kernel(qT_ref, k_ref, vT_ref, gain_ref, o_ref, m_scr, l_scr, acc_scr, kmean_scr, bias_scr, *, T):
    tq, tk = MOBA_TQ, MOBA_TK
    nb = T // MOBA_BLOCK
    nbp = kmean_scr.shape[0]
    i = pl.program_id(2)
    t0 = i * tq
    qT = qT_ref[0]

    @pl.when(i == 0)
    def _():
        kmean_scr[...] = jnp.zeros(kmean_scr.shape, F32)
        kb = k_ref[0].astype(F32).reshape(nb, MOBA_BLOCK, HEAD_DIM)
        kmean_scr[0:nb, :] = jnp.mean(kb, axis=1)

    gate = jnp.zeros((nbp, tq), F32)
    for part in _split3(kmean_scr[...]):
        gate = gate + jnp.dot(part, qT, preferred_element_type=F32)
    n_idx = lax.broadcasted_iota(jnp.int32, (nbp, tq), 0)
    cur = (t0 + lax.broadcasted_iota(jnp.int32, (nbp, tq), 1)) // MOBA_BLOCK
    past = n_idx < cur
    sel = _rank_select(jnp.where(past, gate, NEG), n_idx, min(MOBA_TOPK, nb)) & past
    bias = jnp.where(sel, 0.0, NEG)
    for n in range(nb):
        bias_scr[n] = bias[n:n + 1, :]

    _reset(m_scr, l_scr, acc_scr)

    def body(j, carry):
        k_j = k_ref[0, pl.ds(pl.multiple_of(j * tk, tk), tk), :]
        sj = jnp.dot(k_j, qT, preferred_element_type=F32) + bias_scr[j]
        _online_step(sj, vT_ref[0, j], m_scr, l_scr, acc_scr)
        return carry

    lax.fori_loop(0, i, body, 0)
    k_own = k_ref[0, pl.ds(pl.multiple_of(t0, tk), tk), :]
    s_own = jnp.dot(k_own, qT, preferred_element_type=F32)
    causal = lax.broadcasted_iota(jnp.int32, (tk, tq), 0) <= lax.broadcasted_iota(jnp.int32, (tk, tq), 1)
    _online_step(jnp.where(causal, s_own, NEG), vT_ref[0, i], m_scr, l_scr, acc_scr)

    oT = acc_scr[...] / l_scr[...]
    ms = jnp.mean(oT * oT, axis=0, keepdims=True)
    o_ref[...] = ((oT * lax.rsqrt(ms + EPS)).T * gain_ref[0]).astype(BF16)


def _moba(qT, y, vT, gains, B, T):
    H, tq, tk = MOBA_HEADS, MOBA_TQ, MOBA_TK
    nq = T // tq
    nb = T // MOBA_BLOCK
    nbp = max(16, nb)
    kern = functools.partial(_moba_kernel, T=T)
    return pl.pallas_call(
        kern,
        grid=(B, H, nq),
        in_specs=[pl.BlockSpec((1, HEAD_DIM, tq), lambda b, h, i: (h, 0, b * nq + i)),
                  pl.BlockSpec((1, T, HEAD_DIM), lambda b, h, i: (H_MOBA_K + h, b, 0)),
                  pl.BlockSpec((1, T // tk, HEAD_DIM, tk), lambda b, h, i: (h * B + b, 0, 0, 0)),
                  pl.BlockSpec((1, 1, HEAD_DIM), lambda b, h, i: (h, 0, 0))],
        out_specs=pl.BlockSpec((tq, HEAD_DIM), lambda b, h, i: (b * nq + i, h)),
        out_shape=jax.ShapeDtypeStruct((B * T, H * HEAD_DIM), BF16),
        scratch_shapes=[pltpu.VMEM((1, tq), F32), pltpu.VMEM((1, tq), F32),
                        pltpu.VMEM((HEAD_DIM, tq), F32), pltpu.VMEM((nbp, HEAD_DIM), F32),
                        pltpu.VMEM((nb, 1, tq), F32)],
        compiler_params=_cparams(("parallel", "parallel", "arbitrary")),
        name="moba",
    )(qT, y, vT, gains)


def _outproj_kernel(on_ref, om_ref, w_ref, x_ref, mod_ref, o_ref):
    half = on_ref.shape[1]
    acc = jnp.dot(on_ref[...], w_ref[0:half, :], preferred_element_type=F32)
    acc = acc + jnp.dot(om_ref[...], w_ref[half:, :], preferred_element_type=F32)
    o_ref[...] = x_ref[...] + mod_ref[0, 2:3, :] * acc


def _out_proj(o_nsa, o_moba, w_out, x2, mod3, T):
    BT, D = x2.shape
    tm = min(512, T)
    tpb = T // tm
    half = o_nsa.shape[1]
    return pl.pallas_call(
        _outproj_kernel,
        grid=(BT // tm,),
        in_specs=[pl.BlockSpec((tm, half), lambda i: (i, 0)),
                  pl.BlockSpec((tm, half), lambda i: (i, 0)),
                  pl.BlockSpec((D, D), lambda i: (0, 0)),
                  pl.BlockSpec((tm, D), lambda i: (i, 0)),
                  pl.BlockSpec((1, 6, D), lambda i: (i // tpb, 0, 0))],
        out_specs=pl.BlockSpec((tm, D), lambda i: (i, 0)),
        out_shape=jax.ShapeDtypeStruct((BT, D), F32),
        compiler_params=_cparams(("parallel",)),
        name="out_proj",
    )(o_nsa, o_moba, w_out, x2, mod3)


def _ffn_kernel(x_ref, mod_ref, wg_ref, wu_ref, wo_ref, o_ref, h_scr):
    j = pl.program_id(1)

    @pl.when(j == 0)
    def _():
        x = x_ref[...]
        ms = jnp.mean(x * x, axis=-1, keepdims=True)
        h = x * lax.rsqrt(ms + EPS) * (1.0 + mod_ref[0, 4:5, :]) + mod_ref[0, 3:4, :]
        h_scr[...] = h.astype(BF16)
        o_ref[...] = jnp.zeros(o_ref.shape, F32)

    hb = h_scr[...]
    gate = jnp.dot(hb, wg_ref[...], preferred_element_type=F32)
    up = jnp.dot(hb, wu_ref[...], preferred_element_type=F32)
    act = (gate * jax.nn.sigmoid(gate) * up).astype(BF16)
    o_ref[...] += jnp.dot(act, wo_ref[...], preferred_element_type=F32)

    @pl.when(j == pl.num_programs(1) - 1)
    def _():
        o_ref[...] = x_ref[...] + mod_ref[0, 5:6, :] * o_ref[...]


def _ffn(x1, mod3, w_in, w_out, T):
    BT, D = x1.shape
    Fh = w_out.shape[0]
    tm = min(512, T)
    tf = 512
    tpb = T // tm
    nf = Fh // tf
    return pl.pallas_call(
        _ffn_kernel,
        grid=(BT // tm, nf),
        in_specs=[pl.BlockSpec((tm, D), lambda i, j: (i, 0)),
                  pl.BlockSpec((1, 6, D), lambda i, j: (i // tpb, 0, 0)),
                  pl.BlockSpec((D, tf), lambda i, j: (0, j)),
                  pl.BlockSpec((D, tf), lambda i, j: (0, nf + j)),
                  pl.BlockSpec((tf, D), lambda i, j: (j, 0))],
        out_specs=pl.BlockSpec((tm, D), lambda i, j: (i, 0)),
        out_shape=jax.ShapeDtypeStruct((BT, D), F32),
        scratch_shapes=[pltpu.VMEM((tm, D), BF16)],
        compiler_params=_cparams(("parallel", "arbitrary")),
        name="ffn",
    )(x1, mod3, w_in, w_in, w_out)


def _rope_lane_tables(T):
    inv = ROPE_THETA ** (-jnp.arange(0, ROPE_DIMS, 2, dtype=F32) / ROPE_DIMS)
    ang = jnp.arange(T).astype(F32)[:, None] * inv[None, :]
    cos, sin = jnp.cos(ang), jnp.sin(ang)
    rest = HEAD_DIM - ROPE_DIMS
    cc = jnp.concatenate([cos, cos, jnp.ones((T, rest), F32)], axis=1)
    sa = jnp.concatenate([-sin, jnp.zeros((T, HEAD_DIM - ROPE_HALF), F32)], axis=1)
    sb = jnp.concatenate([jnp.zeros((T, ROPE_HALF), F32), sin, jnp.zeros((T, rest), F32)], axis=1)
    return cc, sa, sb


def _tile_transposed(v, B, T, tk):
    n = v.shape[0]
    return v.reshape(n * B, T // tk, tk, HEAD_DIM).transpose(0, 1, 3, 2)


def _layer(x2, c, B, T, w_ada, b_ada, w_in, nsa_q_norm, nsa_k_norm, moba_q_norm, moba_k_norm,
           cmp_pe_k, cmp_w1_k, cmp_w2_k, cmp_pe_v, cmp_w1_v, cmp_w2_v, out_norm, w_out,
           w_ffn_in, w_ffn_out):
    D = x2.shape[1]
    G = NSA_KV_HEADS
    scale = HEAD_DIM ** -0.5
    assert T % MOBA_BLOCK == 0 and T % SLC_TK == 0 and T % NSA_TQ == 0

    mod3 = _adaln(c, w_ada, b_ada)

    qw, kvw, gw, mw = NSA_HEADS * HEAD_DIM, G * HEAD_DIM, NSA_HEADS * 3, MOBA_HEADS * HEAD_DIM
    offs = [0, qw] + [qw + kvw * n for n in range(1, 7)]
    col = lambda a, n: w_in[:, a:a + n]
    o_q, o_kc, o_vc, o_ks, o_vs, o_kw, o_vw, o_g = offs
    o_mq = o_g + gw
    w_main = jnp.concatenate(
        [col(o_q, qw), col(o_ks, kvw), col(o_kw, kvw), col(o_mq, mw), col(o_mq + mw, mw),
         col(o_kc, kvw), col(o_vc, kvw), col(o_vs, kvw), col(o_vw, kvw), col(o_mq + 2 * mw, mw)],
        axis=1).astype(BF16)
    wg = col(o_g, gw).reshape(D, G, NSA_GROUP * 3)
    w_gate = jnp.pad(wg, ((0, 0), (0, 0), (0, GATE_LANES - NSA_GROUP * 3))).reshape(D, G * GATE_LANES)
    w_gate = w_gate.astype(BF16)
    rep = lambda g_, n: jnp.broadcast_to(g_, (n, HEAD_DIM))
    gains = jnp.concatenate(
        [rep(nsa_q_norm * scale, NSA_HEADS), rep(nsa_k_norm[1], G), rep(nsa_k_norm[2], G),
         rep(moba_q_norm * scale, MOBA_HEADS), rep(moba_k_norm, MOBA_HEADS),
         jnp.ones((N_PROJ_HEADS - N_TREATED, HEAD_DIM), F32)], axis=0)
    gains = gains.reshape(N_PROJ_HEADS // HEADS_PER_TILE, HEADS_PER_TILE, HEAD_DIM)
    cc, sa, sb = _rope_lane_tables(T)

    y, gates = _in_proj(x2, mod3, w_main, w_gate, gains, cc, sa, sb, T)

    qT_nsa = y[H_NSA_Q:H_NSA_Q + NSA_HEADS].transpose(0, 2, 1)
    qT_moba = y[H_MOBA_Q:H_MOBA_Q + MOBA_HEADS].transpose(0, 2, 1)
    vsT = _tile_transposed(y[H_VSLC:H_VSLC + G], B, T, SLC_TK)
    vwT = _tile_transposed(y[H_VWIN:H_VWIN + G], B, T, WIN_TK)
    vmT = _tile_transposed(y[H_MOBA_V:H_MOBA_V + MOBA_HEADS], B, T, MOBA_TK)
    gT = gates.reshape(B * T, G, GATE_LANES)[:, :, :16].transpose(1, 2, 0)

    ncp = T // CMP_STRIDE
    half = CMP_STRIDE * HEAD_DIM
    hc = y[H_KCMP:H_KCMP + 2 * G].reshape(2 * G, B, ncp, half)
    w1 = jnp.stack([cmp_w1_k, cmp_w1_v]).astype(BF16)
    w1cat = jnp.concatenate([w1[:, :half], w1[:, half:]], axis=2)
    pe8 = jnp.broadcast_to(jnp.stack([cmp_pe_k, cmp_pe_v]).reshape(2, 1, 2 * half), (2, 8, 2 * half))
    w2 = jnp.stack([cmp_w2_k, cmp_w2_v]).astype(BF16)
    cmp_rows = lambda t: jnp.pad(t[CMP_BLOCK - 1::CMP_STRIDE], ((0, 1), (0, 0)))
    kcvc = _compress(hc, w1cat, pe8.astype(BF16), w1, w2, nsa_k_norm[0].reshape(1, HEAD_DIM),
                     cmp_rows(cc), cmp_rows(sa), cmp_rows(sb))
    vcT = kcvc[G:].transpose(0, 1, 3, 2)

    nsb = T // SLC_BLOCK
    cs = jnp.arange(ncp)[None, :] * CMP_STRIDE
    sbk = jnp.arange(nsb)[:, None] * SLC_BLOCK
    ovT = ((cs < sbk + SLC_BLOCK) & (cs + CMP_BLOCK > sbk) & (jnp.arange(ncp)[None, :] < ncp - 1)).astype(BF16)

    on = out_norm.reshape(N_HEADS, HEAD_DIM)
    o_nsa = _nsa(qT_nsa, kcvc, vcT, y, vsT, vwT, gT, ovT, on[:NSA_HEADS].reshape(G, NSA_GROUP, HEAD_DIM), B, T)
    o_moba = _moba(qT_moba, y, vmT, on[NSA_HEADS:].reshape(MOBA_HEADS, 1, HEAD_DIM), B, T)

    x1 = _out_proj(o_nsa, o_moba, w_out.astype(BF16), x2, mod3, T)
    return _ffn(x1, mod3, w_ffn_in.astype(BF16), w_ffn_out.astype(BF16), T)


def kernel(x, c, w_ada, b_ada, w_in, nsa_q_norm, nsa_k_norm, moba_q_norm, moba_k_norm, cmp_pe_k, cmp_w1_k, cmp_w2_k, cmp_pe_v, cmp_w1_v, cmp_w2_v, out_norm, w_out, w_ffn_in, w_ffn_out):
    B, T, D = x.shape
    x2 = x.reshape(B * T, D)
    for l in range(w_ada.shape[0]):
        x2 = _layer(x2, c, B, T, w_ada[l], b_ada[l], w_in[l], nsa_q_norm[l], nsa_k_norm[l],
                    moba_q_norm[l], moba_k_norm[l], cmp_pe_k[l], cmp_w1_k[l], cmp_w2_k[l],
                    cmp_pe_v[l], cmp_w1_v[l], cmp_w2_v[l], out_norm[l], w_out[l],
                    w_ffn_in[l], w_ffn_out[l])
    return x2.reshape(B, T, D)
```

```python
import functools

import jax
import jax.numpy as jnp
from jax import lax
from jax.experimental import pallas as pl
from jax.experimental.pallas import tpu as pltpu

F32 = jnp.float32
BF16 = jnp.bfloat16

HEAD_DIM = 128
NSA_HEADS = 8
NSA_KV_HEADS = 2
NSA_GROUP = NSA_HEADS // NSA_KV_HEADS
MOBA_HEADS = 8
N_HEADS = NSA_HEADS + MOBA_HEADS
CMP_BLOCK = 32
CMP_STRIDE = 16
SLC_BLOCK = 64
SLC_TOPK = 16
WINDOW = 512
FORCE_BONUS = 1e4
MOBA_BLOCK = 256
MOBA_TOPK = 3
ROPE_THETA = 500000.0
ROPE_DIMS = HEAD_DIM // 4
ROPE_HALF = ROPE_DIMS // 2
EPS = 1e-6
NEG = -1e30

V7X_VMEM_BYTES = 64 * 1024 * 1024
VMEM_LIMIT = V7X_VMEM_BYTES - 8 * 1024 * 1024

H_NSA_Q = 0
H_KSLC = 8
H_KWIN = 10
H_MOBA_Q = 12
H_MOBA_K = 20
N_TREATED = 28
H_KCMP = 28
H_VCMP = 30
H_VSLC = 32
H_VWIN = 34
H_MOBA_V = 36
N_PROJ_HEADS = 44
HEADS_PER_TILE = 4
GATE_LANES = 128

NSA_TQ = 128
SLC_TK = 256
SLC_TKB = 1024
WIN_TK = 128
WIN_SPAN = WINDOW + NSA_TQ
MOBA_TQ = 256
MOBA_TK = MOBA_BLOCK
MOBA_TKB = 1024
MOBA_HB = 2


def _cparams(sem):
    return pltpu.CompilerParams(dimension_semantics=sem, vmem_limit_bytes=VMEM_LIMIT)


def _split3(a):
    hi = a.astype(BF16)
    r1 = a - hi.astype(F32)
    mid = r1.astype(BF16)
    lo = (r1 - mid.astype(F32)).astype(BF16)
    return hi, mid, lo


def _adaln_kernel(c_ref, w_ref, b_ref, o_ref):
    cv = c_ref[...]
    s = cv * jax.nn.sigmoid(cv)
    w = w_ref[...]
    acc = jnp.zeros(o_ref.shape, F32)
    for part in _split3(s):
        for wpart in _split3(w)[:2]:
            acc = acc + jnp.dot(part, wpart, preferred_element_type=F32)
    o_ref[...] = acc + b_ref[...]


def _adaln(c, w_ada, b_ada):
    B, D = c.shape
    N = w_ada.shape[1]
    tn = 1024
    c8 = jnp.zeros((8, D), F32).at[:B].set(c)
    out = pl.pallas_call(
        _adaln_kernel,
        grid=(N // tn,),
        in_specs=[pl.BlockSpec((8, D), lambda j: (0, 0)),
                  pl.BlockSpec((D, tn), lambda j: (0, j)),
                  pl.BlockSpec((1, tn), lambda j: (0, j))],
        out_specs=pl.BlockSpec((8, tn), lambda j: (0, j)),
        out_shape=jax.ShapeDtypeStruct((8, N), F32),
        compiler_params=_cparams(("arbitrary",)),
        name="adaln",
    )(c8, w_ada, b_ada.reshape(1, N))
    return out[:B].reshape(B, 6, D)


def _rope(y, cc, sa, sb):
    return (y * cc + pltpu.roll(y, HEAD_DIM - ROPE_HALF, 1) * sa + pltpu.roll(y, ROPE_HALF, 1) * sb)


def _inproj_kernel(x_ref, mod_ref, w_ref, wg_ref, gain_ref, cc_ref, sa_ref, sb_ref,
                   y_ref, g_ref, h_scr, *, n_treated_tiles):
    j = pl.program_id(1)

    @pl.when(j == 0)
    def _():
        x = x_ref[...]
        ms = jnp.mean(x * x, axis=-1, keepdims=True)
        h = x * lax.rsqrt(ms + EPS) * (1.0 + mod_ref[0, 1:2, :]) + mod_ref[0, 0:1, :]
        hb = h.astype(BF16)
        h_scr[...] = hb
        g_ref[...] = jax.nn.sigmoid(jnp.dot(hb, wg_ref[...], preferred_element_type=F32))

    acc = jnp.dot(h_scr[...], w_ref[...], preferred_element_type=F32)

    @pl.when(j < n_treated_tiles)
    def _():
        cc, sa, sb = cc_ref[...], sa_ref[...], sb_ref[...]
        for hh in range(HEADS_PER_TILE):
            yh = acc[:, hh * HEAD_DIM:(hh + 1) * HEAD_DIM]
            ms = jnp.mean(yh * yh, axis=-1, keepdims=True)
            yn = yh * lax.rsqrt(ms + EPS) * gain_ref[0, hh:hh + 1, :]
            y_ref[hh] = _rope(yn, cc, sa, sb).astype(BF16)

    @pl.when(j >= n_treated_tiles)
    def _():
        for hh in range(HEADS_PER_TILE):
            y_ref[hh] = acc[:, hh * HEAD_DIM:(hh + 1) * HEAD_DIM].astype(BF16)


def _in_proj(x2, mod3, w_main, w_gate, gains, cc, sa, sb, T):
    BT, D = x2.shape
    tm = min(1024, T)
    tn = HEADS_PER_TILE * HEAD_DIM
    n_tiles = N_PROJ_HEADS // HEADS_PER_TILE
    tpb = T // tm
    kern = functools.partial(_inproj_kernel, n_treated_tiles=N_TREATED // HEADS_PER_TILE)
    y, g = pl.pallas_call(
        kern,
        grid=(BT // tm, n_tiles),
        in_specs=[pl.BlockSpec((tm, D), lambda i, j: (i, 0)),
                  pl.BlockSpec((1, 6, D), lambda i, j: (i // tpb, 0, 0)),
                  pl.BlockSpec((D, tn), lambda i, j: (0, j)),
                  pl.BlockSpec((D, NSA_KV_HEADS * GATE_LANES), lambda i, j: (0, 0)),
                  pl.BlockSpec((1, HEADS_PER_TILE, HEAD_DIM), lambda i, j: (j, 0, 0)),
                  pl.BlockSpec((tm, HEAD_DIM), lambda i, j: (i % tpb, 0)),
                  pl.BlockSpec((tm, HEAD_DIM), lambda i, j: (i % tpb, 0)),
                  pl.BlockSpec((tm, HEAD_DIM), lambda i, j: (i % tpb, 0))],
        out_specs=[pl.BlockSpec((HEADS_PER_TILE, tm, HEAD_DIM), lambda i, j: (j, i, 0)),
                   pl.BlockSpec((tm, NSA_KV_HEADS * GATE_LANES), lambda i, j: (i, 0))],
        out_shape=[jax.ShapeDtypeStruct((N_PROJ_HEADS, BT, HEAD_DIM), BF16),
                   jax.ShapeDtypeStruct((BT, NSA_KV_HEADS * GATE_LANES), F32)],
        scratch_shapes=[pltpu.VMEM((tm, D), BF16)],
        compiler_params=_cparams(("parallel", "arbitrary")),
        name="in_proj",
    )(x2, mod3, w_main, w_gate, gains, cc, sa, sb)
    return y, g


def _compress_kernel(h_ref, w1c_ref, pe_ref, w1_ref, w2_ref, gain_ref, cc_ref, sa_ref, sb_ref, o_ref):
    a = pl.program_id(0)
    ncp = h_ref.shape[2]
    z = jnp.dot(h_ref[0, 0], w1c_ref[0], preferred_element_type=F32)
    top = z[:, :HEAD_DIM]
    bot = pltpu.roll(z[:, HEAD_DIM:], ncp - 1, 0)
    pe_term = jnp.dot(pe_ref[0], w1_ref[0], preferred_element_type=F32)[0:1, :]
    pre = top + bot + pe_term
    act = pre * jax.nn.sigmoid(pre)
    out = jnp.dot(act.astype(BF16), w2_ref[0], preferred_element_type=F32)
    live = lax.broadcasted_iota(jnp.int32, out.shape, 0) < ncp - 1
    out = jnp.where(live, out, 0.0)

    @pl.when(a < NSA_KV_HEADS)
    def _():
        ms = jnp.mean(out * out, axis=-1, keepdims=True)
        yn = out * lax.rsqrt(ms + EPS) * gain_ref[...]
        o_ref[0, 0] = _rope(yn, cc_ref[...], sa_ref[...], sb_ref[...]).astype(BF16)

    @pl.when(a >= NSA_KV_HEADS)
    def _():
        o_ref[0, 0] = out.astype(BF16)


def _compress(hc, w1cat, pe8, w1, w2, gain, cc, sa, sb):
    A, B, ncp, K = hc.shape
    G = NSA_KV_HEADS
    return pl.pallas_call(
        _compress_kernel,
        grid=(A, B),
        in_specs=[pl.BlockSpec((1, 1, ncp, K), lambda a, b: (a, b, 0, 0)),
                  pl.BlockSpec((1, K, 2 * HEAD_DIM), lambda a, b: (a // G, 0, 0)),
                  pl.BlockSpec((1, 8, 2 * K), lambda a, b: (a // G, 0, 0)),
                  pl.BlockSpec((1, 2 * K, HEAD_DIM), lambda a, b: (a // G, 0, 0)),
                  pl.BlockSpec((1, HEAD_DIM, HEAD_DIM), lambda a, b: (a // G, 0, 0)),
                  pl.BlockSpec((1, HEAD_DIM), lambda a, b: (0, 0)),
                  pl.BlockSpec((ncp, HEAD_DIM), lambda a, b: (0, 0)),
                  pl.BlockSpec((ncp, HEAD_DIM), lambda a, b: (0, 0)),
                  pl.BlockSpec((ncp, HEAD_DIM), lambda a, b: (0, 0))],
        out_specs=pl.BlockSpec((1, 1, ncp, HEAD_DIM), lambda a, b: (a, b, 0, 0)),
        out_shape=jax.ShapeDtypeStruct((A, B, ncp, HEAD_DIM), BF16),
        compiler_params=_cparams(("arbitrary", "arbitrary")),
        name="compress",
    )(hc, w1cat, pe8, w1, w2, gain, cc, sa, sb)


def _reset(m_scr, l_scr, acc_scr):
    m_scr[...] = jnp.full(m_scr.shape, NEG, F32)
    l_scr[...] = jnp.zeros(l_scr.shape, F32)
    acc_scr[...] = jnp.zeros(acc_scr.shape, F32)


def _online_step(s, vT, m_scr, l_scr, acc_scr):
    m_prev = m_scr[...]
    m_new = jnp.maximum(m_prev, jnp.max(s, axis=0, keepdims=True))
    alpha = jnp.exp(m_prev - m_new)
    p = jnp.exp(s - m_new)
    l_scr[...] = alpha * l_scr[...] + jnp.sum(p, axis=0, keepdims=True)
    acc_scr[...] = alpha * acc_scr[...] + jnp.dot(vT, p.astype(BF16), preferred_element_type=F32)
    m_scr[...] = m_new


def _rank_select(score, n_idx, k):
    rank = jnp.zeros(score.shape, F32)
    for m in range(score.shape[0]):
        sm = score[m:m + 1, :]
        gt = jnp.where(sm > score, 1.0, 0.0)
        ge = jnp.where(sm >= score, 1.0, 0.0)
        rank = rank + jnp.where(n_idx > m, ge, gt)
    return rank < k


def _pad_rows(a, rows):
    return jnp.concatenate([a, jnp.zeros((rows - a.shape[0], a.shape[1]), a.dtype)], axis=0)


def _masked_scores(k_ref, e_ref, start, size, q_aug):
    rows = pl.ds(pl.multiple_of(start, 128), size)
    k_aug = jnp.concatenate([k_ref[rows, :], e_ref[rows, :]], axis=1)
    return jnp.dot(k_aug, q_aug, preferred_element_type=F32)


def _blocked_attention(k_ref, e_ref, vT_ref, q_aug, n_full, tk, tkb, m_scr, l_scr, acc_scr):
    per = tkb // tk

    def big(jb, carry):
        s = _masked_scores(k_ref, e_ref, jb * tkb, tkb, q_aug)
        vT = jnp.concatenate([vT_ref[jb * per + a] for a in range(per)], axis=1)
        _online_step(s, vT, m_scr, l_scr, acc_scr)
        return carry

    def small(j, carry):
        s = _masked_scores(k_ref, e_ref, j * tk, tk, q_aug)
        _online_step(s, vT_ref[j], m_scr, l_scr, acc_scr)
        return carry

    n_big = n_full // per
    lax.fori_loop(0, n_big, big, 0)
    lax.fori_loop(n_big * per, n_full, small, 0)


def _nsa_kernel(qT_ref, kc_ref, vcT_ref, ks_ref, vsT_ref, kw_ref, vwT_ref, gT_ref, ovT_ref, e_ref,
                gain_ref, o_ref, m_scr, l_scr, acc_scr, comb_scr, qaug_scr, *, T):
    R, tq = NSA_GROUP, NSA_TQ
    i = pl.program_id(2)
    t0 = i * tq
    ncp = T // CMP_STRIDE
    nsb = T // SLC_BLOCK
    qT = jnp.concatenate([qT_ref[r] for r in range(R)], axis=1)
    gT = gT_ref[0]

    def per_head(row0):
        return jnp.concatenate([gT[row0 + 3 * r:row0 + 3 * r + 1, :] for r in range(R)], axis=1)

    def lanes_x_heads(a):
        return jnp.concatenate([a] * R, axis=1)

    s = jnp.dot(kc_ref[0, 0], qT, preferred_element_type=F32)
    c_idx = lax.broadcasted_iota(jnp.int32, (ncp, tq), 0)
    t_c = t0 + lax.broadcasted_iota(jnp.int32, (ncp, tq), 1)
    vis = lanes_x_heads((c_idx * CMP_STRIDE + (CMP_BLOCK - 1) <= t_c) & (c_idx < ncp - 1))
    s = jnp.where(vis, s, NEG)
    m = jnp.max(s, axis=0, keepdims=True)
    p = jnp.where(vis, jnp.exp(s - m), 0.0)
    l = jnp.sum(p, axis=0, keepdims=True)
    p = p / jnp.where(l > 0.0, l, 1.0)
    o_cmp = jnp.dot(vcT_ref[0, 0], p.astype(BF16), preferred_element_type=F32)
    comb_scr[...] = o_cmp * per_head(0)

    psum = p[:, 0:tq]
    for r in range(1, R):
        psum = psum + p[:, r * tq:(r + 1) * tq]
    ov = ovT_ref[...]
    imp = jnp.zeros((nsb, tq), F32)
    for part in _split3(psum):
        imp = imp + jnp.dot(ov, part, preferred_element_type=F32)
    n_idx = lax.broadcasted_iota(jnp.int32, (nsb, tq), 0)
    cur = (t0 + lax.broadcasted_iota(jnp.int32, (nsb, tq), 1)) // SLC_BLOCK
    forced = (n_idx == 0) | (n_idx == cur) | (n_idx == cur - 1)
    valid = n_idx <= cur
    score = jnp.where(valid, imp + jnp.where(forced, FORCE_BONUS, 0.0), NEG)
    sel = _rank_select(score, n_idx, min(SLC_TOPK, nsb)) & valid
    bias = _pad_rows(jnp.where(sel, 0.0, NEG), HEAD_DIM).astype(BF16)
    qaug_scr[0:HEAD_DIM, :] = qT
    qaug_scr[HEAD_DIM:, :] = lanes_x_heads(bias)
    q_aug = qaug_scr[...]

    _reset(m_scr, l_scr, acc_scr)
    j_last = t0 // SLC_TK
    _blocked_attention(ks_ref.at[0], e_ref, vsT_ref.at[0], q_aug, j_last, SLC_TK, SLC_TKB,
                       m_scr, l_scr, acc_scr)
    key = j_last * SLC_TK + lax.broadcasted_iota(jnp.int32, (SLC_TK, tq), 0)
    t_k = t0 + lax.broadcasted_iota(jnp.int32, (SLC_TK, tq), 1)
    s_last = _masked_scores(ks_ref.at[0], e_ref, j_last * SLC_TK, SLC_TK, q_aug)
    s_last = jnp.where(lanes_x_heads(key <= t_k), s_last, NEG)
    _online_step(s_last, vsT_ref[0, j_last], m_scr, l_scr, acc_scr)
    comb_scr[...] += acc_scr[...] / l_scr[...] * per_head(1)

    w0 = jnp.maximum(t0 + tq - WIN_SPAN, 0)
    k_w = kw_ref[0, pl.ds(pl.multiple_of(w0, WIN_TK), WIN_SPAN), :]
    s_w = jnp.dot(k_w, qT, preferred_element_type=F32)
    kpos = w0 + lax.broadcasted_iota(jnp.int32, (WIN_SPAN, tq), 0)
    tpos = t0 + lax.broadcasted_iota(jnp.int32, (WIN_SPAN, tq), 1)
    ok = lanes_x_heads((kpos <= tpos) & (tpos - kpos < WINDOW))
    s_w = jnp.where(ok, s_w, NEG)
    m_w = jnp.max(s_w, axis=0, keepdims=True)
    p_w = jnp.exp(s_w - m_w)
    l_w = jnp.sum(p_w, axis=0, keepdims=True)
    jw = w0 // WIN_TK
    v_w = jnp.concatenate([vwT_ref[0, jw + a] for a in range(WIN_SPAN // WIN_TK)], axis=1)
    o_w = jnp.dot(v_w, p_w.astype(BF16), preferred_element_type=F32)
    comb = comb_scr[...] + o_w / l_w * per_head(2)

    for r in range(R):
        oT = comb[:, r * tq:(r + 1) * tq]
        ms = jnp.mean(oT * oT, axis=0, keepdims=True)
        on = (oT * lax.rsqrt(ms + EPS)).T * gain_ref[0, r:r + 1, :]
        o_ref[:, r * HEAD_DIM:(r + 1) * HEAD_DIM] = on.astype(BF16)


def _nsa(qT, kcvc, vcT, y, vsT, vwT, gT, ovT, e_slc, gains, B, T):
    G, R, tq = NSA_KV_HEADS, NSA_GROUP, NSA_TQ
    nq = T // tq
    ncp = T // CMP_STRIDE
    nsb = T // SLC_BLOCK
    kern = functools.partial(_nsa_kernel, T=T)
    return pl.pallas_call(
        kern,
        grid=(B, G, nq),
        in_specs=[pl.BlockSpec((R, HEAD_DIM, tq), lambda b, g, i: (g, 0, b * nq + i)),
                  pl.BlockSpec((1, 1, ncp, HEAD_DIM), lambda b, g, i: (g, b, 0, 0)),
                  pl.BlockSpec((1, 1, HEAD_DIM, ncp), lambda b, g, i: (g, b, 0, 0)),
                  pl.BlockSpec((1, T, HEAD_DIM), lambda b, g, i: (H_KSLC + g, b, 0)),
                  pl.BlockSpec((1, T // SLC_TK, HEAD_DIM, SLC_TK), lambda b, g, i: (g * B + b, 0, 0, 0)),
                  pl.BlockSpec((1, T, HEAD_DIM), lambda b, g, i: (H_KWIN + g, b, 0)),
                  pl.BlockSpec((1, T // WIN_TK, HEAD_DIM, WIN_TK), lambda b, g, i: (g * B + b, 0, 0, 0)),
                  pl.BlockSpec((1, 16, tq), lambda b, g, i: (g, 0, b * nq + i)),
                  pl.BlockSpec((nsb, ncp), lambda b, g, i: (0, 0)),
                  pl.BlockSpec((T, HEAD_DIM), lambda b, g, i: (0, 0)),
                  pl.BlockSpec((1, R, HEAD_DIM), lambda b, g, i: (g, 0, 0))],
        out_specs=pl.BlockSpec((tq, R * HEAD_DIM), lambda b, g, i: (b * nq + i, g)),
        out_shape=jax.ShapeDtypeStruct((B * T, NSA_HEADS * HEAD_DIM), BF16),
        scratch_shapes=[pltpu.VMEM((1, R * tq), F32), pltpu.VMEM((1, R * tq), F32),
                        pltpu.VMEM((HEAD_DIM, R * tq), F32), pltpu.VMEM((HEAD_DIM, R * tq), F32),
                        pltpu.VMEM((2 * HEAD_DIM, R * tq), BF16)],
        compiler_params=_cparams(("parallel", "parallel", "arbitrary")),
        name="nsa",
    )(qT, kcvc, vcT, y, vsT, y, vwT, gT, ovT, e_slc, gains)


def _moba_kernel(qT_ref, k_ref, vT_ref, e_ref, gain_ref, o_ref, m_scr, l_scr, acc_scr, kmean_scr, qaug_scr, *, T):
    tq, tk = MOBA_TQ, MOBA_TK
    nb = T // MOBA_BLOCK
    nbp = kmean_scr.shape[1]
    i = pl.program_id(2)
    t0 = i * tq

    @pl.when(i == 0)
    def _():
        kmean_scr[...] = jnp.zeros(kmean_scr.shape, F32)
        for a in range(MOBA_HB):
            kb = k_ref[a].astype(F32).reshape(nb, MOBA_BLOCK, HEAD_DIM)
            kmean_scr[a, 0:nb, :] = jnp.mean(kb, axis=1)

    n_idx = lax.broadcasted_iota(jnp.int32, (nbp, tq), 0)
    cur = (t0 + lax.broadcasted_iota(jnp.int32, (nbp, tq), 1)) // MOBA_BLOCK
    past = n_idx < cur
    causal = lax.broadcasted_iota(jnp.int32, (tk, tq), 0) <= lax.broadcasted_iota(jnp.int32, (tk, tq), 1)

    for a in range(MOBA_HB):
        qT = qT_ref[a]
        gate = jnp.zeros((nbp, tq), F32)
        for part in _split3(kmean_scr[a]):
            gate = gate + jnp.dot(part, qT, preferred_element_type=F32)
        sel = _rank_select(jnp.where(past, gate, NEG), n_idx, min(MOBA_TOPK, nb)) & past
        qaug_scr[a, 0:HEAD_DIM, :] = qT
        qaug_scr[a, HEAD_DIM:, :] = _pad_rows(jnp.where(sel, 0.0, NEG), HEAD_DIM).astype(BF16)

    for a in range(MOBA_HB):
        _reset(m_scr.at[a], l_scr.at[a], acc_scr.at[a])

    per = MOBA_TKB // tk
    n_big = i // per

    def big(jb, carry):
        for a in range(MOBA_HB):
            s = _masked_scores(k_ref.at[a], e_ref, jb * MOBA_TKB, MOBA_TKB, qaug_scr[a])
            vT = jnp.concatenate([vT_ref[a, 0, jb * per + c] for c in range(per)], axis=1)
            _online_step(s, vT, m_scr.at[a], l_scr.at[a], acc_scr.at[a])
        return carry

    def small(j, carry):
        for a in range(MOBA_HB):
            s = _masked_scores(k_ref.at[a], e_ref, j * tk, tk, qaug_scr[a])
            _online_step(s, vT_ref[a, 0, j], m_scr.at[a], l_scr.at[a], acc_scr.at[a])
        return carry

    lax.fori_loop(0, n_big, big, 0)
    lax.fori_loop(n_big * per, i, small, 0)

    for a in range(MOBA_HB):
        k_own = k_ref[a, pl.ds(pl.multiple_of(t0, tk), tk), :]
        s_own = jnp.dot(k_own, qT_ref[a], preferred_element_type=F32)
        _online_step(jnp.where(causal, s_own, NEG), vT_ref[a, 0, i], m_scr.at[a], l_scr.at[a], acc_scr.at[a])
        oT = acc_scr[a] / l_scr[a]
        ms = jnp.mean(oT * oT, axis=0, keepdims=True)
        on = (oT * lax.rsqrt(ms + EPS)).T * gain_ref[a]
        o_ref[:, a * HEAD_DIM:(a + 1) * HEAD_DIM] = on.astype(BF16)


def _moba(qT, y, vT, e_moba, gains, B, T):
    H, HB, tq, tk = MOBA_HEADS, MOBA_HB, MOBA_TQ, MOBA_TK
    nq = T // tq
    nb = T // MOBA_BLOCK
    nbp = max(16, nb)
    kern = functools.partial(_moba_kernel, T=T)
    return pl.pallas_call(
        kern,
        grid=(B, H // HB, nq),
        in_specs=[pl.BlockSpec((HB, HEAD_DIM, tq), lambda b, h, i: (h, 0, b * nq + i)),
                  pl.BlockSpec((HB, T, HEAD_DIM), lambda b, h, i: (H_MOBA_K // HB + h, b, 0)),
                  pl.BlockSpec((HB, 1, T // tk, HEAD_DIM, tk), lambda b, h, i: (h, b, 0, 0, 0)),
                  pl.BlockSpec((T, HEAD_DIM), lambda b, h, i: (0, 0)),
                  pl.BlockSpec((HB, 1, HEAD_DIM), lambda b, h, i: (h, 0, 0))],
        out_specs=pl.BlockSpec((tq, HB * HEAD_DIM), lambda b, h, i: (b * nq + i, h)),
        out_shape=jax.ShapeDtypeStruct((B * T, H * HEAD_DIM), BF16),
        scratch_shapes=[pltpu.VMEM((HB, 1, tq), F32), pltpu.VMEM((HB, 1, tq), F32),
                        pltpu.VMEM((HB, HEAD_DIM, tq), F32), pltpu.VMEM((HB, nbp, HEAD_DIM), F32),
                        pltpu.VMEM((HB, 2 * HEAD_DIM, tq), BF16)],
        compiler_params=_cparams(("parallel", "parallel", "arbitrary")),
        name="moba",
    )(qT, y, vT, e_moba, gains)


def _outproj_kernel(on_ref, om_ref, w_ref, x_ref, mod_ref, o_ref):
    half = on_ref.shape[1]
    acc = jnp.dot(on_ref[...], w_ref[0:half, :], preferred_element_type=F32)
    acc = acc + jnp.dot(om_ref[...], w_ref[half:, :], preferred_element_type=F32)
    o_ref[...] = x_ref[...] + mod_ref[0, 2:3, :] * acc


def _out_proj(o_nsa, o_moba, w_out, x2, mod3, T):
    BT, D = x2.shape
    tm = min(512, T)
    tpb = T // tm
    half = o_nsa.shape[1]
    return pl.pallas_call(
        _outproj_kernel,
        grid=(BT // tm,),
        in_specs=[pl.BlockSpec((tm, half), lambda i: (i, 0)),
                  pl.BlockSpec((tm, half), lambda i: (i, 0)),
                  pl.BlockSpec((D, D), lambda i: (0, 0)),
                  pl.BlockSpec((tm, D), lambda i: (i, 0)),
                  pl.BlockSpec((1, 6, D), lambda i: (i // tpb, 0, 0))],
        out_specs=pl.BlockSpec((tm, D), lambda i: (i, 0)),
        out_shape=jax.ShapeDtypeStruct((BT, D), F32),
        compiler_params=_cparams(("parallel",)),
        name="out_proj",
    )(o_nsa, o_moba, w_out, x2, mod3)


def _ffn_kernel(x_ref, mod_ref, wg_ref, wu_ref, wo_ref, o_ref, h_scr):
    j = pl.program_id(1)

    @pl.when(j == 0)
    def _():
        x = x_ref[...]
        ms = jnp.mean(x * x, axis=-1, keepdims=True)
        h = x * lax.rsqrt(ms + EPS) * (1.0 + mod_ref[0, 4:5, :]) + mod_ref[0, 3:4, :]
        h_scr[...] = h.astype(BF16)
        o_ref[...] = jnp.zeros(o_ref.shape, F32)

    hb = h_scr[...]
    gate = jnp.dot(hb, wg_ref[...], preferred_element_type=F32)
    up = jnp.dot(hb, wu_ref[...], preferred_element_type=F32)
    act = (gate * jax.nn.sigmoid(gate) * up).astype(BF16)
    o_ref[...] += jnp.dot(act, wo_ref[...], preferred_element_type=F32)

    @pl.when(j == pl.num_programs(1) - 1)
    def _():
        o_ref[...] = x_ref[...] + mod_ref[0, 5:6, :] * o_ref[...]


def _ffn(x1, mod3, w_in, w_out, T):
    BT, D = x1.shape
    Fh = w_out.shape[0]
    tm = min(512, T)
    tf = 512
    tpb = T // tm
    nf = Fh // tf
    return pl.pallas_call(
        _ffn_kernel,
        grid=(BT // tm, nf),
        in_specs=[pl.BlockSpec((tm, D), lambda i, j: (i, 0)),
                  pl.BlockSpec((1, 6, D), lambda i, j: (i // tpb, 0, 0)),
                  pl.BlockSpec((D, tf), lambda i, j: (0, j)),
                  pl.BlockSpec((D, tf), lambda i, j: (0, nf + j)),
                  pl.BlockSpec((tf, D), lambda i, j: (j, 0))],
        out_specs=pl.BlockSpec((tm, D), lambda i, j: (i, 0)),
        out_shape=jax.ShapeDtypeStruct((BT, D), F32),
        scratch_shapes=[pltpu.VMEM((tm, D), BF16)],
        compiler_params=_cparams(("parallel", "arbitrary")),
        name="ffn",
    )(x1, mod3, w_in, w_in, w_out)


def _rope_lane_tables(T):
    inv = ROPE_THETA ** (-jnp.arange(0, ROPE_DIMS, 2, dtype=F32) / ROPE_DIMS)
    ang = jnp.arange(T).astype(F32)[:, None] * inv[None, :]
    cos, sin = jnp.cos(ang), jnp.sin(ang)
    rest = HEAD_DIM - ROPE_DIMS
    cc = jnp.concatenate([cos, cos, jnp.ones((T, rest), F32)], axis=1)
    sa = jnp.concatenate([-sin, jnp.zeros((T, HEAD_DIM - ROPE_HALF), F32)], axis=1)
    sb = jnp.concatenate([jnp.zeros((T, ROPE_HALF), F32), sin, jnp.zeros((T, rest), F32)], axis=1)
    return cc, sa, sb


def _tile_transposed(v, B, T, tk):
    n = v.shape[0]
    return v.reshape(n, B, T // tk, tk, HEAD_DIM).transpose(0, 1, 2, 4, 3)


def _block_onehot(T, block):
    return (jnp.arange(T)[:, None] // block == jnp.arange(HEAD_DIM)[None, :]).astype(BF16)


def _layer(x2, c, B, T, w_ada, b_ada, w_in, nsa_q_norm, nsa_k_norm, moba_q_norm, moba_k_norm,
           cmp_pe_k, cmp_w1_k, cmp_w2_k, cmp_pe_v, cmp_w1_v, cmp_w2_v, out_norm, w_out,
           w_ffn_in, w_ffn_out):
    D = x2.shape[1]
    G = NSA_KV_HEADS
    scale = HEAD_DIM ** -0.5
    assert T % MOBA_BLOCK == 0 and T % SLC_TK == 0 and T % NSA_TQ == 0 and T >= WIN_SPAN
    assert T // SLC_BLOCK <= HEAD_DIM and T // MOBA_BLOCK <= HEAD_DIM

    mod3 = _adaln(c, w_ada, b_ada)

    qw, kvw, gw, mw = NSA_HEADS * HEAD_DIM, G * HEAD_DIM, NSA_HEADS * 3, MOBA_HEADS * HEAD_DIM
    offs = [0, qw] + [qw + kvw * n for n in range(1, 7)]
    col = lambda a, n: w_in[:, a:a + n]
    o_q, o_kc, o_vc, o_ks, o_vs, o_kw, o_vw, o_g = offs
    o_mq = o_g + gw
    w_main = jnp.concatenate(
        [col(o_q, qw), col(o_ks, kvw), col(o_kw, kvw), col(o_mq, mw), col(o_mq + mw, mw),
         col(o_kc, kvw), col(o_vc, kvw), col(o_vs, kvw), col(o_vw, kvw), col(o_mq + 2 * mw, mw)],
        axis=1).astype(BF16)
    wg = col(o_g, gw).reshape(D, G, NSA_GROUP * 3)
    w_gate = jnp.pad(wg, ((0, 0), (0, 0), (0, GATE_LANES - NSA_GROUP * 3))).reshape(D, G * GATE_LANES)
    w_gate = w_gate.astype(BF16)
    rep = lambda g_, n: jnp.broadcast_to(g_, (n, HEAD_DIM))
    gains = jnp.concatenate(
        [rep(nsa_q_norm * scale, NSA_HEADS), rep(nsa_k_norm[1], G), rep(nsa_k_norm[2], G),
         rep(moba_q_norm * scale, MOBA_HEADS), rep(moba_k_norm, MOBA_HEADS),
         jnp.ones((N_PROJ_HEADS - N_TREATED, HEAD_DIM), F32)], axis=0)
    gains = gains.reshape(N_PROJ_HEADS // HEADS_PER_TILE, HEADS_PER_TILE, HEAD_DIM)
    cc, sa, sb = _rope_lane_tables(T)

    y, gates = _in_proj(x2, mod3, w_main, w_gate, gains, cc, sa, sb, T)

    qT_nsa = y[H_NSA_Q:H_NSA_Q + NSA_HEADS].transpose(0, 2, 1)
    qT_moba = y[H_MOBA_Q:H_MOBA_Q + MOBA_HEADS].transpose(0, 2, 1)
    vsT = _tile_transposed(y[H_VSLC:H_VSLC + G], B, T, SLC_TK).reshape(G * B, T // SLC_TK, HEAD_DIM, SLC_TK)
    vwT = _tile_transposed(y[H_VWIN:H_VWIN + G], B, T, WIN_TK).reshape(G * B, T // WIN_TK, HEAD_DIM, WIN_TK)
    vmT = _tile_transposed(y[H_MOBA_V:H_MOBA_V + MOBA_HEADS], B, T, MOBA_TK)
    gT = gates.reshape(B * T, G, GATE_LANES)[:, :, :16].transpose(1, 2, 0)

    ncp = T // CMP_STRIDE
    half = CMP_STRIDE * HEAD_DIM
    hc = y[H_KCMP:H_KCMP + 2 * G].reshape(2 * G, B, ncp, half)
    w1 = jnp.stack([cmp_w1_k, cmp_w1_v]).astype(BF16)
    w1cat = jnp.concatenate([w1[:, :half], w1[:, half:]], axis=2)
    pe8 = jnp.broadcast_to(jnp.stack([cmp_pe_k, cmp_pe_v]).reshape(2, 1, 2 * half), (2, 8, 2 * half))
    w2 = jnp.stack([cmp_w2_k, cmp_w2_v]).astype(BF16)
    cmp_rows = lambda t: jnp.pad(t[CMP_BLOCK - 1::CMP_STRIDE], ((0, 1), (0, 0)))
    kcvc = _compress(hc, w1cat, pe8.astype(BF16), w1, w2, nsa_k_norm[0].reshape(1, HEAD_DIM),
                     cmp_rows(cc), cmp_rows(sa), cmp_rows(sb))
    vcT = kcvc[G:].transpose(0, 1, 3, 2)

    nsb = T // SLC_BLOCK
    cs = jnp.arange(ncp)[None, :] * CMP_STRIDE
    sbk = jnp.arange(nsb)[:, None] * SLC_BLOCK
    ovT = ((cs < sbk + SLC_BLOCK) & (cs + CMP_BLOCK > sbk) & (jnp.arange(ncp)[None, :] < ncp - 1)).astype(BF16)

    on = out_norm.reshape(N_HEADS, HEAD_DIM)
    o_nsa = _nsa(qT_nsa, kcvc, vcT, y, vsT, vwT, gT, ovT, _block_onehot(T, SLC_BLOCK),
                 on[:NSA_HEADS].reshape(G, NSA_GROUP, HEAD_DIM), B, T)
    o_moba = _moba(qT_moba, y, vmT, _block_onehot(T, MOBA_BLOCK),
                   on[NSA_HEADS:].reshape(MOBA_HEADS, 1, HEAD_DIM), B, T)

    x1 = _out_proj(o_nsa, o_moba, w_out.astype(BF16), x2, mod3, T)
    return _ffn(x1, mod3, w_ffn_in.astype(BF16), w_ffn_out.astype(BF16), T)


def kernel(x, c, w_ada, b_ada, w_in, nsa_q_norm, nsa_k_norm, moba_q_norm, moba_k_norm, cmp_pe_k, cmp_w1_k, cmp_w2_k, cmp_pe_v, cmp_w1_v, cmp_w2_v, out_norm, w_out, w_ffn_in, w_ffn_out):
    B, T, D = x.shape
    x2 = x.reshape(B * T, D)
    for l in range(w_ada.shape[0]):
        x2 = _layer(x2, c, B, T, w_ada[l], b_ada[l], w_in[l], nsa_q_norm[l], nsa_k_norm[l],
                    moba_q_norm[l], moba_k_norm[l], cmp_pe_k[l], cmp_w1_k[l], cmp_w2_k[l],
                    cmp_pe_v[l], cmp_w1_v[l], cmp_w2_v[l], out_norm[l], w_out[l],
                    w_ffn_in[l], w_ffn_out[l])
    return x2.reshape(B, T, D)
```

```python
import functools

import jax
import jax.numpy as jnp
from jax import lax
from jax.experimental import pallas as pl
from jax.experimental.pallas import tpu as pltpu

F32 = jnp.float32
BF16 = jnp.bfloat16

HEAD_DIM = 128
SUBLANES = 8
NSA_HEADS = 8
NSA_KV_HEADS = 2
NSA_GROUP = NSA_HEADS // NSA_KV_HEADS
MOBA_HEADS = 8
N_HEADS = NSA_HEADS + MOBA_HEADS
CMP_BLOCK = 32
CMP_STRIDE = 16
SLC_BLOCK = 64
SLC_TOPK = 16
WINDOW = 512
FORCE_BONUS = 1e4
MOBA_BLOCK = 256
MOBA_TOPK = 3
ROPE_THETA = 500000.0
ROPE_DIMS = HEAD_DIM // 4
ROPE_HALF = ROPE_DIMS // 2
EPS = 1e-6
NEG = -1e30
M_INIT = -1e29

V7X_VMEM_BYTES = 64 * 1024 * 1024
VMEM_LIMIT = V7X_VMEM_BYTES - 8 * 1024 * 1024

H_NSA_Q = 0
H_KSLC = 8
H_KWIN = 10
H_MOBA_Q = 12
H_MOBA_K = 20
N_TREATED = 28
H_KCMP = 28
H_VCMP = 30
H_VSLC = 32
H_VWIN = 34
H_MOBA_V = 36
N_PROJ_HEADS = 44
HEADS_PER_TILE = 4
GATE_LANES = 128

NSA_TQ = 128
SLC_TK = NSA_TQ
SLC_TKS = 512
WIN_TK = 128
WIN_SPAN = WINDOW + NSA_TQ
MOBA_TQ = 256
MOBA_TK = MOBA_BLOCK
MOBA_TKS = 512
MOBA_HB = 2


def _cparams(sem):
    return pltpu.CompilerParams(dimension_semantics=sem, vmem_limit_bytes=VMEM_LIMIT)


def _split3(a):
    hi = a.astype(BF16)
    r1 = a - hi.astype(F32)
    mid = r1.astype(BF16)
    lo = (r1 - mid.astype(F32)).astype(BF16)
    return hi, mid, lo


def _adaln_kernel(c_ref, w_ref, b_ref, o_ref):
    cv = c_ref[...]
    s = cv * jax.nn.sigmoid(cv)
    w = w_ref[...]
    acc = jnp.zeros(o_ref.shape, F32)
    for part in _split3(s):
        for wpart in _split3(w)[:2]:
            acc = acc + jnp.dot(part, wpart, preferred_element_type=F32)
    o_ref[...] = acc + b_ref[...]


def _adaln(c, w_ada, b_ada):
    B, D = c.shape
    N = w_ada.shape[1]
    tn = 1024
    c8 = jnp.zeros((8, D), F32).at[:B].set(c)
    out = pl.pallas_call(
        _adaln_kernel,
        grid=(N // tn,),
        in_specs=[pl.BlockSpec((8, D), lambda j: (0, 0)),
                  pl.BlockSpec((D, tn), lambda j: (0, j)),
                  pl.BlockSpec((1, tn), lambda j: (0, j))],
        out_specs=pl.BlockSpec((8, tn), lambda j: (0, j)),
        out_shape=jax.ShapeDtypeStruct((8, N), F32),
        compiler_params=_cparams(("arbitrary",)),
        name="adaln",
    )(c8, w_ada, b_ada.reshape(1, N))
    return out[:B].reshape(B, 6, D)


def _rope(y, cc, sa, sb):
    return (y * cc + pltpu.roll(y, HEAD_DIM - ROPE_HALF, 1) * sa + pltpu.roll(y, ROPE_HALF, 1) * sb)


def _inproj_kernel(x_ref, mod_ref, w_ref, wg_ref, gain_ref, cc_ref, sa_ref, sb_ref,
                   y_ref, g_ref, h_scr, *, n_treated_tiles):
    j = pl.program_id(1)

    @pl.when(j == 0)
    def _():
        x = x_ref[...]
        ms = jnp.mean(x * x, axis=-1, keepdims=True)
        h = x * lax.rsqrt(ms + EPS) * (1.0 + mod_ref[0, 1:2, :]) + mod_ref[0, 0:1, :]
        hb = h.astype(BF16)
        h_scr[...] = hb
        g_ref[...] = jax.nn.sigmoid(jnp.dot(hb, wg_ref[...], preferred_element_type=F32))

    acc = jnp.dot(h_scr[...], w_ref[...], preferred_element_type=F32)

    @pl.when(j < n_treated_tiles)
    def _():
        cc, sa, sb = cc_ref[...], sa_ref[...], sb_ref[...]
        for hh in range(HEADS_PER_TILE):
            yh = acc[:, hh * HEAD_DIM:(hh + 1) * HEAD_DIM]
            ms = jnp.mean(yh * yh, axis=-1, keepdims=True)
            yn = yh * lax.rsqrt(ms + EPS) * gain_ref[0, hh:hh + 1, :]
            y_ref[hh] = _rope(yn, cc, sa, sb).astype(BF16)

    @pl.when(j >= n_treated_tiles)
    def _():
        for hh in range(HEADS_PER_TILE):
            y_ref[hh] = acc[:, hh * HEAD_DIM:(hh + 1) * HEAD_DIM].astype(BF16)


def _in_proj(x2, mod3, w_main, w_gate, gains, cc, sa, sb, T):
    BT, D = x2.shape
    tm = min(1024, T)
    tn = HEADS_PER_TILE * HEAD_DIM
    n_tiles = N_PROJ_HEADS // HEADS_PER_TILE
    tpb = T // tm
    kern = functools.partial(_inproj_kernel, n_treated_tiles=N_TREATED // HEADS_PER_TILE)
    y, g = pl.pallas_call(
        kern,
        grid=(BT // tm, n_tiles),
        in_specs=[pl.BlockSpec((tm, D), lambda i, j: (i, 0)),
                  pl.BlockSpec((1, 6, D), lambda i, j: (i // tpb, 0, 0)),
                  pl.BlockSpec((D, tn), lambda i, j: (0, j)),
                  pl.BlockSpec((D, NSA_KV_HEADS * GATE_LANES), lambda i, j: (0, 0)),
                  pl.BlockSpec((1, HEADS_PER_TILE, HEAD_DIM), lambda i, j: (j, 0, 0)),
                  pl.BlockSpec((tm, HEAD_DIM), lambda i, j: (i % tpb, 0)),
                  pl.BlockSpec((tm, HEAD_DIM), lambda i, j: (i % tpb, 0)),
                  pl.BlockSpec((tm, HEAD_DIM), lambda i, j: (i % tpb, 0))],
        out_specs=[pl.BlockSpec((HEADS_PER_TILE, tm, HEAD_DIM), lambda i, j: (j, i, 0)),
                   pl.BlockSpec((tm, NSA_KV_HEADS * GATE_LANES), lambda i, j: (i, 0))],
        out_shape=[jax.ShapeDtypeStruct((N_PROJ_HEADS, BT, HEAD_DIM), BF16),
                   jax.ShapeDtypeStruct((BT, NSA_KV_HEADS * GATE_LANES), F32)],
        scratch_shapes=[pltpu.VMEM((tm, D), BF16)],
        compiler_params=_cparams(("parallel", "arbitrary")),
        name="in_proj",
    )(x2, mod3, w_main, w_gate, gains, cc, sa, sb)
    return y, g


def _compress_kernel(h_ref, w1c_ref, pe_ref, w1_ref, w2_ref, gain_ref, cc_ref, sa_ref, sb_ref, o_ref):
    a = pl.program_id(0)
    ncp = h_ref.shape[2]
    z = jnp.dot(h_ref[0, 0], w1c_ref[0], preferred_element_type=F32)
    top = z[:, :HEAD_DIM]
    bot = pltpu.roll(z[:, HEAD_DIM:], ncp - 1, 0)
    pe_term = jnp.dot(pe_ref[0], w1_ref[0], preferred_element_type=F32)[0:1, :]
    pre = top + bot + pe_term
    act = pre * jax.nn.sigmoid(pre)
    out = jnp.dot(act.astype(BF16), w2_ref[0], preferred_element_type=F32)
    live = lax.broadcasted_iota(jnp.int32, out.shape, 0) < ncp - 1
    out = jnp.where(live, out, 0.0)

    @pl.when(a < NSA_KV_HEADS)
    def _():
        ms = jnp.mean(out * out, axis=-1, keepdims=True)
        yn = out * lax.rsqrt(ms + EPS) * gain_ref[...]
        o_ref[0, 0] = _rope(yn, cc_ref[...], sa_ref[...], sb_ref[...]).astype(BF16)

    @pl.when(a >= NSA_KV_HEADS)
    def _():
        o_ref[0, 0] = out.astype(BF16)


def _compress(hc, w1cat, pe8, w1, w2, gain, cc, sa, sb):
    A, B, ncp, K = hc.shape
    G = NSA_KV_HEADS
    return pl.pallas_call(
        _compress_kernel,
        grid=(A, B),
        in_specs=[pl.BlockSpec((1, 1, ncp, K), lambda a, b: (a, b, 0, 0)),
                  pl.BlockSpec((1, K, 2 * HEAD_DIM), lambda a, b: (a // G, 0, 0)),
                  pl.BlockSpec((1, 8, 2 * K), lambda a, b: (a // G, 0, 0)),
                  pl.BlockSpec((1, 2 * K, HEAD_DIM), lambda a, b: (a // G, 0, 0)),
                  pl.BlockSpec((1, HEAD_DIM, HEAD_DIM), lambda a, b: (a // G, 0, 0)),
                  pl.BlockSpec((1, HEAD_DIM), lambda a, b: (0, 0)),
                  pl.BlockSpec((ncp, HEAD_DIM), lambda a, b: (0, 0)),
                  pl.BlockSpec((ncp, HEAD_DIM), lambda a, b: (0, 0)),
                  pl.BlockSpec((ncp, HEAD_DIM), lambda a, b: (0, 0))],
        out_specs=pl.BlockSpec((1, 1, ncp, HEAD_DIM), lambda a, b: (a, b, 0, 0)),
        out_shape=jax.ShapeDtypeStruct((A, B, ncp, HEAD_DIM), BF16),
        compiler_params=_cparams(("arbitrary", "arbitrary")),
        name="compress",
    )(hc, w1cat, pe8, w1, w2, gain, cc, sa, sb)


def _reset(m_scr, l_scr, acc_scr):
    m_scr[...] = jnp.full(m_scr.shape, M_INIT, F32)
    l_scr[...] = jnp.zeros(l_scr.shape, F32)
    acc_scr[...] = jnp.zeros(acc_scr.shape, F32)


def _online_step(s, vT, m_scr, l_scr, acc_scr):
    m_prev = m_scr[...]
    m_new = jnp.maximum(m_prev, jnp.max(s, axis=0, keepdims=True))
    alpha = jnp.exp(m_prev - m_new)
    p = jnp.exp(s - m_new)
    l_scr[...] = alpha * l_scr[...] + jnp.sum(p, axis=0, keepdims=True)
    acc_scr[...] = alpha * acc_scr[...] + jnp.dot(vT, p.astype(BF16), preferred_element_type=F32)
    m_scr[...] = m_new


def _rank_select(score, n_idx, k):
    n = score.shape[0]
    rank = jnp.zeros(score.shape, F32)
    for m in range(n):
        sm = score[m:m + 1, :]
        lo = (m // SUBLANES) * SUBLANES
        hi = min(lo + SUBLANES, n)
        parts = []
        if lo > 0:
            parts.append(jnp.where(sm > score[:lo], 1.0, 0.0))
        gt =jnp.where(sm > score[lo:hi], 1.0, 0.0)
        ge = jnp.where(sm >= score[lo:hi], 1.0, 0.0)
        below = lax.broadcasted_iota(jnp.int32, gt.shape, 0) > m - lo
        parts.append(jnp.where(below, ge, gt))
        if hi < n:
            parts.append(jnp.where(sm >= score[hi:], 1.0, 0.0))
        rank = rank + (jnp.concatenate(parts, axis=0) if len(parts) > 1 else parts[0])
    return rank < k


def _pad_rows(a, rows):
    return jnp.concatenate([a, jnp.zeros((rows - a.shape[0], a.shape[1]), a.dtype)], axis=0)


def _masked_scores(k_ref, e_ref, start, size, q_aug):
    rows = pl.ds(pl.multiple_of(start, 128), size)
    k_aug = jnp.concatenate([k_ref[rows, :], e_ref[rows, :]], axis=1)
    return jnp.dot(k_aug, q_aug, preferred_element_type=F32)


def _pipelined_attention(chains, n_steps, n_max, tks, tk):
    per = tks // tk

    def scores(step, buf):
        st = jnp.minimum(step, n_max - 1)
        dead = jnp.where(step < n_steps, 0, 1)
        for k_ref, e_ref, _, qaug_ref, bufs, _, _, _ in chains:
            bufs[buf][...] = _masked_scores(k_ref, e_ref, st * tks, tks, qaug_ref[dead])

    def consume(step, buf):
        st = jnp.minimum(step, n_max - 1)
        for _, _, vT_ref, _, bufs, m_scr, l_scr, acc_scr in chains:
            vT = jnp.concatenate([vT_ref[st * per + a] for a in range(per)], axis=1)
            _online_step(bufs[buf][...], vT, m_scr, l_scr, acc_scr)

    def pair(p, carry):
        scores(2 * p + 1, 1)
        consume(2 * p, 0)
        scores(2 * p + 2, 0)
        consume(2 * p + 1, 1)
        return carry

    n_pairs = (n_steps + 1) // 2

    @pl.when(n_pairs > 0)
    def _():
        scores(0, 0)

    lax.fori_loop(0, n_pairs, pair, 0)


def _nsa_kernel(qT_ref, kc_ref, vcT_ref, ks_ref, vsT_ref, kw_ref, vwT_ref, gT_ref, ovT_ref, e_ref,
                gain_ref, o_ref, m_scr, l_scr, acc_scr, comb_scr, qaug_scr, s0_scr, s1_scr, *, T):
    R, tq = NSA_GROUP, NSA_TQ
    i = pl.program_id(2)
    t0 = i * tq
    ncp = T // CMP_STRIDE
    nsb = T // SLC_BLOCK
    qT = jnp.concatenate([qT_ref[r] for r in range(R)], axis=1)
    gT = gT_ref[0]

    def per_head(row0):
        return jnp.concatenate([gT[row0 + 3 * r:row0 + 3 * r + 1, :] for r in range(R)], axis=1)

    def lanes_x_heads(a):
        return jnp.concatenate([a] * R, axis=1)

    s = jnp.dot(kc_ref[0, 0], qT, preferred_element_type=F32)
    c_idx = lax.broadcasted_iota(jnp.int32, (ncp, tq), 0)
    t_c = t0 + lax.broadcasted_iota(jnp.int32, (ncp, tq), 1)
    vis = lanes_x_heads((c_idx * CMP_STRIDE + (CMP_BLOCK - 1) <= t_c) & (c_idx < ncp - 1))
    s = jnp.where(vis, s, NEG)
    m = jnp.max(s, axis=0, keepdims=True)
    p = jnp.where(vis, jnp.exp(s - m), 0.0)
    l = jnp.sum(p, axis=0, keepdims=True)
    p = p / jnp.where(l > 0.0, l, 1.0)
    o_cmp = jnp.dot(vcT_ref[0, 0], p.astype(BF16), preferred_element_type=F32)
    comb_scr[...] = o_cmp * per_head(0)

    psum = p[:, 0:tq]
    for r in range(1, R):
        psum = psum + p[:, r * tq:(r + 1) * tq]
    ov = ovT_ref[...]
    imp = jnp.zeros((nsb, tq), F32)
    for part in _split3(psum):
        imp = imp + jnp.dot(ov, part, preferred_element_type=F32)
    n_idx = lax.broadcasted_iota(jnp.int32, (nsb, tq), 0)
    cur = (t0 + lax.broadcasted_iota(jnp.int32, (nsb, tq), 1)) // SLC_BLOCK
    forced = (n_idx == 0) | (n_idx == cur) | (n_idx == cur - 1)
    valid = n_idx <= cur
    score = jnp.where(valid, imp + jnp.where(forced, FORCE_BONUS, 0.0), NEG)
    sel = _rank_select(score, n_idx, min(SLC_TOPK, nsb)) & valid
    sel_past = sel & (n_idx * SLC_BLOCK < t0)
    bias = _pad_rows(jnp.where(sel_past, 0.0, NEG), HEAD_DIM).astype(BF16)
    for d in range(2):
        qaug_scr[d, 0:HEAD_DIM, :] = qT
    qaug_scr[0, HEAD_DIM:, :] = lanes_x_heads(bias)
    qaug_scr[1, HEAD_DIM:, :] = jnp.full((HEAD_DIM, R * tq), NEG, BF16)

    w0 = jnp.maximum(t0 + tq - WIN_SPAN, 0)
    k_w = kw_ref[0, pl.ds(pl.multiple_of(w0, WIN_TK), WIN_SPAN), :]
    s_w = jnp.dot(k_w, qT, preferred_element_type=F32)
    kpos = w0 + lax.broadcasted_iota(jnp.int32, (WIN_SPAN, tq), 0)
    tpos = t0 + lax.broadcasted_iota(jnp.int32, (WIN_SPAN, tq), 1)
    ok = lanes_x_heads((kpos <= tpos) & (tpos - kpos < WINDOW))
    s_w = jnp.where(ok, s_w, NEG)
    m_w = jnp.max(s_w, axis=0, keepdims=True)
    p_w = jnp.exp(s_w - m_w)
    l_w = jnp.sum(p_w, axis=0, keepdims=True)
    jw = w0 // WIN_TK
    v_w = jnp.concatenate([vwT_ref[0, jw + a] for a in range(WIN_SPAN // WIN_TK)], axis=1)
    o_w = jnp.dot(v_w, p_w.astype(BF16), preferred_element_type=F32)
    comb_scr[...] += o_w / l_w * per_head(2)

    k_d = ks_ref[0, pl.ds(pl.multiple_of(t0, tq), tq), :]
    tri = lax.broadcasted_iota(jnp.int32, (tq, tq), 0) <= lax.broadcasted_iota(jnp.int32, (tq, tq), 1)
    s_d = jnp.where(lanes_x_heads(tri), jnp.dot(k_d, qT, preferred_element_type=F32), NEG)
    _reset(m_scr, l_scr, acc_scr)
    n_steps = (t0 + SLC_TKS - 1) // SLC_TKS
    _pipelined_attention([(ks_ref.at[0], e_ref, vsT_ref.at[0], qaug_scr, (s0_scr, s1_scr),
                           m_scr, l_scr, acc_scr)], n_steps, T // SLC_TKS, SLC_TKS, SLC_TK)
    _online_step(s_d, vsT_ref[0, i], m_scr, l_scr, acc_scr)
    comb = comb_scr[...] + acc_scr[...] / l_scr[...] * per_head(1)

    for r in range(R):
        oT = comb[:, r * tq:(r + 1) * tq]
        ms = jnp.mean(oT * oT, axis=0, keepdims=True)
        on = (oT * lax.rsqrt(ms + EPS)).T * gain_ref[0, r:r + 1, :]
        o_ref[:, r * HEAD_DIM:(r + 1) * HEAD_DIM] = on.astype(BF16)


def _nsa(qT, kcvc, vcT, y, vsT, vwT, gT, ovT, e_slc, gains, B, T):
    G, R, tq = NSA_KV_HEADS, NSA_GROUP, NSA_TQ
    nq = T // tq
    ncp = T // CMP_STRIDE
    nsb = T // SLC_BLOCK
    kern = functools.partial(_nsa_kernel, T=T)
    return pl.pallas_call(
        kern,
        grid=(B, G, nq),
        in_specs=[pl.BlockSpec((R, HEAD_DIM, tq), lambda b, g, i: (g, 0, b * nq + i)),
                  pl.BlockSpec((1, 1, ncp, HEAD_DIM), lambda b, g, i: (g, b, 0, 0)),
                  pl.BlockSpec((1, 1, HEAD_DIM, ncp), lambda b, g, i: (g, b, 0, 0)),
                  pl.BlockSpec((1, T, HEAD_DIM), lambda b, g, i: (H_KSLC + g, b, 0)),
                  pl.BlockSpec((1, T // SLC_TK, HEAD_DIM, SLC_TK), lambda b, g, i: (g * B + b, 0, 0, 0)),
                  pl.BlockSpec((1, T, HEAD_DIM), lambda b, g, i: (H_KWIN + g, b, 0)),
                  pl.BlockSpec((1, T // WIN_TK, HEAD_DIM, WIN_TK), lambda b, g, i: (g * B + b, 0, 0, 0)),
                  pl.BlockSpec((1, 16, tq), lambda b, g, i: (g, 0, b * nq + i)),
                  pl.BlockSpec((nsb, ncp), lambda b, g, i: (0, 0)),
                  pl.BlockSpec((T, HEAD_DIM), lambda b, g, i: (0, 0)),
                  pl.BlockSpec((1, R, HEAD_DIM), lambda b, g, i: (g, 0, 0))],
        out_specs=pl.BlockSpec((tq, R * HEAD_DIM), lambda b, g, i: (b * nq + i, g)),
        out_shape=jax.ShapeDtypeStruct((B * T, NSA_HEADS * HEAD_DIM), BF16),
        scratch_shapes=[pltpu.VMEM((1, R * tq), F32), pltpu.VMEM((1, R * tq), F32),
                        pltpu.VMEM((HEAD_DIM, R * tq), F32), pltpu.VMEM((HEAD_DIM, R * tq), F32),
                        pltpu.VMEM((2, 2 * HEAD_DIM, R * tq), BF16),
                        pltpu.VMEM((SLC_TKS, R * tq), F32), pltpu.VMEM((SLC_TKS, R * tq), F32)],
        compiler_params=_cparams(("parallel", "parallel", "arbitrary")),
        name="nsa",
    )(qT, kcvc, vcT, y, vsT, y, vwT, gT, ovT, e_slc, gains)


def _moba_kernel(qT_ref, k_ref, vT_ref, e_ref, gain_ref, o_ref, m_scr, l_scr, acc_scr, kmean_scr, qaug_scr,
                 s_scr, *, T):
    tq, tk = MOBA_TQ, MOBA_TK
    nb = T // MOBA_BLOCK
    nbp = kmean_scr.shape[1]
    i = pl.program_id(2)
    t0 = i * tq

    @pl.when(i == 0)
    def _():
        kmean_scr[...] = jnp.zeros(kmean_scr.shape, F32)
        for a in range(MOBA_HB):
            kb = k_ref[a].astype(F32).reshape(nb, MOBA_BLOCK, HEAD_DIM)
            kmean_scr[a, 0:nb, :] = jnp.mean(kb, axis=1)

    n_idx = lax.broadcasted_iota(jnp.int32, (nbp, tq), 0)
    cur = (t0 + lax.broadcasted_iota(jnp.int32, (nbp, tq), 1)) // MOBA_BLOCK
    past = n_idx < cur
    causal = lax.broadcasted_iota(jnp.int32, (tk, tq), 0) <= lax.broadcasted_iota(jnp.int32, (tk, tq), 1)

    for a in range(MOBA_HB):
        qT = qT_ref[a]
        gate = jnp.zeros((nbp, tq), F32)
        for part in _split3(kmean_scr[a]):
            gate = gate + jnp.dot(part, qT, preferred_element_type=F32)
        sel = _rank_select(jnp.where(past, gate, NEG), n_idx, min(MOBA_TOPK, nb)) & past
        for d in range(2):
            qaug_scr[a, d, 0:HEAD_DIM, :] = qT
        qaug_scr[a, 0, HEAD_DIM:, :] = _pad_rows(jnp.where(sel, 0.0, NEG), HEAD_DIM).astype(BF16)
        qaug_scr[a, 1, HEAD_DIM:, :] = jnp.full((HEAD_DIM, tq), NEG, BF16)

    s_own = []
    for a in range(MOBA_HB):
        _reset(m_scr.at[a], l_scr.at[a], acc_scr.at[a])
        k_own = k_ref[a, pl.ds(pl.multiple_of(t0, tk), tk), :]
        s_own.append(jnp.where(causal, jnp.dot(k_own, qT_ref[a], preferred_element_type=F32), NEG))

    chains = [(k_ref.at[a], e_ref, vT_ref.at[a, 0], qaug_scr.at[a], (s_scr.at[a, 0], s_scr.at[a, 1]),
               m_scr.at[a], l_scr.at[a], acc_scr.at[a]) for a in range(MOBA_HB)]
    n_steps = (t0 + MOBA_TKS - 1) // MOBA_TKS
    _pipelined_attention(chains, n_steps, T // MOBA_TKS, MOBA_TKS, tk)

    for a in range(MOBA_HB):
        _online_step(s_own[a], vT_ref[a, 0, i], m_scr.at[a], l_scr.at[a], acc_scr.at[a])
        oT = acc_scr[a] / l_scr[a]
        ms = jnp.mean(oT * oT, axis=0, keepdims=True)
        on = (oT * lax.rsqrt(ms + EPS)).T * gain_ref[a]
        o_ref[:, a * HEAD_DIM:(a + 1) * HEAD_DIM] = on.astype(BF16)


def _moba(qT, y, vT, e_moba, gains, B, T):
    H, HB, tq, tk = MOBA_HEADS, MOBA_HB, MOBA_TQ, MOBA_TK
    nq = T // tq
    nb = T // MOBA_BLOCK
    nbp = max(16, nb)
    kern = functools.partial(_moba_kernel, T=T)
    return pl.pallas_call(
        kern,
        grid=(B, H // HB, nq),
        in_specs=[pl.BlockSpec((HB, HEAD_DIM, tq), lambda b, h, i: (h, 0, b * nq + i)),
                  pl.BlockSpec((HB, T, HEAD_DIM), lambda b, h, i: (H_MOBA_K // HB + h, b, 0)),
                  pl.BlockSpec((HB, 1, T // tk, HEAD_DIM, tk), lambda b, h, i: (h, b, 0, 0, 0)),
                  pl.BlockSpec((T, HEAD_DIM), lambda b, h, i: (0, 0)),
                  pl.BlockSpec((HB, 1, HEAD_DIM), lambda b, h, i: (h, 0, 0))],
        out_specs=pl.BlockSpec((tq, HB * HEAD_DIM), lambda b, h, i: (b * nq + i, h)),
        out_shape=jax.ShapeDtypeStruct((B * T, H * HEAD_DIM), BF16),
        scratch_shapes=[pltpu.VMEM((HB, 1, tq), F32), pltpu.VMEM((HB, 1, tq), F32),
                        pltpu.VMEM((HB, HEAD_DIM, tq), F32), pltpu.VMEM((HB, nbp, HEAD_DIM), F32),
                        pltpu.VMEM((HB, 2, 2 * HEAD_DIM, tq), BF16),
                        pltpu.VMEM((HB, 2, MOBA_TKS, tq), F32)],
        compiler_params=_cparams(("parallel", "parallel", "arbitrary")),
        name="moba",
    )(qT, y, vT, e_moba, gains)


def _outproj_kernel(on_ref, om_ref, w_ref, x_ref, mod_ref, o_ref):
    half = on_ref.shape[1]
    acc = jnp.dot(on_ref[...], w_ref[0:half, :], preferred_element_type=F32)
    acc = acc + jnp.dot(om_ref[...], w_ref[half:, :], preferred_element_type=F32)
    o_ref[...] = x_ref[...] + mod_ref[0, 2:3, :] * acc


def _out_proj(o_nsa, o_moba, w_out, x2, mod3, T):
    BT, D = x2.shape
    tm = min(512, T)
    tpb = T // tm
    half = o_nsa.shape[1]
    return pl.pallas_call(
        _outproj_kernel,
        grid=(BT // tm,),
        in_specs=[pl.BlockSpec((tm, half), lambda i: (i, 0)),
                  pl.BlockSpec((tm, half), lambda i: (i, 0)),
                  pl.BlockSpec((D, D), lambda i: (0, 0)),
                  pl.BlockSpec((tm, D), lambda i: (i, 0)),
                  pl.BlockSpec((1, 6, D), lambda i: (i // tpb, 0, 0))],
        out_specs=pl.BlockSpec((tm, D), lambda i: (i, 0)),
        out_shape=jax.ShapeDtypeStruct((BT, D), F32),
        compiler_params=_cparams(("parallel",)),
        name="out_proj",
    )(o_nsa, o_moba, w_out, x2, mod3)


def _ffn_kernel(x_ref, mod_ref, wg_ref, wu_ref, wo_ref, o_ref, h_scr):
    j = pl.program_id(1)

    @pl.when(j == 0)
    def _():
        x = x_ref[...]
        ms = jnp.mean(x * x, axis=-1, keepdims=True)
        h = x * lax.rsqrt(ms + EPS) * (1.0 + mod_ref[0, 4:5, :]) + mod_ref[0, 3:4, :]
        h_scr[...] = h.astype(BF16)
        o_ref[...] = jnp.zeros(o_ref.shape, F32)

    hb = h_scr[...]
    gate = jnp.dot(hb, wg_ref[...], preferred_element_type=F32)
    up = jnp.dot(hb, wu_ref[...], preferred_element_type=F32)
    act = (gate * jax.nn.sigmoid(gate) * up).astype(BF16)
    o_ref[...] += jnp.dot(act, wo_ref[...], preferred_element_type=F32)

    @pl.when(j == pl.num_programs(1) - 1)
    def _():
        o_ref[...] = x_ref[...] + mod_ref[0, 5:6, :] * o_ref[...]


def _ffn(x1, mod3, w_in, w_out, T):
    BT, D = x1.shape
    Fh = w_out.shape[0]
    tm = min(512, T)
    tf = 512
    tpb = T // tm
    nf = Fh // tf
    return pl.pallas_call(
        _ffn_kernel,
        grid=(BT // tm, nf),
        in_specs=[pl.BlockSpec((tm, D), lambda i, j: (i, 0)),
                  pl.BlockSpec((1, 6, D), lambda i, j: (i // tpb, 0, 0)),
                  pl.BlockSpec((D, tf), lambda i, j: (0, j)),
                  pl.BlockSpec((D, tf), lambda i, j: (0, nf + j)),
                  pl.BlockSpec((tf, D), lambda i, j: (j, 0))],
        out_specs=pl.BlockSpec((tm, D), lambda i, j: (i, 0)),
        out_shape=jax.ShapeDtypeStruct((BT, D), F32),
        scratch_shapes=[pltpu.VMEM((tm, D), BF16)],
        compiler_params=_cparams(("parallel", "arbitrary")),
        name="ffn",
    )(x1, mod3, w_in, w_in, w_out)


def _rope_lane_tables(T):
    inv = ROPE_THETA ** (-jnp.arange(0, ROPE_DIMS, 2, dtype=F32) / ROPE_DIMS)
    ang = jnp.arange(T).astype(F32)[:, None] * inv[None, :]
    cos, sin = jnp.cos(ang), jnp.sin(ang)
    rest = HEAD_DIM - ROPE_DIMS
    cc = jnp.concatenate([cos, cos, jnp.ones((T, rest), F32)], axis=1)
    sa = jnp.concatenate([-sin, jnp.zeros((T, HEAD_DIM - ROPE_HALF), F32)], axis=1)
    sb = jnp.concatenate([jnp.zeros((T, ROPE_HALF), F32), sin, jnp.zeros((T, rest), F32)], axis=1)
    return cc, sa, sb


def _tile_transposed(v, B, T, tk):
    n = v.shape[0]
    return v.reshape(n, B, T // tk, tk, HEAD_DIM).transpose(0, 1, 2, 4, 3)


def _block_onehot(T, block):
    return (jnp.arange(T)[:, None] // block == jnp.arange(HEAD_DIM)[None, :]).astype(BF16)


def _layer(x2, c, B, T, w_ada, b_ada, w_in, nsa_q_norm, nsa_k_norm, moba_q_norm, moba_k_norm,
           cmp_pe_k, cmp_w1_k, cmp_w2_k, cmp_pe_v, cmp_w1_v, cmp_w2_v, out_norm, w_out,
           w_ffn_in, w_ffn_out):
    D = x2.shape[1]
    G = NSA_KV_HEADS
    scale = HEAD_DIM ** -0.5
    assert T % MOBA_BLOCK == 0 and T % SLC_TK == 0 and T % NSA_TQ == 0 and T >= WIN_SPAN
    assert T // SLC_BLOCK <= HEAD_DIM and T // MOBA_BLOCK <= HEAD_DIM

    mod3 = _adaln(c, w_ada, b_ada)

    qw, kvw, gw, mw = NSA_HEADS * HEAD_DIM, G * HEAD_DIM, NSA_HEADS * 3, MOBA_HEADS * HEAD_DIM
    offs = [0, qw] + [qw + kvw * n for n in range(1, 7)]
    col = lambda a, n: w_in[:, a:a + n]
    o_q, o_kc, o_vc, o_ks, o_vs, o_kw, o_vw, o_g = offs
    o_mq = o_g + gw
    w_main = jnp.concatenate(
        [col(o_q, qw), col(o_ks, kvw), col(o_kw, kvw), col(o_mq, mw), col(o_mq + mw, mw),
         col(o_kc, kvw), col(o_vc, kvw), col(o_vs, kvw), col(o_vw, kvw), col(o_mq + 2 * mw, mw)],
        axis=1).astype(BF16)
    wg = col(o_g, gw).reshape(D, G, NSA_GROUP * 3)
    w_gate = jnp.pad(wg, ((0, 0), (0, 0), (0, GATE_LANES - NSA_GROUP * 3))).reshape(D, G * GATE_LANES)
    w_gate = w_gate.astype(BF16)
    rep = lambda g_, n: jnp.broadcast_to(g_, (n, HEAD_DIM))
    gains = jnp.concatenate(
        [rep(nsa_q_norm * scale, NSA_HEADS), rep(nsa_k_norm[1], G), rep(nsa_k_norm[2], G),
         rep(moba_q_norm * scale, MOBA_HEADS), rep(moba_k_norm, MOBA_HEADS),
         jnp.ones((N_PROJ_HEADS - N_TREATED, HEAD_DIM), F32)], axis=0)
    gains = gains.reshape(N_PROJ_HEADS // HEADS_PER_TILE, HEADS_PER_TILE, HEAD_DIM)
    cc, sa, sb = _rope_lane_tables(T)

    y, gates = _in_proj(x2, mod3, w_main, w_gate, gains, cc, sa, sb, T)

    qT_nsa = y[H_NSA_Q:H_NSA_Q + NSA_HEADS].transpose(0, 2, 1)
    qT_moba = y[H_MOBA_Q:H_MOBA_Q + MOBA_HEADS].transpose(0, 2, 1)
    vsT = _tile_transposed(y[H_VSLC:H_VSLC + G], B, T, SLC_TK).reshape(G * B, T // SLC_TK, HEAD_DIM, SLC_TK)
    vwT = _tile_transposed(y[H_VWIN:H_VWIN + G], B, T, WIN_TK).reshape(G * B, T // WIN_TK, HEAD_DIM, WIN_TK)
    vmT = _tile_transposed(y[H_MOBA_V:H_MOBA_V + MOBA_HEADS], B, T, MOBA_TK)
    gT = gates.reshape(B * T, G, GATE_LANES)[:, :, :16].transpose(1, 2, 0)

    ncp = T // CMP_STRIDE
    half = CMP_STRIDE * HEAD_DIM
    hc = y[H_KCMP:H_KCMP + 2 * G].reshape(2 * G, B, ncp, half)
    w1 = jnp.stack([cmp_w1_k, cmp_w1_v]).astype(BF16)
    w1cat = jnp.concatenate([w1[:, :half], w1[:, half:]], axis=2)
    pe8 = jnp.broadcast_to(jnp.stack([cmp_pe_k, cmp_pe_v]).reshape(2, 1, 2 * half), (2, 8, 2 * half))
    w2 = jnp.stack([cmp_w2_k, cmp_w2_v]).astype(BF16)
    cmp_rows = lambda t: jnp.pad(t[CMP_BLOCK - 1::CMP_STRIDE], ((0, 1), (0, 0)))
    kcvc = _compress(hc, w1cat, pe8.astype(BF16), w1, w2, nsa_k_norm[0].reshape(1, HEAD_DIM),
                     cmp_rows(cc), cmp_rows(sa), cmp_rows(sb))
    vcT = kcvc[G:].transpose(0, 1, 3, 2)

    nsb = T // SLC_BLOCK
    cs = jnp.arange(ncp)[None, :] * CMP_STRIDE
    sbk = jnp.arange(nsb)[:, None] * SLC_BLOCK
    ovT = ((cs < sbk + SLC_BLOCK) & (cs + CMP_BLOCK > sbk) & (jnp.arange(ncp)[None, :] < ncp - 1)).astype(BF16)

    on = out_norm.reshape(N_HEADS, HEAD_DIM)
    o_nsa = _nsa(qT_nsa, kcvc, vcT, y, vsT, vwT, gT, ovT, _block_onehot(T, SLC_BLOCK),
                 on[:NSA_HEADS].reshape(G, NSA_GROUP, HEAD_DIM), B, T)
    o_moba = _moba(qT_moba, y, vmT, _block_onehot(T, MOBA_BLOCK),
                   on[NSA_HEADS:].reshape(MOBA_HEADS, 1, HEAD_DIM), B, T)

    x1 = _out_proj(o_nsa, o_moba, w_out.astype(BF16), x2, mod3, T)
    return _ffn(x1, mod3, w_ffn_in.astype(BF16), w_ffn_out.astype(BF16), T)


def kernel(x, c, w_ada, b_ada, w_in, nsa_q_norm, nsa_k_norm, moba_q_norm, moba_k_norm, cmp_pe_k, cmp_w1_k, cmp_w2_k, cmp_pe_v, cmp_w1_v, cmp_w2_v, out_norm, w_out, w_ffn_in, w_ffn_out):
    B, T, D = x.shape
    x2 = x.reshape(B * T, D)
    for l in range(w_ada.shape[0]):
        x2 = _layer(x2, c, B, T, w_ada[l], b_ada[l], w_in[l], nsa_q_norm[l], nsa_k_norm[l],
                    moba_q_norm[l], moba_k_norm[l], cmp_pe_k[l], cmp_w1_k[l], cmp_w2_k[l],
                    cmp_pe_v[l], cmp_w1_v[l], cmp_w2_v[l], out_norm[l], w_out[l],
                    w_ffn_in[l], w_ffn_out[l])
    return x2.reshape(B, T, D)
```

```python
import functools

import jax
import jax.numpy as jnp
from jax import lax
from jax.experimental import pallas as pl
from jax.experimental.pallas import tpu as pltpu

F32 = jnp.float32
BF16 = jnp.bfloat16

HEAD_DIM = 128
SUBLANES = 8
NSA_HEADS = 8
NSA_KV_HEADS = 2
NSA_GROUP = NSA_HEADS // NSA_KV_HEADS
MOBA_HEADS = 8
N_HEADS = NSA_HEADS + MOBA_HEADS
CMP_BLOCK = 32
CMP_STRIDE = 16
SLC_BLOCK = 64
SLC_TOPK = 16
WINDOW = 512
FORCE_BONUS = 1e4
MOBA_BLOCK = 256
MOBA_TOPK = 3
ROPE_THETA = 500000.0
ROPE_DIMS = HEAD_DIM // 4
ROPE_HALF = ROPE_DIMS // 2
EPS = 1e-6
LOG2E = 1.4426950408889634
ONES_ROWS = 16
NEG = -1e30
M_INIT = -1e29

V7X_VMEM_BYTES = 64 * 1024 * 1024
VMEM_LIMIT = V7X_VMEM_BYTES - 8 * 1024 * 1024

H_NSA_Q = 0
H_KSLC = 8
H_KWIN = 10
H_MOBA_Q = 12
H_MOBA_K = 20
N_TREATED = 28
H_KCMP = 28
H_VCMP = 30
H_VSLC = 32
H_VWIN = 34
H_MOBA_V = 36
N_PROJ_HEADS = 44
HEADS_PER_TILE = 4
GATE_LANES = 128

NSA_TQ = 256
SLC_TK = 128
SLC_TKS = 512
WIN_TK = 128
WIN_SPAN = WINDOW + NSA_TQ
MOBA_TQ = 256
MOBA_TK = MOBA_BLOCK
MOBA_TKS = 512
MOBA_HB = 4


def _cparams(sem):
    return pltpu.CompilerParams(dimension_semantics=sem, vmem_limit_bytes=VMEM_LIMIT)


def _split3(a):
    hi = a.astype(BF16)
    r1 = a - hi.astype(F32)
    mid = r1.astype(BF16)
    lo = (r1 - mid.astype(F32)).astype(BF16)
    return hi, mid, lo


def _adaln_kernel(c_ref, w_ref, b_ref, o_ref):
    cv = c_ref[...]
    s = cv * jax.nn.sigmoid(cv)
    w = w_ref[...]
    acc = jnp.zeros(o_ref.shape, F32)
    for part in _split3(s):
        for wpart in _split3(w)[:2]:
            acc = acc + jnp.dot(part, wpart, preferred_element_type=F32)
    o_ref[...] = acc + b_ref[...]


def _adaln(c, w_ada, b_ada):
    B, D = c.shape
    N = w_ada.shape[1]
    tn = 1024
    c8 = jnp.zeros((8, D), F32).at[:B].set(c)
    out = pl.pallas_call(
        _adaln_kernel,
        grid=(N // tn,),
        in_specs=[pl.BlockSpec((8, D), lambda j: (0, 0)),
                  pl.BlockSpec((D, tn), lambda j: (0, j)),
                  pl.BlockSpec((1, tn), lambda j: (0, j))],
        out_specs=pl.BlockSpec((8, tn), lambda j: (0, j)),
        out_shape=jax.ShapeDtypeStruct((8, N), F32),
        compiler_params=_cparams(("arbitrary",)),
        name="adaln",
    )(c8, w_ada, b_ada.reshape(1, N))
    return out[:B].reshape(B, 6, D)


def _rope(y, cc, sa, sb):
    return (y * cc + pltpu.roll(y, HEAD_DIM - ROPE_HALF, 1) * sa + pltpu.roll(y, ROPE_HALF, 1) * sb)


def _inproj_kernel(x_ref, mod_ref, w_ref, wg_ref, gain_ref, cc_ref, sa_ref, sb_ref,
                   y_ref, g_ref, h_scr, *, n_treated_tiles):
    j = pl.program_id(1)

    @pl.when(j == 0)
    def _():
        x = x_ref[...]
        ms = jnp.mean(x * x, axis=-1, keepdims=True)
        h = x * lax.rsqrt(ms + EPS) * (1.0 + mod_ref[0, 1:2, :]) + mod_ref[0, 0:1, :]
        hb = h.astype(BF16)
        h_scr[...] = hb
        g_ref[...] = jax.nn.sigmoid(jnp.dot(hb, wg_ref[...], preferred_element_type=F32))

    acc = jnp.dot(h_scr[...], w_ref[...], preferred_element_type=F32)

    @pl.when(j < n_treated_tiles)
    def _():
        cc, sa, sb = cc_ref[...], sa_ref[...], sb_ref[...]
        for hh in range(HEADS_PER_TILE):
            yh = acc[:, hh * HEAD_DIM:(hh + 1) * HEAD_DIM]
            ms = jnp.mean(yh * yh, axis=-1, keepdims=True)
            yn = yh * lax.rsqrt(ms + EPS) * gain_ref[0, hh:hh + 1, :]
            y_ref[hh] = _rope(yn, cc, sa, sb).astype(BF16)

    @pl.when(j >= n_treated_tiles)
    def _():
        for hh in range(HEADS_PER_TILE):
            y_ref[hh] = acc[:, hh * HEAD_DIM:(hh + 1) * HEAD_DIM].astype(BF16)


def _in_proj(x2, mod3, w_main, w_gate, gains, cc, sa, sb, T):
    BT, D = x2.shape
    tm = min(1024, T)
    tn = HEADS_PER_TILE * HEAD_DIM
    n_tiles = N_PROJ_HEADS // HEADS_PER_TILE
    tpb = T // tm
    kern = functools.partial(_inproj_kernel, n_treated_tiles=N_TREATED // HEADS_PER_TILE)
    y, g = pl.pallas_call(
        kern,
        grid=(BT // tm, n_tiles),
        in_specs=[pl.BlockSpec((tm, D), lambda i, j: (i, 0)),
                  pl.BlockSpec((1, 6, D), lambda i, j: (i // tpb, 0, 0)),
                  pl.BlockSpec((D, tn), lambda i, j: (0, j)),
                  pl.BlockSpec((D, NSA_KV_HEADS * GATE_LANES), lambda i, j: (0, 0)),
                  pl.BlockSpec((1, HEADS_PER_TILE, HEAD_DIM), lambda i, j: (j, 0, 0)),
                  pl.BlockSpec((tm, HEAD_DIM), lambda i, j: (i % tpb, 0)),
                  pl.BlockSpec((tm, HEAD_DIM), lambda i, j: (i % tpb, 0)),
                  pl.BlockSpec((tm, HEAD_DIM), lambda i, j: (i % tpb, 0))],
        out_specs=[pl.BlockSpec((HEADS_PER_TILE, tm, HEAD_DIM), lambda i, j: (j, i, 0)),
                   pl.BlockSpec((tm, NSA_KV_HEADS * GATE_LANES), lambda i, j: (i, 0))],
        out_shape=[jax.ShapeDtypeStruct((N_PROJ_HEADS, BT, HEAD_DIM), BF16),
                   jax.ShapeDtypeStruct((BT, NSA_KV_HEADS * GATE_LANES), F32)],
        scratch_shapes=[pltpu.VMEM((tm, D), BF16)],
        compiler_params=_cparams(("parallel", "arbitrary")),
        name="in_proj",
    )(x2, mod3, w_main, w_gate, gains, cc, sa, sb)
    return y, g


def _compress_kernel(h_ref, w1c_ref, pe_ref, w1_ref, w2_ref, gain_ref, cc_ref, sa_ref, sb_ref, o_ref):
    a = pl.program_id(0)
    ncp = h_ref.shape[2]
    z = jnp.dot(h_ref[0, 0], w1c_ref[0], preferred_element_type=F32)
    top = z[:, :HEAD_DIM]
    bot = pltpu.roll(z[:, HEAD_DIM:], ncp - 1, 0)
    pe_term = jnp.dot(pe_ref[0], w1_ref[0], preferred_element_type=F32)[0:1, :]
    pre = top + bot + pe_term
    act = pre * jax.nn.sigmoid(pre)
    out = jnp.dot(act.astype(BF16), w2_ref[0], preferred_element_type=F32)
    live = lax.broadcasted_iota(jnp.int32, out.shape, 0) < ncp - 1
    out = jnp.where(live, out, 0.0)

    @pl.when(a < NSA_KV_HEADS)
    def _():
        ms = jnp.mean(out * out, axis=-1, keepdims=True)
        yn = out * lax.rsqrt(ms + EPS) * gain_ref[...]
        o_ref[0, 0] = _rope(yn, cc_ref[...], sa_ref[...], sb_ref[...]).astype(BF16)

    @pl.when(a >= NSA_KV_HEADS)
    def _():
        o_ref[0, 0] = out.astype(BF16)


def _compress(hc, w1cat, pe8, w1, w2, gain, cc, sa, sb):
    A, B, ncp, K = hc.shape
    G = NSA_KV_HEADS
    return pl.pallas_call(
        _compress_kernel,
        grid=(A, B),
        in_specs=[pl.BlockSpec((1, 1, ncp, K), lambda a, b: (a, b, 0, 0)),
                  pl.BlockSpec((1, K, 2 * HEAD_DIM), lambda a, b: (a // G, 0, 0)),
                  pl.BlockSpec((1, 8, 2 * K), lambda a, b: (a // G, 0, 0)),
                  pl.BlockSpec((1, 2 * K, HEAD_DIM), lambda a, b: (a // G, 0, 0)),
                  pl.BlockSpec((1, HEAD_DIM, HEAD_DIM), lambda a, b: (a // G, 0, 0)),
                  pl.BlockSpec((1, HEAD_DIM), lambda a, b: (0, 0)),
                  pl.BlockSpec((ncp, HEAD_DIM), lambda a, b: (0, 0)),
                  pl.BlockSpec((ncp, HEAD_DIM), lambda a, b: (0, 0)),
                  pl.BlockSpec((ncp, HEAD_DIM), lambda a, b: (0, 0))],
        out_specs=pl.BlockSpec((1, 1, ncp, HEAD_DIM), lambda a, b: (a, b, 0, 0)),
        out_shape=jax.ShapeDtypeStruct((A, B, ncp, HEAD_DIM), BF16),
        compiler_params=_cparams(("arbitrary", "arbitrary")),
        name="compress",
    )(hc, w1cat, pe8, w1, w2, gain, cc, sa, sb)


def _reset(m_scr, acc_scr):
    m_scr[...] = jnp.full(m_scr.shape, M_INIT, F32)
    acc_scr[...] = jnp.zeros(acc_scr.shape, F32)


def _with_ones(vT):
    return jnp.concatenate([vT, jnp.ones((ONES_ROWS, vT.shape[1]), vT.dtype)], axis=0)


def _normalized(acc):
    return acc[0:HEAD_DIM] / acc[HEAD_DIM:HEAD_DIM + 1]


def _online_step(s, vT, m_scr, acc_scr):
    m_prev = m_scr[...]
    m_new = jnp.maximum(m_prev, jnp.max(s, axis=0, keepdims=True))
    alpha = jnp.exp2(m_prev - m_new)
    p = jnp.exp2(s - m_new)
    acc_scr[...] = alpha * acc_scr[...] + jnp.dot(_with_ones(vT), p.astype(BF16), preferred_element_type=F32)
    m_scr[...] = m_new


def _rank_select(score, k):
    n = score.shape[0]
    rank = jnp.zeros(score.shape, F32)
    for m in range(n):
        sm = score[m:m + 1, :]
        lo = (m // SUBLANES) * SUBLANES
        hi = min(lo + SUBLANES, n)
        parts = []
        if lo > 0:
            parts.append(jnp.where(sm > score[:lo], 1.0, 0.0))
        gt =jnp.where(sm > score[lo:hi], 1.0, 0.0)
        ge = jnp.where(sm >= score[lo:hi], 1.0, 0.0)
        below = lax.broadcasted_iota(jnp.int32, gt.shape, 0) > m - lo
        parts.append(jnp.where(below, ge, gt))
        if hi < n:
            parts.append(jnp.where(sm >= score[hi:], 1.0, 0.0))
        rank = rank + (jnp.concatenate(parts, axis=0) if len(parts) > 1 else parts[0])
    return rank < k


def _pad_rows(a, rows):
    return jnp.concatenate([a, jnp.zeros((rows - a.shape[0], a.shape[1]), a.dtype)], axis=0)


def _masked_scores(k_ref, e_ref, start, size, q_aug):
    rows = pl.ds(pl.multiple_of(start, 128), size)
    k_aug = jnp.concatenate([k_ref[rows, :], e_ref[rows, :]], axis=1)
    return jnp.dot(k_aug, q_aug, preferred_element_type=F32)


def _pipelined_attention(chains, n_steps, n_max, tks, tk):
    per = tks // tk

    def scores(step, buf):
        st = jnp.minimum(step, n_max - 1)
        dead = jnp.where(step < n_steps, 0, 1)
        for k_ref, e_ref, _, qaug_ref, bufs, _, _ in chains:
            bufs[buf][...] = _masked_scores(k_ref, e_ref, st * tks, tks, qaug_ref[dead])

    def consume(step, buf):
        st = jnp.minimum(step, n_max - 1)
        for _, _, vT_ref, _, bufs, m_scr, acc_scr in chains:
            vT = jnp.concatenate([vT_ref[st * per + a] for a in range(per)], axis=1)
            _online_step(bufs[buf][...], vT, m_scr, acc_scr)

    def pair(p, carry):
        scores(2 * p + 1, 1)
        consume(2 * p, 0)
        scores(2 * p + 2, 0)
        consume(2 * p + 1, 1)
        return carry

    n_pairs = (n_steps + 1) // 2

    @pl.when(n_pairs > 0)
    def _():
        scores(0, 0)

    lax.fori_loop(0, n_pairs, pair, 0)


def _nsa_kernel(qT_ref, kc_ref, vcT_ref, ks_ref, vsT_ref, kw_ref, vwT_ref, gT_ref, ovT_ref, e_ref,
                gain_ref, o_ref, m_scr, acc_scr, comb_scr, qaug_scr, s0_scr, s1_scr, *, T):
    R, tq = NSA_GROUP, NSA_TQ
    i = pl.program_id(2)
    t0 = i * tq
    ncp = T // CMP_STRIDE
    nsb = T // SLC_BLOCK
    qT = jnp.concatenate([qT_ref[r] for r in range(R)], axis=1)
    gT = gT_ref[0]

    def per_head(row0):
        return jnp.concatenate([gT[row0 + 3 * r:row0 + 3 * r + 1, :] for r in range(R)], axis=1)

    def lanes_x_heads(a):
        return jnp.concatenate([a] * R, axis=1)

    s = jnp.dot(kc_ref[0, 0], qT, preferred_element_type=F32)
    c_idx = lax.broadcasted_iota(jnp.int32, (ncp, tq), 0)
    t_c = t0 + lax.broadcasted_iota(jnp.int32, (ncp, tq), 1)
    vis = lanes_x_heads((c_idx * CMP_STRIDE + (CMP_BLOCK - 1) <= t_c) & (c_idx < ncp - 1))
    s = jnp.where(vis, s, NEG)
    m = jnp.max(s, axis=0, keepdims=True)
    p = jnp.where(vis, jnp.exp2(s - m), 0.0)
    l = jnp.sum(p, axis=0, keepdims=True)
    p = p / jnp.where(l > 0.0, l, 1.0)
    o_cmp = jnp.dot(vcT_ref[0, 0], p.astype(BF16), preferred_element_type=F32)
    comb_scr[...] = o_cmp * per_head(0)

    psum = p[:, 0:tq]
    for r in range(1, R):
        psum = psum + p[:, r * tq:(r + 1) * tq]
    ov = ovT_ref[...]
    imp = jnp.zeros((nsb, tq), F32)
    for part in _split3(psum):
        imp = imp + jnp.dot(ov, part, preferred_element_type=F32)
    n_idx = lax.broadcasted_iota(jnp.int32, (nsb, tq), 0)
    cur = (t0 + lax.broadcasted_iota(jnp.int32, (nsb, tq), 1)) // SLC_BLOCK
    forced = (n_idx == 0) | (n_idx == cur) | (n_idx == cur - 1)
    valid = n_idx <= cur
    score = jnp.where(valid, imp + jnp.where(forced, FORCE_BONUS, 0.0), NEG)
    sel = _rank_select(score, min(SLC_TOPK, nsb)) & valid
    sel_past = sel & (n_idx * SLC_BLOCK < t0)
    bias = _pad_rows(jnp.where(sel_past, 0.0, NEG), HEAD_DIM).astype(BF16)
    bias_own = _pad_rows(jnp.where(sel, 0.0, NEG), HEAD_DIM).astype(BF16)
    for d in range(3):
        qaug_scr[d, 0:HEAD_DIM, :] = qT
    qaug_scr[0, HEAD_DIM:, :] = lanes_x_heads(bias)
    qaug_scr[1, HEAD_DIM:, :] = jnp.full((HEAD_DIM, R * tq), NEG, BF16)
    qaug_scr[2, HEAD_DIM:, :] = lanes_x_heads(bias_own)

    w0 = jnp.maximum(t0 + tq - WIN_SPAN, 0)
    k_w = kw_ref[0, pl.ds(pl.multiple_of(w0, WIN_TK), WIN_SPAN), :]
    s_w = jnp.dot(k_w, qT, preferred_element_type=F32)
    kpos = w0 + lax.broadcasted_iota(jnp.int32, (WIN_SPAN, tq), 0)
    tpos = t0 + lax.broadcasted_iota(jnp.int32, (WIN_SPAN, tq), 1)
    ok = lanes_x_heads((kpos <= tpos) & (tpos - kpos < WINDOW))
    s_w = jnp.where(ok, s_w, NEG)
    m_w = jnp.max(s_w, axis=0, keepdims=True)
    p_w = jnp.exp2(s_w - m_w)
    jw = w0 // WIN_TK
    v_w = jnp.concatenate([vwT_ref[0, jw + a] for a in range(WIN_SPAN // WIN_TK)], axis=1)
    o_w = jnp.dot(_with_ones(v_w), p_w.astype(BF16), preferred_element_type=F32)
    comb_scr[...] += _normalized(o_w) * per_head(2)

    tri = lax.broadcasted_iota(jnp.int32, (tq, tq), 0) <= lax.broadcasted_iota(jnp.int32, (tq, tq), 1)
    s_d = jnp.where(lanes_x_heads(tri), _masked_scores(ks_ref.at[0], e_ref, t0, tq, qaug_scr[2]), NEG)
    _reset(m_scr, acc_scr)
    n_steps = (t0 + SLC_TKS - 1) // SLC_TKS
    _pipelined_attention([(ks_ref.at[0], e_ref, vsT_ref.at[0], qaug_scr, (s0_scr, s1_scr),
                           m_scr, acc_scr)], n_steps, T // SLC_TKS, SLC_TKS, SLC_TK)
    own = tq // SLC_TK
    v_d = jnp.concatenate([vsT_ref[0, i * own + a] for a in range(own)], axis=1)
    _online_step(s_d, v_d, m_scr, acc_scr)
    comb = comb_scr[...] + _normalized(acc_scr[...]) * per_head(1)

    for r in range(R):
        oT = comb[:, r * tq:(r + 1) * tq]
        ms = jnp.mean(oT * oT, axis=0, keepdims=True)
        on = (oT * lax.rsqrt(ms + EPS)).T * gain_ref[0, r:r + 1, :]
        o_ref[:, r * HEAD_DIM:(r + 1) * HEAD_DIM] = on.astype(BF16)


def _nsa(qT, kcvc, vcT, y, vsT, vwT, gT, ovT, e_slc, gains, B, T):
    G, R, tq = NSA_KV_HEADS, NSA_GROUP, NSA_TQ
    nq = T // tq
    ncp = T // CMP_STRIDE
    nsb = T // SLC_BLOCK
    kern = functools.partial(_nsa_kernel, T=T)
    return pl.pallas_call(
        kern,
        grid=(B, G, nq),
        in_specs=[pl.BlockSpec((R, HEAD_DIM, tq), lambda b, g, i: (g, 0, b * nq + i)),
                  pl.BlockSpec((1, 1, ncp, HEAD_DIM), lambda b, g, i: (g, b, 0, 0)),
                  pl.BlockSpec((1, 1, HEAD_DIM, ncp), lambda b, g, i: (g, b, 0, 0)),
                  pl.BlockSpec((1, T, HEAD_DIM), lambda b, g, i: (H_KSLC + g, b, 0)),
                  pl.BlockSpec((1, T // SLC_TK, HEAD_DIM, SLC_TK), lambda b, g, i: (g * B + b, 0, 0, 0)),
                  pl.BlockSpec((1, T, HEAD_DIM), lambda b, g, i: (H_KWIN + g, b, 0)),
                  pl.BlockSpec((1, T // WIN_TK, HEAD_DIM, WIN_TK), lambda b, g, i: (g * B + b, 0, 0, 0)),
                  pl.BlockSpec((1, 16, tq), lambda b, g, i: (g, 0, b * nq + i)),
                  pl.BlockSpec((nsb, ncp), lambda b, g, i: (0, 0)),
                  pl.BlockSpec((T, HEAD_DIM), lambda b, g, i: (0, 0)),
                  pl.BlockSpec((1, R, HEAD_DIM), lambda b, g, i: (g, 0, 0))],
        out_specs=pl.BlockSpec((tq, R * HEAD_DIM), lambda b, g, i: (b * nq + i, g)),
        out_shape=jax.ShapeDtypeStruct((B * T, NSA_HEADS * HEAD_DIM), BF16),
        scratch_shapes=[pltpu.VMEM((1, R * tq), F32),
                        pltpu.VMEM((HEAD_DIM + ONES_ROWS, R * tq), F32), pltpu.VMEM((HEAD_DIM, R * tq), F32),
                        pltpu.VMEM((3, 2 * HEAD_DIM, R * tq), BF16),
                        pltpu.VMEM((SLC_TKS, R * tq), F32), pltpu.VMEM((SLC_TKS, R * tq), F32)],
        compiler_params=_cparams(("parallel", "parallel", "arbitrary")),
        name="nsa",
    )(qT, kcvc, vcT, y, vsT, y, vwT, gT, ovT, e_slc, gains)


def _moba_kernel(qT_ref, k_ref, vT_ref, e_ref, gain_ref, o_ref, m_scr, acc_scr, kmean_scr, qaug_scr,
                 s_scr, *, T):
    tq, tk = MOBA_TQ, MOBA_TK
    nb = T // MOBA_BLOCK
    nbp = kmean_scr.shape[1]
    i = pl.program_id(2)
    t0 = i * tq

    @pl.when(i == 0)
    def _():
        kmean_scr[...] = jnp.zeros(kmean_scr.shape, F32)
        for a in range(MOBA_HB):
            kb = k_ref[a].astype(F32).reshape(nb, MOBA_BLOCK, HEAD_DIM)
            kmean_scr[a, 0:nb, :] = jnp.mean(kb, axis=1)

    n_idx = lax.broadcasted_iota(jnp.int32, (nbp, tq), 0)
    cur = (t0 + lax.broadcasted_iota(jnp.int32, (nbp, tq), 1)) // MOBA_BLOCK
    past = n_idx < cur
    causal = lax.broadcasted_iota(jnp.int32, (tk, tq), 0) <= lax.broadcasted_iota(jnp.int32, (tk, tq), 1)

    for a in range(MOBA_HB):
        qT = qT_ref[a]
        gate = jnp.zeros((nbp, tq), F32)
        for part in _split3(kmean_scr[a]):
            gate = gate + jnp.dot(part, qT, preferred_element_type=F32)
        sel = _rank_select(jnp.where(past, gate, NEG), min(MOBA_TOPK, nb)) & past
        for d in range(2):
            qaug_scr[a, d, 0:HEAD_DIM, :] = qT
        qaug_scr[a, 0, HEAD_DIM:, :] = _pad_rows(jnp.where(sel, 0.0, NEG), HEAD_DIM).astype(BF16)
        qaug_scr[a, 1, HEAD_DIM:, :] = jnp.full((HEAD_DIM, tq), NEG, BF16)

    s_own = []
    for a in range(MOBA_HB):
        _reset(m_scr.at[a], acc_scr.at[a])
        k_own = k_ref[a, pl.ds(pl.multiple_of(t0, tk), tk), :]
        s_own.append(jnp.where(causal, jnp.dot(k_own, qT_ref[a], preferred_element_type=F32), NEG))

    chains = [(k_ref.at[a], e_ref, vT_ref.at[a, 0], qaug_scr.at[a], (s_scr.at[a, 0], s_scr.at[a, 1]),
               m_scr.at[a], acc_scr.at[a]) for a in range(MOBA_HB)]
    n_steps = (t0 + MOBA_TKS - 1) // MOBA_TKS
    _pipelined_attention(chains, n_steps, T // MOBA_TKS, MOBA_TKS, tk)

    for a in range(MOBA_HB):
        _online_step(s_own[a], vT_ref[a, 0, i], m_scr.at[a], acc_scr.at[a])
        oT = _normalized(acc_scr[a])
        ms = jnp.mean(oT * oT, axis=0, keepdims=True)
        on = (oT * lax.rsqrt(ms + EPS)).T * gain_ref[a]
        o_ref[:, a * HEAD_DIM:(a + 1) * HEAD_DIM] = on.astype(BF16)


def _moba(qT, y, vT, e_moba, gains, B, T):
    H, HB, tq, tk = MOBA_HEADS, MOBA_HB, MOBA_TQ, MOBA_TK
    nq = T // tq
    nb = T // MOBA_BLOCK
    nbp = max(16, nb)
    kern = functools.partial(_moba_kernel, T=T)
    return pl.pallas_call(
        kern,
        grid=(B, H // HB, nq),
        in_specs=[pl.BlockSpec((HB, HEAD_DIM, tq), lambda b, h, i: (h, 0, b * nq + i)),
                  pl.BlockSpec((HB, T, HEAD_DIM), lambda b, h, i: (H_MOBA_K // HB + h, b, 0)),
                  pl.BlockSpec((HB, 1, T // tk, HEAD_DIM, tk), lambda b, h, i: (h, b, 0, 0, 0)),
                  pl.BlockSpec((T, HEAD_DIM), lambda b, h, i: (0, 0)),
                  pl.BlockSpec((HB, 1, HEAD_DIM), lambda b, h, i: (h, 0, 0))],
        out_specs=pl.BlockSpec((tq, HB * HEAD_DIM), lambda b, h, i: (b * nq + i, h)),
        out_shape=jax.ShapeDtypeStruct((B * T, H * HEAD_DIM), BF16),
        scratch_shapes=[pltpu.VMEM((HB, 1, tq), F32),
                        pltpu.VMEM((HB, HEAD_DIM + ONES_ROWS, tq), F32), pltpu.VMEM((HB, nbp, HEAD_DIM), F32),
                        pltpu.VMEM((HB, 2, 2 * HEAD_DIM, tq), BF16),
                        pltpu.VMEM((HB, 2, MOBA_TKS, tq), F32)],
        compiler_params=_cparams(("parallel", "parallel", "arbitrary")),
        name="moba",
    )(qT, y, vT, e_moba, gains)


def _outproj_kernel(on_ref, om_ref, w_ref, x_ref, mod_ref, o_ref):
    half = on_ref.shape[1]
    acc = jnp.dot(on_ref[...], w_ref[0:half, :], preferred_element_type=F32)
    acc = acc + jnp.dot(om_ref[...], w_ref[half:, :], preferred_element_type=F32)
    o_ref[...] = x_ref[...] + mod_ref[0, 2:3, :] * acc


def _out_proj(o_nsa, o_moba, w_out, x2, mod3, T):
    BT, D = x2.shape
    tm = min(512, T)
    tpb = T // tm
    half = o_nsa.shape[1]
    return pl.pallas_call(
        _outproj_kernel,
        grid=(BT // tm,),
        in_specs=[pl.BlockSpec((tm, half), lambda i: (i, 0)),
                  pl.BlockSpec((tm, half), lambda i: (i, 0)),
                  pl.BlockSpec((D, D), lambda i: (0, 0)),
                  pl.BlockSpec((tm, D), lambda i: (i, 0)),
                  pl.BlockSpec((1, 6, D), lambda i: (i // tpb, 0, 0))],
        out_specs=pl.BlockSpec((tm, D), lambda i: (i, 0)),
        out_shape=jax.ShapeDtypeStruct((BT, D), F32),
        compiler_params=_cparams(("parallel",)),
        name="out_proj",
    )(o_nsa, o_moba, w_out, x2, mod3)


def _ffn_kernel(x_ref, mod_ref, wg_ref, wu_ref, wo_ref, o_ref, h_scr):
    j = pl.program_id(1)

    @pl.when(j == 0)
    def _():
        x = x_ref[...]
        ms = jnp.mean(x * x, axis=-1, keepdims=True)
        h = x * lax.rsqrt(ms + EPS) * (1.0 + mod_ref[0, 4:5, :]) + mod_ref[0, 3:4, :]
        h_scr[...] = h.astype(BF16)
        o_ref[...] = jnp.zeros(o_ref.shape, F32)

    hb = h_scr[...]
    gate = jnp.dot(hb, wg_ref[...], preferred_element_type=F32)
    up = jnp.dot(hb, wu_ref[...], preferred_element_type=F32)
    act = (gate * jax.nn.sigmoid(gate) * up).astype(BF16)
    o_ref[...] += jnp.dot(act, wo_ref[...], preferred_element_type=F32)

    @pl.when(j == pl.num_programs(1) - 1)
    def _():
        o_ref[...] = x_ref[...] + mod_ref[0, 5:6, :] * o_ref[...]


def _ffn(x1, mod3, w_in, w_out, T):
    BT, D = x1.shape
    Fh = w_out.shape[0]
    tm = min(512, T)
    tf = 512
    tpb = T // tm
    nf = Fh // tf
    return pl.pallas_call(
        _ffn_kernel,
        grid=(BT // tm, nf),
        in_specs=[pl.BlockSpec((tm, D), lambda i, j: (i, 0)),
                  pl.BlockSpec((1, 6, D), lambda i, j: (i // tpb, 0, 0)),
                  pl.BlockSpec((D, tf), lambda i, j: (0, j)),
                  pl.BlockSpec((D, tf), lambda i, j: (0, nf + j)),
                  pl.BlockSpec((tf, D), lambda i, j: (j, 0))],
        out_specs=pl.BlockSpec((tm, D), lambda i, j: (i, 0)),
        out_shape=jax.ShapeDtypeStruct((BT, D), F32),
        scratch_shapes=[pltpu.VMEM((tm, D), BF16)],
        compiler_params=_cparams(("parallel", "arbitrary")),
        name="ffn",
    )(x1, mod3, w_in, w_in, w_out)


def _rope_lane_tables(T):
    inv = ROPE_THETA ** (-jnp.arange(0, ROPE_DIMS, 2, dtype=F32) / ROPE_DIMS)
    ang = jnp.arange(T).astype(F32)[:, None] * inv[None, :]
    cos, sin = jnp.cos(ang), jnp.sin(ang)
    rest = HEAD_DIM - ROPE_DIMS
    cc = jnp.concatenate([cos, cos, jnp.ones((T, rest), F32)], axis=1)
    sa = jnp.concatenate([-sin, jnp.zeros((T, HEAD_DIM - ROPE_HALF), F32)], axis=1)
    sb = jnp.concatenate([jnp.zeros((T, ROPE_HALF), F32), sin, jnp.zeros((T, rest), F32)], axis=1)
    return cc, sa, sb


def _tile_transposed(v, B, T, tk):
    n = v.shape[0]
    return v.reshape(n, B, T // tk, tk, HEAD_DIM).transpose(0, 1, 2, 4, 3)


def _block_onehot(T, block):
    return (jnp.arange(T)[:, None] // block == jnp.arange(HEAD_DIM)[None, :]).astype(BF16)


def _layer(x2, c, B, T, w_ada, b_ada, w_in, nsa_q_norm, nsa_k_norm, moba_q_norm, moba_k_norm,
           cmp_pe_k, cmp_w1_k, cmp_w2_k, cmp_pe_v, cmp_w1_v, cmp_w2_v, out_norm, w_out,
           w_ffn_in, w_ffn_out):
    D = x2.shape[1]
    G = NSA_KV_HEADS
    scale = HEAD_DIM ** -0.5 * LOG2E
    assert T % MOBA_BLOCK == 0 and T % SLC_TK == 0 and T % NSA_TQ == 0 and T >= WIN_SPAN
    assert T // SLC_BLOCK <= HEAD_DIM and T // MOBA_BLOCK <= HEAD_DIM

    mod3 = _adaln(c, w_ada, b_ada)

    qw, kvw, gw, mw = NSA_HEADS * HEAD_DIM, G * HEAD_DIM, NSA_HEADS * 3, MOBA_HEADS * HEAD_DIM
    offs = [0, qw] + [qw + kvw * n for n in range(1, 7)]
    col = lambda a, n: w_in[:, a:a + n]
    o_q, o_kc, o_vc, o_ks, o_vs, o_kw, o_vw, o_g = offs
    o_mq = o_g + gw
    w_main = jnp.concatenate(
        [col(o_q, qw), col(o_ks, kvw), col(o_kw, kvw), col(o_mq, mw), col(o_mq + mw, mw),
         col(o_kc, kvw), col(o_vc, kvw), col(o_vs, kvw), col(o_vw, kvw), col(o_mq + 2 * mw, mw)],
        axis=1).astype(BF16)
    wg = col(o_g, gw).reshape(D, G, NSA_GROUP * 3)
    w_gate = jnp.pad(wg, ((0, 0), (0, 0), (0, GATE_LANES - NSA_GROUP * 3))).reshape(D, G * GATE_LANES)
    w_gate = w_gate.astype(BF16)
    rep = lambda g_, n: jnp.broadcast_to(g_, (n, HEAD_DIM))
    gains = jnp.concatenate(
        [rep(nsa_q_norm * scale, NSA_HEADS), rep(nsa_k_norm[1], G), rep(nsa_k_norm[2], G),
         rep(moba_q_norm * scale, MOBA_HEADS), rep(moba_k_norm, MOBA_HEADS),
         jnp.ones((N_PROJ_HEADS - N_TREATED, HEAD_DIM), F32)], axis=0)
    gains = gains.reshape(N_PROJ_HEADS // HEADS_PER_TILE, HEADS_PER_TILE, HEAD_DIM)
    cc, sa, sb = _rope_lane_tables(T)

    y, gates = _in_proj(x2, mod3, w_main, w_gate, gains, cc, sa, sb, T)

    qT_nsa = y[H_NSA_Q:H_NSA_Q + NSA_HEADS].transpose(0, 2, 1)
    qT_moba = y[H_MOBA_Q:H_MOBA_Q + MOBA_HEADS].transpose(0, 2, 1)
    vsT = _tile_transposed(y[H_VSLC:H_VSLC + G], B, T, SLC_TK).reshape(G * B, T // SLC_TK, HEAD_DIM, SLC_TK)
    vwT = _tile_transposed(y[H_VWIN:H_VWIN + G], B, T, WIN_TK).reshape(G * B, T // WIN_TK, HEAD_DIM, WIN_TK)
    vmT = _tile_transposed(y[H_MOBA_V:H_MOBA_V + MOBA_HEADS], B, T, MOBA_TK)
    gT = gates.reshape(B * T, G, GATE_LANES)[:, :, :16].transpose(1, 2, 0)

    ncp = T // CMP_STRIDE
    half = CMP_STRIDE * HEAD_DIM
    hc = y[H_KCMP:H_KCMP + 2 * G].reshape(2 * G, B, ncp, half)
    w1 = jnp.stack([cmp_w1_k, cmp_w1_v]).astype(BF16)
    w1cat = jnp.concatenate([w1[:, :half], w1[:, half:]], axis=2)
    pe8 = jnp.broadcast_to(jnp.stack([cmp_pe_k, cmp_pe_v]).reshape(2, 1, 2 * half), (2, 8, 2 * half))
    w2 = jnp.stack([cmp_w2_k, cmp_w2_v]).astype(BF16)
    cmp_rows = lambda t: jnp.pad(t[CMP_BLOCK - 1::CMP_STRIDE], ((0, 1), (0, 0)))
    kcvc = _compress(hc, w1cat, pe8.astype(BF16), w1, w2, nsa_k_norm[0].reshape(1, HEAD_DIM),
                     cmp_rows(cc), cmp_rows(sa), cmp_rows(sb))
    vcT = kcvc[G:].transpose(0, 1, 3, 2)

    nsb = T // SLC_BLOCK
    cs = jnp.arange(ncp)[None, :] * CMP_STRIDE
    sbk = jnp.arange(nsb)[:, None] * SLC_BLOCK
    ovT = ((cs < sbk + SLC_BLOCK) & (cs + CMP_BLOCK > sbk) & (jnp.arange(ncp)[None, :] < ncp - 1)).astype(BF16)

    on = out_norm.reshape(N_HEADS, HEAD_DIM)
    o_nsa = _nsa(qT_nsa, kcvc, vcT, y, vsT, vwT, gT, ovT, _block_onehot(T, SLC_BLOCK),
                 on[:NSA_HEADS].reshape(G, NSA_GROUP, HEAD_DIM), B, T)
    o_moba = _moba(qT_moba, y, vmT, _block_onehot(T, MOBA_BLOCK),
                   on[NSA_HEADS:].reshape(MOBA_HEADS, 1, HEAD_DIM), B, T)

    x1 = _out_proj(o_nsa, o_moba, w_out.astype(BF16), x2, mod3, T)
    return _ffn(x1, mod3, w_ffn_in.astype(BF16), w_ffn_out.astype(BF16), T)


def kernel(x, c, w_ada, b_ada, w_in, nsa_q_norm, nsa_k_norm, moba_q_norm, moba_k_norm, cmp_pe_k, cmp_w1_k, cmp_w2_k, cmp_pe_v, cmp_w1_v, cmp_w2_v, out_norm, w_out, w_ffn_in, w_ffn_out):
    B, T, D = x.shape
    x2 = x.reshape(B * T, D)
    for l in range(w_ada.shape[0]):
        x2 = _layer(x2, c, B, T, w_ada[l], b_ada[l], w_in[l], nsa_q_norm[l], nsa_k_norm[l],
                    moba_q_norm[l], moba_k_norm[l], cmp_pe_k[l], cmp_w1_k[l], cmp_w2_k[l],
                    cmp_pe_v[l], cmp_w1_v[l], cmp_w2_v[l], out_norm[l], w_out[l],
                    w_ffn_in[l], w_ffn_out[l])
    return x2.reshape(B, T, D)
```

```python
import functools

import jax
import jax.numpy as jnp
from jax import lax
from jax.experimental import pallas as pl
from jax.experimental.pallas import tpu as pltpu

F32 = jnp.float32
BF16 = jnp.bfloat16

HEAD_DIM = 128
SUBLANES = 8
NSA_HEADS = 8
NSA_KV_HEADS = 2
NSA_GROUP = NSA_HEADS // NSA_KV_HEADS
MOBA_HEADS = 8
N_HEADS = NSA_HEADS + MOBA_HEADS
CMP_BLOCK = 32
CMP_STRIDE = 16
SLC_BLOCK = 64
SLC_TOPK = 16
WINDOW = 512
FORCE_BONUS = 1e4
MOBA_BLOCK = 256
MOBA_TOPK = 3
ROPE_THETA = 500000.0
ROPE_DIMS = HEAD_DIM // 4
ROPE_HALF = ROPE_DIMS // 2
EPS = 1e-6
LOG2E = 1.4426950408889634
ONES_ROWS = 16
NEG = -1e30
M_INIT = -1e29

V7X_VMEM_BYTES = 64 * 1024 * 1024
VMEM_LIMIT = V7X_VMEM_BYTES - 8 * 1024 * 1024

HEADS_PER_TILE = 4
PROJ_TILES = ("nsa_q", "nsa_q", "moba_q", "moba_q", "v_slc_win", "moba_v", "moba_v",
              "k_slc_win", "moba_k", "moba_k", "kv_cmp")
N_T_TILES = 7
N_STD_TILES = 3
TREATED_TILES = (0, 1, 2, 3, 7, 8, 9)
T_NSA_Q, T_MOBA_Q, T_VSLC, T_VWIN, T_MOBA_V = 0, 8, 16, 18, 20
S_KSLC, S_KWIN, S_MOBA_K = 0, 2, 4
N_T_HEADS, N_STD_HEADS, N_CMP_HEADS = 28, 12, 4
CHUNK = 128
GATE_LANES = 128
GATE_ROWS = 16

NSA_TQ = 256
SLC_TKS = 512
WIN_SPAN = WINDOW + NSA_TQ
MOBA_TQ = 256
MOBA_TK = MOBA_BLOCK
MOBA_TKS = 512
MOBA_HB = 4


def _cparams(sem):
    return pltpu.CompilerParams(dimension_semantics=sem, vmem_limit_bytes=VMEM_LIMIT)


def _split3(a):
    hi = a.astype(BF16)
    r1 = a - hi.astype(F32)
    mid = r1.astype(BF16)
    lo = (r1 - mid.astype(F32)).astype(BF16)
    return hi, mid, lo


def _adaln_kernel(c_ref, w_ref, b_ref, o_ref):
    cv = c_ref[...]
    s = cv * jax.nn.sigmoid(cv)
    w = w_ref[...]
    acc = jnp.zeros(o_ref.shape, F32)
    for part in _split3(s):
        for wpart in _split3(w)[:2]:
            acc = acc + jnp.dot(part, wpart, preferred_element_type=F32)
    o_ref[...] = acc + b_ref[...]


def _adaln(c, w_ada, b_ada):
    B, D = c.shape
    N = w_ada.shape[1]
    tn = 1024
    c8 = jnp.zeros((8, D), F32).at[:B].set(c)
    out = pl.pallas_call(
        _adaln_kernel,
        grid=(N // tn,),
        in_specs=[pl.BlockSpec((8, D), lambda j: (0, 0)),
                  pl.BlockSpec((D, tn), lambda j: (0, j)),
                  pl.BlockSpec((1, tn), lambda j: (0, j))],
        out_specs=pl.BlockSpec((8, tn), lambda j: (0, j)),
        out_shape=jax.ShapeDtypeStruct((8, N), F32),
        compiler_params=_cparams(("arbitrary",)),
        name="adaln",
    )(c8, w_ada, b_ada.reshape(1, N))
    return out[:B].reshape(B, 6, D)


def _rope(y, cc, sa, sb):
    return (y * cc + pltpu.roll(y, HEAD_DIM - ROPE_HALF, 1) * sa + pltpu.roll(y, ROPE_HALF, 1) * sb)


def _inproj_kernel(x_ref, mod_ref, w_ref, wg_ref, gain_ref, cc_ref, sa_ref, sb_ref,
                   yT_ref, ystd_ref, hc_ref, gT_ref, h_scr, rows_scr):
    j = pl.program_id(1)
    tm = x_ref.shape[0]
    n_chunks = tm // CHUNK

    @pl.when(j == 0)
    def _():
        x = x_ref[...]
        ms = jnp.mean(x * x, axis=-1, keepdims=True)
        h = x * lax.rsqrt(ms + EPS) * (1.0 + mod_ref[0, 1:2, :]) + mod_ref[0, 0:1, :]
        hb = h.astype(BF16)
        h_scr[...] = hb
        g = jax.nn.sigmoid(jnp.dot(hb, wg_ref[...], preferred_element_type=F32))
        for grp in range(NSA_KV_HEADS):
            for c in range(n_chunks):
                blk = g[c * CHUNK:(c + 1) * CHUNK, grp * GATE_LANES:(grp + 1) * GATE_LANES].T
                gT_ref[grp, :, c * CHUNK:(c + 1) * CHUNK] = blk[0:GATE_ROWS, :]

    acc = jnp.dot(h_scr[...], w_ref[...], preferred_element_type=F32)

    def head(hh, treated):
        yh = acc[:, hh * HEAD_DIM:(hh + 1) * HEAD_DIM]
        if not treated:
            return yh
        ms = jnp.mean(yh * yh, axis=-1, keepdims=True)
        yn = yh * lax.rsqrt(ms + EPS) * gain_ref[0, hh:hh + 1, :]
        return _rope(yn, cc_ref[...], sa_ref[...], sb_ref[...])

    def to_yT(treated):
        for hh in range(HEADS_PER_TILE):
            yh = head(hh, treated)
            for c in range(n_chunks):
                yT_ref[hh, c] = yh[c * CHUNK:(c + 1) * CHUNK, :].T.astype(BF16)

    is_t = j < N_T_TILES
    treated = functools.reduce(jnp.logical_or, [j == t for t in TREATED_TILES])

    @pl.when(is_t & treated)
    def _():
        to_yT(True)

    @pl.when(is_t & jnp.logical_not(treated))
    def _():
        to_yT(False)

    @pl.when((j >= N_T_TILES) & (j < N_T_TILES + N_STD_TILES))
    def _():
        for hh in range(HEADS_PER_TILE):
            ystd_ref[hh] = head(hh, True).astype(BF16)

    @pl.when(j == N_T_TILES + N_STD_TILES)
    def _():
        for hh in range(HEADS_PER_TILE):
            rows_scr[...] = head(hh, False)
            flat = [rows_scr[pl.ds(l, tm // CMP_STRIDE, stride=CMP_STRIDE), :] for l in range(CMP_STRIDE)]
            hc_ref[hh] = jnp.concatenate(flat, axis=1).astype(BF16)


def _in_proj(x2, mod3, w_main, w_gate, gains, cc, sa, sb, T):
    BT, D = x2.shape
    G = NSA_KV_HEADS
    tm = min(1024, T)
    tn = HEADS_PER_TILE * HEAD_DIM
    n_tiles = len(PROJ_TILES)
    tpb = T // tm
    return pl.pallas_call(
        _inproj_kernel,
        grid=(BT // tm, n_tiles),
        in_specs=[pl.BlockSpec((tm, D), lambda i, j: (i, 0)),
                  pl.BlockSpec((1, 6, D), lambda i, j: (i // tpb, 0, 0)),
                  pl.BlockSpec((D, tn), lambda i, j: (0, j)),
                  pl.BlockSpec((D, G * GATE_LANES), lambda i, j: (0, 0)),
                  pl.BlockSpec((1, HEADS_PER_TILE, HEAD_DIM), lambda i, j: (j, 0, 0)),
                  pl.BlockSpec((tm, HEAD_DIM), lambda i, j: (i % tpb, 0)),
                  pl.BlockSpec((tm, HEAD_DIM), lambda i, j: (i % tpb, 0)),
                  pl.BlockSpec((tm, HEAD_DIM), lambda i, j: (i % tpb, 0))],
        out_specs=[pl.BlockSpec((HEADS_PER_TILE, tm // CHUNK, HEAD_DIM, CHUNK),
                                lambda i, j: (jnp.minimum(j, N_T_TILES - 1), i, 0, 0)),
                   pl.BlockSpec((HEADS_PER_TILE, tm, HEAD_DIM),
                                lambda i, j: (jnp.clip(j - N_T_TILES, 0, N_STD_TILES - 1), i, 0)),
                   pl.BlockSpec((N_CMP_HEADS, tm // CMP_STRIDE, CMP_STRIDE * HEAD_DIM), lambda i, j: (0, i, 0)),
                   pl.BlockSpec((G, GATE_ROWS, tm), lambda i, j: (0, 0, i))],
        out_shape=[jax.ShapeDtypeStruct((N_T_HEADS, BT // CHUNK, HEAD_DIM, CHUNK), BF16),
                   jax.ShapeDtypeStruct((N_STD_HEADS, BT, HEAD_DIM), BF16),
                   jax.ShapeDtypeStruct((N_CMP_HEADS, BT // CMP_STRIDE, CMP_STRIDE * HEAD_DIM), BF16),
                   jax.ShapeDtypeStruct((G, GATE_ROWS, BT), F32)],
        scratch_shapes=[pltpu.VMEM((tm, D), BF16), pltpu.VMEM((tm, HEAD_DIM), F32)],
        compiler_params=_cparams(("parallel", "arbitrary")),
        name="in_proj",
    )(x2, mod3, w_main, w_gate, gains, cc, sa, sb)


def _compress_kernel(h_ref, w1c_ref, pe_ref, w1_ref, w2_ref, gain_ref, cc_ref, sa_ref, sb_ref, o_ref):
    a = pl.program_id(0)
    ncp = h_ref.shape[2]
    z = jnp.dot(h_ref[0, 0], w1c_ref[0], preferred_element_type=F32)
    top = z[:, :HEAD_DIM]
    bot = pltpu.roll(z[:, HEAD_DIM:], ncp - 1, 0)
    pe_term = jnp.dot(pe_ref[0], w1_ref[0], preferred_element_type=F32)[0:1, :]
    pre = top + bot + pe_term
    act = pre * jax.nn.sigmoid(pre)
    out = jnp.dot(act.astype(BF16), w2_ref[0], preferred_element_type=F32)
    live = lax.broadcasted_iota(jnp.int32, out.shape, 0) < ncp - 1
    out = jnp.where(live, out, 0.0)

    @pl.when(a < NSA_KV_HEADS)
    def _():
        ms = jnp.mean(out * out, axis=-1, keepdims=True)
        yn = out * lax.rsqrt(ms + EPS) * gain_ref[...]
        o_ref[0, 0] = _rope(yn, cc_ref[...], sa_ref[...], sb_ref[...]).astype(BF16)

    @pl.when(a >= NSA_KV_HEADS)
    def _():
        o_ref[0, 0] = out.astype(BF16)


def _compress(hc, w1cat, pe8, w1, w2, gain, cc, sa, sb):
    A, B, ncp, K = hc.shape
    G = NSA_KV_HEADS
    return pl.pallas_call(
        _compress_kernel,
        grid=(A, B),
        in_specs=[pl.BlockSpec((1, 1, ncp, K), lambda a, b: (a, b, 0, 0)),
                  pl.BlockSpec((1, K, 2 * HEAD_DIM), lambda a, b: (a // G, 0, 0)),
                  pl.BlockSpec((1, 8, 2 * K), lambda a, b: (a // G, 0, 0)),
                  pl.BlockSpec((1, 2 * K, HEAD_DIM), lambda a, b: (a // G, 0, 0)),
                  pl.BlockSpec((1, HEAD_DIM, HEAD_DIM), lambda a, b: (a // G, 0, 0)),
                  pl.BlockSpec((1, HEAD_DIM), lambda a, b: (0, 0)),
                  pl.BlockSpec((ncp, HEAD_DIM), lambda a, b: (0, 0)),
                  pl.BlockSpec((ncp, HEAD_DIM), lambda a, b: (0, 0)),
                  pl.BlockSpec((ncp, HEAD_DIM), lambda a, b: (0, 0))],
        out_specs=pl.BlockSpec((1, 1, ncp, HEAD_DIM), lambda a, b: (a, b, 0, 0)),
        out_shape=jax.ShapeDtypeStruct((A, B, ncp, HEAD_DIM), BF16),
        compiler_params=_cparams(("arbitrary", "arbitrary")),
        name="compress",
    )(hc, w1cat, pe8, w1, w2, gain, cc, sa, sb)


def _reset(m_scr, acc_scr):
    m_scr[...] = jnp.full(m_scr.shape, M_INIT, F32)
    acc_scr[...] = jnp.zeros(acc_scr.shape, F32)


def _with_ones(vT):
    return jnp.concatenate([vT, jnp.ones((ONES_ROWS, vT.shape[1]), vT.dtype)], axis=0)


def _normalized(acc):
    return acc[0:HEAD_DIM] / acc[HEAD_DIM:HEAD_DIM + 1]


def _online_step(s, vT, m_scr, acc_scr):
    m_prev = m_scr[...]
    m_new = jnp.maximum(m_prev, jnp.max(s, axis=0, keepdims=True))
    alpha = jnp.exp2(m_prev - m_new)
    p = jnp.exp2(s - m_new)
    acc_scr[...] = alpha * acc_scr[...] + jnp.dot(_with_ones(vT), p.astype(BF16), preferred_element_type=F32)
    m_scr[...] = m_new


def _rank_select(score, k):
    n = score.shape[0]
    rank = jnp.zeros(score.shape, F32)
    for m in range(n):
        sm = score[m:m + 1, :]
        lo = (m // SUBLANES) * SUBLANES
        hi = min(lo + SUBLANES, n)
        parts = []
        if lo > 0:
            parts.append(jnp.where(sm > score[:lo], 1.0, 0.0))
        gt =jnp.where(sm > score[lo:hi], 1.0, 0.0)
        ge = jnp.where(sm >= score[lo:hi], 1.0, 0.0)
        below = lax.broadcasted_iota(jnp.int32, gt.shape, 0) > m - lo
        parts.append(jnp.where(below, ge, gt))
        if hi < n:
            parts.append(jnp.where(sm >= score[hi:], 1.0, 0.0))
        rank = rank + (jnp.concatenate(parts, axis=0) if len(parts) > 1 else parts[0])
    return rank < k


def _pad_rows(a, rows):
    return jnp.concatenate([a, jnp.zeros((rows - a.shape[0], a.shape[1]), a.dtype)], axis=0)


def _masked_scores(k_ref, e_ref, start, size, q_aug):
    rows = pl.ds(pl.multiple_of(start, 128), size)
    k_aug = jnp.concatenate([k_ref[rows, :], e_ref[rows, :]], axis=1)
    return jnp.dot(k_aug, q_aug, preferred_element_type=F32)


def _pipelined_attention(chains, n_steps, n_max, tks, tk):
    per = tks // tk

    def scores(step, buf):
        st = jnp.minimum(step, n_max - 1)
        dead = jnp.where(step < n_steps, 0, 1)
        for k_ref, e_ref, _, qaug_ref, bufs, _, _ in chains:
            bufs[buf][...] = _masked_scores(k_ref, e_ref, st * tks, tks, qaug_ref[dead])

    def consume(step, buf):
        st = jnp.minimum(step, n_max - 1)
        for _, _, vT_ref, _, bufs, m_scr, acc_scr in chains:
            vT = jnp.concatenate([vT_ref[st * per + a] for a in range(per)], axis=1)
            _online_step(bufs[buf][...], vT, m_scr, acc_scr)

    def pair(p, carry):
        scores(2 * p + 1, 1)
        consume(2 * p, 0)
        scores(2 * p + 2, 0)
        consume(2 * p + 1, 1)
        return carry

    n_pairs = (n_steps + 1) // 2

    @pl.when(n_pairs > 0)
    def _():
        scores(0, 0)

    lax.fori_loop(0, n_pairs, pair, 0)


def _nsa_kernel(qT_ref, kc_ref, vcT_ref, ks_ref, vsT_ref, kw_ref, vwT_ref, gT_ref, ovT_ref, e_ref,
                gain_ref, o_ref, m_scr, acc_scr, comb_scr, qaug_scr, s0_scr, s1_scr, *, T):
    R, tq = NSA_GROUP, NSA_TQ
    i = pl.program_id(2)
    t0 = i * tq
    ncp = T // CMP_STRIDE
    nsb = T // SLC_BLOCK
    qT = jnp.concatenate([qT_ref[r, c] for r in range(R) for c in range(tq // CHUNK)], axis=1)
    gT = gT_ref[0]

    def per_head(row0):
        return jnp.concatenate([gT[row0 + 3 * r:row0 + 3 * r + 1, :] for r in range(R)], axis=1)

    def lanes_x_heads(a):
        return jnp.concatenate([a] * R, axis=1)

    s = jnp.dot(kc_ref[0, 0], qT, preferred_element_type=F32)
    c_idx = lax.broadcasted_iota(jnp.int32, (ncp, tq), 0)
    t_c = t0 + lax.broadcasted_iota(jnp.int32, (ncp, tq), 1)
    vis = lanes_x_heads((c_idx * CMP_STRIDE + (CMP_BLOCK - 1) <= t_c) & (c_idx < ncp - 1))
    s = jnp.where(vis, s, NEG)
    m = jnp.max(s, axis=0, keepdims=True)
    p = jnp.where(vis, jnp.exp2(s - m), 0.0)
    l = jnp.sum(p, axis=0, keepdims=True)
    p = p / jnp.where(l > 0.0, l, 1.0)
    o_cmp = jnp.dot(vcT_ref[0, 0], p.astype(BF16), preferred_element_type=F32)
    comb_scr[...] = o_cmp * per_head(0)

    psum = p[:, 0:tq]
    for r in range(1, R):
        psum = psum + p[:, r * tq:(r + 1) * tq]
    ov = ovT_ref[...]
    imp = jnp.zeros((nsb, tq), F32)
    for part in _split3(psum):
        imp = imp + jnp.dot(ov, part, preferred_element_type=F32)
    n_idx = lax.broadcasted_iota(jnp.int32, (nsb, tq), 0)
    cur = (t0 + lax.broadcasted_iota(jnp.int32, (nsb, tq), 1)) // SLC_BLOCK
    forced = (n_idx == 0) | (n_idx == cur) | (n_idx == cur - 1)
    valid = n_idx <= cur
    score = jnp.where(valid, imp + jnp.where(forced, FORCE_BONUS, 0.0), NEG)
    sel = _rank_select(score, min(SLC_TOPK, nsb)) & valid
    sel_past = sel & (n_idx * SLC_BLOCK < t0)
    bias = _pad_rows(jnp.where(sel_past, 0.0, NEG), HEAD_DIM).astype(BF16)
    bias_own = _pad_rows(jnp.where(sel, 0.0, NEG), HEAD_DIM).astype(BF16)
    for d in range(3):
        qaug_scr[d, 0:HEAD_DIM, :] = qT
    qaug_scr[0, HEAD_DIM:, :] = lanes_x_heads(bias)
    qaug_scr[1, HEAD_DIM:, :] = jnp.full((HEAD_DIM, R * tq), NEG, BF16)
    qaug_scr[2, HEAD_DIM:, :] = lanes_x_heads(bias_own)

    w0 = jnp.maximum(t0 + tq - WIN_SPAN, 0)
    k_w = kw_ref[0, pl.ds(pl.multiple_of(w0, CHUNK), WIN_SPAN), :]
    s_w = jnp.dot(k_w, qT, preferred_element_type=F32)
    kpos = w0 + lax.broadcasted_iota(jnp.int32, (WIN_SPAN, tq), 0)
    tpos = t0 + lax.broadcasted_iota(jnp.int32, (WIN_SPAN, tq), 1)
    ok = lanes_x_heads((kpos <= tpos) & (tpos - kpos < WINDOW))
    s_w = jnp.where(ok, s_w, NEG)
    m_w = jnp.max(s_w, axis=0, keepdims=True)
    p_w = jnp.exp2(s_w - m_w)
    jw = w0 // CHUNK
    v_w = jnp.concatenate([vwT_ref[0, jw + a] for a in range(WIN_SPAN // CHUNK)], axis=1)
    o_w = jnp.dot(_with_ones(v_w), p_w.astype(BF16), preferred_element_type=F32)
    comb_scr[...] += _normalized(o_w) * per_head(2)

    tri = lax.broadcasted_iota(jnp.int32, (tq, tq), 0) <= lax.broadcasted_iota(jnp.int32, (tq, tq), 1)
    s_d = jnp.where(lanes_x_heads(tri), _masked_scores(ks_ref.at[0], e_ref, t0, tq, qaug_scr[2]), NEG)
    _reset(m_scr, acc_scr)
    n_steps = (t0 + SLC_TKS - 1) // SLC_TKS
    _pipelined_attention([(ks_ref.at[0], e_ref, vsT_ref.at[0], qaug_scr, (s0_scr, s1_scr),
                           m_scr, acc_scr)], n_steps, T // SLC_TKS, SLC_TKS, CHUNK)
    own = tq // CHUNK
    v_d = jnp.concatenate([vsT_ref[0, i * own + a] for a in range(own)], axis=1)
    _online_step(s_d, v_d, m_scr, acc_scr)
    comb = comb_scr[...] + _normalized(acc_scr[...]) * per_head(1)

    for r in range(R):
        oT = comb[:, r * tq:(r + 1) * tq]
        ms = jnp.mean(oT * oT, axis=0, keepdims=True)
        on = (oT * lax.rsqrt(ms + EPS)).T * gain_ref[0, r:r + 1, :]
        o_ref[:, r * HEAD_DIM:(r + 1) * HEAD_DIM] = on.astype(BF16)


def _nsa(yT, ystd, kcvc, vcT, gT, ovT, e_slc, gains, B, T):
    G, R, tq = NSA_KV_HEADS, NSA_GROUP, NSA_TQ
    nq = T // tq
    ncp = T // CMP_STRIDE
    nsb = T // SLC_BLOCK
    kern = functools.partial(_nsa_kernel, T=T)
    keys = lambda head0: pl.BlockSpec((1, T, HEAD_DIM), lambda b, g, i: (head0 + g, b, 0))
    values = lambda head0: pl.BlockSpec((1, T // CHUNK, HEAD_DIM, CHUNK), lambda b, g, i: (head0 + g, b, 0, 0))
    return pl.pallas_call(
        kern,
        grid=(B, G, nq),
        in_specs=[pl.BlockSpec((R, tq // CHUNK, HEAD_DIM, CHUNK),
                               lambda b, g, i: (T_NSA_Q // R + g, b * nq + i, 0, 0)),
                  pl.BlockSpec((1, 1, ncp, HEAD_DIM), lambda b, g, i: (g, b, 0, 0)),
                  pl.BlockSpec((1, 1, HEAD_DIM, ncp), lambda b, g, i: (g, b, 0, 0)),
                  keys(S_KSLC), values(T_VSLC), keys(S_KWIN), values(T_VWIN),
                  pl.BlockSpec((1, GATE_ROWS, tq), lambda b, g, i: (g, 0, b * nq + i)),
                  pl.BlockSpec((nsb, ncp), lambda b, g, i: (0, 0)),
                  pl.BlockSpec((T, HEAD_DIM), lambda b, g, i: (0, 0)),
                  pl.BlockSpec((1, R, HEAD_DIM), lambda b, g, i: (g, 0, 0))],
        out_specs=pl.BlockSpec((tq, R * HEAD_DIM), lambda b, g, i: (b * nq + i, g)),
        out_shape=jax.ShapeDtypeStruct((B * T, NSA_HEADS * HEAD_DIM), BF16),
        scratch_shapes=[pltpu.VMEM((1, R * tq), F32),
                        pltpu.VMEM((HEAD_DIM + ONES_ROWS, R * tq), F32), pltpu.VMEM((HEAD_DIM, R * tq), F32),
                        pltpu.VMEM((3, 2 * HEAD_DIM, R * tq), BF16),
                        pltpu.VMEM((SLC_TKS, R * tq), F32), pltpu.VMEM((SLC_TKS, R * tq), F32)],
        compiler_params=_cparams(("parallel", "parallel", "arbitrary")),
        name="nsa",
    )(yT, kcvc, vcT, ystd, yT, ystd, yT, gT, ovT, e_slc, gains)


def _moba_kernel(qT_ref, k_ref, vT_ref, e_ref, gain_ref, o_ref, m_scr, acc_scr, kmean_scr, qaug_scr,
                 s_scr, *, T):
    tq, tk = MOBA_TQ, MOBA_TK
    nb = T // MOBA_BLOCK
    nbp = kmean_scr.shape[1]
    i = pl.program_id(2)
    t0 = i * tq

    @pl.when(i == 0)
    def _():
        kmean_scr[...] = jnp.zeros(kmean_scr.shape, F32)
        for a in range(MOBA_HB):
            kb = k_ref[a].astype(F32).reshape(nb, MOBA_BLOCK, HEAD_DIM)
            kmean_scr[a, 0:nb, :] = jnp.mean(kb, axis=1)

    n_idx = lax.broadcasted_iota(jnp.int32, (nbp, tq), 0)
    cur = (t0 + lax.broadcasted_iota(jnp.int32, (nbp, tq), 1)) // MOBA_BLOCK
    past = n_idx < cur
    causal = lax.broadcasted_iota(jnp.int32, (tk, tq), 0) <= lax.broadcasted_iota(jnp.int32, (tk, tq), 1)

    qTs = [jnp.concatenate([qT_ref[a, c] for c in range(tq // CHUNK)], axis=1) for a in range(MOBA_HB)]
    for a in range(MOBA_HB):
        qT = qTs[a]
        gate = jnp.zeros((nbp, tq), F32)
        for part in _split3(kmean_scr[a]):
            gate = gate + jnp.dot(part, qT, preferred_element_type=F32)
        sel = _rank_select(jnp.where(past, gate, NEG), min(MOBA_TOPK, nb)) & past
        for d in range(2):
            qaug_scr[a, d, 0:HEAD_DIM, :] = qT
        qaug_scr[a, 0, HEAD_DIM:, :] = _pad_rows(jnp.where(sel, 0.0, NEG), HEAD_DIM).astype(BF16)
        qaug_scr[a, 1, HEAD_DIM:, :] = jnp.full((HEAD_DIM, tq), NEG, BF16)

    s_own = []
    for a in range(MOBA_HB):
        _reset(m_scr.at[a], acc_scr.at[a])
        k_own = k_ref[a, pl.ds(pl.multiple_of(t0, tk), tk), :]
        s_own.append(jnp.where(causal, jnp.dot(k_own, qTs[a], preferred_element_type=F32), NEG))

    chains = [(k_ref.at[a], e_ref, vT_ref.at[a], qaug_scr.at[a], (s_scr.at[a, 0], s_scr.at[a, 1]),
               m_scr.at[a], acc_scr.at[a]) for a in range(MOBA_HB)]
    n_steps = (t0 + MOBA_TKS - 1) // MOBA_TKS
    _pipelined_attention(chains, n_steps, T // MOBA_TKS, MOBA_TKS, CHUNK)

    own = tq // CHUNK
    for a in range(MOBA_HB):
        v_own = jnp.concatenate([vT_ref[a, i * own + c] for c in range(own)], axis=1)
        _online_step(s_own[a], v_own, m_scr.at[a], acc_scr.at[a])
        oT = _normalized(acc_scr[a])
        ms = jnp.mean(oT * oT, axis=0, keepdims=True)
        on = (oT * lax.rsqrt(ms + EPS)).T * gain_ref[a]
        o_ref[:, a * HEAD_DIM:(a + 1) * HEAD_DIM] = on.astype(BF16)


def _moba(yT, ystd, e_moba, gains, B, T):
    H, HB, tq = MOBA_HEADS, MOBA_HB, MOBA_TQ
    nq = T // tq
    nb = T // MOBA_BLOCK
    nbp = max(16, nb)
    kern = functools.partial(_moba_kernel, T=T)
    return pl.pallas_call(
        kern,
        grid=(B, H // HB, nq),
        in_specs=[pl.BlockSpec((HB, tq // CHUNK, HEAD_DIM, CHUNK),
                               lambda b, h, i: (T_MOBA_Q // HB + h, b * nq + i, 0, 0)),
                  pl.BlockSpec((HB, T, HEAD_DIM), lambda b, h, i: (S_MOBA_K // HB + h, b, 0)),
                  pl.BlockSpec((HB, T // CHUNK, HEAD_DIM, CHUNK), lambda b, h, i: (T_MOBA_V // HB + h, b, 0, 0)),
                  pl.BlockSpec((T, HEAD_DIM), lambda b, h, i: (0, 0)),
                  pl.BlockSpec((HB, 1, HEAD_DIM), lambda b, h, i: (h, 0, 0))],
        out_specs=pl.BlockSpec((tq, HB * HEAD_DIM), lambda b, h, i: (b * nq + i, h)),
        out_shape=jax.ShapeDtypeStruct((B * T, H * HEAD_DIM), BF16),
        scratch_shapes=[pltpu.VMEM((HB, 1, tq), F32),
                        pltpu.VMEM((HB, HEAD_DIM + ONES_ROWS, tq), F32), pltpu.VMEM((HB, nbp, HEAD_DIM), F32),
                        pltpu.VMEM((HB, 2, 2 * HEAD_DIM, tq), BF16),
                        pltpu.VMEM((HB, 2, MOBA_TKS, tq), F32)],
        compiler_params=_cparams(("parallel", "parallel", "arbitrary")),
        name="moba",
    )(yT, ystd, yT, e_moba, gains)


def _outproj_kernel(on_ref, om_ref, w_ref, x_ref, mod_ref, o_ref):
    half = on_ref.shape[1]
    acc = jnp.dot(on_ref[...], w_ref[0:half, :], preferred_element_type=F32)
    acc = acc + jnp.dot(om_ref[...], w_ref[half:, :], preferred_element_type=F32)
    o_ref[...] = x_ref[...] + mod_ref[0, 2:3, :] * acc


def _out_proj(o_nsa, o_moba, w_out, x2, mod3, T):
    BT, D = x2.shape
    tm = min(512, T)
    tpb = T // tm
    half = o_nsa.shape[1]
    return pl.pallas_call(
        _outproj_kernel,
        grid=(BT // tm,),
        in_specs=[pl.BlockSpec((tm, half), lambda i: (i, 0)),
                  pl.BlockSpec((tm, half), lambda i: (i, 0)),
                  pl.BlockSpec((D, D), lambda i: (0, 0)),
                  pl.BlockSpec((tm, D), lambda i: (i, 0)),
                  pl.BlockSpec((1, 6, D), lambda i: (i // tpb, 0, 0))],
        out_specs=pl.BlockSpec((tm, D), lambda i: (i, 0)),
        out_shape=jax.ShapeDtypeStruct((BT, D), F32),
        compiler_params=_cparams(("parallel",)),
        name="out_proj",
    )(o_nsa, o_moba, w_out, x2, mod3)


def _ffn_kernel(x_ref, mod_ref, wg_ref, wu_ref, wo_ref, o_ref, h_scr):
    j = pl.program_id(1)

    @pl.when(j == 0)
    def _():
        x = x_ref[...]
        ms = jnp.mean(x * x, axis=-1, keepdims=True)
        h = x * lax.rsqrt(ms + EPS) * (1.0 + mod_ref[0, 4:5, :]) + mod_ref[0, 3:4, :]
        h_scr[...] = h.astype(BF16)
        o_ref[...] = jnp.zeros(o_ref.shape, F32)

    hb = h_scr[...]
    gate = jnp.dot(hb, wg_ref[...], preferred_element_type=F32)
    up = jnp.dot(hb, wu_ref[...], preferred_element_type=F32)
    act = (gate * jax.nn.sigmoid(gate) * up).astype(BF16)
    o_ref[...] += jnp.dot(act, wo_ref[...], preferred_element_type=F32)

    @pl.when(j == pl.num_programs(1) - 1)
    def _():
        o_ref[...] = x_ref[...] + mod_ref[0, 5:6, :] * o_ref[...]


def _ffn(x1, mod3, w_in, w_out, T):
    BT, D = x1.shape
    Fh = w_out.shape[0]
    tm = min(512, T)
    tf = 512
    tpb = T // tm
    nf = Fh // tf
    return pl.pallas_call(
        _ffn_kernel,
        grid=(BT // tm, nf),
        in_specs=[pl.BlockSpec((tm, D), lambda i, j: (i, 0)),
                  pl.BlockSpec((1, 6, D), lambda i, j: (i // tpb, 0, 0)),
                  pl.BlockSpec((D, tf), lambda i, j: (0, j)),
                  pl.BlockSpec((D, tf), lambda i, j: (0, nf + j)),
                  pl.BlockSpec((tf, D), lambda i, j: (j, 0))],
        out_specs=pl.BlockSpec((tm, D), lambda i, j: (i, 0)),
        out_shape=jax.ShapeDtypeStruct((BT, D), F32),
        scratch_shapes=[pltpu.VMEM((tm, D), BF16)],
        compiler_params=_cparams(("parallel", "arbitrary")),
        name="ffn",
    )(x1, mod3, w_in, w_in, w_out)


def _rope_lane_tables(T):
    inv = ROPE_THETA ** (-jnp.arange(0, ROPE_DIMS, 2, dtype=F32) / ROPE_DIMS)
    ang = jnp.arange(T).astype(F32)[:, None] * inv[None, :]
    cos, sin = jnp.cos(ang), jnp.sin(ang)
    rest = HEAD_DIM - ROPE_DIMS
    cc = jnp.concatenate([cos, cos, jnp.ones((T, rest), F32)], axis=1)
    sa = jnp.concatenate([-sin, jnp.zeros((T, HEAD_DIM - ROPE_HALF), F32)], axis=1)
    sb = jnp.concatenate([jnp.zeros((T, ROPE_HALF), F32), sin, jnp.zeros((T, rest), F32)], axis=1)
    return cc, sa, sb


def _block_onehot(T, block):
    return (jnp.arange(T)[:, None] // block == jnp.arange(HEAD_DIM)[None, :]).astype(BF16)


def _layer(x2, c, B, T, w_ada, b_ada, w_in, nsa_q_norm, nsa_k_norm, moba_q_norm, moba_k_norm,
           cmp_pe_k, cmp_w1_k, cmp_w2_k, cmp_pe_v, cmp_w1_v, cmp_w2_v, out_norm, w_out,
           w_ffn_in, w_ffn_out):
    D = x2.shape[1]
    G = NSA_KV_HEADS
    scale = HEAD_DIM ** -0.5 * LOG2E
    assert T % MOBA_BLOCK == 0 and T % SLC_TKS == 0 and T % MOBA_TKS == 0 and T % NSA_TQ == 0 and T >= WIN_SPAN
    assert T // SLC_BLOCK <= HEAD_DIM and T // MOBA_BLOCK <= HEAD_DIM

    mod3 = _adaln(c, w_ada, b_ada)

    qw, kvw, gw, mw = NSA_HEADS * HEAD_DIM, G * HEAD_DIM, NSA_HEADS * 3, MOBA_HEADS * HEAD_DIM
    offs = [0, qw] + [qw + kvw * n for n in range(1, 7)]
    col = lambda a, n: w_in[:, a:a + n]
    o_q, o_kc, o_vc, o_ks, o_vs, o_kw, o_vw, o_g = offs
    o_mq = o_g + gw
    w_main = jnp.concatenate(
        [col(o_q, qw), col(o_mq, mw), col(o_vs, kvw), col(o_vw, kvw), col(o_mq + 2 * mw, mw),
         col(o_ks, kvw), col(o_kw, kvw), col(o_mq + mw, mw), col(o_kc, kvw), col(o_vc, kvw)],
        axis=1).astype(BF16)
    wg = col(o_g, gw).reshape(D, G, NSA_GROUP * 3)
    w_gate = jnp.pad(wg, ((0, 0), (0, 0), (0, GATE_LANES - NSA_GROUP * 3))).reshape(D, G * GATE_LANES)
    w_gate = w_gate.astype(BF16)
    rep = lambda g_, n: jnp.broadcast_to(g_, (n, HEAD_DIM))
    plain = lambda n: jnp.ones((n, HEAD_DIM), F32)
    gains = jnp.concatenate(
        [rep(nsa_q_norm * scale, NSA_HEADS), rep(moba_q_norm * scale, MOBA_HEADS), plain(2 * G), plain(MOBA_HEADS),
         rep(nsa_k_norm[1], G), rep(nsa_k_norm[2], G), rep(moba_k_norm, MOBA_HEADS), plain(2 * G)], axis=0)
    gains = gains.reshape(len(PROJ_TILES), HEADS_PER_TILE, HEAD_DIM)
    cc, sa, sb = _rope_lane_tables(T)

    yT, ystd, hc, gT = _in_proj(x2, mod3, w_main, w_gate, gains, cc, sa, sb, T)

    ncp = T // CMP_STRIDE
    half = CMP_STRIDE * HEAD_DIM
    w1 = jnp.stack([cmp_w1_k, cmp_w1_v]).astype(BF16)
    w1cat = jnp.concatenate([w1[:, :half], w1[:, half:]], axis=2)
    pe8 = jnp.broadcast_to(jnp.stack([cmp_pe_k, cmp_pe_v]).reshape(2, 1, 2 * half), (2, 8, 2 * half))
    w2 = jnp.stack([cmp_w2_k, cmp_w2_v]).astype(BF16)
    cmp_rows = lambda t: jnp.pad(t[CMP_BLOCK - 1::CMP_STRIDE], ((0, 1), (0, 0)))
    kcvc = _compress(hc.reshape(2 * G, B, ncp, half), w1cat, pe8.astype(BF16), w1, w2,
                     nsa_k_norm[0].reshape(1, HEAD_DIM),
                     cmp_rows(cc), cmp_rows(sa), cmp_rows(sb))
    vcT = kcvc[G:].transpose(0, 1, 3, 2)

    nsb = T // SLC_BLOCK
    cs = jnp.arange(ncp)[None, :] * CMP_STRIDE
    sbk = jnp.arange(nsb)[:, None] * SLC_BLOCK
    ovT = ((cs < sbk + SLC_BLOCK) & (cs + CMP_BLOCK > sbk) & (jnp.arange(ncp)[None, :] < ncp - 1)).astype(BF16)

    on = out_norm.reshape(N_HEADS, HEAD_DIM)
    o_nsa = _nsa(yT, ystd, kcvc, vcT, gT, ovT, _block_onehot(T, SLC_BLOCK),
                 on[:NSA_HEADS].reshape(G, NSA_GROUP, HEAD_DIM), B, T)
    o_moba = _moba(yT, ystd, _block_onehot(T, MOBA_BLOCK),
                   on[NSA_HEADS:].reshape(MOBA_HEADS, 1, HEAD_DIM), B, T)

    x1 = _out_proj(o_nsa, o_moba, w_out.astype(BF16), x2, mod3, T)
    return _ffn(x1, mod3, w_ffn_in.astype(BF16), w_ffn_out.astype(BF16), T)


def kernel(x, c, w_ada, b_ada, w_in, nsa_q_norm, nsa_k_norm, moba_q_norm, moba_k_norm, cmp_pe_k, cmp_w1_k, cmp_w2_k, cmp_pe_v, cmp_w1_v, cmp_w2_v, out_norm, w_out, w_ffn_in, w_ffn_out):
    B, T, D = x.shape
    x2 = x.reshape(B * T, D)
    for l in range(w_ada.shape[0]):
        x2 = _layer(x2, c, B, T, w_ada[l], b_ada[l], w_in[l], nsa_q_norm[l], nsa_k_norm[l],
                    moba_q_norm[l], moba_k_norm[l], cmp_pe_k[l], cmp_w1_k[l], cmp_w2_k[l],
                    cmp_pe_v[l], cmp_w1_v[l], cmp_w2_v[l], out_norm[l], w_out[l],
                    w_ffn_in[l], w_ffn_out[l])
    return x2.reshape(B, T, D)
```

```python
import functools

import jax
import jax.numpy as jnp
from jax import lax
from jax.experimental import pallas as pl
from jax.experimental.pallas import tpu as pltpu

F32 = jnp.float32
BF16 = jnp.bfloat16

HEAD_DIM = 128
SUBLANES = 8
NSA_HEADS = 8
NSA_KV_HEADS = 2
NSA_GROUP = NSA_HEADS // NSA_KV_HEADS
MOBA_HEADS = 8
N_HEADS = NSA_HEADS + MOBA_HEADS
CMP_BLOCK = 32
CMP_STRIDE = 16
SLC_BLOCK = 64
SLC_TOPK = 16
WINDOW = 512
FORCE_BONUS = 1e4
MOBA_BLOCK = 256
MOBA_TOPK = 3
ROPE_THETA = 500000.0
ROPE_DIMS = HEAD_DIM // 4
ROPE_HALF = ROPE_DIMS // 2
EPS = 1e-6
LOG2E = 1.4426950408889634
ONES_ROWS = 16
NEG = -1e30
M_INIT = -1e29

V7X_VMEM_BYTES = 64 * 1024 * 1024
VMEM_LIMIT = V7X_VMEM_BYTES - 8 * 1024 * 1024

HEADS_PER_TILE = 4
PROJ_TILES = ("nsa_q", "nsa_q", "kv_cmp", "kv_slc", "kv_win",
              "moba_q", "moba_q", "moba_k", "moba_k", "moba_v", "moba_v")
N_NSA_TILES = 5
YT_ADVANCE = (1, 3, 5, 6, 9, 10)
YSTD_ADVANCE = (7, 8)
T_NSA_Q, T_VSLC, T_VWIN, T_MOBA_Q, T_MOBA_V = 0, 8, 10, 12, 20
S_KSLC, S_KWIN, S_MOBA_K = 0, 2, 4
N_T_HEADS, N_STD_HEADS, N_CMP_HEADS = 28, 12, 4
CHUNK = 128
GATE_LANES = 128
GATE_ROWS = 16

NSA_TQ = 256
SLC_TKS = 512
WIN_SPAN = WINDOW + NSA_TQ
MOBA_TQ = 256
MOBA_TK = MOBA_BLOCK
MOBA_TKS = 512
MOBA_HB = 4


def _cparams(sem):
    return pltpu.CompilerParams(dimension_semantics=sem, vmem_limit_bytes=VMEM_LIMIT)


def _split3(a):
    hi = a.astype(BF16)
    r1 = a - hi.astype(F32)
    mid = r1.astype(BF16)
    lo = (r1 - mid.astype(F32)).astype(BF16)
    return hi, mid, lo


def _adaln_kernel(c_ref, w_ref, b_ref, o_ref):
    cv = c_ref[...]
    s = cv * jax.nn.sigmoid(cv)
    w = w_ref[...].astype(BF16)
    acc = b_ref[...] + jnp.zeros(o_ref.shape, F32)
    for part in _split3(s)[:2]:
        acc = acc + jnp.dot(part, w, preferred_element_type=F32)
    o_ref[...] = acc


def _adaln(c, w_ada, b_ada):
    B, D = c.shape
    N = w_ada.shape[1]
    tn = 1024
    c8 = jnp.zeros((8, D), F32).at[:B].set(c)
    out = pl.pallas_call(
        _adaln_kernel,
        grid=(N // tn,),
        in_specs=[pl.BlockSpec((8, D), lambda j: (0, 0)),
                  pl.BlockSpec((D, tn), lambda j: (0, j)),
                  pl.BlockSpec((1, tn), lambda j: (0, j))],
        out_specs=pl.BlockSpec((8, tn), lambda j: (0, j)),
        out_shape=jax.ShapeDtypeStruct((8, N), F32),
        compiler_params=_cparams(("arbitrary",)),
        name="adaln",
    )(c8, w_ada, b_ada.reshape(1, N))
    return out[:B].reshape(B, 6, D)


def _rope(y, cc, sa, sb):
    return (y * cc + pltpu.roll(y, HEAD_DIM - ROPE_HALF, 1) * sa + pltpu.roll(y, ROPE_HALF, 1) * sb)


def _inproj_kernel(x_ref, mod_ref, wa_ref, wb_ref, wg_ref, gain_ref, cc_ref, sa_ref, sb_ref,
                   yT_ref, ystd_ref, hc_ref, gT_ref, h_scr, acc_scr, rows_scr):
    j = pl.program_id(1)
    tm = x_ref.shape[0]
    n_chunks = tm // CHUNK

    @pl.when(j == 0)
    def _():
        x = x_ref[...]
        ms = jnp.mean(x * x, axis=-1, keepdims=True)
        h = x * lax.rsqrt(ms + EPS) * (1.0 + mod_ref[0, 1:2, :]) + mod_ref[0, 0:1, :]
        hb = h.astype(BF16)
        h_scr[...] = hb
        g = jax.nn.sigmoid(jnp.dot(hb, wg_ref[...], preferred_element_type=F32))
        for grp in range(NSA_KV_HEADS):
            for c in range(n_chunks):
                blk = g[c * CHUNK:(c + 1) * CHUNK, grp * GATE_LANES:(grp + 1) * GATE_LANES].T
                gT_ref[grp, :, c * CHUNK:(c + 1) * CHUNK] = blk[0:GATE_ROWS, :]

    @pl.when(j < N_NSA_TILES)
    def _():
        acc_scr[...] = jnp.dot(h_scr[...], wa_ref[...], preferred_element_type=F32)

    @pl.when(j >= N_NSA_TILES)
    def _():
        acc_scr[...] = jnp.dot(h_scr[...], wb_ref[...], preferred_element_type=F32)

    def head(hh, treated):
        yh = acc_scr[:, hh * HEAD_DIM:(hh + 1) * HEAD_DIM]
        if not treated:
            return yh
        ms = jnp.mean(yh * yh, axis=-1, keepdims=True)
        yn = yh * lax.rsqrt(ms + EPS) * gain_ref[0, hh:hh + 1, :]
        return _rope(yn, cc_ref[...], sa_ref[...], sb_ref[...])

    def to_yT(hh, slot, treated):
        yh = head(hh, treated)
        for c in range(n_chunks):
            yT_ref[slot, c] = yh[c * CHUNK:(c + 1) * CHUNK, :].T.astype(BF16)

    def to_ystd(hh, slot):
        ystd_ref[slot] = head(hh, True).astype(BF16)

    def when_kind(kind):
        steps = [t for t, name in enumerate(PROJ_TILES) if name == kind]
        return pl.when(functools.reduce(jnp.logical_or, [j == t for t in steps]))

    for kind in ("nsa_q", "moba_q"):
        @when_kind(kind)
        def _():
            for hh in range(HEADS_PER_TILE):
                to_yT(hh, hh, True)

    @when_kind("moba_v")
    def _():
        for hh in range(HEADS_PER_TILE):
            to_yT(hh, hh, False)

    @when_kind("moba_k")
    def _():
        for hh in range(HEADS_PER_TILE):
            to_ystd(hh, hh)

    for kind, base in (("kv_slc", 0), ("kv_win", NSA_KV_HEADS)):
        @when_kind(kind)
        def _(base=base):
            for g in range(NSA_KV_HEADS):
                to_ystd(g, base + g)
                to_yT(NSA_KV_HEADS + g, base + g, False)

    @when_kind("kv_cmp")
    def _():
        for hh in range(HEADS_PER_TILE):
            rows_scr[...] = head(hh, False)
            flat = [rows_scr[pl.ds(l, tm // CMP_STRIDE, stride=CMP_STRIDE), :] for l in range(CMP_STRIDE)]
            hc_ref[hh] = jnp.concatenate(flat, axis=1).astype(BF16)


def _in_proj(x2, mod3, w_nsa, w_moba, w_gate, gains, cc, sa, sb, T):
    BT, D = x2.shape
    G = NSA_KV_HEADS
    tm = min(1024, T)
    tn = HEADS_PER_TILE * HEAD_DIM
    tpb = T // tm
    advance = lambda j, steps: sum((j >= t).astype(jnp.int32) for t in steps)
    return pl.pallas_call(
        _inproj_kernel,
        grid=(BT // tm, len(PROJ_TILES)),
        in_specs=[pl.BlockSpec((tm, D), lambda i, j: (i, 0)),
                  pl.BlockSpec((1, 6, D), lambda i, j: (i // tpb, 0, 0)),
                  pl.BlockSpec((D, tn), lambda i, j: (0, jnp.minimum(j, N_NSA_TILES - 1))),
                  pl.BlockSpec((D, tn), lambda i, j: (0, jnp.maximum(j - N_NSA_TILES, 0))),
                  pl.BlockSpec((D, G * GATE_LANES), lambda i, j: (0, 0)),
                  pl.BlockSpec((1, HEADS_PER_TILE, HEAD_DIM), lambda i, j: (j, 0, 0)),
                  pl.BlockSpec((tm, HEAD_DIM), lambda i, j: (i % tpb, 0)),
                  pl.BlockSpec((tm, HEAD_DIM), lambda i, j: (i % tpb, 0)),
                  pl.BlockSpec((tm, HEAD_DIM), lambda i, j: (i % tpb, 0))],
        out_specs=[pl.BlockSpec((HEADS_PER_TILE, tm // CHUNK, HEAD_DIM, CHUNK),
                                lambda i, j: (advance(j, YT_ADVANCE), i, 0, 0)),
                   pl.BlockSpec((HEADS_PER_TILE, tm, HEAD_DIM), lambda i, j: (advance(j, YSTD_ADVANCE), i, 0)),
                   pl.BlockSpec((N_CMP_HEADS, tm // CMP_STRIDE, CMP_STRIDE * HEAD_DIM), lambda i, j: (0, i, 0)),
                   pl.BlockSpec((G, GATE_ROWS, tm), lambda i, j: (0, 0, i))],
        out_shape=[jax.ShapeDtypeStruct((N_T_HEADS, BT // CHUNK, HEAD_DIM, CHUNK), BF16),
                   jax.ShapeDtypeStruct((N_STD_HEADS, BT, HEAD_DIM), BF16),
                   jax.ShapeDtypeStruct((N_CMP_HEADS, BT // CMP_STRIDE, CMP_STRIDE * HEAD_DIM), BF16),
                   jax.ShapeDtypeStruct((G, GATE_ROWS, BT), F32)],
        scratch_shapes=[pltpu.VMEM((tm, D), BF16), pltpu.VMEM((tm, tn), F32), pltpu.VMEM((tm, HEAD_DIM), F32)],
        compiler_params=_cparams(("parallel", "arbitrary")),
        name="in_proj",
    )(x2, mod3, w_nsa, w_moba, w_gate, gains, cc, sa, sb)


def _compress_kernel(h_ref, w1c_ref, pe_ref, w1_ref, w2_ref, gain_ref, cc_ref, sa_ref, sb_ref, o_ref):
    a = pl.program_id(0)
    ncp = h_ref.shape[2]
    z = jnp.dot(h_ref[0, 0], w1c_ref[0], preferred_element_type=F32)
    top = z[:, :HEAD_DIM]
    bot = pltpu.roll(z[:, HEAD_DIM:], ncp - 1, 0)
    pe_term = jnp.dot(pe_ref[0], w1_ref[0], preferred_element_type=F32)[0:1, :]
    pre = top + bot + pe_term
    act = pre * jax.nn.sigmoid(pre)
    out = jnp.dot(act.astype(BF16), w2_ref[0], preferred_element_type=F32)
    live = lax.broadcasted_iota(jnp.int32, out.shape, 0) < ncp - 1
    out = jnp.where(live, out, 0.0)

    @pl.when(a < NSA_KV_HEADS)
    def _():
        ms = jnp.mean(out * out, axis=-1, keepdims=True)
        yn = out * lax.rsqrt(ms + EPS) * gain_ref[...]
        o_ref[0, 0] = _rope(yn, cc_ref[...], sa_ref[...], sb_ref[...]).astype(BF16)

    @pl.when(a >= NSA_KV_HEADS)
    def _():
        o_ref[0, 0] = out.astype(BF16)


def _compress(hc, w1cat, pe8, w1, w2, gain, cc, sa, sb):
    A, B, ncp, K = hc.shape
    G = NSA_KV_HEADS
    return pl.pallas_call(
        _compress_kernel,
        grid=(A, B),
        in_specs=[pl.BlockSpec((1, 1, ncp, K), lambda a, b: (a, b, 0, 0)),
                  pl.BlockSpec((1, K, 2 * HEAD_DIM), lambda a, b: (a // G, 0, 0)),
                  pl.BlockSpec((1, 8, 2 * K), lambda a, b: (a // G, 0, 0)),
                  pl.BlockSpec((1, 2 * K, HEAD_DIM), lambda a, b: (a // G, 0, 0)),
                  pl.BlockSpec((1, HEAD_DIM, HEAD_DIM), lambda a, b: (a // G, 0, 0)),
                  pl.BlockSpec((1, HEAD_DIM), lambda a, b: (0, 0)),
                  pl.BlockSpec((ncp, HEAD_DIM), lambda a, b: (0, 0)),
                  pl.BlockSpec((ncp, HEAD_DIM), lambda a, b: (0, 0)),
                  pl.BlockSpec((ncp, HEAD_DIM), lambda a, b: (0, 0))],
        out_specs=pl.BlockSpec((1, 1, ncp, HEAD_DIM), lambda a, b: (a, b, 0, 0)),
        out_shape=jax.ShapeDtypeStruct((A, B, ncp, HEAD_DIM), BF16),
        compiler_params=_cparams(("arbitrary", "arbitrary")),
        name="compress",
    )(hc, w1cat, pe8, w1, w2, gain, cc, sa, sb)


def _reset(m_scr, acc_scr):
    m_scr[...] = jnp.full(m_scr.shape, M_INIT, F32)
    acc_scr[...] = jnp.zeros(acc_scr.shape, F32)


def _with_ones(vT):
    return jnp.concatenate([vT, jnp.ones((ONES_ROWS, vT.shape[1]), vT.dtype)], axis=0)


def _normalized(acc):
    return acc[0:HEAD_DIM] / acc[HEAD_DIM:HEAD_DIM + 1]


def _online_step(s, vT, m_scr, acc_scr):
    m_prev = m_scr[...]
    m_new = jnp.maximum(m_prev, jnp.max(s, axis=0, keepdims=True))
    alpha = jnp.exp2(m_prev - m_new)
    p = jnp.exp2(s - m_new)
    acc_scr[...] = alpha * acc_scr[...] + jnp.dot(_with_ones(vT), p.astype(BF16), preferred_element_type=F32)
    m_scr[...] = m_new


def _rank_select(score, k):
    n = score.shape[0]
    rank = jnp.zeros(score.shape, F32)
    for m in range(n):
        sm = score[m:m + 1, :]
        lo = (m // SUBLANES) * SUBLANES
        hi = min(lo + SUBLANES, n)
        parts = []
        if lo > 0:
            parts.append(jnp.where(sm > score[:lo], 1.0, 0.0))
        gt =jnp.where(sm > score[lo:hi], 1.0, 0.0)
        ge = jnp.where(sm >= score[lo:hi], 1.0, 0.0)
        below = lax.broadcasted_iota(jnp.int32, gt.shape, 0) > m - lo
        parts.append(jnp.where(below, ge, gt))
        if hi < n:
            parts.append(jnp.where(sm >= score[hi:], 1.0, 0.0))
        rank = rank + (jnp.concatenate(parts, axis=0) if len(parts) > 1 else parts[0])
    return rank < k


def _pad_rows(a, rows):
    return jnp.concatenate([a, jnp.zeros((rows - a.shape[0], a.shape[1]), a.dtype)], axis=0)


def _masked_scores(k_ref, e_ref, start, size, q_aug):
    rows = pl.ds(pl.multiple_of(start, 128), size)
    k_aug = jnp.concatenate([k_ref[rows, :], e_ref[rows, :]], axis=1)
    return jnp.dot(k_aug, q_aug, preferred_element_type=F32)


def _pipelined_attention(chains, n_steps, n_max, tks, tk):
    per = tks // tk

    def scores(step, buf):
        st = jnp.minimum(step, n_max - 1)
        dead = jnp.where(step < n_steps, 0, 1)
        for k_ref, e_ref, _, qaug_ref, bufs, _, _ in chains:
            bufs[buf][...] = _masked_scores(k_ref, e_ref, st * tks, tks, qaug_ref[dead])

    def consume(step, buf):
        st = jnp.minimum(step, n_max - 1)
        for _, _, vT_ref, _, bufs, m_scr, acc_scr in chains:
            vT = jnp.concatenate([vT_ref[st * per + a] for a in range(per)], axis=1)
            _online_step(bufs[buf][...], vT, m_scr, acc_scr)

    def pair(p, carry):
        scores(2 * p + 1, 1)
        consume(2 * p, 0)
        scores(2 * p + 2, 0)
        consume(2 * p + 1, 1)
        return carry

    n_pairs = (n_steps + 1) // 2

    @pl.when(n_pairs > 0)
    def _():
        scores(0, 0)

    lax.fori_loop(0, n_pairs, pair, 0)


def _nsa_kernel(qT_ref, kc_ref, vcT_ref, ks_ref, vsT_ref, kw_ref, vwT_ref, gT_ref, ovT_ref, e_ref,
                gain_ref, o_ref, m_scr, acc_scr, comb_scr, qaug_scr, s0_scr, s1_scr, *, T):
    R, tq = NSA_GROUP, NSA_TQ
    i = pl.program_id(2)
    t0 = i * tq
    ncp = T // CMP_STRIDE
    nsb = T // SLC_BLOCK
    qT = jnp.concatenate([qT_ref[r, c] for r in range(R) for c in range(tq // CHUNK)], axis=1)
    gT = gT_ref[0]

    def per_head(row0):
        return jnp.concatenate([gT[row0 + 3 * r:row0 + 3 * r + 1, :] for r in range(R)], axis=1)

    def lanes_x_heads(a):
        return jnp.concatenate([a] * R, axis=1)

    s = jnp.dot(kc_ref[0, 0], qT, preferred_element_type=F32)
    c_idx = lax.broadcasted_iota(jnp.int32, (ncp, tq), 0)
    t_c = t0 + lax.broadcasted_iota(jnp.int32, (ncp, tq), 1)
    vis = lanes_x_heads((c_idx * CMP_STRIDE + (CMP_BLOCK - 1) <= t_c) & (c_idx < ncp - 1))
    s = jnp.where(vis, s, NEG)
    m = jnp.max(s, axis=0, keepdims=True)
    p = jnp.where(vis, jnp.exp2(s - m), 0.0)
    l = jnp.sum(p, axis=0, keepdims=True)
    p = p / jnp.where(l > 0.0, l, 1.0)
    o_cmp = jnp.dot(vcT_ref[0, 0], p.astype(BF16), preferred_element_type=F32)
    comb_scr[...] = o_cmp * per_head(0)

    psum = p[:, 0:tq]
    for r in range(1, R):
        psum = psum + p[:, r * tq:(r + 1) * tq]
    ov = ovT_ref[...]
    imp = jnp.zeros((nsb, tq), F32)
    for part in _split3(psum):
        imp = imp + jnp.dot(ov, part, preferred_element_type=F32)
    n_idx = lax.broadcasted_iota(jnp.int32, (nsb, tq), 0)
    cur = (t0 + lax.broadcasted_iota(jnp.int32, (nsb, tq), 1)) // SLC_BLOCK
    forced = (n_idx == 0) | (n_idx == cur) | (n_idx == cur - 1)
    valid = n_idx <= cur
    score = jnp.where(valid, imp + jnp.where(forced, FORCE_BONUS, 0.0), NEG)
    sel = _rank_select(score, min(SLC_TOPK, nsb)) & valid
    sel_past = sel & (n_idx * SLC_BLOCK < t0)
    bias = _pad_rows(jnp.where(sel_past, 0.0, NEG), HEAD_DIM).astype(BF16)
    bias_own = _pad_rows(jnp.where(sel, 0.0, NEG), HEAD_DIM).astype(BF16)
    for d in range(3):
        qaug_scr[d, 0:HEAD_DIM, :] = qT
    qaug_scr[0, HEAD_DIM:, :] = lanes_x_heads(bias)
    qaug_scr[1, HEAD_DIM:, :] = jnp.full((HEAD_DIM, R * tq), NEG, BF16)
    qaug_scr[2, HEAD_DIM:, :] = lanes_x_heads(bias_own)

    w0 = jnp.maximum(t0 + tq - WIN_SPAN, 0)
    k_w = kw_ref[0, pl.ds(pl.multiple_of(w0, CHUNK), WIN_SPAN), :]
    s_w = jnp.dot(k_w, qT, preferred_element_type=F32)
    kpos = w0 + lax.broadcasted_iota(jnp.int32, (WIN_SPAN, tq), 0)
    tpos = t0 + lax.broadcasted_iota(jnp.int32, (WIN_SPAN, tq), 1)
    ok = lanes_x_heads((kpos <= tpos) & (tpos - kpos < WINDOW))
    s_w = jnp.where(ok, s_w, NEG)
    m_w = jnp.max(s_w, axis=0, keepdims=True)
    p_w = jnp.exp2(s_w - m_w)
    jw = w0 // CHUNK
    v_w = jnp.concatenate([vwT_ref[0, jw + a] for a in range(WIN_SPAN // CHUNK)], axis=1)
    o_w = jnp.dot(_with_ones(v_w), p_w.astype(BF16), preferred_element_type=F32)
    comb_scr[...] += _normalized(o_w) * per_head(2)

    tri = lax.broadcasted_iota(jnp.int32, (tq, tq), 0) <= lax.broadcasted_iota(jnp.int32, (tq, tq), 1)
    s_d = jnp.where(lanes_x_heads(tri), _masked_scores(ks_ref.at[0], e_ref, t0, tq, qaug_scr[2]), NEG)
    _reset(m_scr, acc_scr)
    n_steps = (t0 + SLC_TKS - 1) // SLC_TKS
    _pipelined_attention([(ks_ref.at[0], e_ref, vsT_ref.at[0], qaug_scr, (s0_scr, s1_scr),
                           m_scr, acc_scr)], n_steps, T // SLC_TKS, SLC_TKS, CHUNK)
    own = tq // CHUNK
    v_d = jnp.concatenate([vsT_ref[0, i * own + a] for a in range(own)], axis=1)
    _online_step(s_d, v_d, m_scr, acc_scr)
    comb = comb_scr[...] + _normalized(acc_scr[...]) * per_head(1)

    for r in range(R):
        oT = comb[:, r * tq:(r + 1) * tq]
        ms = jnp.mean(oT * oT, axis=0, keepdims=True)
        on = (oT * lax.rsqrt(ms + EPS)).T * gain_ref[0, r:r + 1, :]
        o_ref[:, r * HEAD_DIM:(r + 1) * HEAD_DIM] = on.astype(BF16)


def _nsa(yT, ystd, kcvc, vcT, gT, ovT, e_slc, gains, B, T):
    G, R, tq = NSA_KV_HEADS, NSA_GROUP, NSA_TQ
    nq = T // tq
    ncp = T // CMP_STRIDE
    nsb = T // SLC_BLOCK
    kern = functools.partial(_nsa_kernel, T=T)
    keys = lambda head0: pl.BlockSpec((1, T, HEAD_DIM), lambda b, g, i: (head0 + g, b, 0))
    values = lambda head0: pl.BlockSpec((1, T // CHUNK, HEAD_DIM, CHUNK), lambda b, g, i: (head0 + g, b, 0, 0))
    return pl.pallas_call(
        kern,
        grid=(B, G, nq),
        in_specs=[pl.BlockSpec((R, tq // CHUNK, HEAD_DIM, CHUNK),
                               lambda b, g, i: (T_NSA_Q // R + g, b * nq + i, 0, 0)),
                  pl.BlockSpec((1, 1, ncp, HEAD_DIM), lambda b, g, i: (g, b, 0, 0)),
                  pl.BlockSpec((1, 1, HEAD_DIM, ncp), lambda b, g, i: (g, b, 0, 0)),
                  keys(S_KSLC), values(T_VSLC), keys(S_KWIN), values(T_VWIN),
                  pl.BlockSpec((1, GATE_ROWS, tq), lambda b, g, i: (g, 0, b * nq + i)),
                  pl.BlockSpec((nsb, ncp), lambda b, g, i: (0, 0)),
                  pl.BlockSpec((T, HEAD_DIM), lambda b, g, i: (0, 0)),
                  pl.BlockSpec((1, R, HEAD_DIM), lambda b, g, i: (g, 0, 0))],
        out_specs=pl.BlockSpec((tq, R * HEAD_DIM), lambda b, g, i: (b * nq + i, g)),
        out_shape=jax.ShapeDtypeStruct((B * T, NSA_HEADS * HEAD_DIM), BF16),
        scratch_shapes=[pltpu.VMEM((1, R * tq), F32),
                        pltpu.VMEM((HEAD_DIM + ONES_ROWS, R * tq), F32), pltpu.VMEM((HEAD_DIM, R * tq), F32),
                        pltpu.VMEM((3, 2 * HEAD_DIM, R * tq), BF16),
                        pltpu.VMEM((SLC_TKS, R * tq), F32), pltpu.VMEM((SLC_TKS, R * tq), F32)],
        compiler_params=_cparams(("parallel", "parallel", "arbitrary")),
        name="nsa",
    )(yT, kcvc, vcT, ystd, yT, ystd, yT, gT, ovT, e_slc, gains)


def _moba_kernel(qT_ref, k_ref, vT_ref, e_ref, gain_ref, o_ref, m_scr, acc_scr, kmean_scr, qaug_scr,
                 s_scr, *, T):
    tq, tk = MOBA_TQ, MOBA_TK
    nb = T // MOBA_BLOCK
    nbp = kmean_scr.shape[1]
    i = pl.program_id(2)
    t0 = i * tq

    @pl.when(i == 0)
    def _():
        kmean_scr[...] = jnp.zeros(kmean_scr.shape, F32)
        for a in range(MOBA_HB):
            kb = k_ref[a].astype(F32).reshape(nb, MOBA_BLOCK, HEAD_DIM)
            kmean_scr[a, 0:nb, :] = jnp.mean(kb, axis=1)

    n_idx = lax.broadcasted_iota(jnp.int32, (nbp, tq), 0)
    cur = (t0 + lax.broadcasted_iota(jnp.int32, (nbp, tq), 1)) // MOBA_BLOCK
    past = n_idx < cur
    causal = lax.broadcasted_iota(jnp.int32, (tk, tq), 0) <= lax.broadcasted_iota(jnp.int32, (tk, tq), 1)

    qTs = [jnp.concatenate([qT_ref[a, c] for c in range(tq // CHUNK)], axis=1) for a in range(MOBA_HB)]
    for a in range(MOBA_HB):
        qT = qTs[a]
        gate = jnp.zeros((nbp, tq), F32)
        for part in _split3(kmean_scr[a]):
            gate = gate + jnp.dot(part, qT, preferred_element_type=F32)
        sel = _rank_select(jnp.where(past, gate, NEG), min(MOBA_TOPK, nb)) & past
        for d in range(2):
            qaug_scr[a, d, 0:HEAD_DIM, :] = qT
        qaug_scr[a, 0, HEAD_DIM:, :] = _pad_rows(jnp.where(sel, 0.0, NEG), HEAD_DIM).astype(BF16)
        qaug_scr[a, 1, HEAD_DIM:, :] = jnp.full((HEAD_DIM, tq), NEG, BF16)

    s_own = []
    for a in range(MOBA_HB):
        _reset(m_scr.at[a], acc_scr.at[a])
        k_own = k_ref[a, pl.ds(pl.multiple_of(t0, tk), tk), :]
        s_own.append(jnp.where(causal, jnp.dot(k_own, qTs[a], preferred_element_type=F32), NEG))

    chains = [(k_ref.at[a], e_ref, vT_ref.at[a], qaug_scr.at[a], (s_scr.at[a, 0], s_scr.at[a, 1]),
               m_scr.at[a], acc_scr.at[a]) for a in range(MOBA_HB)]
    n_steps = (t0 + MOBA_TKS - 1) // MOBA_TKS
    _pipelined_attention(chains, n_steps, T // MOBA_TKS, MOBA_TKS, CHUNK)

    own = tq // CHUNK
    for a in range(MOBA_HB):
        v_own = jnp.concatenate([vT_ref[a, i * own + c] for c in range(own)], axis=1)
        _online_step(s_own[a], v_own, m_scr.at[a], acc_scr.at[a])
        oT = _normalized(acc_scr[a])
        ms = jnp.mean(oT * oT, axis=0, keepdims=True)
        on = (oT * lax.rsqrt(ms + EPS)).T * gain_ref[a]
        o_ref[:, a * HEAD_DIM:(a + 1) * HEAD_DIM] = on.astype(BF16)


def _moba(yT, ystd, e_moba, gains, B, T):
    H, HB, tq = MOBA_HEADS, MOBA_HB, MOBA_TQ
    nq = T // tq
    nb = T // MOBA_BLOCK
    nbp = max(16, nb)
    kern = functools.partial(_moba_kernel, T=T)
    return pl.pallas_call(
        kern,
        grid=(B, H // HB, nq),
        in_specs=[pl.BlockSpec((HB, tq // CHUNK, HEAD_DIM, CHUNK),
                               lambda b, h, i: (T_MOBA_Q // HB + h, b * nq + i, 0, 0)),
                  pl.BlockSpec((HB, T, HEAD_DIM), lambda b, h, i: (S_MOBA_K // HB + h, b, 0)),
                  pl.BlockSpec((HB, T // CHUNK, HEAD_DIM, CHUNK), lambda b, h, i: (T_MOBA_V // HB + h, b, 0, 0)),
                  pl.BlockSpec((T, HEAD_DIM), lambda b, h, i: (0, 0)),
                  pl.BlockSpec((HB, 1, HEAD_DIM), lambda b, h, i: (h, 0, 0))],
        out_specs=pl.BlockSpec((tq, HB * HEAD_DIM), lambda b, h, i: (b * nq + i, h)),
        out_shape=jax.ShapeDtypeStruct((B * T, H * HEAD_DIM), BF16),
        scratch_shapes=[pltpu.VMEM((HB, 1, tq), F32),
                        pltpu.VMEM((HB, HEAD_DIM + ONES_ROWS, tq), F32), pltpu.VMEM((HB, nbp, HEAD_DIM), F32),
                        pltpu.VMEM((HB, 2, 2 * HEAD_DIM, tq), BF16),
                        pltpu.VMEM((HB, 2, MOBA_TKS, tq), F32)],
        compiler_params=_cparams(("parallel", "parallel", "arbitrary")),
        name="moba",
    )(yT, ystd, yT, e_moba, gains)


def _outproj_kernel(on_ref, om_ref, w_ref, x_ref, mod_ref, o_ref):
    half = on_ref.shape[1]
    acc = jnp.dot(on_ref[...], w_ref[0:half, :], preferred_element_type=F32)
    acc = acc + jnp.dot(om_ref[...], w_ref[half:, :], preferred_element_type=F32)
    o_ref[...] = x_ref[...] + mod_ref[0, 2:3, :] * acc


def _out_proj(o_nsa, o_moba, w_out, x2, mod3, T):
    BT, D = x2.shape
    tm = min(512, T)
    tpb = T // tm
    half = o_nsa.shape[1]
    return pl.pallas_call(
        _outproj_kernel,
        grid=(BT // tm,),
        in_specs=[pl.BlockSpec((tm, half), lambda i: (i, 0)),
                  pl.BlockSpec((tm, half), lambda i: (i, 0)),
                  pl.BlockSpec((D, D), lambda i: (0, 0)),
                  pl.BlockSpec((tm, D), lambda i: (i, 0)),
                  pl.BlockSpec((1, 6, D), lambda i: (i // tpb, 0, 0))],
        out_specs=pl.BlockSpec((tm, D), lambda i: (i, 0)),
        out_shape=jax.ShapeDtypeStruct((BT, D), F32),
        compiler_params=_cparams(("parallel",)),
        name="out_proj",
    )(o_nsa, o_moba, w_out, x2, mod3)


def _ffn_kernel(x_ref, mod_ref, wg_ref, wu_ref, wo_ref, o_ref, h_scr):
    j = pl.program_id(1)

    @pl.when(j == 0)
    def _():
        x = x_ref[...]
        ms = jnp.mean(x * x, axis=-1, keepdims=True)
        h = x * lax.rsqrt(ms + EPS) * (1.0 + mod_ref[0, 4:5, :]) + mod_ref[0, 3:4, :]
        h_scr[...] = h.astype(BF16)
        o_ref[...] = jnp.zeros(o_ref.shape, F32)

    hb = h_scr[...]
    gate = jnp.dot(hb, wg_ref[...], preferred_element_type=F32)
    up = jnp.dot(hb, wu_ref[...], preferred_element_type=F32)
    act = (gate * jax.nn.sigmoid(gate) * up).astype(BF16)
    o_ref[...] += jnp.dot(act, wo_ref[...], preferred_element_type=F32)

    @pl.when(j == pl.num_programs(1) - 1)
    def _():
        o_ref[...] = x_ref[...] + mod_ref[0, 5:6, :] * o_ref[...]


def _ffn(x1, mod3, w_in, w_out, T):
    BT, D = x1.shape
    Fh = w_out.shape[0]
    tm = min(512, T)
    tf = 512
    tpb = T // tm
    nf = Fh // tf
    return pl.pallas_call(
        _ffn_kernel,
        grid=(BT // tm, nf),
        in_specs=[pl.BlockSpec((tm, D), lambda i, j: (i, 0)),
                  pl.BlockSpec((1, 6, D), lambda i, j: (i // tpb, 0, 0)),
                  pl.BlockSpec((D, tf), lambda i, j: (0, j)),
                  pl.BlockSpec((D, tf), lambda i, j: (0, nf + j)),
                  pl.BlockSpec((tf, D), lambda i, j: (j, 0))],
        out_specs=pl.BlockSpec((tm, D), lambda i, j: (i, 0)),
        out_shape=jax.ShapeDtypeStruct((BT, D), F32),
        scratch_shapes=[pltpu.VMEM((tm, D), BF16)],
        compiler_params=_cparams(("parallel", "arbitrary")),
        name="ffn",
    )(x1, mod3, w_in, w_in, w_out)


def _rope_lane_tables(T):
    inv = ROPE_THETA ** (-jnp.arange(0, ROPE_DIMS, 2, dtype=F32) / ROPE_DIMS)
    ang = jnp.arange(T).astype(F32)[:, None] * inv[None, :]
    cos, sin = jnp.cos(ang), jnp.sin(ang)
    rest = HEAD_DIM - ROPE_DIMS
    cc = jnp.concatenate([cos, cos, jnp.ones((T, rest), F32)], axis=1)
    sa = jnp.concatenate([-sin, jnp.zeros((T, HEAD_DIM - ROPE_HALF), F32)], axis=1)
    sb = jnp.concatenate([jnp.zeros((T, ROPE_HALF), F32), sin, jnp.zeros((T, rest), F32)], axis=1)
    return cc, sa, sb


def _block_onehot(T, block):
    return (jnp.arange(T)[:, None] // block == jnp.arange(HEAD_DIM)[None, :]).astype(BF16)


def _layer(x2, c, B, T, w_ada, b_ada, w_in, nsa_q_norm, nsa_k_norm, moba_q_norm, moba_k_norm,
           cmp_pe_k, cmp_w1_k, cmp_w2_k, cmp_pe_v, cmp_w1_v, cmp_w2_v, out_norm, w_out,
           w_ffn_in, w_ffn_out):
    D = x2.shape[1]
    G = NSA_KV_HEADS
    scale = HEAD_DIM ** -0.5 * LOG2E
    assert T % MOBA_BLOCK == 0 and T % SLC_TKS == 0 and T % MOBA_TKS == 0 and T % NSA_TQ == 0 and T >= WIN_SPAN
    assert T // SLC_BLOCK <= HEAD_DIM and T // MOBA_BLOCK <= HEAD_DIM

    mod3 = _adaln(c, w_ada, b_ada)

    nsa_w = NSA_HEADS * HEAD_DIM + 6 * G * HEAD_DIM
    gw = NSA_HEADS * 3
    w_nsa = w_in[:, :nsa_w].astype(BF16)
    w_moba = w_in[:, nsa_w + gw:].astype(BF16)
    wg = w_in[:, nsa_w:nsa_w + gw].reshape(D, G, NSA_GROUP * 3)
    w_gate = jnp.pad(wg, ((0, 0), (0, 0), (0, GATE_LANES - NSA_GROUP * 3))).reshape(D, G * GATE_LANES)
    w_gate = w_gate.astype(BF16)
    rep = lambda g_, n: jnp.broadcast_to(g_, (n, HEAD_DIM))
    plain = lambda n: jnp.ones((n, HEAD_DIM), F32)
    gains = jnp.concatenate(
        [rep(nsa_q_norm * scale, NSA_HEADS), plain(2 * G),
         rep(nsa_k_norm[1], G), plain(G), rep(nsa_k_norm[2], G), plain(G),
         rep(moba_q_norm * scale, MOBA_HEADS), rep(moba_k_norm, MOBA_HEADS), plain(MOBA_HEADS)], axis=0)
    gains = gains.reshape(len(PROJ_TILES), HEADS_PER_TILE, HEAD_DIM)
    cc, sa, sb = _rope_lane_tables(T)

    yT, ystd, hc, gT = _in_proj(x2, mod3, w_nsa, w_moba, w_gate, gains, cc, sa, sb, T)

    ncp = T // CMP_STRIDE
    half = CMP_STRIDE * HEAD_DIM
    w1 = jnp.stack([cmp_w1_k, cmp_w1_v]).astype(BF16)
    w1cat = jnp.concatenate([w1[:, :half], w1[:, half:]], axis=2)
    pe8 = jnp.broadcast_to(jnp.stack([cmp_pe_k, cmp_pe_v]).reshape(2, 1, 2 * half), (2, 8, 2 * half))
    w2 = jnp.stack([cmp_w2_k, cmp_w2_v]).astype(BF16)
    cmp_rows = lambda t: jnp.pad(t[CMP_BLOCK - 1::CMP_STRIDE], ((0, 1), (0, 0)))
    kcvc = _compress(hc.reshape(2 * G, B, ncp, half), w1cat, pe8.astype(BF16), w1, w2,
                     nsa_k_norm[0].reshape(1, HEAD_DIM),
                     cmp_rows(cc), cmp_rows(sa), cmp_rows(sb))
    vcT = kcvc[G:].transpose(0, 1, 3, 2)

    nsb = T // SLC_BLOCK
    cs = jnp.arange(ncp)[None, :] * CMP_STRIDE
    sbk = jnp.arange(nsb)[:, None] * SLC_BLOCK
    ovT = ((cs < sbk + SLC_BLOCK) & (cs + CMP_BLOCK > sbk) & (jnp.arange(ncp)[None, :] < ncp - 1)).astype(BF16)

    on = out_norm.reshape(N_HEADS, HEAD_DIM)
    o_nsa = _nsa(yT, ystd, kcvc, vcT, gT, ovT, _block_onehot(T, SLC_BLOCK),
                 on[:NSA_HEADS].reshape(G, NSA_GROUP, HEAD_DIM), B, T)
    o_moba = _moba(yT, ystd, _block_onehot(T, MOBA_BLOCK),
                   on[NSA_HEADS:].reshape(MOBA_HEADS, 1, HEAD_DIM), B, T)

    x1 = _out_proj(o_nsa, o_moba, w_out.astype(BF16), x2, mod3, T)
    return _ffn(x1, mod3, w_ffn_in.astype(BF16), w_ffn_out.astype(BF16), T)


def kernel(x, c, w_ada, b_ada, w_in, nsa_q_norm, nsa_k_norm, moba_q_norm, moba_k_norm, cmp_pe_k, cmp_w1_k, cmp_w2_k, cmp_pe_v, cmp_w1_v, cmp_w2_v, out_norm, w_out, w_ffn_in, w_ffn_out):
    B, T, D = x.shape
    x2 = x.reshape(B * T, D)
    for l in range(w_ada.shape[0]):
        x2 = _layer(x2, c, B, T, w_ada[l], b_ada[l], w_in[l], nsa_q_norm[l], nsa_k_norm[l],
                    moba_q_norm[l], moba_k_norm[l], cmp_pe_k[l], cmp_w1_k[l], cmp_w2_k[l],
                    cmp_pe_v[l], cmp_w1_v[l], cmp_w2_v[l], out_norm[l], w_out[l],
                    w_ffn_in[l], w_ffn_out[l])
    return x2.reshape(B, T, D)
```

```python
import functools

import jax
import jax.numpy as jnp
from jax import lax
from jax.experimental import pallas as pl
from jax.experimental.pallas import tpu as pltpu

F32 = jnp.float32
BF16 = jnp.bfloat16

HEAD_DIM = 128
SUBLANES = 8
NSA_HEADS = 8
NSA_KV_HEADS = 2
NSA_GROUP = NSA_HEADS // NSA_KV_HEADS
MOBA_HEADS = 8
N_HEADS = NSA_HEADS + MOBA_HEADS
CMP_BLOCK = 32
CMP_STRIDE = 16
SLC_BLOCK = 64
SLC_TOPK = 16
WINDOW = 512
FORCE_BONUS = 1e4
MOBA_BLOCK = 256
MOBA_TOPK = 3
ROPE_THETA = 500000.0
ROPE_DIMS = HEAD_DIM // 4
ROPE_HALF = ROPE_DIMS // 2
EPS = 1e-6
LOG2E = 1.4426950408889634
ONES_ROWS = 16
NEG = -1e30
M_INIT = -1e29

V7X_VMEM_BYTES = 64 * 1024 * 1024
VMEM_LIMIT = V7X_VMEM_BYTES - 8 * 1024 * 1024

HEADS_PER_TILE = 4
PROJ_TILES = ("nsa_q", "nsa_q", "kv_cmp", "kv_slc", "kv_win",
              "moba_q", "moba_q", "moba_k", "moba_k", "moba_v", "moba_v")
N_NSA_TILES = 5
YT_ADVANCE = (1, 3, 5, 6, 9, 10)
YSTD_ADVANCE = (7, 8)
T_NSA_Q, T_VSLC, T_VWIN, T_MOBA_Q, T_MOBA_V = 0, 8, 10, 12, 20
S_KSLC, S_KWIN, S_MOBA_K = 0, 2, 4
N_T_HEADS, N_STD_HEADS, N_CMP_HEADS = 28, 12, 4
CHUNK = 128
GATE_LANES = 128
GATE_ROWS = 16

NSA_TQ = 256
SLC_TKS = 512
WIN_SPAN = WINDOW + NSA_TQ
MOBA_TQ = 256
MOBA_TK = MOBA_BLOCK
MOBA_TKS = 512
MOBA_HB = 4


def _cparams(sem):
    return pltpu.CompilerParams(dimension_semantics=sem, vmem_limit_bytes=VMEM_LIMIT)


def _split3(a):
    hi = a.astype(BF16)
    r1 = a - hi.astype(F32)
    mid = r1.astype(BF16)
    lo = (r1 - mid.astype(F32)).astype(BF16)
    return hi, mid, lo


def _adaln_kernel(c_ref, w_ref, b_ref, o_ref):
    cv = c_ref[...]
    s = cv * jax.nn.sigmoid(cv)
    w = w_ref[...].astype(BF16)
    acc = b_ref[...] + jnp.zeros(o_ref.shape, F32)
    for part in _split3(s)[:2]:
        acc = acc + jnp.dot(part, w, preferred_element_type=F32)
    o_ref[...] = acc


def _adaln(c, w_ada, b_ada):
    B, D = c.shape
    N = w_ada.shape[1]
    tn = 1024
    c8 = jnp.zeros((8, D), F32).at[:B].set(c)
    out = pl.pallas_call(
        _adaln_kernel,
        grid=(N // tn,),
        in_specs=[pl.BlockSpec((8, D), lambda j: (0, 0)),
                  pl.BlockSpec((D, tn), lambda j: (0, j)),
                  pl.BlockSpec((1, tn), lambda j: (0, j))],
        out_specs=pl.BlockSpec((8, tn), lambda j: (0, j)),
        out_shape=jax.ShapeDtypeStruct((8, N), F32),
        compiler_params=_cparams(("arbitrary",)),
        name="adaln",
    )(c8, w_ada, b_ada.reshape(1, N))
    return out[:B].reshape(B, 6, D)


def _rope(y, cc, sa, sb):
    return (y * cc + pltpu.roll(y, HEAD_DIM - ROPE_HALF, 1) * sa + pltpu.roll(y, ROPE_HALF, 1) * sb)


def _inproj_kernel(x_ref, mod_ref, wa_ref, wb_ref, wg_ref, gain_ref, gainT_ref, cc_ref, sa_ref, sb_ref,
                   cosT_ref, sinT_ref, yT_ref, ystd_ref, hc_ref, gT_ref, h_scr, acc_scr, rows_scr):
    j = pl.program_id(1)
    tm = x_ref.shape[0]
    n_chunks = tm // CHUNK
    pair = 2 * HEAD_DIM

    @pl.when(j == 0)
    def _():
        x = x_ref[...]
        ms = jnp.mean(x * x, axis=-1, keepdims=True)
        h = x * lax.rsqrt(ms + EPS) * (1.0 + mod_ref[0, 1:2, :]) + mod_ref[0, 0:1, :]
        hb = h.astype(BF16)
        h_scr[...] = hb
        g = jax.nn.sigmoid(jnp.dot(hb, wg_ref[...], preferred_element_type=F32))
        for grp in range(NSA_KV_HEADS):
            for c in range(n_chunks):
                blk = g[c * CHUNK:(c + 1) * CHUNK, grp * GATE_LANES:(grp + 1) * GATE_LANES].T
                gT_ref[grp, :, c * CHUNK:(c + 1) * CHUNK] = blk[0:GATE_ROWS, :]

    def head_pairs(w_ref):
        for half in range(HEADS_PER_TILE // 2):
            acc = jnp.dot(h_scr[...], w_ref[:, half * pair:(half + 1) * pair], preferred_element_type=F32)
            for h2 in range(2):
                yield 2 * half + h2, acc[:, h2 * HEAD_DIM:(h2 + 1) * HEAD_DIM]

    def k_head(yh, hh):
        ms = jnp.mean(yh * yh, axis=-1, keepdims=True)
        yn = yh * lax.rsqrt(ms + EPS) * gain_ref[0, hh:hh + 1, :]
        return _rope(yn, cc_ref[...], sa_ref[...], sb_ref[...]).astype(BF16)

    def q_chunk(t, c):
        ms = jnp.mean(t * t, axis=0, keepdims=True)
        tn = t * lax.rsqrt(ms + EPS) * gainT_ref[0]
        cs = cosT_ref[:, c * CHUNK:(c + 1) * CHUNK]
        sn = sinT_ref[:, c * CHUNK:(c + 1) * CHUNK]
        a, b = tn[0:ROPE_HALF], tn[ROPE_HALF:ROPE_DIMS]
        return jnp.concatenate([a * cs - b * sn, b * cs + a * sn, tn[ROPE_DIMS:]], axis=0)

    def store_T(yh, slot, treated):
        for c in range(n_chunks):
            t = yh[c * CHUNK:(c + 1) * CHUNK, :].T
            yT_ref[slot, c] = (q_chunk(t, c) if treated else t).astype(BF16)

    def when_kind(kind):
        steps = [t for t, name in enumerate(PROJ_TILES) if name == kind]
        return pl.when(functools.reduce(jnp.logical_or, [j == t for t in steps]))

    for kind, w_ref in (("nsa_q", wa_ref), ("moba_q", wb_ref)):
        @when_kind(kind)
        def _(w_ref=w_ref):
            for hh, yh in head_pairs(w_ref):
                store_T(yh, hh, True)

    @when_kind("moba_v")
    def _():
        for hh, yh in head_pairs(wb_ref):
            store_T(yh, hh, False)

    def project(kind, w_ref):
        steps = [t for t, name in enumerate(PROJ_TILES) if name == kind]

        @pl.when((j >= steps[0]) & (j <= steps[-1]))
        def _():
            acc_scr[...] = jnp.dot(h_scr[...], w_ref[...], preferred_element_type=F32)

    def scratch_head(hh):
        return acc_scr[:, hh * HEAD_DIM:(hh + 1) * HEAD_DIM]

    project("moba_k", wb_ref)

    @when_kind("moba_k")
    def _():
        for hh in range(HEADS_PER_TILE):
            ystd_ref[hh] = k_head(scratch_head(hh), hh)

    for kind, base in (("kv_slc", 0), ("kv_win", NSA_KV_HEADS)):
        project(kind, wa_ref)

        @when_kind(kind)
        def _(base=base):
            for g in range(NSA_KV_HEADS):
                ystd_ref[base + g] = k_head(scratch_head(g), g)
                store_T(scratch_head(NSA_KV_HEADS + g), base + g, False)

    @when_kind("kv_cmp")
    def _():
        for hh, yh in head_pairs(wa_ref):
            rows_scr[...] = yh
            flat = [rows_scr[pl.ds(l, tm // CMP_STRIDE, stride=CMP_STRIDE), :] for l in range(CMP_STRIDE)]
            hc_ref[hh] = jnp.concatenate(flat, axis=1).astype(BF16)


def _in_proj(x2, mod3, w_nsa, w_moba, w_gate, gains, gainsT, cc, sa, sb, cosT, sinT, T):
    BT, D = x2.shape
    G = NSA_KV_HEADS
    tm = min(1024, T)
    tn = HEADS_PER_TILE * HEAD_DIM
    tpb = T // tm
    advance = lambda j, steps: sum((j >= t).astype(jnp.int32) for t in steps)
    return pl.pallas_call(
        _inproj_kernel,
        grid=(BT // tm, len(PROJ_TILES)),
        in_specs=[pl.BlockSpec((tm, D), lambda i, j: (i, 0)),
                  pl.BlockSpec((1, 6, D), lambda i, j: (i // tpb, 0, 0)),
                  pl.BlockSpec((D, tn), lambda i, j: (0, jnp.minimum(j, N_NSA_TILES - 1))),
                  pl.BlockSpec((D, tn), lambda i, j: (0, jnp.maximum(j - N_NSA_TILES, 0))),
                  pl.BlockSpec((D, G * GATE_LANES), lambda i, j: (0, 0)),
                  pl.BlockSpec((1, HEADS_PER_TILE, HEAD_DIM), lambda i, j: (j, 0, 0)),
                  pl.BlockSpec((1, HEAD_DIM, CHUNK), lambda i, j: ((j >= N_NSA_TILES).astype(jnp.int32), 0, 0)),
                  pl.BlockSpec((tm, HEAD_DIM), lambda i, j: (i % tpb, 0)),
                  pl.BlockSpec((tm, HEAD_DIM), lambda i, j: (i % tpb, 0)),
                  pl.BlockSpec((tm, HEAD_DIM), lambda i, j: (i % tpb, 0)),
                  pl.BlockSpec((ROPE_HALF, tm), lambda i, j: (0, i % tpb)),
                  pl.BlockSpec((ROPE_HALF, tm), lambda i, j: (0, i % tpb))],
        out_specs=[pl.BlockSpec((HEADS_PER_TILE, tm // CHUNK, HEAD_DIM, CHUNK),
                                lambda i, j: (advance(j, YT_ADVANCE), i, 0, 0)),
                   pl.BlockSpec((HEADS_PER_TILE, tm, HEAD_DIM), lambda i, j: (advance(j, YSTD_ADVANCE), i, 0)),
                   pl.BlockSpec((N_CMP_HEADS, tm // CMP_STRIDE, CMP_STRIDE * HEAD_DIM), lambda i, j: (0, i, 0)),
                   pl.BlockSpec((G, GATE_ROWS, tm), lambda i, j: (0, 0, i))],
        out_shape=[jax.ShapeDtypeStruct((N_T_HEADS, BT // CHUNK, HEAD_DIM, CHUNK), BF16),
                   jax.ShapeDtypeStruct((N_STD_HEADS, BT, HEAD_DIM), BF16),
                   jax.ShapeDtypeStruct((N_CMP_HEADS, BT // CMP_STRIDE, CMP_STRIDE * HEAD_DIM), BF16),
                   jax.ShapeDtypeStruct((G, GATE_ROWS, BT), F32)],
        scratch_shapes=[pltpu.VMEM((tm, D), BF16), pltpu.VMEM((tm, tn), F32), pltpu.VMEM((tm, HEAD_DIM), F32)],
        compiler_params=_cparams(("parallel", "arbitrary")),
        name="in_proj",
    )(x2, mod3, w_nsa, w_moba, w_gate, gains, gainsT, cc, sa, sb, cosT, sinT)


def _compress_kernel(h_ref, w1c_ref, pe_ref, w1_ref, w2_ref, gain_ref, cc_ref, sa_ref, sb_ref, o_ref):
    a = pl.program_id(0)
    ncp = h_ref.shape[2]
    z = jnp.dot(h_ref[0, 0], w1c_ref[0], preferred_element_type=F32)
    top = z[:, :HEAD_DIM]
    bot = pltpu.roll(z[:, HEAD_DIM:], ncp - 1, 0)
    pe_term = jnp.dot(pe_ref[0], w1_ref[0], preferred_element_type=F32)[0:1, :]
    pre = top + bot + pe_term
    act = pre * jax.nn.sigmoid(pre)
    out = jnp.dot(act.astype(BF16), w2_ref[0], preferred_element_type=F32)
    live = lax.broadcasted_iota(jnp.int32, out.shape, 0) < ncp - 1
    out = jnp.where(live, out, 0.0)

    @pl.when(a < NSA_KV_HEADS)
    def _():
        ms = jnp.mean(out * out, axis=-1, keepdims=True)
        yn = out * lax.rsqrt(ms + EPS) * gain_ref[...]
        o_ref[0, 0] = _rope(yn, cc_ref[...], sa_ref[...], sb_ref[...]).astype(BF16)

    @pl.when(a >= NSA_KV_HEADS)
    def _():
        o_ref[0, 0] = out.astype(BF16)


def _compress(hc, w1cat, pe8, w1, w2, gain, cc, sa, sb):
    A, B, ncp, K = hc.shape
    G = NSA_KV_HEADS
    return pl.pallas_call(
        _compress_kernel,
        grid=(A, B),
        in_specs=[pl.BlockSpec((1, 1, ncp, K), lambda a, b: (a, b, 0, 0)),
                  pl.BlockSpec((1, K, 2 * HEAD_DIM), lambda a, b: (a // G, 0, 0)),
                  pl.BlockSpec((1, 8, 2 * K), lambda a, b: (a // G, 0, 0)),
                  pl.BlockSpec((1, 2 * K, HEAD_DIM), lambda a, b: (a // G, 0, 0)),
                  pl.BlockSpec((1, HEAD_DIM, HEAD_DIM), lambda a, b: (a // G, 0, 0)),
                  pl.BlockSpec((1, HEAD_DIM), lambda a, b: (0, 0)),
                  pl.BlockSpec((ncp, HEAD_DIM), lambda a, b: (0, 0)),
                  pl.BlockSpec((ncp, HEAD_DIM), lambda a, b: (0, 0)),
                  pl.BlockSpec((ncp, HEAD_DIM), lambda a, b: (0, 0))],
        out_specs=pl.BlockSpec((1, 1, ncp, HEAD_DIM), lambda a, b: (a, b, 0, 0)),
        out_shape=jax.ShapeDtypeStruct((A, B, ncp, HEAD_DIM), BF16),
        compiler_params=_cparams(("arbitrary", "arbitrary")),
        name="compress",
    )(hc, w1cat, pe8, w1, w2, gain, cc, sa, sb)


def _reset(m_scr, acc_scr):
    m_scr[...] = jnp.full(m_scr.shape, M_INIT, F32)
    acc_scr[...] = jnp.zeros(acc_scr.shape, F32)


def _with_ones(vT):
    return jnp.concatenate([vT, jnp.ones((ONES_ROWS, vT.shape[1]), vT.dtype)], axis=0)


def _normalized(acc):
    return acc[0:HEAD_DIM] / acc[HEAD_DIM:HEAD_DIM + 1]


def _online_step(s, vT, m_scr, acc_scr):
    m_prev = m_scr[...]
    m_new = jnp.maximum(m_prev, jnp.max(s, axis=0, keepdims=True))
    alpha = jnp.exp2(m_prev - m_new)
    p = jnp.exp2(s - m_new)
    acc_scr[...] = alpha * acc_scr[...] + jnp.dot(_with_ones(vT), p.astype(BF16), preferred_element_type=F32)
    m_scr[...] = m_new


def _rank_select(score, k):
    n = score.shape[0]
    rank = jnp.zeros(score.shape, F32)
    for m in range(n):
        sm = score[m:m + 1, :]
        lo = (m // SUBLANES) * SUBLANES
        hi = min(lo + SUBLANES, n)
        parts = []
        if lo > 0:
            parts.append(jnp.where(sm > score[:lo], 1.0, 0.0))
        gt =jnp.where(sm > score[lo:hi], 1.0, 0.0)
        ge = jnp.where(sm >= score[lo:hi], 1.0, 0.0)
        below = lax.broadcasted_iota(jnp.int32, gt.shape, 0) > m - lo
        parts.append(jnp.where(below, ge, gt))
        if hi < n:
            parts.append(jnp.where(sm >= score[hi:], 1.0, 0.0))
        rank = rank + (jnp.concatenate(parts, axis=0) if len(parts) > 1 else parts[0])
    return rank < k


def _pad_rows(a, rows):
    return jnp.concatenate([a, jnp.zeros((rows - a.shape[0], a.shape[1]), a.dtype)], axis=0)


def _masked_scores(k_ref, e_ref, start, size, q_aug):
    rows = pl.ds(pl.multiple_of(start, 128), size)
    k_aug = jnp.concatenate([k_ref[rows, :], e_ref[rows, :]], axis=1)
    return jnp.dot(k_aug, q_aug, preferred_element_type=F32)


def _pipelined_attention(chains, n_steps, n_max, tks, tk):
    per = tks // tk

    def scores(step, buf):
        st = jnp.minimum(step, n_max - 1)
        dead = jnp.where(step < n_steps, 0, 1)
        for k_ref, e_ref, _, qaug_ref, bufs, _, _ in chains:
            bufs[buf][...] = _masked_scores(k_ref, e_ref, st * tks, tks, qaug_ref[dead])

    def consume(step, buf):
        st = jnp.minimum(step, n_max - 1)
        for _, _, vT_ref, _, bufs, m_scr, acc_scr in chains:
            vT = jnp.concatenate([vT_ref[st * per + a] for a in range(per)], axis=1)
            _online_step(bufs[buf][...], vT, m_scr, acc_scr)

    def pair(p, carry):
        scores(2 * p + 1, 1)
        consume(2 * p, 0)
        scores(2 * p + 2, 0)
        consume(2 * p + 1, 1)
        return carry

    n_pairs = (n_steps + 1) // 2

    @pl.when(n_pairs > 0)
    def _():
        scores(0, 0)

    lax.fori_loop(0, n_pairs, pair, 0)


def _nsa_kernel(qT_ref, kc_ref, vcT_ref, ks_ref, vsT_ref, kw_ref, vwT_ref, gT_ref, ovT_ref, e_ref,
                gain_ref, o_ref, m_scr, acc_scr, comb_scr, qaug_scr, s0_scr, s1_scr, *, T):
    R, tq = NSA_GROUP, NSA_TQ
    i = pl.program_id(2)
    t0 = i * tq
    ncp = T // CMP_STRIDE
    nsb = T // SLC_BLOCK
    qT = jnp.concatenate([qT_ref[r, c] for r in range(R) for c in range(tq // CHUNK)], axis=1)
    gT = gT_ref[0]

    def per_head(row0):
        return jnp.concatenate([gT[row0 + 3 * r:row0 + 3 * r + 1, :] for r in range(R)], axis=1)

    def lanes_x_heads(a):
        return jnp.concatenate([a] * R, axis=1)

    s = jnp.dot(kc_ref[0, 0], qT, preferred_element_type=F32)
    c_idx = lax.broadcasted_iota(jnp.int32, (ncp, tq), 0)
    t_c = t0 + lax.broadcasted_iota(jnp.int32, (ncp, tq), 1)
    vis = lanes_x_heads((c_idx * CMP_STRIDE + (CMP_BLOCK - 1) <= t_c) & (c_idx < ncp - 1))
    s = jnp.where(vis, s, NEG)
    m = jnp.max(s, axis=0, keepdims=True)
    p = jnp.where(vis, jnp.exp2(s - m), 0.0)
    l = jnp.sum(p, axis=0, keepdims=True)
    p = p / jnp.where(l > 0.0, l, 1.0)
    o_cmp = jnp.dot(vcT_ref[0, 0], p.astype(BF16), preferred_element_type=F32)
    comb_scr[...] = o_cmp * per_head(0)

    psum = p[:, 0:tq]
    for r in range(1, R):
        psum = psum + p[:, r * tq:(r + 1) * tq]
    ov = ovT_ref[...]
    imp = jnp.zeros((nsb, tq), F32)
    for part in _split3(psum):
        imp = imp + jnp.dot(ov, part, preferred_element_type=F32)
    n_idx = lax.broadcasted_iota(jnp.int32, (nsb, tq), 0)
    cur = (t0 + lax.broadcasted_iota(jnp.int32, (nsb, tq), 1)) // SLC_BLOCK
    forced = (n_idx == 0) | (n_idx == cur) | (n_idx == cur - 1)
    valid = n_idx <= cur
    score = jnp.where(valid, imp + jnp.where(forced, FORCE_BONUS, 0.0), NEG)
    sel = _rank_select(score, min(SLC_TOPK, nsb)) & valid
    sel_past = sel & (n_idx * SLC_BLOCK < t0)
    bias = _pad_rows(jnp.where(sel_past, 0.0, NEG), HEAD_DIM).astype(BF16)
    bias_own = _pad_rows(jnp.where(sel, 0.0, NEG), HEAD_DIM).astype(BF16)
    for d in range(3):
        qaug_scr[d, 0:HEAD_DIM, :] = qT
    qaug_scr[0, HEAD_DIM:, :] = lanes_x_heads(bias)
    qaug_scr[1, HEAD_DIM:, :] = jnp.full((HEAD_DIM, R * tq), NEG, BF16)
    qaug_scr[2, HEAD_DIM:, :] = lanes_x_heads(bias_own)

    w0 = jnp.maximum(t0 + tq - WIN_SPAN, 0)
    k_w = kw_ref[0, pl.ds(pl.multiple_of(w0, CHUNK), WIN_SPAN), :]
    s_w = jnp.dot(k_w, qT, preferred_element_type=F32)
    kpos = w0 + lax.broadcasted_iota(jnp.int32, (WIN_SPAN, tq), 0)
    tpos = t0 + lax.broadcasted_iota(jnp.int32, (WIN_SPAN, tq), 1)
    ok = lanes_x_heads((kpos <= tpos) & (tpos - kpos < WINDOW))
    s_w = jnp.where(ok, s_w, NEG)
    m_w = jnp.max(s_w, axis=0, keepdims=True)
    p_w = jnp.exp2(s_w - m_w)
    jw = w0 // CHUNK
    v_w = jnp.concatenate([vwT_ref[0, jw + a] for a in range(WIN_SPAN // CHUNK)], axis=1)
    o_w = jnp.dot(_with_ones(v_w), p_w.astype(BF16), preferred_element_type=F32)
    comb_scr[...] += _normalized(o_w) * per_head(2)

    tri = lax.broadcasted_iota(jnp.int32, (tq, tq), 0) <= lax.broadcasted_iota(jnp.int32, (tq, tq), 1)
    s_d = jnp.where(lanes_x_heads(tri), _masked_scores(ks_ref.at[0], e_ref, t0, tq, qaug_scr[2]), NEG)
    _reset(m_scr, acc_scr)
    n_steps = (t0 + SLC_TKS - 1) // SLC_TKS
    _pipelined_attention([(ks_ref.at[0], e_ref, vsT_ref.at[0], qaug_scr, (s0_scr, s1_scr),
                           m_scr, acc_scr)], n_steps, T // SLC_TKS, SLC_TKS, CHUNK)
    own = tq // CHUNK
    v_d = jnp.concatenate([vsT_ref[0, i * own + a] for a in range(own)], axis=1)
    _online_step(s_d, v_d, m_scr, acc_scr)
    comb = comb_scr[...] + _normalized(acc_scr[...]) * per_head(1)

    for r in range(R):
        oT = comb[:, r * tq:(r + 1) * tq]
        ms = jnp.mean(oT * oT, axis=0, keepdims=True)
        on = (oT * lax.rsqrt(ms + EPS)).T * gain_ref[0, r:r + 1, :]
        o_ref[:, r * HEAD_DIM:(r + 1) * HEAD_DIM] = on.astype(BF16)


def _nsa(yT, ystd, kcvc, vcT, gT, ovT, e_slc, gains, B, T):
    G, R, tq = NSA_KV_HEADS, NSA_GROUP, NSA_TQ
    nq = T // tq
    ncp = T // CMP_STRIDE
    nsb = T // SLC_BLOCK
    kern = functools.partial(_nsa_kernel, T=T)
    keys = lambda head0: pl.BlockSpec((1, T, HEAD_DIM), lambda b, g, i: (head0 + g, b, 0))
    values = lambda head0: pl.BlockSpec((1, T // CHUNK, HEAD_DIM, CHUNK), lambda b, g, i: (head0 + g, b, 0, 0))
    return pl.pallas_call(
        kern,
        grid=(B, G, nq),
        in_specs=[pl.BlockSpec((R, tq // CHUNK, HEAD_DIM, CHUNK),
                               lambda b, g, i: (T_NSA_Q // R + g, b * nq + i, 0, 0)),
                  pl.BlockSpec((1, 1, ncp, HEAD_DIM), lambda b, g, i: (g, b, 0, 0)),
                  pl.BlockSpec((1, 1, HEAD_DIM, ncp), lambda b, g, i: (g, b, 0, 0)),
                  keys(S_KSLC), values(T_VSLC), keys(S_KWIN), values(T_VWIN),
                  pl.BlockSpec((1, GATE_ROWS, tq), lambda b, g, i: (g, 0, b * nq + i)),
                  pl.BlockSpec((nsb, ncp), lambda b, g, i: (0, 0)),
                  pl.BlockSpec((T, HEAD_DIM), lambda b, g, i: (0, 0)),
                  pl.BlockSpec((1, R, HEAD_DIM), lambda b, g, i: (g, 0, 0))],
        out_specs=pl.BlockSpec((tq, R * HEAD_DIM), lambda b, g, i: (b * nq + i, g)),
        out_shape=jax.ShapeDtypeStruct((B * T, NSA_HEADS * HEAD_DIM), BF16),
        scratch_shapes=[pltpu.VMEM((1, R * tq), F32),
                        pltpu.VMEM((HEAD_DIM + ONES_ROWS, R * tq), F32), pltpu.VMEM((HEAD_DIM, R * tq), F32),
                        pltpu.VMEM((3, 2 * HEAD_DIM, R * tq), BF16),
                        pltpu.VMEM((SLC_TKS, R * tq), F32), pltpu.VMEM((SLC_TKS, R * tq), F32)],
        compiler_params=_cparams(("parallel", "parallel", "arbitrary")),
        name="nsa",
    )(yT, kcvc, vcT, ystd, yT, ystd, yT, gT, ovT, e_slc, gains)


def _moba_kernel(qT_ref, k_ref, vT_ref, e_ref, gain_ref, o_ref, m_scr, acc_scr, kmean_scr, qaug_scr,
                 s_scr, *, T):
    tq, tk = MOBA_TQ, MOBA_TK
    nb = T // MOBA_BLOCK
    nbp = kmean_scr.shape[1]
    i = pl.program_id(2)
    t0 = i * tq

    @pl.when(i == 0)
    def _():
        kmean_scr[...] = jnp.zeros(kmean_scr.shape, F32)
        for a in range(MOBA_HB):
            kb = k_ref[a].astype(F32).reshape(nb, MOBA_BLOCK, HEAD_DIM)
            kmean_scr[a, 0:nb, :] = jnp.mean(kb, axis=1)

    n_idx = lax.broadcasted_iota(jnp.int32, (nbp, tq), 0)
    cur = (t0 + lax.broadcasted_iota(jnp.int32, (nbp, tq), 1)) // MOBA_BLOCK
    past = n_idx < cur
    causal = lax.broadcasted_iota(jnp.int32, (tk, tq), 0) <= lax.broadcasted_iota(jnp.int32, (tk, tq), 1)

    qTs = [jnp.concatenate([qT_ref[a, c] for c in range(tq // CHUNK)], axis=1) for a in range(MOBA_HB)]
    for a in range(MOBA_HB):
        qT = qTs[a]
        gate = jnp.zeros((nbp, tq), F32)
        for part in _split3(kmean_scr[a]):
            gate = gate + jnp.dot(part, qT, preferred_element_type=F32)
        sel = _rank_select(jnp.where(past, gate, NEG), min(MOBA_TOPK, nb)) & past
        for d in range(2):
            qaug_scr[a, d, 0:HEAD_DIM, :] = qT
        qaug_scr[a, 0, HEAD_DIM:, :] = _pad_rows(jnp.where(sel, 0.0, NEG), HEAD_DIM).astype(BF16)
        qaug_scr[a, 1, HEAD_DIM:, :] = jnp.full((HEAD_DIM, tq), NEG, BF16)

    s_own = []
    for a in range(MOBA_HB):
        _reset(m_scr.at[a], acc_scr.at[a])
        k_own = k_ref[a, pl.ds(pl.multiple_of(t0, tk), tk), :]
        s_own.append(jnp.where(causal, jnp.dot(k_own, qTs[a], preferred_element_type=F32), NEG))

    chains = [(k_ref.at[a], e_ref, vT_ref.at[a], qaug_scr.at[a], (s_scr.at[a, 0], s_scr.at[a, 1]),
               m_scr.at[a], acc_scr.at[a]) for a in range(MOBA_HB)]
    n_steps = (t0 + MOBA_TKS - 1) // MOBA_TKS
    _pipelined_attention(chains, n_steps, T // MOBA_TKS, MOBA_TKS, CHUNK)

    own = tq // CHUNK
    for a in range(MOBA_HB):
        v_own = jnp.concatenate([vT_ref[a, i * own + c] for c in range(own)], axis=1)
        _online_step(s_own[a], v_own, m_scr.at[a], acc_scr.at[a])
        oT = _normalized(acc_scr[a])
        ms = jnp.mean(oT * oT, axis=0, keepdims=True)
        on = (oT * lax.rsqrt(ms + EPS)).T * gain_ref[a]
        o_ref[:, a * HEAD_DIM:(a + 1) * HEAD_DIM] = on.astype(BF16)


def _moba(yT, ystd, e_moba, gains, B, T):
    H, HB, tq = MOBA_HEADS, MOBA_HB, MOBA_TQ
    nq = T // tq
    nb = T // MOBA_BLOCK
    nbp = max(16, nb)
    kern = functools.partial(_moba_kernel, T=T)
    return pl.pallas_call(
        kern,
        grid=(B, H // HB, nq),
        in_specs=[pl.BlockSpec((HB, tq // CHUNK, HEAD_DIM, CHUNK),
                               lambda b, h, i: (T_MOBA_Q // HB + h, b * nq + i, 0, 0)),
                  pl.BlockSpec((HB, T, HEAD_DIM), lambda b, h, i: (S_MOBA_K // HB + h, b, 0)),
                  pl.BlockSpec((HB, T // CHUNK, HEAD_DIM, CHUNK), lambda b, h, i: (T_MOBA_V // HB + h, b, 0, 0)),
                  pl.BlockSpec((T, HEAD_DIM), lambda b, h, i: (0, 0)),
                  pl.BlockSpec((HB, 1, HEAD_DIM), lambda b, h, i: (h, 0, 0))],
        out_specs=pl.BlockSpec((tq, HB * HEAD_DIM), lambda b, h, i: (b * nq + i, h)),
        out_shape=jax.ShapeDtypeStruct((B * T, H * HEAD_DIM), BF16),
        scratch_shapes=[pltpu.VMEM((HB, 1, tq), F32),
                        pltpu.VMEM((HB, HEAD_DIM + ONES_ROWS, tq), F32), pltpu.VMEM((HB, nbp, HEAD_DIM), F32),
                        pltpu.VMEM((HB, 2, 2 * HEAD_DIM, tq), BF16),
                        pltpu.VMEM((HB, 2, MOBA_TKS, tq), F32)],
        compiler_params=_cparams(("parallel", "parallel", "arbitrary")),
        name="moba",
    )(yT, ystd, yT, e_moba, gains)


def _outproj_kernel(on_ref, om_ref, w_ref, x_ref, mod_ref, o_ref):
    half = on_ref.shape[1]
    acc = jnp.dot(on_ref[...], w_ref[0:half, :], preferred_element_type=F32)
    acc = acc + jnp.dot(om_ref[...], w_ref[half:, :], preferred_element_type=F32)
    o_ref[...] = x_ref[...] + mod_ref[0, 2:3, :] * acc


def _out_proj(o_nsa, o_moba, w_out, x2, mod3, T):
    BT, D = x2.shape
    tm = min(512, T)
    tpb = T // tm
    half = o_nsa.shape[1]
    return pl.pallas_call(
        _outproj_kernel,
        grid=(BT // tm,),
        in_specs=[pl.BlockSpec((tm, half), lambda i: (i, 0)),
                  pl.BlockSpec((tm, half), lambda i: (i, 0)),
                  pl.BlockSpec((D, D), lambda i: (0, 0)),
                  pl.BlockSpec((tm, D), lambda i: (i, 0)),
                  pl.BlockSpec((1, 6, D), lambda i: (i // tpb, 0, 0))],
        out_specs=pl.BlockSpec((tm, D), lambda i: (i, 0)),
        out_shape=jax.ShapeDtypeStruct((BT, D), F32),
        compiler_params=_cparams(("parallel",)),
        name="out_proj",
    )(o_nsa, o_moba, w_out, x2, mod3)


def _ffn_kernel(x_ref, mod_ref, wg_ref, wu_ref, wo_ref, o_ref, h_scr):
    j = pl.program_id(1)

    @pl.when(j == 0)
    def _():
        x = x_ref[...]
        ms = jnp.mean(x * x, axis=-1, keepdims=True)
        h = x * lax.rsqrt(ms + EPS) * (1.0 + mod_ref[0, 4:5, :]) + mod_ref[0, 3:4, :]
        h_scr[...] = h.astype(BF16)
        o_ref[...] = jnp.zeros(o_ref.shape, F32)

    hb = h_scr[...]
    gate = jnp.dot(hb, wg_ref[...], preferred_element_type=F32)
    up = jnp.dot(hb, wu_ref[...], preferred_element_type=F32)
    act = (gate * jax.nn.sigmoid(gate) * up).astype(BF16)
    o_ref[...] += jnp.dot(act, wo_ref[...], preferred_element_type=F32)

    @pl.when(j == pl.num_programs(1) - 1)
    def _():
        o_ref[...] = x_ref[...] + mod_ref[0, 5:6, :] * o_ref[...]


def _ffn(x1, mod3, w_in, w_out, T):
    BT, D = x1.shape
    Fh = w_out.shape[0]
    tm = min(512, T)
    tf = 512
    tpb = T // tm
    nf = Fh // tf
    return pl.pallas_call(
        _ffn_kernel,
        grid=(BT // tm, nf),
        in_specs=[pl.BlockSpec((tm, D), lambda i, j: (i, 0)),
                  pl.BlockSpec((1, 6, D), lambda i, j: (i // tpb, 0, 0)),
                  pl.BlockSpec((D, tf), lambda i, j: (0, j)),
                  pl.BlockSpec((D, tf), lambda i, j: (0, nf + j)),
                  pl.BlockSpec((tf, D), lambda i, j: (j, 0))],
        out_specs=pl.BlockSpec((tm, D), lambda i, j: (i, 0)),
        out_shape=jax.ShapeDtypeStruct((BT, D), F32),
        scratch_shapes=[pltpu.VMEM((tm, D), BF16)],
        compiler_params=_cparams(("parallel", "arbitrary")),
        name="ffn",
    )(x1, mod3, w_in, w_in, w_out)


def _rope_tables(T):
    inv = ROPE_THETA ** (-jnp.arange(0, ROPE_DIMS, 2, dtype=F32) / ROPE_DIMS)
    ang = jnp.arange(T).astype(F32)[:, None] * inv[None, :]
    cos, sin = jnp.cos(ang), jnp.sin(ang)
    rest = HEAD_DIM - ROPE_DIMS
    cc = jnp.concatenate([cos, cos, jnp.ones((T, rest), F32)], axis=1)
    sa = jnp.concatenate([-sin, jnp.zeros((T, HEAD_DIM - ROPE_HALF), F32)], axis=1)
    sb = jnp.concatenate([jnp.zeros((T, ROPE_HALF), F32), sin, jnp.zeros((T, rest), F32)], axis=1)
    return cc, sa, sb, cos.T, sin.T


def _block_onehot(T, block):
    return (jnp.arange(T)[:, None] // block == jnp.arange(HEAD_DIM)[None, :]).astype(BF16)


def _layer(x2, c, B, T, w_ada, b_ada, w_in, nsa_q_norm, nsa_k_norm, moba_q_norm, moba_k_norm,
           cmp_pe_k, cmp_w1_k, cmp_w2_k, cmp_pe_v, cmp_w1_v, cmp_w2_v, out_norm, w_out,
           w_ffn_in, w_ffn_out):
    D = x2.shape[1]
    G = NSA_KV_HEADS
    scale = HEAD_DIM ** -0.5 * LOG2E
    assert T % MOBA_BLOCK == 0 and T % SLC_TKS == 0 and T % MOBA_TKS == 0 and T % NSA_TQ == 0 and T >= WIN_SPAN
    assert T // SLC_BLOCK <= HEAD_DIM and T // MOBA_BLOCK <= HEAD_DIM

    mod3 = _adaln(c, w_ada, b_ada)

    nsa_w = NSA_HEADS * HEAD_DIM + 6 * G * HEAD_DIM
    gw = NSA_HEADS * 3
    w_nsa = w_in.astype(BF16)
    w_moba = w_nsa[:, nsa_w + gw:]
    wg = w_in[:, nsa_w:nsa_w + gw].reshape(D, G, NSA_GROUP * 3)
    w_gate = jnp.pad(wg, ((0, 0), (0, 0), (0, GATE_LANES - NSA_GROUP * 3))).reshape(D, G * GATE_LANES)
    w_gate = w_gate.astype(BF16)
    rep = lambda g_, n: jnp.broadcast_to(g_, (n, HEAD_DIM))
    plain = lambda n: jnp.ones((n, HEAD_DIM), F32)
    gains = jnp.concatenate(
        [rep(nsa_q_norm * scale, NSA_HEADS), plain(2 * G),
         rep(nsa_k_norm[1], G), plain(G), rep(nsa_k_norm[2], G), plain(G),
         rep(moba_q_norm * scale, MOBA_HEADS), rep(moba_k_norm, MOBA_HEADS), plain(MOBA_HEADS)], axis=0)
    gains = gains.reshape(len(PROJ_TILES), HEADS_PER_TILE, HEAD_DIM)
    q_gain = lambda g_: jnp.broadcast_to((g_ * scale)[:, None], (HEAD_DIM, CHUNK))
    gainsT = jnp.stack([q_gain(nsa_q_norm), q_gain(moba_q_norm)])
    cc, sa, sb, cosT, sinT = _rope_tables(T)

    yT, ystd, hc, gT = _in_proj(x2, mod3, w_nsa, w_moba, w_gate, gains, gainsT, cc, sa, sb, cosT, sinT, T)

    ncp = T // CMP_STRIDE
    half = CMP_STRIDE * HEAD_DIM
    w1 = jnp.stack([cmp_w1_k, cmp_w1_v]).astype(BF16)
    w1cat = jnp.concatenate([w1[:, :half], w1[:, half:]], axis=2)
    pe8 = jnp.broadcast_to(jnp.stack([cmp_pe_k, cmp_pe_v]).reshape(2, 1, 2 * half), (2, 8, 2 * half))
    w2 = jnp.stack([cmp_w2_k, cmp_w2_v]).astype(BF16)
    cmp_rows = lambda t: jnp.pad(t[CMP_BLOCK - 1::CMP_STRIDE], ((0, 1), (0, 0)))
    kcvc = _compress(hc.reshape(2 * G, B, ncp, half), w1cat, pe8.astype(BF16), w1, w2,
                     nsa_k_norm[0].reshape(1, HEAD_DIM),
                     cmp_rows(cc), cmp_rows(sa), cmp_rows(sb))
    vcT = kcvc[G:].transpose(0, 1, 3, 2)

    nsb = T // SLC_BLOCK
    cs = jnp.arange(ncp)[None, :] * CMP_STRIDE
    sbk = jnp.arange(nsb)[:, None] * SLC_BLOCK
    ovT = ((cs < sbk + SLC_BLOCK) & (cs + CMP_BLOCK > sbk) & (jnp.arange(ncp)[None, :] < ncp - 1)).astype(BF16)

    on = out_norm.reshape(N_HEADS, HEAD_DIM)
    o_nsa = _nsa(yT, ystd, kcvc, vcT, gT, ovT, _block_onehot(T, SLC_BLOCK),
                 on[:NSA_HEADS].reshape(G, NSA_GROUP, HEAD_DIM), B, T)
    o_moba = _moba(yT, ystd, _block_onehot(T, MOBA_BLOCK),
                   on[NSA_HEADS:].reshape(MOBA_HEADS, 1, HEAD_DIM), B, T)

    x1 = _out_proj(o_nsa, o_moba, w_out.astype(BF16), x2, mod3, T)
    return _ffn(x1, mod3, w_ffn_in.astype(BF16), w_ffn_out.astype(BF16), T)


def kernel(x, c, w_ada, b_ada, w_in, nsa_q_norm, nsa_k_norm, moba_q_norm, moba_k_norm, cmp_pe_k, cmp_w1_k, cmp_w2_k, cmp_pe_v, cmp_w1_v, cmp_w2_v, out_norm, w_out, w_ffn_in, w_ffn_out):
    B, T, D = x.shape
    x2 = x.reshape(B * T, D)
    for l in range(w_ada.shape[0]):
        x2 = _layer(x2, c, B, T, w_ada[l], b_ada[l], w_in[l], nsa_q_norm[l], nsa_k_norm[l],
                    moba_q_norm[l], moba_k_norm[l], cmp_pe_k[l], cmp_w1_k[l], cmp_w2_k[l],
                    cmp_pe_v[l], cmp_w1_v[l], cmp_w2_v[l], out_norm[l], w_out[l],
                    w_ffn_in[l], w_ffn_out[l])
    return x2.reshape(B, T, D)
```

```python
import functools

import jax
import jax.numpy as jnp
from jax import lax
from jax.experimental import pallas as pl
from jax.experimental.pallas import tpu as pltpu

F32 = jnp.float32
BF16 = jnp.bfloat16

HEAD_DIM = 128
SUBLANES = 8
NSA_HEADS = 8
NSA_KV_HEADS = 2
NSA_GROUP = NSA_HEADS // NSA_KV_HEADS
MOBA_HEADS = 8
N_HEADS = NSA_HEADS + MOBA_HEADS
CMP_BLOCK = 32
CMP_STRIDE = 16
SLC_BLOCK = 64
SLC_TOPK = 16
WINDOW = 512
FORCE_BONUS = 1e4
MOBA_BLOCK = 256
MOBA_TOPK = 3
ROPE_THETA = 500000.0
ROPE_DIMS = HEAD_DIM // 4
ROPE_HALF = ROPE_DIMS // 2
EPS = 1e-6
LOG2E = 1.4426950408889634
ONES_ROWS = 16
NEG = -1e30
M_INIT = -1e29

V7X_VMEM_BYTES = 64 * 1024 * 1024
VMEM_LIMIT = V7X_VMEM_BYTES - 8 * 1024 * 1024

HEADS_PER_TILE = 4
PROJ_TILES = ("nsa_q", "nsa_q", "kv_cmp", "kv_slc", "kv_win",
              "moba_q", "moba_q", "moba_k", "moba_k", "moba_v", "moba_v")
N_NSA_TILES = 5
YT_ADVANCE = (1, 3, 5, 6, 9, 10)
YSTD_ADVANCE = (7, 8)
T_NSA_Q, T_VSLC, T_VWIN, T_MOBA_Q, T_MOBA_V = 0, 8, 10, 12, 20
S_KSLC, S_KWIN, S_MOBA_K = 0, 2, 4
N_T_HEADS, N_STD_HEADS, N_CMP_HEADS = 28, 12, 4
CHUNK = 128
GATE_LANES = 128
GATE_ROWS = 16

NSA_TQ = 256
SLC_TKS = 512
WIN_SPAN = WINDOW + NSA_TQ
MOBA_TQ = 256
MOBA_TK = MOBA_BLOCK
MOBA_TKS = 512
MOBA_HB = 4


def _cparams(sem):
    return pltpu.CompilerParams(dimension_semantics=sem, vmem_limit_bytes=VMEM_LIMIT)


def _split3(a):
    hi = a.astype(BF16)
    r1 = a - hi.astype(F32)
    mid = r1.astype(BF16)
    lo = (r1 - mid.astype(F32)).astype(BF16)
    return hi, mid, lo


def _adaln_kernel(c_ref, w_ref, b_ref, o_ref):
    cv = c_ref[...]
    s = cv * jax.nn.sigmoid(cv)
    w = w_ref[...].astype(BF16)
    acc = b_ref[...] + jnp.zeros(o_ref.shape, F32)
    for part in _split3(s)[:2]:
        acc = acc + jnp.dot(part, w, preferred_element_type=F32)
    o_ref[...] = acc


def _adaln(c, w_ada, b_ada):
    B, D = c.shape
    N = w_ada.shape[1]
    tn = 1024
    c8 = jnp.zeros((8, D), F32).at[:B].set(c)
    out = pl.pallas_call(
        _adaln_kernel,
        grid=(N // tn,),
        in_specs=[pl.BlockSpec((8, D), lambda j: (0, 0)),
                  pl.BlockSpec((D, tn), lambda j: (0, j)),
                  pl.BlockSpec((1, tn), lambda j: (0, j))],
        out_specs=pl.BlockSpec((8, tn), lambda j: (0, j)),
        out_shape=jax.ShapeDtypeStruct((8, N), F32),
        compiler_params=_cparams(("arbitrary",)),
        name="adaln",
    )(c8, w_ada, b_ada.reshape(1, N))
    return out[:B].reshape(B, 6, D)


def _rope(y, cc, sa, sb):
    return (y * cc + pltpu.roll(y, HEAD_DIM - ROPE_HALF, 1) * sa + pltpu.roll(y, ROPE_HALF, 1) * sb)


def _inproj_kernel(x_ref, mod_ref, wa_ref, wb_ref, wg_ref, gain_ref, gainT_ref, cc_ref, sa_ref, sb_ref,
                   cosT_ref, sinT_ref, yT_ref, ystd_ref, hc_ref, gT_ref, h_scr, acc_scr, rows_scr):
    j = pl.program_id(1)
    tm = x_ref.shape[0]
    n_chunks = tm // CHUNK
    pair = 2 * HEAD_DIM

    @pl.when(j == 0)
    def _():
        x = x_ref[...]
        ms = jnp.mean(x * x, axis=-1, keepdims=True)
        h = x * lax.rsqrt(ms + EPS) * (1.0 + mod_ref[0, 1:2, :]) + mod_ref[0, 0:1, :]
        hb = h.astype(BF16)
        h_scr[...] = hb
        g = jax.nn.sigmoid(jnp.dot(hb, wg_ref[...], preferred_element_type=F32))
        for grp in range(NSA_KV_HEADS):
            for c in range(n_chunks):
                blk = g[c * CHUNK:(c + 1) * CHUNK, grp * GATE_LANES:(grp + 1) * GATE_LANES].T
                gT_ref[grp, :, c * CHUNK:(c + 1) * CHUNK] = blk[0:GATE_ROWS, :]

    def head_pairs(w_ref):
        for half in range(HEADS_PER_TILE // 2):
            acc = jnp.dot(h_scr[...], w_ref[:, half * pair:(half + 1) * pair], preferred_element_type=F32)
            for h2 in range(2):
                yield 2 * half + h2, acc[:, h2 * HEAD_DIM:(h2 + 1) * HEAD_DIM]

    def k_head(yh, hh):
        ms = jnp.mean(yh * yh, axis=-1, keepdims=True)
        yn = yh * lax.rsqrt(ms + EPS) * gain_ref[0, hh:hh + 1, :]
        return _rope(yn, cc_ref[...], sa_ref[...], sb_ref[...]).astype(BF16)

    def q_chunk(t, c):
        ms = jnp.mean(t * t, axis=0, keepdims=True)
        tn = t * lax.rsqrt(ms + EPS) * gainT_ref[0]
        cs = cosT_ref[:, c * CHUNK:(c + 1) * CHUNK]
        sn = sinT_ref[:, c * CHUNK:(c + 1) * CHUNK]
        a, b = tn[0:ROPE_HALF], tn[ROPE_HALF:ROPE_DIMS]
        return jnp.concatenate([a * cs - b * sn, b * cs + a * sn, tn[ROPE_DIMS:]], axis=0)

    def store_T(yh, slot, treated):
        for c in range(n_chunks):
            t = yh[c * CHUNK:(c + 1) * CHUNK, :].T
            yT_ref[slot, c] = (q_chunk(t, c) if treated else t).astype(BF16)

    def when_kind(kind):
        steps = [t for t, name in enumerate(PROJ_TILES) if name == kind]
        return pl.when(functools.reduce(jnp.logical_or, [j == t for t in steps]))

    for kind, w_ref in (("nsa_q", wa_ref), ("moba_q", wb_ref)):
        @when_kind(kind)
        def _(w_ref=w_ref):
            for hh, yh in head_pairs(w_ref):
                store_T(yh, hh, True)

    @when_kind("moba_v")
    def _():
        for hh, yh in head_pairs(wb_ref):
            store_T(yh, hh, False)

    def project(kind, w_ref):
        steps = [t for t, name in enumerate(PROJ_TILES) if name == kind]

        @pl.when((j >= steps[0]) & (j <= steps[-1]))
        def _():
            acc_scr[...] = jnp.dot(h_scr[...], w_ref[...], preferred_element_type=F32)

    def scratch_head(hh):
        return acc_scr[:, hh * HEAD_DIM:(hh + 1) * HEAD_DIM]

    project("moba_k", wb_ref)

    @when_kind("moba_k")
    def _():
        for hh in range(HEADS_PER_TILE):
            ystd_ref[hh] = k_head(scratch_head(hh), hh)

    for kind, base in (("kv_slc", 0), ("kv_win", NSA_KV_HEADS)):
        project(kind, wa_ref)

        @when_kind(kind)
        def _(base=base):
            for g in range(NSA_KV_HEADS):
                ystd_ref[base + g] = k_head(scratch_head(g), g)
                store_T(scratch_head(NSA_KV_HEADS + g), base + g, False)

    @when_kind("kv_cmp")
    def _():
        for hh, yh in head_pairs(wa_ref):
            rows_scr[...] = yh
            flat = [rows_scr[pl.ds(l, tm // CMP_STRIDE, stride=CMP_STRIDE), :] for l in range(CMP_STRIDE)]
            hc_ref[hh] = jnp.concatenate(flat, axis=1).astype(BF16)


def _in_proj(x2, mod3, w_nsa, w_moba, w_gate, gains, gainsT, cc, sa, sb, cosT, sinT, T):
    BT, D = x2.shape
    G = NSA_KV_HEADS
    tm = min(1024, T)
    tn = HEADS_PER_TILE * HEAD_DIM
    tpb = T // tm
    advance = lambda j, steps: sum((j >= t).astype(jnp.int32) for t in steps)
    return pl.pallas_call(
        _inproj_kernel,
        grid=(BT // tm, len(PROJ_TILES)),
        in_specs=[pl.BlockSpec((tm, D), lambda i, j: (i, 0)),
                  pl.BlockSpec((1, 6, D), lambda i, j: (i // tpb, 0, 0)),
                  pl.BlockSpec((D, tn), lambda i, j: (0, jnp.minimum(j, N_NSA_TILES - 1))),
                  pl.BlockSpec((D, tn), lambda i, j: (0, jnp.maximum(j - N_NSA_TILES, 0))),
                  pl.BlockSpec((D, G * GATE_LANES), lambda i, j: (0, 0)),
                  pl.BlockSpec((1, HEADS_PER_TILE, HEAD_DIM), lambda i, j: (j, 0, 0)),
                  pl.BlockSpec((1, HEAD_DIM, CHUNK), lambda i, j: ((j >= N_NSA_TILES).astype(jnp.int32), 0, 0)),
                  pl.BlockSpec((tm, HEAD_DIM), lambda i, j: (i % tpb, 0)),
                  pl.BlockSpec((tm, HEAD_DIM), lambda i, j: (i % tpb, 0)),
                  pl.BlockSpec((tm, HEAD_DIM), lambda i, j: (i % tpb, 0)),
                  pl.BlockSpec((ROPE_HALF, tm), lambda i, j: (0, i % tpb)),
                  pl.BlockSpec((ROPE_HALF, tm), lambda i, j: (0, i % tpb))],
        out_specs=[pl.BlockSpec((HEADS_PER_TILE, tm // CHUNK, HEAD_DIM, CHUNK),
                                lambda i, j: (advance(j, YT_ADVANCE), i, 0, 0)),
                   pl.BlockSpec((HEADS_PER_TILE, tm, HEAD_DIM), lambda i, j: (advance(j, YSTD_ADVANCE), i, 0)),
                   pl.BlockSpec((N_CMP_HEADS, tm // CMP_STRIDE, CMP_STRIDE * HEAD_DIM), lambda i, j: (0, i, 0)),
                   pl.BlockSpec((G, GATE_ROWS, tm), lambda i, j: (0, 0, i))],
        out_shape=[jax.ShapeDtypeStruct((N_T_HEADS, BT // CHUNK, HEAD_DIM, CHUNK), BF16),
                   jax.ShapeDtypeStruct((N_STD_HEADS, BT, HEAD_DIM), BF16),
                   jax.ShapeDtypeStruct((N_CMP_HEADS, BT // CMP_STRIDE, CMP_STRIDE * HEAD_DIM), BF16),
                   jax.ShapeDtypeStruct((G, GATE_ROWS, BT), F32)],
        scratch_shapes=[pltpu.VMEM((tm, D), BF16), pltpu.VMEM((tm, tn), F32), pltpu.VMEM((tm, HEAD_DIM), F32)],
        compiler_params=_cparams(("parallel", "arbitrary")),
        name="in_proj",
    )(x2, mod3, w_nsa, w_moba, w_gate, gains, gainsT, cc, sa, sb, cosT, sinT)


def _compress_kernel(h_ref, w1c_ref, pe_ref, w1_ref, w2_ref, gain_ref, cc_ref, sa_ref, sb_ref, o_ref):
    a = pl.program_id(0)
    ncp = h_ref.shape[2]
    z = jnp.dot(h_ref[0, 0], w1c_ref[0], preferred_element_type=F32)
    top = z[:, :HEAD_DIM]
    bot = pltpu.roll(z[:, HEAD_DIM:], ncp - 1, 0)
    pe_term = jnp.dot(pe_ref[0], w1_ref[0], preferred_element_type=F32)[0:1, :]
    pre = top + bot + pe_term
    act = pre * jax.nn.sigmoid(pre)
    out = jnp.dot(act.astype(BF16), w2_ref[0], preferred_element_type=F32)
    live = lax.broadcasted_iota(jnp.int32, out.shape, 0) < ncp - 1
    out = jnp.where(live, out, 0.0)

    @pl.when(a < NSA_KV_HEADS)
    def _():
        ms = jnp.mean(out * out, axis=-1, keepdims=True)
        yn = out * lax.rsqrt(ms + EPS) * gain_ref[...]
        o_ref[0, 0] = _rope(yn, cc_ref[...], sa_ref[...], sb_ref[...]).astype(BF16)

    @pl.when(a >= NSA_KV_HEADS)
    def _():
        o_ref[0, 0] = out.astype(BF16)


def _compress(hc, w1cat, pe8, w1, w2, gain, cc, sa, sb):
    A, B, ncp, K = hc.shape
    G = NSA_KV_HEADS
    return pl.pallas_call(
        _compress_kernel,
        grid=(A, B),
        in_specs=[pl.BlockSpec((1, 1, ncp, K), lambda a, b: (a, b, 0, 0)),
                  pl.BlockSpec((1, K, 2 * HEAD_DIM), lambda a, b: (a // G, 0, 0)),
                  pl.BlockSpec((1, 8, 2 * K), lambda a, b: (a // G, 0, 0)),
                  pl.BlockSpec((1, 2 * K, HEAD_DIM), lambda a, b: (a // G, 0, 0)),
                  pl.BlockSpec((1, HEAD_DIM, HEAD_DIM), lambda a, b: (a // G, 0, 0)),
                  pl.BlockSpec((1, HEAD_DIM), lambda a, b: (0, 0)),
                  pl.BlockSpec((ncp, HEAD_DIM), lambda a, b: (0, 0)),
                  pl.BlockSpec((ncp, HEAD_DIM), lambda a, b: (0, 0)),
                  pl.BlockSpec((ncp, HEAD_DIM), lambda a, b: (0, 0))],
        out_specs=pl.BlockSpec((1, 1, ncp, HEAD_DIM), lambda a, b: (a, b, 0, 0)),
        out_shape=jax.ShapeDtypeStruct((A, B, ncp, HEAD_DIM), BF16),
        compiler_params=_cparams(("arbitrary", "arbitrary")),
        name="compress",
    )(hc, w1cat, pe8, w1, w2, gain, cc, sa, sb)


def _reset(m_scr, acc_scr):
    m_scr[...] = jnp.full(m_scr.shape, M_INIT, F32)
    acc_scr[...] = jnp.zeros(acc_scr.shape, F32)


def _with_ones(vT):
    return jnp.concatenate([vT, jnp.ones((ONES_ROWS, vT.shape[1]), vT.dtype)], axis=0)


def _normalized(acc):
    return acc[0:HEAD_DIM] / acc[HEAD_DIM:HEAD_DIM + 1]


def _online_step(s, vT, m_scr, acc_scr):
    m_prev = m_scr[...]
    m_new = jnp.maximum(m_prev, jnp.max(s, axis=0, keepdims=True))
    alpha = jnp.exp2(m_prev - m_new)
    p = jnp.exp2(s - m_new)
    acc_scr[...] = alpha * acc_scr[...] + jnp.dot(_with_ones(vT), p.astype(BF16), preferred_element_type=F32)
    m_scr[...] = m_new


def _rank_select(score, k):
    n = score.shape[0]
    rank = jnp.zeros(score.shape, F32)
    for m in range(n):
        sm = score[m:m + 1, :]
        lo = (m // SUBLANES) * SUBLANES
        hi = min(lo + SUBLANES, n)
        parts = []
        if lo > 0:
            parts.append(jnp.where(sm > score[:lo], 1.0, 0.0))
        gt =jnp.where(sm > score[lo:hi], 1.0, 0.0)
        ge = jnp.where(sm >= score[lo:hi], 1.0, 0.0)
        below = lax.broadcasted_iota(jnp.int32, gt.shape, 0) > m - lo
        parts.append(jnp.where(below, ge, gt))
        if hi < n:
            parts.append(jnp.where(sm >= score[hi:], 1.0, 0.0))
        rank = rank + (jnp.concatenate(parts, axis=0) if len(parts) > 1 else parts[0])
    return rank < k


def _pad_rows(a, rows):
    return jnp.concatenate([a, jnp.zeros((rows - a.shape[0], a.shape[1]), a.dtype)], axis=0)


def _masked_scores(k_ref, e_ref, start, size, q_aug):
    rows = pl.ds(pl.multiple_of(start, 128), size)
    k_aug = jnp.concatenate([k_ref[rows, :], e_ref[rows, :]], axis=1)
    return jnp.dot(k_aug, q_aug, preferred_element_type=F32)


def _pipelined_attention(chains, n_steps, n_max, tks, tk):
    per = tks // tk

    def scores(step, buf):
        st = jnp.minimum(step, n_max - 1)
        dead = jnp.where(step < n_steps, 0, 1)
        for k_ref, e_ref, _, qaug_ref, bufs, _, _ in chains:
            bufs[buf][...] = _masked_scores(k_ref, e_ref, st * tks, tks, qaug_ref[dead])

    def consume(step, buf):
        st = jnp.minimum(step, n_max - 1)
        for _, _, vT_ref, _, bufs, m_scr, acc_scr in chains:
            vT = jnp.concatenate([vT_ref[st * per + a] for a in range(per)], axis=1)
            _online_step(bufs[buf][...], vT, m_scr, acc_scr)

    def pair(p, carry):
        scores(2 * p + 1, 1)
        consume(2 * p, 0)
        scores(2 * p + 2, 0)
        consume(2 * p + 1, 1)
        return carry

    n_pairs = (n_steps + 1) // 2

    @pl.when(n_pairs > 0)
    def _():
        scores(0, 0)

    lax.fori_loop(0, n_pairs, pair, 0)


def _carry_casts(kernel_fn, n_in, n_out, n_cast):
    def wrapped(*refs):
        ins, cast_in = refs[:n_in], refs[n_in:n_in + n_cast]
        outs = refs[n_in + n_cast:n_in + n_cast + n_out]
        cast_out = refs[n_in + n_cast + n_out:n_in + n_cast + n_out + n_cast]
        for src, dst in zip(cast_in, cast_out):
            dst[...] = src[...].astype(BF16)
        kernel_fn(*ins, *outs, *refs[n_in + n_cast + n_out + n_cast:])
    return wrapped


def _cast_specs(weights, grid):
    n_steps = grid[0] * grid[1] * grid[2]
    specs, shapes = [], []
    for w, rows in weights:
        while w.shape[0] // rows > n_steps:
            rows *= 2
        n_blk = w.shape[0] // rows
        assert w.shape[0] % rows == 0
        index = lambda a, b, c, n_blk=n_blk: (jnp.minimum((a * grid[1] + b) * grid[2] + c, n_blk - 1), 0)
        specs.append(pl.BlockSpec((rows, w.shape[1]), index))
        shapes.append(jax.ShapeDtypeStruct(w.shape, BF16))
    return specs, shapes


def _nsa_kernel(qT_ref, kc_ref, vcT_ref, ks_ref, vsT_ref, kw_ref, vwT_ref, gT_ref, ovT_ref, e_ref,
                gain_ref, o_ref, m_scr, acc_scr, comb_scr, qaug_scr, s0_scr, s1_scr, *, T):
    R, tq = NSA_GROUP, NSA_TQ
    i = pl.program_id(2)
    t0 = i * tq
    ncp = T // CMP_STRIDE
    nsb = T // SLC_BLOCK
    qT = jnp.concatenate([qT_ref[r, c] for r in range(R) for c in range(tq // CHUNK)], axis=1)
    gT = gT_ref[0]

    def per_head(row0):
        return jnp.concatenate([gT[row0 + 3 * r:row0 + 3 * r + 1, :] for r in range(R)], axis=1)

    def lanes_x_heads(a):
        return jnp.concatenate([a] * R, axis=1)

    s = jnp.dot(kc_ref[0, 0], qT, preferred_element_type=F32)
    c_idx = lax.broadcasted_iota(jnp.int32, (ncp, tq), 0)
    t_c = t0 + lax.broadcasted_iota(jnp.int32, (ncp, tq), 1)
    vis = lanes_x_heads((c_idx * CMP_STRIDE + (CMP_BLOCK - 1) <= t_c) & (c_idx < ncp - 1))
    s = jnp.where(vis, s, NEG)
    m = jnp.max(s, axis=0, keepdims=True)
    p = jnp.where(vis, jnp.exp2(s - m), 0.0)
    l = jnp.sum(p, axis=0, keepdims=True)
    p = p / jnp.where(l > 0.0, l, 1.0)
    o_cmp = jnp.dot(vcT_ref[0, 0], p.astype(BF16), preferred_element_type=F32)
    comb_scr[...] = o_cmp * per_head(0)

    psum = p[:, 0:tq]
    for r in range(1, R):
        psum = psum + p[:, r * tq:(r + 1) * tq]
    ov = ovT_ref[...]
    imp = jnp.zeros((nsb, tq), F32)
    for part in _split3(psum):
        imp = imp + jnp.dot(ov, part, preferred_element_type=F32)
    n_idx = lax.broadcasted_iota(jnp.int32, (nsb, tq), 0)
    cur = (t0 + lax.broadcasted_iota(jnp.int32, (nsb, tq), 1)) // SLC_BLOCK
    forced = (n_idx == 0) | (n_idx == cur) | (n_idx == cur - 1)
    valid = n_idx <= cur
    score = jnp.where(valid, imp + jnp.where(forced, FORCE_BONUS, 0.0), NEG)
    sel = _rank_select(score, min(SLC_TOPK, nsb)) & valid
    sel_past = sel & (n_idx * SLC_BLOCK < t0)
    bias = _pad_rows(jnp.where(sel_past, 0.0, NEG), HEAD_DIM).astype(BF16)
    bias_own = _pad_rows(jnp.where(sel, 0.0, NEG), HEAD_DIM).astype(BF16)
    for d in range(3):
        qaug_scr[d, 0:HEAD_DIM, :] = qT
    qaug_scr[0, HEAD_DIM:, :] = lanes_x_heads(bias)
    qaug_scr[1, HEAD_DIM:, :] = jnp.full((HEAD_DIM, R * tq), NEG, BF16)
    qaug_scr[2, HEAD_DIM:, :] = lanes_x_heads(bias_own)

    w0 = jnp.maximum(t0 + tq - WIN_SPAN, 0)
    k_w = kw_ref[0, pl.ds(pl.multiple_of(w0, CHUNK), WIN_SPAN), :]
    s_w = jnp.dot(k_w, qT, preferred_element_type=F32)
    kpos = w0 + lax.broadcasted_iota(jnp.int32, (WIN_SPAN, tq), 0)
    tpos = t0 + lax.broadcasted_iota(jnp.int32, (WIN_SPAN, tq), 1)
    ok = lanes_x_heads((kpos <= tpos) & (tpos - kpos < WINDOW))
    s_w = jnp.where(ok, s_w, NEG)
    m_w = jnp.max(s_w, axis=0, keepdims=True)
    p_w = jnp.exp2(s_w - m_w)
    jw = w0 // CHUNK
    v_w = jnp.concatenate([vwT_ref[0, jw + a] for a in range(WIN_SPAN // CHUNK)], axis=1)
    o_w = jnp.dot(_with_ones(v_w), p_w.astype(BF16), preferred_element_type=F32)
    comb_scr[...] += _normalized(o_w) * per_head(2)

    tri = lax.broadcasted_iota(jnp.int32, (tq, tq), 0) <= lax.broadcasted_iota(jnp.int32, (tq, tq), 1)
    s_d = jnp.where(lanes_x_heads(tri), _masked_scores(ks_ref.at[0], e_ref, t0, tq, qaug_scr[2]), NEG)
    _reset(m_scr, acc_scr)
    n_steps = (t0 + SLC_TKS - 1) // SLC_TKS
    _pipelined_attention([(ks_ref.at[0], e_ref, vsT_ref.at[0], qaug_scr, (s0_scr, s1_scr),
                           m_scr, acc_scr)], n_steps, T // SLC_TKS, SLC_TKS, CHUNK)
    own = tq // CHUNK
    v_d = jnp.concatenate([vsT_ref[0, i * own + a] for a in range(own)], axis=1)
    _online_step(s_d, v_d, m_scr, acc_scr)
    comb = comb_scr[...] + _normalized(acc_scr[...]) * per_head(1)

    for r in range(R):
        oT = comb[:, r * tq:(r + 1) * tq]
        ms = jnp.mean(oT * oT, axis=0, keepdims=True)
        on = (oT * lax.rsqrt(ms + EPS)).T * gain_ref[0, r:r + 1, :]
        o_ref[:, r * HEAD_DIM:(r + 1) * HEAD_DIM] = on.astype(BF16)


def _nsa(yT, ystd, kcvc, vcT, gT, ovT, e_slc, gains, casts, B, T):
    G, R, tq = NSA_KV_HEADS, NSA_GROUP, NSA_TQ
    nq = T // tq
    ncp = T // CMP_STRIDE
    nsb = T // SLC_BLOCK
    grid = (B, G, nq)
    cast_specs, cast_shapes = _cast_specs(casts, grid)
    kern = _carry_casts(functools.partial(_nsa_kernel, T=T), 11, 1, len(casts))
    keys = lambda head0: pl.BlockSpec((1, T, HEAD_DIM), lambda b, g, i: (head0 + g, b, 0))
    values = lambda head0: pl.BlockSpec((1, T // CHUNK, HEAD_DIM, CHUNK), lambda b, g, i: (head0 + g, b, 0, 0))
    return pl.pallas_call(
        kern,
        grid=grid,
        in_specs=[pl.BlockSpec((R, tq // CHUNK, HEAD_DIM, CHUNK),
                               lambda b, g, i: (T_NSA_Q // R + g, b * nq + i, 0, 0)),
                  pl.BlockSpec((1, 1, ncp, HEAD_DIM), lambda b, g, i: (g, b, 0, 0)),
                  pl.BlockSpec((1, 1, HEAD_DIM, ncp), lambda b, g, i: (g, b, 0, 0)),
                  keys(S_KSLC), values(T_VSLC), keys(S_KWIN), values(T_VWIN),
                  pl.BlockSpec((1, GATE_ROWS, tq), lambda b, g, i: (g, 0, b * nq + i)),
                  pl.BlockSpec((nsb, ncp), lambda b, g, i: (0, 0)),
                  pl.BlockSpec((T, HEAD_DIM), lambda b, g, i: (0, 0)),
                  pl.BlockSpec((1, R, HEAD_DIM), lambda b, g, i: (g, 0, 0))] + cast_specs,
        out_specs=[pl.BlockSpec((tq, R * HEAD_DIM), lambda b, g, i: (b * nq + i, g))] + cast_specs,
        out_shape=[jax.ShapeDtypeStruct((B * T, NSA_HEADS * HEAD_DIM), BF16)] + cast_shapes,
        scratch_shapes=[pltpu.VMEM((1, R * tq), F32),
                        pltpu.VMEM((HEAD_DIM + ONES_ROWS, R * tq), F32), pltpu.VMEM((HEAD_DIM, R * tq), F32),
                        pltpu.VMEM((3, 2 * HEAD_DIM, R * tq), BF16),
                        pltpu.VMEM((SLC_TKS, R * tq), F32), pltpu.VMEM((SLC_TKS, R * tq), F32)],
        compiler_params=_cparams(("parallel", "parallel", "arbitrary")),
        name="nsa",
    )(yT, kcvc, vcT, ystd, yT, ystd, yT, gT, ovT, e_slc, gains, *[w for w, _ in casts])


def _moba_kernel(qT_ref, k_ref, vT_ref, e_ref, gain_ref, o_ref, m_scr, acc_scr, kmean_scr, qaug_scr,
                 s_scr, *, T):
    tq, tk = MOBA_TQ, MOBA_TK
    nb = T // MOBA_BLOCK
    nbp = kmean_scr.shape[1]
    i = pl.program_id(2)
    t0 = i * tq

    @pl.when(i == 0)
    def _():
        kmean_scr[...] = jnp.zeros(kmean_scr.shape, F32)
        for a in range(MOBA_HB):
            kb = k_ref[a].astype(F32).reshape(nb, MOBA_BLOCK, HEAD_DIM)
            kmean_scr[a, 0:nb, :] = jnp.mean(kb, axis=1)

    n_idx = lax.broadcasted_iota(jnp.int32, (nbp, tq), 0)
    cur = (t0 + lax.broadcasted_iota(jnp.int32, (nbp, tq), 1)) // MOBA_BLOCK
    past = n_idx < cur
    causal = lax.broadcasted_iota(jnp.int32, (tk, tq), 0) <= lax.broadcasted_iota(jnp.int32, (tk, tq), 1)

    qTs = [jnp.concatenate([qT_ref[a, c] for c in range(tq // CHUNK)], axis=1) for a in range(MOBA_HB)]
    for a in range(MOBA_HB):
        qT = qTs[a]
        gate = jnp.zeros((nbp, tq), F32)
        for part in _split3(kmean_scr[a]):
            gate = gate + jnp.dot(part, qT, preferred_element_type=F32)
        sel = _rank_select(jnp.where(past, gate, NEG), min(MOBA_TOPK, nb)) & past
        for d in range(2):
            qaug_scr[a, d, 0:HEAD_DIM, :] = qT
        qaug_scr[a, 0, HEAD_DIM:, :] = _pad_rows(jnp.where(sel, 0.0, NEG), HEAD_DIM).astype(BF16)
        qaug_scr[a, 1, HEAD_DIM:, :] = jnp.full((HEAD_DIM, tq), NEG, BF16)

    s_own = []
    for a in range(MOBA_HB):
        _reset(m_scr.at[a], acc_scr.at[a])
        k_own = k_ref[a, pl.ds(pl.multiple_of(t0, tk), tk), :]
        s_own.append(jnp.where(causal, jnp.dot(k_own, qTs[a], preferred_element_type=F32), NEG))

    chains = [(k_ref.at[a], e_ref, vT_ref.at[a], qaug_scr.at[a], (s_scr.at[a, 0], s_scr.at[a, 1]),
               m_scr.at[a], acc_scr.at[a]) for a in range(MOBA_HB)]
    n_steps = (t0 + MOBA_TKS - 1) // MOBA_TKS
    _pipelined_attention(chains, n_steps, T // MOBA_TKS, MOBA_TKS, CHUNK)

    own = tq // CHUNK
    for a in range(MOBA_HB):
        v_own = jnp.concatenate([vT_ref[a, i * own + c] for c in range(own)], axis=1)
        _online_step(s_own[a], v_own, m_scr.at[a], acc_scr.at[a])
        oT = _normalized(acc_scr[a])
        ms = jnp.mean(oT * oT, axis=0, keepdims=True)
        on = (oT * lax.rsqrt(ms + EPS)).T * gain_ref[a]
        o_ref[:, a * HEAD_DIM:(a + 1) * HEAD_DIM] = on.astype(BF16)


def _moba(yT, ystd, e_moba, gains, casts, B, T):
    H, HB, tq = MOBA_HEADS, MOBA_HB, MOBA_TQ
    nq = T // tq
    nb = T // MOBA_BLOCK
    nbp = max(16, nb)
    grid = (B, H // HB, nq)
    cast_specs, cast_shapes = _cast_specs(casts, grid)
    kern = _carry_casts(functools.partial(_moba_kernel, T=T), 5, 1, len(casts))
    return pl.pallas_call(
        kern,
        grid=grid,
        in_specs=[pl.BlockSpec((HB, tq // CHUNK, HEAD_DIM, CHUNK),
                               lambda b, h, i: (T_MOBA_Q // HB + h, b * nq + i, 0, 0)),
                  pl.BlockSpec((HB, T, HEAD_DIM), lambda b, h, i: (S_MOBA_K // HB + h, b, 0)),
                  pl.BlockSpec((HB, T // CHUNK, HEAD_DIM, CHUNK), lambda b, h, i: (T_MOBA_V // HB + h, b, 0, 0)),
                  pl.BlockSpec((T, HEAD_DIM), lambda b, h, i: (0, 0)),
                  pl.BlockSpec((HB, 1, HEAD_DIM), lambda b, h, i: (h, 0, 0))] + cast_specs,
        out_specs=[pl.BlockSpec((tq, HB * HEAD_DIM), lambda b, h, i: (b * nq + i, h))] + cast_specs,
        out_shape=[jax.ShapeDtypeStruct((B * T, H * HEAD_DIM), BF16)] + cast_shapes,
        scratch_shapes=[pltpu.VMEM((HB, 1, tq), F32),
                        pltpu.VMEM((HB, HEAD_DIM + ONES_ROWS, tq), F32), pltpu.VMEM((HB, nbp, HEAD_DIM), F32),
                        pltpu.VMEM((HB, 2, 2 * HEAD_DIM, tq), BF16),
                        pltpu.VMEM((HB, 2, MOBA_TKS, tq), F32)],
        compiler_params=_cparams(("parallel", "parallel", "arbitrary")),
        name="moba",
    )(yT, ystd, yT, e_moba, gains, *[w for w, _ in casts])


def _outproj_kernel(on_ref, om_ref, w_ref, x_ref, mod_ref, o_ref):
    half = on_ref.shape[1]
    acc = jnp.dot(on_ref[...], w_ref[0:half, :], preferred_element_type=F32)
    acc = acc + jnp.dot(om_ref[...], w_ref[half:, :], preferred_element_type=F32)
    o_ref[...] = x_ref[...] + mod_ref[0, 2:3, :] * acc


def _out_proj(o_nsa, o_moba, w_out, x2, mod3, T):
    BT, D = x2.shape
    tm = min(512, T)
    tpb = T // tm
    half = o_nsa.shape[1]
    return pl.pallas_call(
        _outproj_kernel,
        grid=(BT // tm,),
        in_specs=[pl.BlockSpec((tm, half), lambda i: (i, 0)),
                  pl.BlockSpec((tm, half), lambda i: (i, 0)),
                  pl.BlockSpec((D, D), lambda i: (0, 0)),
                  pl.BlockSpec((tm, D), lambda i: (i, 0)),
                  pl.BlockSpec((1, 6, D), lambda i: (i // tpb, 0, 0))],
        out_specs=pl.BlockSpec((tm, D), lambda i: (i, 0)),
        out_shape=jax.ShapeDtypeStruct((BT, D), F32),
        compiler_params=_cparams(("parallel",)),
        name="out_proj",
    )(o_nsa, o_moba, w_out, x2, mod3)


def _ffn_kernel(x_ref, mod_ref, wg_ref, wu_ref, wo_ref, o_ref, h_scr):
    j = pl.program_id(1)

    @pl.when(j == 0)
    def _():
        x = x_ref[...]
        ms = jnp.mean(x * x, axis=-1, keepdims=True)
        h = x * lax.rsqrt(ms + EPS) * (1.0 + mod_ref[0, 4:5, :]) + mod_ref[0, 3:4, :]
        h_scr[...] = h.astype(BF16)
        o_ref[...] = jnp.zeros(o_ref.shape, F32)

    hb = h_scr[...]
    gate = jnp.dot(hb, wg_ref[...], preferred_element_type=F32)
    up = jnp.dot(hb, wu_ref[...], preferred_element_type=F32)
    act = (gate * jax.nn.sigmoid(gate) * up).astype(BF16)
    o_ref[...] += jnp.dot(act, wo_ref[...], preferred_element_type=F32)

    @pl.when(j == pl.num_programs(1) - 1)
    def _():
        o_ref[...] = x_ref[...] + mod_ref[0, 5:6, :] * o_ref[...]


def _ffn(x1, mod3, w_in, w_out, T):
    BT, D = x1.shape
    Fh = w_out.shape[0]
    tm = min(512, T)
    tf = 512
    tpb = T // tm
    nf = Fh // tf
    return pl.pallas_call(
        _ffn_kernel,
        grid=(BT // tm, nf),
        in_specs=[pl.BlockSpec((tm, D), lambda i, j: (i, 0)),
                  pl.BlockSpec((1, 6, D), lambda i, j: (i // tpb, 0, 0)),
                  pl.BlockSpec((D, tf), lambda i, j: (0, j)),
                  pl.BlockSpec((D, tf), lambda i, j: (0, nf + j)),
                  pl.BlockSpec((tf, D), lambda i, j: (j, 0))],
        out_specs=pl.BlockSpec((tm, D), lambda i, j: (i, 0)),
        out_shape=jax.ShapeDtypeStruct((BT, D), F32),
        scratch_shapes=[pltpu.VMEM((tm, D), BF16)],
        compiler_params=_cparams(("parallel", "arbitrary")),
        name="ffn",
    )(x1, mod3, w_in, w_in, w_out)


def _rope_tables(T):
    inv = ROPE_THETA ** (-jnp.arange(0, ROPE_DIMS, 2, dtype=F32) / ROPE_DIMS)
    ang = jnp.arange(T).astype(F32)[:, None] * inv[None, :]
    cos, sin = jnp.cos(ang), jnp.sin(ang)
    rest = HEAD_DIM - ROPE_DIMS
    cc = jnp.concatenate([cos, cos, jnp.ones((T, rest), F32)], axis=1)
    sa = jnp.concatenate([-sin, jnp.zeros((T, HEAD_DIM - ROPE_HALF), F32)], axis=1)
    sb = jnp.concatenate([jnp.zeros((T, ROPE_HALF), F32), sin, jnp.zeros((T, rest), F32)], axis=1)
    return cc, sa, sb, cos.T, sin.T


def _block_onehot(T, block):
    return (jnp.arange(T)[:, None] // block == jnp.arange(HEAD_DIM)[None, :]).astype(BF16)


def _layer(x2, c, B, T, w_ada, b_ada, w_in, nsa_q_norm, nsa_k_norm, moba_q_norm, moba_k_norm,
           cmp_pe_k, cmp_w1_k, cmp_w2_k, cmp_pe_v, cmp_w1_v, cmp_w2_v, out_norm, w_out,
           w_ffn_in, w_ffn_out):
    D = x2.shape[1]
    G = NSA_KV_HEADS
    scale = HEAD_DIM ** -0.5 * LOG2E
    assert T % MOBA_BLOCK == 0 and T % SLC_TKS == 0 and T % MOBA_TKS == 0 and T % NSA_TQ == 0 and T >= WIN_SPAN
    assert T // SLC_BLOCK <= HEAD_DIM and T // MOBA_BLOCK <= HEAD_DIM

    mod3 = _adaln(c, w_ada, b_ada)

    nsa_w = NSA_HEADS * HEAD_DIM + 6 * G * HEAD_DIM
    gw = NSA_HEADS * 3
    w_nsa = w_in.astype(BF16)
    w_moba = w_nsa[:, nsa_w + gw:]
    wg = w_in[:, nsa_w:nsa_w + gw].reshape(D, G, NSA_GROUP * 3)
    w_gate = jnp.pad(wg, ((0, 0), (0, 0), (0, GATE_LANES - NSA_GROUP * 3))).reshape(D, G * GATE_LANES)
    w_gate = w_gate.astype(BF16)
    rep = lambda g_, n: jnp.broadcast_to(g_, (n, HEAD_DIM))
    plain = lambda n: jnp.ones((n, HEAD_DIM), F32)
    gains = jnp.concatenate(
        [rep(nsa_q_norm * scale, NSA_HEADS), plain(2 * G),
         rep(nsa_k_norm[1], G), plain(G), rep(nsa_k_norm[2], G), plain(G),
         rep(moba_q_norm * scale, MOBA_HEADS), rep(moba_k_norm, MOBA_HEADS), plain(MOBA_HEADS)], axis=0)
    gains = gains.reshape(len(PROJ_TILES), HEADS_PER_TILE, HEAD_DIM)
    q_gain = lambda g_: jnp.broadcast_to((g_ * scale)[:, None], (HEAD_DIM, CHUNK))
    gainsT = jnp.stack([q_gain(nsa_q_norm), q_gain(moba_q_norm)])
    cc, sa, sb, cosT, sinT = _rope_tables(T)

    yT, ystd, hc, gT = _in_proj(x2, mod3, w_nsa, w_moba, w_gate, gains, gainsT, cc, sa, sb, cosT, sinT, T)

    ncp = T // CMP_STRIDE
    half = CMP_STRIDE * HEAD_DIM
    w1 = jnp.stack([cmp_w1_k, cmp_w1_v]).astype(BF16)
    w1cat = jnp.concatenate([w1[:, :half], w1[:, half:]], axis=2)
    pe8 = jnp.broadcast_to(jnp.stack([cmp_pe_k, cmp_pe_v]).reshape(2, 1, 2 * half), (2, 8, 2 * half))
    w2 = jnp.stack([cmp_w2_k, cmp_w2_v]).astype(BF16)
    cmp_rows = lambda t: jnp.pad(t[CMP_BLOCK - 1::CMP_STRIDE], ((0, 1), (0, 0)))
    kcvc = _compress(hc.reshape(2 * G, B, ncp, half), w1cat, pe8.astype(BF16), w1, w2,
                     nsa_k_norm[0].reshape(1, HEAD_DIM),
                     cmp_rows(cc), cmp_rows(sa), cmp_rows(sb))
    vcT = kcvc[G:].transpose(0, 1, 3, 2)

    nsb = T // SLC_BLOCK
    cs = jnp.arange(ncp)[None, :] * CMP_STRIDE
    sbk = jnp.arange(nsb)[:, None] * SLC_BLOCK
    ovT = ((cs < sbk + SLC_BLOCK) & (cs + CMP_BLOCK > sbk) & (jnp.arange(ncp)[None, :] < ncp - 1)).astype(BF16)

    on = out_norm.reshape(N_HEADS, HEAD_DIM)
    o_nsa, w_ffn_out_b, w_out_b = _nsa(yT, ystd, kcvc, vcT, gT, ovT, _block_onehot(T, SLC_BLOCK),
                                       on[:NSA_HEADS].reshape(G, NSA_GROUP, HEAD_DIM),
                                       [(w_ffn_out, 128), (w_out, 128)], B, T)
    o_moba, w_ffn_in_b = _moba(yT, ystd, _block_onehot(T, MOBA_BLOCK),
                               on[NSA_HEADS:].reshape(MOBA_HEADS, 1, HEAD_DIM), [(w_ffn_in, 32)], B, T)

    x1 = _out_proj(o_nsa, o_moba, w_out_b, x2, mod3, T)
    return _ffn(x1, mod3, w_ffn_in_b, w_ffn_out_b, T)


def kernel(x, c, w_ada, b_ada, w_in, nsa_q_norm, nsa_k_norm, moba_q_norm, moba_k_norm, cmp_pe_k, cmp_w1_k, cmp_w2_k, cmp_pe_v, cmp_w1_v, cmp_w2_v, out_norm, w_out, w_ffn_in, w_ffn_out):
    B, T, D = x.shape
    x2 = x.reshape(B * T, D)
    for l in range(w_ada.shape[0]):
        x2 = _layer(x2, c, B, T, w_ada[l], b_ada[l], w_in[l], nsa_q_norm[l], nsa_k_norm[l],
                    moba_q_norm[l], moba_k_norm[l], cmp_pe_k[l], cmp_w1_k[l], cmp_w2_k[l],
                    cmp_pe_v[l], cmp_w1_v[l], cmp_w2_v[l], out_norm[l], w_out[l],
                    w_ffn_in[l], w_ffn_out[l])
    return x2.reshape(B, T, D)
```

```python
import functools

import jax
import jax.numpy as jnp
from jax import lax
from jax.experimental import pallas as pl
from jax.experimental.pallas import tpu as pltpu

F32 = jnp.float32
BF16 = jnp.bfloat16

HEAD_DIM = 128
SUBLANES = 8
NSA_HEADS = 8
NSA_KV_HEADS = 2
NSA_GROUP = NSA_HEADS // NSA_KV_HEADS
MOBA_HEADS = 8
N_HEADS = NSA_HEADS + MOBA_HEADS
CMP_BLOCK = 32
CMP_STRIDE = 16
SLC_BLOCK = 64
SLC_TOPK = 16
WINDOW = 512
FORCE_BONUS = 1e4
MOBA_BLOCK = 256
MOBA_TOPK = 3
ROPE_THETA = 500000.0
ROPE_DIMS = HEAD_DIM // 4
ROPE_HALF = ROPE_DIMS // 2
EPS = 1e-6
LOG2E = 1.4426950408889634
ONES_ROWS = 16
NEG = -1e30
M_INIT = -1e29

V7X_VMEM_BYTES = 64 * 1024 * 1024
VMEM_LIMIT = V7X_VMEM_BYTES - 8 * 1024 * 1024

HEADS_PER_TILE = 4
PROJ_TILES = ("nsa_q", "nsa_q", "kv_cmp", "kv_slc", "kv_win",
              "moba_q", "moba_q", "moba_k", "moba_k", "moba_v", "moba_v")
N_NSA_TILES = 5
GAIN_KINDS = ("nsa_q", "moba_q", "k_slc", "k_win", "moba_k")
PROJ_GAIN = {"nsa_q": "nsa_q", "moba_q": "moba_q", "kv_slc": "k_slc", "kv_win": "k_win", "moba_k": "moba_k"}
YT_ADVANCE = (1, 3, 5, 6, 9, 10)
YSTD_ADVANCE = (7, 8)
T_NSA_Q, T_VSLC, T_VWIN, T_MOBA_Q, T_MOBA_V = 0, 8, 10, 12, 20
S_KSLC, S_KWIN, S_MOBA_K = 0, 2, 4
N_T_HEADS, N_STD_HEADS, N_CMP_HEADS = 28, 12, 4
CHUNK = 128
GATE_LANES = 128
GATE_ROWS = 16

NSA_TQ = 256
SLC_TKS = 256
WIN_SPAN = WINDOW + NSA_TQ
MOBA_TQ = 256
MOBA_TK = MOBA_BLOCK
MOBA_TKS = 512
MOBA_HB = 4


def _cparams(sem):
    return pltpu.CompilerParams(dimension_semantics=sem, vmem_limit_bytes=VMEM_LIMIT)


def _split3(a):
    hi = a.astype(BF16)
    r1 = a - hi.astype(F32)
    mid = r1.astype(BF16)
    lo = (r1 - mid.astype(F32)).astype(BF16)
    return hi, mid, lo


def _adaln_kernel(c_ref, w_ref, b_ref, o_ref):
    cv = c_ref[...]
    s = cv * jax.nn.sigmoid(cv)
    w = w_ref[...].astype(BF16)
    acc = b_ref[...] + jnp.zeros(o_ref.shape, F32)
    for part in _split3(s)[:2]:
        acc = acc + jnp.dot(part, w, preferred_element_type=F32)
    o_ref[...] = acc


def _adaln(c, w_ada, b_ada):
    B, D = c.shape
    N = w_ada.shape[1]
    tn = 1024
    c8 = jnp.zeros((8, D), F32).at[:B].set(c)
    out = pl.pallas_call(
        _adaln_kernel,
        grid=(N // tn,),
        in_specs=[pl.BlockSpec((8, D), lambda j: (0, 0)),
                  pl.BlockSpec((D, tn), lambda j: (0, j)),
                  pl.BlockSpec((1, tn), lambda j: (0, j))],
        out_specs=pl.BlockSpec((8, tn), lambda j: (0, j)),
        out_shape=jax.ShapeDtypeStruct((8, N), F32),
        compiler_params=_cparams(("arbitrary",)),
        name="adaln",
    )(c8, w_ada, b_ada.reshape(1, N))
    return out[:B].reshape(B, 6, D)


def _rope(y, cc, sa, sb):
    return (y * cc + pltpu.roll(y, HEAD_DIM - ROPE_HALF, 1) * sa + pltpu.roll(y, ROPE_HALF, 1) * sb)


def _inproj_kernel(x_ref, mod_ref, wa_ref, wb_ref, wg_ref, gainT_ref, cosT_ref, sinT_ref,
                   yT_ref, ystd_ref, hc_ref, gT_ref, h_scr, rows_scr):
    j = pl.program_id(1)
    tm = x_ref.shape[0]
    n_chunks = tm // CHUNK
    pair = 2 * HEAD_DIM

    @pl.when(j == 0)
    def _():
        x = x_ref[...]
        ms = jnp.mean(x * x, axis=-1, keepdims=True)
        h = x * lax.rsqrt(ms + EPS) * (1.0 + mod_ref[0, 1:2, :]) + mod_ref[0, 0:1, :]
        hb = h.astype(BF16)
        h_scr[...] = hb
        g = jax.nn.sigmoid(jnp.dot(hb, wg_ref[...], preferred_element_type=F32))
        for grp in range(NSA_KV_HEADS):
            for c in range(n_chunks):
                blk = g[c * CHUNK:(c + 1) * CHUNK, grp * GATE_LANES:(grp + 1) * GATE_LANES].T
                gT_ref[grp, :, c * CHUNK:(c + 1) * CHUNK] = blk[0:GATE_ROWS, :]

    def head_pairs(w_ref):
        for half in range(HEADS_PER_TILE // 2):
            acc = jnp.dot(h_scr[...], w_ref[:, half * pair:(half + 1) * pair], preferred_element_type=F32)
            for h2 in range(2):
                yield 2 * half + h2, acc[:, h2 * HEAD_DIM:(h2 + 1) * HEAD_DIM]

    def qk_chunk(t, c):
        ms = jnp.mean(t * t, axis=0, keepdims=True)
        tn = t * lax.rsqrt(ms + EPS) * gainT_ref[0]
        cs = cosT_ref[:, c * CHUNK:(c + 1) * CHUNK]
        sn = sinT_ref[:, c * CHUNK:(c + 1) * CHUNK]
        a, b = tn[0:ROPE_HALF], tn[ROPE_HALF:ROPE_DIMS]
        return jnp.concatenate([a * cs - b * sn, b * cs + a * sn, tn[ROPE_DIMS:]], axis=0)

    def store_T(yh, slot, treated):
        for c in range(n_chunks):
            t = yh[c * CHUNK:(c + 1) * CHUNK, :].T
            yT_ref[slot, c] = (qk_chunk(t, c) if treated else t).astype(BF16)

    def store_rows(yh, slot):
        for c in range(n_chunks):
            rows = slice(c * CHUNK, (c + 1) * CHUNK)
            ystd_ref[slot, rows, :] = qk_chunk(yh[rows, :].T, c).T.astype(BF16)

    def when_kind(kind):
        steps = [t for t, name in enumerate(PROJ_TILES) if name == kind]
        return pl.when(functools.reduce(jnp.logical_or, [j == t for t in steps]))

    for kind, w_ref in (("nsa_q", wa_ref), ("moba_q", wb_ref)):
        @when_kind(kind)
        def _(w_ref=w_ref):
            for hh, yh in head_pairs(w_ref):
                store_T(yh, hh, True)

    @when_kind("moba_v")
    def _():
        for hh, yh in head_pairs(wb_ref):
            store_T(yh, hh, False)

    @when_kind("moba_k")
    def _():
        for hh, yh in head_pairs(wb_ref):
            store_rows(yh, hh)

    for kind, base in (("kv_slc", 0), ("kv_win", NSA_KV_HEADS)):
        @when_kind(kind)
        def _(base=base):
            for hh, yh in head_pairs(wa_ref):
                if hh < NSA_KV_HEADS:
                    store_rows(yh, base + hh)
                else:
                    store_T(yh, base + hh - NSA_KV_HEADS, False)

    @when_kind("kv_cmp")
    def _():
        for hh, yh in head_pairs(wa_ref):
            rows_scr[...] = yh
            flat = [rows_scr[pl.ds(l, tm // CMP_STRIDE, stride=CMP_STRIDE), :] for l in range(CMP_STRIDE)]
            hc_ref[hh] = jnp.concatenate(flat, axis=1).astype(BF16)


def _in_proj(x2, mod3, w_nsa, w_moba, w_gate, gainsT, cosT, sinT, T):
    BT, D = x2.shape
    G = NSA_KV_HEADS
    tm = min(1024, T)
    tn = HEADS_PER_TILE * HEAD_DIM
    tpb = T // tm
    advance = lambda j, steps: sum((j >= t).astype(jnp.int32) for t in steps)
    gain_kind = lambda j: sum((j == t).astype(jnp.int32) * GAIN_KINDS.index(PROJ_GAIN[name])
                              for t, name in enumerate(PROJ_TILES) if name in PROJ_GAIN)
    return pl.pallas_call(
        _inproj_kernel,
        grid=(BT // tm, len(PROJ_TILES)),
        in_specs=[pl.BlockSpec((tm, D), lambda i, j: (i, 0)),
                  pl.BlockSpec((1, 6, D), lambda i, j: (i // tpb, 0, 0)),
                  pl.BlockSpec((D, tn), lambda i, j: (0, jnp.minimum(j, N_NSA_TILES - 1))),
                  pl.BlockSpec((D, tn), lambda i, j: (0, jnp.maximum(j - N_NSA_TILES, 0))),
                  pl.BlockSpec((D, G * GATE_LANES), lambda i, j: (0, 0)),
                  pl.BlockSpec((1, HEAD_DIM, CHUNK), lambda i, j: (gain_kind(j), 0, 0)),
                  pl.BlockSpec((ROPE_HALF, tm), lambda i, j: (0, i % tpb)),
                  pl.BlockSpec((ROPE_HALF, tm), lambda i, j: (0, i % tpb))],
        out_specs=[pl.BlockSpec((HEADS_PER_TILE, tm // CHUNK, HEAD_DIM, CHUNK),
                                lambda i, j: (advance(j, YT_ADVANCE), i, 0, 0)),
                   pl.BlockSpec((HEADS_PER_TILE, tm, HEAD_DIM), lambda i, j: (advance(j, YSTD_ADVANCE), i, 0)),
                   pl.BlockSpec((N_CMP_HEADS, tm // CMP_STRIDE, CMP_STRIDE * HEAD_DIM), lambda i, j: (0, i, 0)),
                   pl.BlockSpec((G, GATE_ROWS, tm), lambda i, j: (0, 0, i))],
        out_shape=[jax.ShapeDtypeStruct((N_T_HEADS, BT // CHUNK, HEAD_DIM, CHUNK), BF16),
                   jax.ShapeDtypeStruct((N_STD_HEADS, BT, HEAD_DIM), BF16),
                   jax.ShapeDtypeStruct((N_CMP_HEADS, BT // CMP_STRIDE, CMP_STRIDE * HEAD_DIM), BF16),
                   jax.ShapeDtypeStruct((G, GATE_ROWS, BT), F32)],
        scratch_shapes=[pltpu.VMEM((tm, D), BF16), pltpu.VMEM((tm, HEAD_DIM), F32)],
        compiler_params=_cparams(("parallel", "arbitrary")),
        name="in_proj",
    )(x2, mod3, w_nsa, w_moba, w_gate, gainsT, cosT, sinT)


def _compress_kernel(h_ref, w1c_ref, pe_ref, w1_ref, w2_ref, gain_ref, cc_ref, sa_ref, sb_ref, o_ref):
    a = pl.program_id(0)
    ncp = h_ref.shape[2]
    z = jnp.dot(h_ref[0, 0], w1c_ref[0], preferred_element_type=F32)
    top = z[:, :HEAD_DIM]
    bot = pltpu.roll(z[:, HEAD_DIM:], ncp - 1, 0)
    pe_term = jnp.dot(pe_ref[0], w1_ref[0], preferred_element_type=F32)[0:1, :]
    pre = top + bot + pe_term
    act = pre * jax.nn.sigmoid(pre)
    out = jnp.dot(act.astype(BF16), w2_ref[0], preferred_element_type=F32)
    live = lax.broadcasted_iota(jnp.int32, out.shape, 0) < ncp - 1
    out = jnp.where(live, out, 0.0)

    @pl.when(a < NSA_KV_HEADS)
    def _():
        ms = jnp.mean(out * out, axis=-1, keepdims=True)
        yn = out * lax.rsqrt(ms + EPS) * gain_ref[...]
        o_ref[0, 0] = _rope(yn, cc_ref[...], sa_ref[...], sb_ref[...]).astype(BF16)

    @pl.when(a >= NSA_KV_HEADS)
    def _():
        o_ref[0, 0] = out.astype(BF16)


def _compress(hc, w1cat, pe8, w1, w2, gain, cc, sa, sb):
    A, B, ncp, K = hc.shape
    G = NSA_KV_HEADS
    return pl.pallas_call(
        _compress_kernel,
        grid=(A, B),
        in_specs=[pl.BlockSpec((1, 1, ncp, K), lambda a, b: (a, b, 0, 0)),
                  pl.BlockSpec((1, K, 2 * HEAD_DIM), lambda a, b: (a // G, 0, 0)),
                  pl.BlockSpec((1, 8, 2 * K), lambda a, b: (a // G, 0, 0)),
                  pl.BlockSpec((1, 2 * K, HEAD_DIM), lambda a, b: (a // G, 0, 0)),
                  pl.BlockSpec((1, HEAD_DIM, HEAD_DIM), lambda a, b: (a // G, 0, 0)),
                  pl.BlockSpec((1, HEAD_DIM), lambda a, b: (0, 0)),
                  pl.BlockSpec((ncp, HEAD_DIM), lambda a, b: (0, 0)),
                  pl.BlockSpec((ncp, HEAD_DIM), lambda a, b: (0, 0)),
                  pl.BlockSpec((ncp, HEAD_DIM), lambda a, b: (0, 0))],
        out_specs=pl.BlockSpec((1, 1, ncp, HEAD_DIM), lambda a, b: (a, b, 0, 0)),
        out_shape=jax.ShapeDtypeStruct((A, B, ncp, HEAD_DIM), BF16),
        compiler_params=_cparams(("arbitrary", "arbitrary")),
        name="compress",
    )(hc, w1cat, pe8, w1, w2, gain, cc, sa, sb)


def _reset(m_scr, acc_scr):
    m_scr[...] = jnp.full(m_scr.shape, M_INIT, F32)
    acc_scr[...] = jnp.zeros(acc_scr.shape, F32)


def _with_ones(vT):
    return jnp.concatenate([vT, jnp.ones((ONES_ROWS, vT.shape[1]), vT.dtype)], axis=0)


def _normalized(acc):
    return acc[0:HEAD_DIM] / acc[HEAD_DIM:HEAD_DIM + 1]


def _online_step(s, vT, m_scr, acc_scr):
    m_prev = m_scr[...]
    m_new = jnp.maximum(m_prev, jnp.max(s, axis=0, keepdims=True))
    alpha = jnp.exp2(m_prev - m_new)
    p = jnp.exp2(s - m_new)
    acc_scr[...] = alpha * acc_scr[...] + jnp.dot(_with_ones(vT), p.astype(BF16), preferred_element_type=F32)
    m_scr[...] = m_new


def _rank_select(score, k):
    n = score.shape[0]
    rank = jnp.zeros(score.shape, F32)
    for m in range(n):
        sm = score[m:m + 1, :]
        lo = (m // SUBLANES) * SUBLANES
        hi = min(lo + SUBLANES, n)
        parts = []
        if lo > 0:
            parts.append(jnp.where(sm > score[:lo], 1.0, 0.0))
        gt =jnp.where(sm > score[lo:hi], 1.0, 0.0)
        ge = jnp.where(sm >= score[lo:hi], 1.0, 0.0)
        below = lax.broadcasted_iota(jnp.int32, gt.shape, 0) > m - lo
        parts.append(jnp.where(below, ge, gt))
        if hi < n:
            parts.append(jnp.where(sm >= score[hi:], 1.0, 0.0))
        rank = rank + (jnp.concatenate(parts, axis=0) if len(parts) > 1 else parts[0])
    return rank < k


def _pad_rows(a, rows):
    return jnp.concatenate([a, jnp.zeros((rows - a.shape[0], a.shape[1]), a.dtype)], axis=0)


def _masked_scores(k_ref, e_ref, start, size, q_aug):
    rows = pl.ds(pl.multiple_of(start, 128), size)
    k_aug = jnp.concatenate([k_ref[rows, :], e_ref[rows, :]], axis=1)
    return jnp.dot(k_aug, q_aug, preferred_element_type=F32)


def _pipelined_attention(chains, n_steps, n_max, tks, tk):
    per = tks // tk

    def scores(step, buf):
        st = jnp.minimum(step, n_max - 1)
        dead = jnp.where(step < n_steps, 0, 1)
        for k_ref, e_ref, _, qaug_ref, bufs, _, _ in chains:
            bufs[buf][...] = _masked_scores(k_ref, e_ref, st * tks, tks, qaug_ref[dead])

    def consume(step, buf):
        st = jnp.minimum(step, n_max - 1)
        for _, _, vT_ref, _, bufs, m_scr, acc_scr in chains:
            vT = jnp.concatenate([vT_ref[st * per + a] for a in range(per)], axis=1)
            _online_step(bufs[buf][...], vT, m_scr, acc_scr)

    def pair(p, carry):
        scores(2 * p + 1, 1)
        consume(2 * p, 0)
        scores(2 * p + 2, 0)
        consume(2 * p + 1, 1)
        return carry

    n_pairs = (n_steps + 1) // 2

    @pl.when(n_pairs > 0)
    def _():
        scores(0, 0)

    lax.fori_loop(0, n_pairs, pair, 0)


def _carry_casts(kernel_fn, n_in, n_out, n_cast):
    def wrapped(*refs):
        ins, cast_in = refs[:n_in], refs[n_in:n_in + n_cast]
        outs = refs[n_in + n_cast:n_in + n_cast + n_out]
        cast_out = refs[n_in + n_cast + n_out:n_in + n_cast + n_out + n_cast]
        for src, dst in zip(cast_in, cast_out):
            dst[...] = src[...].astype(BF16)
        kernel_fn(*ins, *outs, *refs[n_in + n_cast + n_out + n_cast:])
    return wrapped


def _cast_specs(weights, grid):
    n_steps = grid[0] * grid[1] * grid[2]
    specs, shapes = [], []
    for w, rows in weights:
        while w.shape[0] // rows > n_steps:
            rows *= 2
        n_blk = w.shape[0] // rows
        assert w.shape[0] % rows == 0
        index = lambda a, b, c, n_blk=n_blk: (jnp.minimum((a * grid[1] + b) * grid[2] + c, n_blk - 1), 0)
        specs.append(pl.BlockSpec((rows, w.shape[1]), index))
        shapes.append(jax.ShapeDtypeStruct(w.shape, BF16))
    return specs, shapes


def _nsa_kernel(qT_ref, kc_ref, vcT_ref, ks_ref, vsT_ref, kw_ref, vwT_ref, gT_ref, e_ref,
                gain_ref, o_ref, m_scr, acc_scr, comb_scr, qaug_scr, s0_scr, s1_scr, psum_scr, *, T):
    R, tq = NSA_GROUP, NSA_TQ
    i = pl.program_id(2)
    t0 = i * tq
    ncp = T // CMP_STRIDE
    nsb = T // SLC_BLOCK
    qT = jnp.concatenate([qT_ref[r, c] for r in range(R) for c in range(tq // CHUNK)], axis=1)
    gT = gT_ref[0]

    def per_head(row0):
        return jnp.concatenate([gT[row0 + 3 * r:row0 + 3 * r + 1, :] for r in range(R)], axis=1)

    def lanes_x_heads(a):
        return jnp.concatenate([a] * R, axis=1)

    s = jnp.dot(kc_ref[0, 0], qT, preferred_element_type=F32)
    c_idx = lax.broadcasted_iota(jnp.int32, (ncp, tq), 0)
    t_c = t0 + lax.broadcasted_iota(jnp.int32, (ncp, tq), 1)
    vis = lanes_x_heads((c_idx * CMP_STRIDE + (CMP_BLOCK - 1) <= t_c) & (c_idx < ncp - 1))
    s = jnp.where(vis, s, NEG)
    m = jnp.maximum(jnp.max(s, axis=0, keepdims=True), M_INIT)
    p = jnp.exp2(s - m)
    l = jnp.sum(p, axis=0, keepdims=True)
    p = p / jnp.where(l > 0.0, l, 1.0)
    o_cmp = jnp.dot(vcT_ref[0, 0], p.astype(BF16), preferred_element_type=F32)
    comb_scr[...] = o_cmp * per_head(0)

    psum = p[:, 0:tq]
    for r in range(1, R):
        psum = psum + p[:, r * tq:(r + 1) * tq]
    per = SLC_BLOCK // CMP_STRIDE
    imp_chunks = []
    for c in range(tq // CHUNK):
        psum_scr[c, 0:SUBLANES, :] = jnp.zeros((SUBLANES, CHUNK), F32)
        psum_scr[c, SUBLANES:, :] = psum[:, c * CHUNK:(c + 1) * CHUNK]
        acc = jnp.zeros((nsb, CHUNK), F32)
        for k in range(1 - CMP_BLOCK // CMP_STRIDE, per):
            acc = acc + psum_scr[c, pl.ds(SUBLANES + k, nsb, stride=per), :]
        imp_chunks.append(acc)
    imp = jnp.concatenate(imp_chunks, axis=1)
    n_idx = lax.broadcasted_iota(jnp.int32, (nsb, tq), 0)
    cur = (t0 + lax.broadcasted_iota(jnp.int32, (nsb, tq), 1)) // SLC_BLOCK
    forced = (n_idx == 0) | (n_idx == cur) | (n_idx == cur - 1)
    valid = n_idx <= cur
    score = jnp.where(valid, imp + jnp.where(forced, FORCE_BONUS, 0.0), NEG)
    sel = _rank_select(score, min(SLC_TOPK, nsb)) & valid
    sel_past = sel & (n_idx * SLC_BLOCK < t0)
    bias = _pad_rows(jnp.where(sel_past, 0.0, NEG), HEAD_DIM).astype(BF16)
    bias_own = _pad_rows(jnp.where(sel, 0.0, NEG), HEAD_DIM).astype(BF16)
    for d in range(3):
        qaug_scr[d, 0:HEAD_DIM, :] = qT
    qaug_scr[0, HEAD_DIM:, :] = lanes_x_heads(bias)
    qaug_scr[1, HEAD_DIM:, :] = jnp.full((HEAD_DIM, R * tq), NEG, BF16)
    qaug_scr[2, HEAD_DIM:, :] = lanes_x_heads(bias_own)

    w0 = jnp.maximum(t0 + tq - WIN_SPAN, 0)
    k_w = kw_ref[0, pl.ds(pl.multiple_of(w0, CHUNK), WIN_SPAN), :]
    s_w = jnp.dot(k_w, qT, preferred_element_type=F32)
    kpos = w0 + lax.broadcasted_iota(jnp.int32, (WIN_SPAN, tq), 0)
    tpos = t0 + lax.broadcasted_iota(jnp.int32, (WIN_SPAN, tq), 1)
    ok = lanes_x_heads((kpos <= tpos) & (tpos - kpos < WINDOW))
    s_w = jnp.where(ok, s_w, NEG)
    m_w = jnp.max(s_w, axis=0, keepdims=True)
    p_w = jnp.exp2(s_w - m_w)
    jw = w0 // CHUNK
    v_w = jnp.concatenate([vwT_ref[0, jw + a] for a in range(WIN_SPAN // CHUNK)], axis=1)
    o_w = jnp.dot(_with_ones(v_w), p_w.astype(BF16), preferred_element_type=F32)
    comb_scr[...] += _normalized(o_w) * per_head(2)

    tri = lax.broadcasted_iota(jnp.int32, (tq, tq), 0) <= lax.broadcasted_iota(jnp.int32, (tq, tq), 1)
    s_d = jnp.where(lanes_x_heads(tri), _masked_scores(ks_ref.at[0], e_ref, t0, tq, qaug_scr[2]), NEG)
    _reset(m_scr, acc_scr)
    n_steps = (t0 + SLC_TKS - 1) // SLC_TKS
    _pipelined_attention([(ks_ref.at[0], e_ref, vsT_ref.at[0], qaug_scr, (s0_scr, s1_scr),
                           m_scr, acc_scr)], n_steps, T // SLC_TKS, SLC_TKS, CHUNK)
    own = tq // CHUNK
    v_d = jnp.concatenate([vsT_ref[0, i * own + a] for a in range(own)], axis=1)
    _online_step(s_d, v_d, m_scr, acc_scr)
    comb = comb_scr[...] + _normalized(acc_scr[...]) * per_head(1)

    for r in range(R):
        oT = comb[:, r * tq:(r + 1) * tq]
        ms = jnp.mean(oT * oT, axis=0, keepdims=True)
        on = (oT * lax.rsqrt(ms + EPS)).T * gain_ref[0, r:r + 1, :]
        o_ref[:, r * HEAD_DIM:(r + 1) * HEAD_DIM] = on.astype(BF16)


def _nsa(yT, ystd, kcvc, vcT, gT, e_slc, gains, casts, B, T):
    G, R, tq = NSA_KV_HEADS, NSA_GROUP, NSA_TQ
    nq = T // tq
    ncp = T // CMP_STRIDE
    nsb = T // SLC_BLOCK
    grid = (B, G, nq)
    cast_specs, cast_shapes = _cast_specs(casts, grid)
    kern = _carry_casts(functools.partial(_nsa_kernel, T=T), 10, 1, len(casts))
    keys = lambda head0: pl.BlockSpec((1, T, HEAD_DIM), lambda b, g, i: (head0 + g, b, 0))
    values = lambda head0: pl.BlockSpec((1, T // CHUNK, HEAD_DIM, CHUNK), lambda b, g, i: (head0 + g, b, 0, 0))
    return pl.pallas_call(
        kern,
        grid=grid,
        in_specs=[pl.BlockSpec((R, tq // CHUNK, HEAD_DIM, CHUNK),
                               lambda b, g, i: (T_NSA_Q // R + g, b * nq + i, 0, 0)),
                  pl.BlockSpec((1, 1, ncp, HEAD_DIM), lambda b, g, i: (g, b, 0, 0)),
                  pl.BlockSpec((1, 1, HEAD_DIM, ncp), lambda b, g, i: (g, b, 0, 0)),
                  keys(S_KSLC), values(T_VSLC), keys(S_KWIN), values(T_VWIN),
                  pl.BlockSpec((1, GATE_ROWS, tq), lambda b, g, i: (g, 0, b * nq + i)),
                  pl.BlockSpec((T, HEAD_DIM), lambda b, g, i: (0, 0)),
                  pl.BlockSpec((1, R, HEAD_DIM), lambda b, g, i: (g, 0, 0))] + cast_specs,
        out_specs=[pl.BlockSpec((tq, R * HEAD_DIM), lambda b, g, i: (b * nq + i, g))] + cast_specs,
        out_shape=[jax.ShapeDtypeStruct((B * T, NSA_HEADS * HEAD_DIM), BF16)] + cast_shapes,
        scratch_shapes=[pltpu.VMEM((1, R * tq), F32),
                        pltpu.VMEM((HEAD_DIM + ONES_ROWS, R * tq), F32), pltpu.VMEM((HEAD_DIM, R * tq), F32),
                        pltpu.VMEM((3, 2 * HEAD_DIM, R * tq), BF16),
                        pltpu.VMEM((SLC_TKS, R * tq), F32), pltpu.VMEM((SLC_TKS, R * tq), F32),
                        pltpu.VMEM((tq // CHUNK, SUBLANES + ncp, CHUNK), F32)],
        compiler_params=_cparams(("parallel", "parallel", "arbitrary")),
        name="nsa",
    )(yT, kcvc, vcT, ystd, yT, ystd, yT, gT, e_slc, gains, *[w for w, _ in casts])


def _moba_kernel(qT_ref, k_ref, vT_ref, e_ref, gain_ref, o_ref, m_scr, acc_scr, kmean_scr, qaug_scr,
                 s_scr, *, T):
    tq, tk = MOBA_TQ, MOBA_TK
    nb = T // MOBA_BLOCK
    nbp = kmean_scr.shape[1]
    i = pl.program_id(2)
    t0 = i * tq

    @pl.when(i == 0)
    def _():
        kmean_scr[...] = jnp.zeros(kmean_scr.shape, F32)
        for a in range(MOBA_HB):
            kb = k_ref[a].astype(F32).reshape(nb, MOBA_BLOCK, HEAD_DIM)
            kmean_scr[a, 0:nb, :] = jnp.mean(kb, axis=1)

    n_idx = lax.broadcasted_iota(jnp.int32, (nbp, tq), 0)
    cur = (t0 + lax.broadcasted_iota(jnp.int32, (nbp, tq), 1)) // MOBA_BLOCK
    past = n_idx < cur
    causal = lax.broadcasted_iota(jnp.int32, (tk, tq), 0) <= lax.broadcasted_iota(jnp.int32, (tk, tq), 1)

    qTs = [jnp.concatenate([qT_ref[a, c] for c in range(tq // CHUNK)], axis=1) for a in range(MOBA_HB)]
    for a in range(MOBA_HB):
        qT = qTs[a]
        gate = jnp.zeros((nbp, tq), F32)
        for part in _split3(kmean_scr[a]):
            gate = gate + jnp.dot(part, qT, preferred_element_type=F32)
        sel = _rank_select(jnp.where(past, gate, NEG), min(MOBA_TOPK, nb)) & past
        for d in range(2):
            qaug_scr[a, d, 0:HEAD_DIM, :] = qT
        qaug_scr[a, 0, HEAD_DIM:, :] = _pad_rows(jnp.where(sel, 0.0, NEG), HEAD_DIM).astype(BF16)
        qaug_scr[a, 1, HEAD_DIM:, :] = jnp.full((HEAD_DIM, tq), NEG, BF16)

    s_own = []
    for a in range(MOBA_HB):
        _reset(m_scr.at[a], acc_scr.at[a])
        k_own = k_ref[a, pl.ds(pl.multiple_of(t0, tk), tk), :]
        s_own.append(jnp.where(causal, jnp.dot(k_own, qTs[a], preferred_element_type=F32), NEG))

    chains = [(k_ref.at[a], e_ref, vT_ref.at[a], qaug_scr.at[a], (s_scr.at[a, 0], s_scr.at[a, 1]),
               m_scr.at[a], acc_scr.at[a]) for a in range(MOBA_HB)]
    n_steps = (t0 + MOBA_TKS - 1) // MOBA_TKS
    _pipelined_attention(chains, n_steps, T // MOBA_TKS, MOBA_TKS, CHUNK)

    own = tq // CHUNK
    for a in range(MOBA_HB):
        v_own = jnp.concatenate([vT_ref[a, i * own + c] for c in range(own)], axis=1)
        _online_step(s_own[a], v_own, m_scr.at[a], acc_scr.at[a])
        oT = _normalized(acc_scr[a])
        ms = jnp.mean(oT * oT, axis=0, keepdims=True)
        on = (oT * lax.rsqrt(ms + EPS)).T * gain_ref[a]
        o_ref[:, a * HEAD_DIM:(a + 1) * HEAD_DIM] = on.astype(BF16)


def _moba(yT, ystd, e_moba, gains, casts, B, T):
    H, HB, tq = MOBA_HEADS, MOBA_HB, MOBA_TQ
    nq = T // tq
    nb = T // MOBA_BLOCK
    nbp = max(16, nb)
    grid = (B, H // HB, nq)
    cast_specs, cast_shapes = _cast_specs(casts, grid)
    kern = _carry_casts(functools.partial(_moba_kernel, T=T), 5, 1, len(casts))
    return pl.pallas_call(
        kern,
        grid=grid,
        in_specs=[pl.BlockSpec((HB, tq // CHUNK, HEAD_DIM, CHUNK),
                               lambda b, h, i: (T_MOBA_Q // HB + h, b * nq + i, 0, 0)),
                  pl.BlockSpec((HB, T, HEAD_DIM), lambda b, h, i: (S_MOBA_K // HB + h, b, 0)),
                  pl.BlockSpec((HB, T // CHUNK, HEAD_DIM, CHUNK), lambda b, h, i: (T_MOBA_V // HB + h, b, 0, 0)),
                  pl.BlockSpec((T, HEAD_DIM), lambda b, h, i: (0, 0)),
                  pl.BlockSpec((HB, 1, HEAD_DIM), lambda b, h, i: (h, 0, 0))] + cast_specs,
        out_specs=[pl.BlockSpec((tq, HB * HEAD_DIM), lambda b, h, i: (b * nq + i, h))] + cast_specs,
        out_shape=[jax.ShapeDtypeStruct((B * T, H * HEAD_DIM), BF16)] + cast_shapes,
        scratch_shapes=[pltpu.VMEM((HB, 1, tq), F32),
                        pltpu.VMEM((HB, HEAD_DIM + ONES_ROWS, tq), F32), pltpu.VMEM((HB, nbp, HEAD_DIM), F32),
                        pltpu.VMEM((HB, 2, 2 * HEAD_DIM, tq), BF16),
                        pltpu.VMEM((HB, 2, MOBA_TKS, tq), F32)],
        compiler_params=_cparams(("parallel", "parallel", "arbitrary")),
        name="moba",
    )(yT, ystd, yT, e_moba, gains, *[w for w, _ in casts])


def _outproj_kernel(on_ref, om_ref, w_ref, x_ref, mod_ref, o_ref):
    half = on_ref.shape[1]
    acc = jnp.dot(on_ref[...], w_ref[0:half, :], preferred_element_type=F32)
    acc = acc + jnp.dot(om_ref[...], w_ref[half:, :], preferred_element_type=F32)
    o_ref[...] = x_ref[...] + mod_ref[0, 2:3, :] * acc


def _out_proj(o_nsa, o_moba, w_out, x2, mod3, T):
    BT, D = x2.shape
    tm = min(512, T)
    tpb = T // tm
    half = o_nsa.shape[1]
    return pl.pallas_call(
        _outproj_kernel,
        grid=(BT // tm,),
        in_specs=[pl.BlockSpec((tm, half), lambda i: (i, 0)),
                  pl.BlockSpec((tm, half), lambda i: (i, 0)),
                  pl.BlockSpec((D, D), lambda i: (0, 0)),
                  pl.BlockSpec((tm, D), lambda i: (i, 0)),
                  pl.BlockSpec((1, 6, D), lambda i: (i // tpb, 0, 0))],
        out_specs=pl.BlockSpec((tm, D), lambda i: (i, 0)),
        out_shape=jax.ShapeDtypeStruct((BT, D), F32),
        compiler_params=_cparams(("parallel",)),
        name="out_proj",
    )(o_nsa, o_moba, w_out, x2, mod3)


def _ffn_kernel(x_ref, mod_ref, wg_ref, wu_ref, wo_ref, o_ref, h_scr):
    j = pl.program_id(1)

    @pl.when(j == 0)
    def _():
        x = x_ref[...]
        ms = jnp.mean(x * x, axis=-1, keepdims=True)
        h = x * lax.rsqrt(ms + EPS) * (1.0 + mod_ref[0, 4:5, :]) + mod_ref[0, 3:4, :]
        h_scr[...] = h.astype(BF16)
        o_ref[...] = jnp.zeros(o_ref.shape, F32)

    hb = h_scr[...]
    gate = jnp.dot(hb, wg_ref[...], preferred_element_type=F32)
    up = jnp.dot(hb, wu_ref[...], preferred_element_type=F32)
    act = (gate * jax.nn.sigmoid(gate) * up).astype(BF16)
    o_ref[...] += jnp.dot(act, wo_ref[...], preferred_element_type=F32)

    @pl.when(j == pl.num_programs(1) - 1)
    def _():
        o_ref[...] = x_ref[...] + mod_ref[0, 5:6, :] * o_ref[...]


def _ffn(x1, mod3, w_in, w_out, T):
    BT, D = x1.shape
    Fh = w_out.shape[0]
    tm = min(512, T)
    tf = 512
    tpb = T // tm
    nf = Fh // tf
    return pl.pallas_call(
        _ffn_kernel,
        grid=(BT // tm, nf),
        in_specs=[pl.BlockSpec((tm, D), lambda i, j: (i, 0)),
                  pl.BlockSpec((1, 6, D), lambda i, j: (i // tpb, 0, 0)),
                  pl.BlockSpec((D, tf), lambda i, j: (0, j)),
                  pl.BlockSpec((D, tf), lambda i, j: (0, nf + j)),
                  pl.BlockSpec((tf, D), lambda i, j: (j, 0))],
        out_specs=pl.BlockSpec((tm, D), lambda i, j: (i, 0)),
        out_shape=jax.ShapeDtypeStruct((BT, D), F32),
        scratch_shapes=[pltpu.VMEM((tm, D), BF16)],
        compiler_params=_cparams(("parallel", "arbitrary")),
        name="ffn",
    )(x1, mod3, w_in, w_in, w_out)


def _rope_tables(T):
    inv = ROPE_THETA ** (-jnp.arange(0, ROPE_DIMS, 2, dtype=F32) / ROPE_DIMS)
    ang = jnp.arange(T).astype(F32)[:, None] * inv[None, :]
    cos, sin = jnp.cos(ang), jnp.sin(ang)
    rest = HEAD_DIM - ROPE_DIMS
    cc = jnp.concatenate([cos, cos, jnp.ones((T, rest), F32)], axis=1)
    sa = jnp.concatenate([-sin, jnp.zeros((T, HEAD_DIM - ROPE_HALF), F32)], axis=1)
    sb = jnp.concatenate([jnp.zeros((T, ROPE_HALF), F32), sin, jnp.zeros((T, rest), F32)], axis=1)
    return cc, sa, sb, cos.T, sin.T


def _block_onehot(T, block):
    return (jnp.arange(T)[:, None] // block == jnp.arange(HEAD_DIM)[None, :]).astype(BF16)


def _layer(x2, c, B, T, w_ada, b_ada, w_in, nsa_q_norm, nsa_k_norm, moba_q_norm, moba_k_norm,
           cmp_pe_k, cmp_w1_k, cmp_w2_k, cmp_pe_v, cmp_w1_v, cmp_w2_v, out_norm, w_out,
           w_ffn_in, w_ffn_out):
    D = x2.shape[1]
    G = NSA_KV_HEADS
    scale = HEAD_DIM ** -0.5 * LOG2E
    assert T % MOBA_BLOCK == 0 and T % SLC_TKS == 0 and T % MOBA_TKS == 0 and T % NSA_TQ == 0 and T >= WIN_SPAN
    assert T // SLC_BLOCK <= HEAD_DIM and T // MOBA_BLOCK <= HEAD_DIM

    mod3 = _adaln(c, w_ada, b_ada)

    nsa_w = NSA_HEADS * HEAD_DIM + 6 * G * HEAD_DIM
    gw = NSA_HEADS * 3
    w_nsa = w_in.astype(BF16)
    w_moba = w_nsa[:, nsa_w + gw:]
    wg = w_in[:, nsa_w:nsa_w + gw].reshape(D, G, NSA_GROUP * 3)
    w_gate = jnp.pad(wg, ((0, 0), (0, 0), (0, GATE_LANES - NSA_GROUP * 3))).reshape(D, G * GATE_LANES)
    w_gate = w_gate.astype(BF16)
    across = lambda g_: jnp.broadcast_to(g_[:, None], (HEAD_DIM, CHUNK))
    gainsT = jnp.stack([across(nsa_q_norm * scale), across(moba_q_norm * scale),
                        across(nsa_k_norm[1]), across(nsa_k_norm[2]), across(moba_k_norm)])
    cc, sa, sb, cosT, sinT = _rope_tables(T)

    yT, ystd, hc, gT = _in_proj(x2, mod3, w_nsa, w_moba, w_gate, gainsT, cosT, sinT, T)

    ncp = T // CMP_STRIDE
    half = CMP_STRIDE * HEAD_DIM
    w1 = jnp.stack([cmp_w1_k, cmp_w1_v]).astype(BF16)
    w1cat = jnp.concatenate([w1[:, :half], w1[:, half:]], axis=2)
    pe8 = jnp.broadcast_to(jnp.stack([cmp_pe_k, cmp_pe_v]).reshape(2, 1, 2 * half), (2, 8, 2 * half))
    w2 = jnp.stack([cmp_w2_k, cmp_w2_v]).astype(BF16)
    cmp_rows = lambda t: jnp.pad(t[CMP_BLOCK - 1::CMP_STRIDE], ((0, 1), (0, 0)))
    kcvc = _compress(hc.reshape(2 * G, B, ncp, half), w1cat, pe8.astype(BF16), w1, w2,
                     nsa_k_norm[0].reshape(1, HEAD_DIM),
                     cmp_rows(cc), cmp_rows(sa), cmp_rows(sb))
    vcT = kcvc[G:].transpose(0, 1, 3, 2)

    on = out_norm.reshape(N_HEADS, HEAD_DIM)
    o_nsa, w_ffn_out_b, w_out_b = _nsa(yT, ystd, kcvc, vcT, gT, _block_onehot(T, SLC_BLOCK),
                                       on[:NSA_HEADS].reshape(G, NSA_GROUP, HEAD_DIM),
                                       [(w_ffn_out, 128), (w_out, 128)], B, T)
    o_moba, w_ffn_in_b = _moba(yT, ystd, _block_onehot(T, MOBA_BLOCK),
                               on[NSA_HEADS:].reshape(MOBA_HEADS, 1, HEAD_DIM), [(w_ffn_in, 32)], B, T)

    x1 = _out_proj(o_nsa, o_moba, w_out_b, x2, mod3, T)
    return _ffn(x1, mod3, w_ffn_in_b, w_ffn_out_b, T)


def kernel(x, c, w_ada, b_ada, w_in, nsa_q_norm, nsa_k_norm, moba_q_norm, moba_k_norm, cmp_pe_k, cmp_w1_k, cmp_w2_k, cmp_pe_v, cmp_w1_v, cmp_w2_v, out_norm, w_out, w_ffn_in, w_ffn_out):
    B, T, D = x.shape
    x2 = x.reshape(B * T, D)
    for l in range(w_ada.shape[0]):
        x2 = _layer(x2, c, B, T, w_ada[l], b_ada[l], w_in[l], nsa_q_norm[l], nsa_k_norm[l],
                    moba_q_norm[l], moba_k_norm[l], cmp_pe_k[l], cmp_w1_k[l], cmp_w2_k[l],
                    cmp_pe_v[l], cmp_w1_v[l], cmp_w2_v[l], out_norm[l], w_out[l],
                    w_ffn_in[l], w_ffn_out[l])
    return x2.reshape(B, T, D)
```

```python
import functools

import jax
import jax.numpy as jnp
from jax import lax
from jax.experimental import pallas as pl
from jax.experimental.pallas import tpu as pltpu

F32 = jnp.float32
BF16 = jnp.bfloat16

HEAD_DIM = 128
SUBLANES = 8
NSA_HEADS = 8
NSA_KV_HEADS = 2
NSA_GROUP = NSA_HEADS // NSA_KV_HEADS
MOBA_HEADS = 8
N_HEADS = NSA_HEADS + MOBA_HEADS
CMP_BLOCK = 32
CMP_STRIDE = 16
SLC_BLOCK = 64
SLC_TOPK = 16
WINDOW = 512
FORCE_BONUS = 1e4
MOBA_BLOCK = 256
MOBA_TOPK = 3
ROPE_THETA = 500000.0
ROPE_DIMS = HEAD_DIM // 4
ROPE_HALF = ROPE_DIMS // 2
EPS = 1e-6
LOG2E = 1.4426950408889634
ONES_ROWS = 16
NEG = -1e30
M_INIT = -1e29

V7X_VMEM_BYTES = 64 * 1024 * 1024
VMEM_LIMIT = V7X_VMEM_BYTES - 8 * 1024 * 1024

HEADS_PER_TILE = 4
PROJ_TILES = ("nsa_q", "nsa_q", "kv_cmp", "kv_slc", "kv_win",
              "moba_q", "moba_q", "moba_k", "moba_k", "moba_v", "moba_v")
N_NSA_TILES = 5
GAIN_KINDS = ("nsa_q", "moba_q", "k_slc", "k_win", "moba_k")
PROJ_GAIN = {"nsa_q": "nsa_q", "moba_q": "moba_q", "kv_slc": "k_slc", "kv_win": "k_win", "moba_k": "moba_k"}
YT_ADVANCE = (1, 3, 5, 6, 9, 10)
YSTD_ADVANCE = (7, 8)
T_NSA_Q, T_VSLC, T_VWIN, T_MOBA_Q, T_MOBA_V = 0, 8, 10, 12, 20
S_KSLC, S_KWIN, S_MOBA_K = 0, 2, 4
N_T_HEADS, N_STD_HEADS, N_CMP_HEADS = 28, 12, 4
CHUNK = 128
GATE_LANES = 128
GATE_ROWS = 16

NSA_TQ = 256
SLC_TKS = 256
WIN_SPAN = WINDOW + NSA_TQ
MOBA_TQ = 256
MOBA_TK = MOBA_BLOCK
MOBA_TKS = 512
MOBA_HB = 4


def _cparams(sem):
    return pltpu.CompilerParams(dimension_semantics=sem, vmem_limit_bytes=VMEM_LIMIT)


def _split3(a):
    hi = a.astype(BF16)
    r1 = a - hi.astype(F32)
    mid = r1.astype(BF16)
    lo = (r1 - mid.astype(F32)).astype(BF16)
    return hi, mid, lo


def _adaln_kernel(c_ref, w_ref, b_ref, o_ref):
    cv = c_ref[...]
    s = cv * jax.nn.sigmoid(cv)
    w = w_ref[...].astype(BF16)
    acc = b_ref[...] + jnp.zeros(o_ref.shape, F32)
    for part in _split3(s)[:2]:
        acc = acc + jnp.dot(part, w, preferred_element_type=F32)
    o_ref[...] = acc


def _adaln(c, w_ada, b_ada):
    B, D = c.shape
    N = w_ada.shape[1]
    tn = 1024
    c8 = jnp.zeros((8, D), F32).at[:B].set(c)
    out = pl.pallas_call(
        _adaln_kernel,
        grid=(N // tn,),
        in_specs=[pl.BlockSpec((8, D), lambda j: (0, 0)),
                  pl.BlockSpec((D, tn), lambda j: (0, j)),
                  pl.BlockSpec((1, tn), lambda j: (0, j))],
        out_specs=pl.BlockSpec((8, tn), lambda j: (0, j)),
        out_shape=jax.ShapeDtypeStruct((8, N), F32),
        compiler_params=_cparams(("arbitrary",)),
        name="adaln",
    )(c8, w_ada, b_ada.reshape(1, N))
    return out[:B].reshape(B, 6, D)


def _rope(y, cc, sa, sb):
    return (y * cc + pltpu.roll(y, HEAD_DIM - ROPE_HALF, 1) * sa + pltpu.roll(y, ROPE_HALF, 1) * sb)


def _inproj_kernel(x_ref, mod_ref, wa_ref, wb_ref, wg_ref, gainT_ref, cosT_ref, sinT_ref,
                   yT_ref, ystd_ref, hc_ref, gT_ref, h_scr, rows_scr):
    j = pl.program_id(1)
    tm = x_ref.shape[0]
    n_chunks = tm // CHUNK
    pair = 2 * HEAD_DIM

    @pl.when(j == 0)
    def _():
        x = x_ref[...]
        ms = jnp.mean(x * x, axis=-1, keepdims=True)
        h = x * lax.rsqrt(ms + EPS) * (1.0 + mod_ref[0, 1:2, :]) + mod_ref[0, 0:1, :]
        hb = h.astype(BF16)
        h_scr[...] = hb
        g = jax.nn.sigmoid(jnp.dot(hb, wg_ref[...], preferred_element_type=F32))
        for grp in range(NSA_KV_HEADS):
            for c in range(n_chunks):
                blk = g[c * CHUNK:(c + 1) * CHUNK, grp * GATE_LANES:(grp + 1) * GATE_LANES].T
                gT_ref[grp, :, c * CHUNK:(c + 1) * CHUNK] = blk[0:GATE_ROWS, :]

    def head_pairs(w_ref):
        for half in range(HEADS_PER_TILE // 2):
            acc = jnp.dot(h_scr[...], w_ref[:, half * pair:(half + 1) * pair], preferred_element_type=F32)
            for h2 in range(2):
                yield 2 * half + h2, acc[:, h2 * HEAD_DIM:(h2 + 1) * HEAD_DIM]

    def qk_chunk(t, c):
        ms = jnp.mean(t * t, axis=0, keepdims=True)
        tn = t * lax.rsqrt(ms + EPS) * gainT_ref[0]
        cs = cosT_ref[:, c * CHUNK:(c + 1) * CHUNK]
        sn = sinT_ref[:, c * CHUNK:(c + 1) * CHUNK]
        a, b = tn[0:ROPE_HALF], tn[ROPE_HALF:ROPE_DIMS]
        return jnp.concatenate([a * cs - b * sn, b * cs + a * sn, tn[ROPE_DIMS:]], axis=0)

    def store_T(yh, slot, treated):
        for c in range(n_chunks):
            t = yh[c * CHUNK:(c + 1) * CHUNK, :].T
            yT_ref[slot, c] = (qk_chunk(t, c) if treated else t).astype(BF16)

    def store_rows(yh, slot):
        for c in range(n_chunks):
            rows = slice(c * CHUNK, (c + 1) * CHUNK)
            ystd_ref[slot, rows, :] = qk_chunk(yh[rows, :].T, c).T.astype(BF16)

    def when_kind(kind):
        steps = [t for t, name in enumerate(PROJ_TILES) if name == kind]
        return pl.when(functools.reduce(jnp.logical_or, [j == t for t in steps]))

    for kind, w_ref in (("nsa_q", wa_ref), ("moba_q", wb_ref)):
        @when_kind(kind)
        def _(w_ref=w_ref):
            for hh, yh in head_pairs(w_ref):
                store_T(yh, hh, True)

    @when_kind("moba_v")
    def _():
        for hh, yh in head_pairs(wb_ref):
            store_T(yh, hh, False)

    @when_kind("moba_k")
    def _():
        for hh, yh in head_pairs(wb_ref):
            store_rows(yh, hh)

    for kind, base in (("kv_slc", 0), ("kv_win", NSA_KV_HEADS)):
        @when_kind(kind)
        def _(base=base):
            for hh, yh in head_pairs(wa_ref):
                if hh < NSA_KV_HEADS:
                    store_rows(yh, base + hh)
                else:
                    store_T(yh, base + hh - NSA_KV_HEADS, False)

    @when_kind("kv_cmp")
    def _():
        for hh, yh in head_pairs(wa_ref):
            rows_scr[...] = yh
            flat = [rows_scr[pl.ds(l, tm // CMP_STRIDE, stride=CMP_STRIDE), :] for l in range(CMP_STRIDE)]
            hc_ref[hh] = jnp.concatenate(flat, axis=1).astype(BF16)


def _in_proj(x2, mod3, w_nsa, w_moba, w_gate, gainsT, cosT, sinT, T):
    BT, D = x2.shape
    G = NSA_KV_HEADS
    tm = min(1024, T)
    tn = HEADS_PER_TILE * HEAD_DIM
    tpb = T // tm
    advance = lambda j, steps: sum((j >= t).astype(jnp.int32) for t in steps)
    gain_kind = lambda j: sum((j == t).astype(jnp.int32) * GAIN_KINDS.index(PROJ_GAIN[name])
                              for t, name in enumerate(PROJ_TILES) if name in PROJ_GAIN)
    return pl.pallas_call(
        _inproj_kernel,
        grid=(BT // tm, len(PROJ_TILES)),
        in_specs=[pl.BlockSpec((tm, D), lambda i, j: (i, 0)),
                  pl.BlockSpec((1, 6, D), lambda i, j: (i // tpb, 0, 0)),
                  pl.BlockSpec((D, tn), lambda i, j: (0, jnp.minimum(j, N_NSA_TILES - 1))),
                  pl.BlockSpec((D, tn), lambda i, j: (0, jnp.maximum(j - N_NSA_TILES, 0))),
                  pl.BlockSpec((D, G * GATE_LANES), lambda i, j: (0, 0)),
                  pl.BlockSpec((1, HEAD_DIM, CHUNK), lambda i, j: (gain_kind(j), 0, 0)),
                  pl.BlockSpec((ROPE_HALF, tm), lambda i, j: (0, i % tpb)),
                  pl.BlockSpec((ROPE_HALF, tm), lambda i, j: (0, i % tpb))],
        out_specs=[pl.BlockSpec((HEADS_PER_TILE, tm // CHUNK, HEAD_DIM, CHUNK),
                                lambda i, j: (advance(j, YT_ADVANCE), i, 0, 0)),
                   pl.BlockSpec((HEADS_PER_TILE, tm, HEAD_DIM), lambda i, j: (advance(j, YSTD_ADVANCE), i, 0)),
                   pl.BlockSpec((N_CMP_HEADS, tm // CMP_STRIDE, CMP_STRIDE * HEAD_DIM), lambda i, j: (0, i, 0)),
                   pl.BlockSpec((G, GATE_ROWS, tm), lambda i, j: (0, 0, i))],
        out_shape=[jax.ShapeDtypeStruct((N_T_HEADS, BT // CHUNK, HEAD_DIM, CHUNK), BF16),
                   jax.ShapeDtypeStruct((N_STD_HEADS, BT, HEAD_DIM), BF16),
                   jax.ShapeDtypeStruct((N_CMP_HEADS, BT // CMP_STRIDE, CMP_STRIDE * HEAD_DIM), BF16),
                   jax.ShapeDtypeStruct((G, GATE_ROWS, BT), F32)],
        scratch_shapes=[pltpu.VMEM((tm, D), BF16), pltpu.VMEM((tm, HEAD_DIM), F32)],
        compiler_params=_cparams(("parallel", "arbitrary")),
        name="in_proj",
    )(x2, mod3, w_nsa, w_moba, w_gate, gainsT, cosT, sinT)


def _compress_kernel(h_ref, w1c_ref, pe_ref, w1_ref, w2_ref, gain_ref, cc_ref, sa_ref, sb_ref, o_ref):
    a = pl.program_id(0)
    ncp = h_ref.shape[2]
    z = jnp.dot(h_ref[0, 0], w1c_ref[0], preferred_element_type=F32)
    top = z[:, :HEAD_DIM]
    bot = pltpu.roll(z[:, HEAD_DIM:], ncp - 1, 0)
    pe_term = jnp.dot(pe_ref[0], w1_ref[0], preferred_element_type=F32)[0:1, :]
    pre = top + bot + pe_term
    act = pre * jax.nn.sigmoid(pre)
    out = jnp.dot(act.astype(BF16), w2_ref[0], preferred_element_type=F32)
    live = lax.broadcasted_iota(jnp.int32, out.shape, 0) < ncp - 1
    out = jnp.where(live, out, 0.0)

    @pl.when(a < NSA_KV_HEADS)
    def _():
        ms = jnp.mean(out * out, axis=-1, keepdims=True)
        yn = out * lax.rsqrt(ms + EPS) * gain_ref[...]
        o_ref[0, 0] = _rope(yn, cc_ref[...], sa_ref[...], sb_ref[...]).astype(BF16)

    @pl.when(a >= NSA_KV_HEADS)
    def _():
        o_ref[0, 0] = out.astype(BF16)


def _compress(hc, w1cat, pe8, w1, w2, gain, cc, sa, sb):
    A, B, ncp, K = hc.shape
    G = NSA_KV_HEADS
    return pl.pallas_call(
        _compress_kernel,
        grid=(A, B),
        in_specs=[pl.BlockSpec((1, 1, ncp, K), lambda a, b: (a, b, 0, 0)),
                  pl.BlockSpec((1, K, 2 * HEAD_DIM), lambda a, b: (a // G, 0, 0)),
                  pl.BlockSpec((1, 8, 2 * K), lambda a, b: (a // G, 0, 0)),
                  pl.BlockSpec((1, 2 * K, HEAD_DIM), lambda a, b: (a // G, 0, 0)),
                  pl.BlockSpec((1, HEAD_DIM, HEAD_DIM), lambda a, b: (a // G, 0, 0)),
                  pl.BlockSpec((1, HEAD_DIM), lambda a, b: (0, 0)),
                  pl.BlockSpec((ncp, HEAD_DIM), lambda a, b: (0, 0)),
                  pl.BlockSpec((ncp, HEAD_DIM), lambda a, b: (0, 0)),
                  pl.BlockSpec((ncp, HEAD_DIM), lambda a, b: (0, 0))],
        out_specs=pl.BlockSpec((1, 1, ncp, HEAD_DIM), lambda a, b: (a, b, 0, 0)),
        out_shape=jax.ShapeDtypeStruct((A, B, ncp, HEAD_DIM), BF16),
        compiler_params=_cparams(("arbitrary", "arbitrary")),
        name="compress",
    )(hc, w1cat, pe8, w1, w2, gain, cc, sa, sb)


def _first_step(s, vT, m_scr, acc_scr):
    m = jnp.max(s, axis=0, keepdims=True)
    p = jnp.exp2(s - m)
    acc_scr[...] = jnp.dot(_with_ones(vT), p.astype(BF16), preferred_element_type=F32)
    m_scr[...] = m


def _with_ones(vT):
    return jnp.concatenate([vT, jnp.ones((ONES_ROWS, vT.shape[1]), vT.dtype)], axis=0)


def _normalized(acc):
    return acc[0:HEAD_DIM] / acc[HEAD_DIM:HEAD_DIM + 1]


def _online_step(s, vT, m_scr, acc_scr):
    m_prev = m_scr[...]
    m_new = jnp.maximum(m_prev, jnp.max(s, axis=0, keepdims=True))
    alpha = jnp.exp2(m_prev - m_new)
    p = jnp.exp2(s - m_new)
    acc_scr[...] = alpha * acc_scr[...] + jnp.dot(_with_ones(vT), p.astype(BF16), preferred_element_type=F32)
    m_scr[...] = m_new


def _rank_select(score, k):
    n = score.shape[0]
    rank = jnp.zeros(score.shape, F32)
    for m in range(n):
        sm = score[m:m + 1, :]
        lo = (m // SUBLANES) * SUBLANES
        hi = min(lo + SUBLANES, n)
        parts = []
        if lo > 0:
            parts.append(jnp.where(sm > score[:lo], 1.0, 0.0))
        gt =jnp.where(sm > score[lo:hi], 1.0, 0.0)
        ge = jnp.where(sm >= score[lo:hi], 1.0, 0.0)
        below = lax.broadcasted_iota(jnp.int32, gt.shape, 0) > m - lo
        parts.append(jnp.where(below, ge, gt))
        if hi < n:
            parts.append(jnp.where(sm >= score[hi:], 1.0, 0.0))
        rank = rank + (jnp.concatenate(parts, axis=0) if len(parts) > 1 else parts[0])
    return rank < k


def _pad_rows(a, rows):
    return jnp.concatenate([a, jnp.zeros((rows - a.shape[0], a.shape[1]), a.dtype)], axis=0)


def _masked_scores(k_ref, e_ref, start, size, q_aug):
    rows = pl.ds(pl.multiple_of(start, 128), size)
    k_aug = jnp.concatenate([k_ref[rows, :], e_ref[rows, :]], axis=1)
    return jnp.dot(k_aug, q_aug, preferred_element_type=F32)


def _pipelined_attention(chains, n_steps, n_max, tks, tk):
    per = tks // tk

    def scores(step, buf):
        st = jnp.minimum(step, n_max - 1)
        dead = jnp.where(step < n_steps, 0, 1)
        for k_ref, e_ref, _, qaug_ref, bufs, _, _ in chains:
            bufs[buf][...] = _masked_scores(k_ref, e_ref, st * tks, tks, qaug_ref[dead])

    def consume(step, buf):
        st = jnp.minimum(step, n_max - 1)
        for _, _, vT_ref, _, bufs, m_scr, acc_scr in chains:
            vT = jnp.concatenate([vT_ref[st * per + a] for a in range(per)], axis=1)
            _online_step(bufs[buf][...], vT, m_scr, acc_scr)

    def pair(p, carry):
        scores(2 * p + 1, 1)
        consume(2 * p, 0)
        scores(2 * p + 2, 0)
        consume(2 * p + 1, 1)
        return carry

    scores(0, 0)
    lax.fori_loop(0, (n_steps + 1) // 2, pair, 0)


def _carry_casts(kernel_fn, n_in, n_out, n_cast):
    def wrapped(*refs):
        ins, cast_in = refs[:n_in], refs[n_in:n_in + n_cast]
        outs = refs[n_in + n_cast:n_in + n_cast + n_out]
        cast_out = refs[n_in + n_cast + n_out:n_in + n_cast + n_out + n_cast]
        for src, dst in zip(cast_in, cast_out):
            dst[...] = src[...].astype(BF16)
        kernel_fn(*ins, *outs, *refs[n_in + n_cast + n_out + n_cast:])
    return wrapped


def _cast_specs(weights, grid):
    n_steps = grid[0] * grid[1] * grid[2]
    specs, shapes = [], []
    for w, rows in weights:
        while w.shape[0] // rows > n_steps:
            rows *= 2
        n_blk = w.shape[0] // rows
        assert w.shape[0] % rows == 0
        index = lambda a, b, c, n_blk=n_blk: (jnp.minimum((a * grid[1] + b) * grid[2] + c, n_blk - 1), 0)
        specs.append(pl.BlockSpec((rows, w.shape[1]), index))
        shapes.append(jax.ShapeDtypeStruct(w.shape, BF16))
    return specs, shapes


def _nsa_kernel(qT_ref, kc_ref, vcT_ref, ks_ref, vsT_ref, kw_ref, vwT_ref, gT_ref, e_ref,
                gain_ref, o_ref, m_scr, acc_scr, comb_scr, qaug_scr, s0_scr, s1_scr, psum_scr, *, T):
    R, tq = NSA_GROUP, NSA_TQ
    i = pl.program_id(2)
    t0 = i * tq
    ncp = T // CMP_STRIDE
    nsb = T // SLC_BLOCK
    qT = jnp.concatenate([qT_ref[r, c] for r in range(R) for c in range(tq // CHUNK)], axis=1)
    gT = gT_ref[0]

    def per_head(row0):
        return jnp.concatenate([gT[row0 + 3 * r:row0 + 3 * r + 1, :] for r in range(R)], axis=1)

    def lanes_x_heads(a):
        return jnp.concatenate([a] * R, axis=1)

    s = jnp.dot(kc_ref[0, 0], qT, preferred_element_type=F32)
    c_idx = lax.broadcasted_iota(jnp.int32, (ncp, tq), 0)
    t_c = t0 + lax.broadcasted_iota(jnp.int32, (ncp, tq), 1)
    vis = lanes_x_heads((c_idx * CMP_STRIDE + (CMP_BLOCK - 1) <= t_c) & (c_idx < ncp - 1))
    s = jnp.where(vis, s, NEG)
    m = jnp.maximum(jnp.max(s, axis=0, keepdims=True), M_INIT)
    p = jnp.exp2(s - m)
    l = jnp.sum(p, axis=0, keepdims=True)
    p = p / jnp.where(l > 0.0, l, 1.0)
    o_cmp = jnp.dot(vcT_ref[0, 0], p.astype(BF16), preferred_element_type=F32)
    comb_scr[...] = o_cmp * per_head(0)

    psum = p[:, 0:tq]
    for r in range(1, R):
        psum = psum + p[:, r * tq:(r + 1) * tq]
    per = SLC_BLOCK // CMP_STRIDE
    imp_chunks = []
    for c in range(tq // CHUNK):
        psum_scr[c, 0:SUBLANES, :] = jnp.zeros((SUBLANES, CHUNK), F32)
        psum_scr[c, SUBLANES:, :] = psum[:, c * CHUNK:(c + 1) * CHUNK]
        acc = jnp.zeros((nsb, CHUNK), F32)
        for k in range(1 - CMP_BLOCK // CMP_STRIDE, per):
            acc = acc + psum_scr[c, pl.ds(SUBLANES + k, nsb, stride=per), :]
        imp_chunks.append(acc)
    imp = jnp.concatenate(imp_chunks, axis=1)
    n_idx = lax.broadcasted_iota(jnp.int32, (nsb, tq), 0)
    cur = (t0 + lax.broadcasted_iota(jnp.int32, (nsb, tq), 1)) // SLC_BLOCK
    forced = (n_idx == 0) | (n_idx == cur) | (n_idx == cur - 1)
    valid = n_idx <= cur
    score = jnp.where(valid, imp + jnp.where(forced, FORCE_BONUS, 0.0), NEG)
    sel = _rank_select(score, min(SLC_TOPK, nsb)) & valid
    sel_past = sel & (n_idx * SLC_BLOCK < t0)
    bias = _pad_rows(jnp.where(sel_past, 0.0, NEG), HEAD_DIM).astype(BF16)
    bias_own = _pad_rows(jnp.where(sel, 0.0, NEG), HEAD_DIM).astype(BF16)
    for d in range(3):
        qaug_scr[d, 0:HEAD_DIM, :] = qT
    qaug_scr[0, HEAD_DIM:, :] = lanes_x_heads(bias)
    qaug_scr[1, HEAD_DIM:, :] = jnp.full((HEAD_DIM, R * tq), NEG, BF16)
    qaug_scr[2, HEAD_DIM:, :] = lanes_x_heads(bias_own)

    w0 = jnp.maximum(t0 + tq - WIN_SPAN, 0)
    k_w = kw_ref[0, pl.ds(pl.multiple_of(w0, CHUNK), WIN_SPAN), :]
    s_w = jnp.dot(k_w, qT, preferred_element_type=F32)
    kpos = w0 + lax.broadcasted_iota(jnp.int32, (WIN_SPAN, tq), 0)
    tpos = t0 + lax.broadcasted_iota(jnp.int32, (WIN_SPAN, tq), 1)
    ok = lanes_x_heads((kpos <= tpos) & (tpos - kpos < WINDOW))
    s_w = jnp.where(ok, s_w, NEG)
    m_w = jnp.max(s_w, axis=0, keepdims=True)
    p_w = jnp.exp2(s_w - m_w)
    jw = w0 // CHUNK
    v_w = jnp.concatenate([vwT_ref[0, jw + a] for a in range(WIN_SPAN // CHUNK)], axis=1)
    o_w = jnp.dot(_with_ones(v_w), p_w.astype(BF16), preferred_element_type=F32)
    comb_scr[...] += _normalized(o_w) * per_head(2)

    tri =lax.broadcasted_iota(jnp.int32, (tq, tq), 0) <= lax.broadcasted_iota(jnp.int32, (tq, tq), 1)
    s_d = jnp.where(lanes_x_heads(tri), _masked_scores(ks_ref.at[0], e_ref, t0, tq, qaug_scr[2]), NEG)
    own = tq // CHUNK
    v_d = jnp.concatenate([vsT_ref[0, i * own + a] for a in range(own)], axis=1)
    _first_step(s_d, v_d, m_scr, acc_scr)
    n_steps = (t0 + SLC_TKS - 1) // SLC_TKS
    _pipelined_attention([(ks_ref.at[0], e_ref, vsT_ref.at[0], qaug_scr, (s0_scr, s1_scr),
                           m_scr, acc_scr)], n_steps, T // SLC_TKS, SLC_TKS, CHUNK)
    comb =comb_scr[...] + _normalized(acc_scr[...]) * per_head(1)

    for r in range(R):
        oT = comb[:, r * tq:(r + 1) * tq]
        ms = jnp.mean(oT * oT, axis=0, keepdims=True)
        on = (oT * lax.rsqrt(ms + EPS)).T * gain_ref[0, r:r + 1, :]
        o_ref[:, r * HEAD_DIM:(r + 1) * HEAD_DIM] = on.astype(BF16)


def _nsa(yT, ystd, kcvc, vcT, gT, e_slc, gains, casts, B, T):
    G, R, tq = NSA_KV_HEADS, NSA_GROUP, NSA_TQ
    nq = T // tq
    ncp = T // CMP_STRIDE
    nsb = T // SLC_BLOCK
    grid = (B, G, nq)
    cast_specs, cast_shapes = _cast_specs(casts, grid)
    kern = _carry_casts(functools.partial(_nsa_kernel, T=T), 10, 1, len(casts))
    keys = lambda head0: pl.BlockSpec((1, T, HEAD_DIM), lambda b, g, i: (head0 + g, b, 0))
    values = lambda head0: pl.BlockSpec((1, T // CHUNK, HEAD_DIM, CHUNK), lambda b, g, i: (head0 + g, b, 0, 0))
    return pl.pallas_call(
        kern,
        grid=grid,
        in_specs=[pl.BlockSpec((R, tq // CHUNK, HEAD_DIM, CHUNK),
                               lambda b, g, i: (T_NSA_Q // R + g, b * nq + i, 0, 0)),
                  pl.BlockSpec((1, 1, ncp, HEAD_DIM), lambda b, g, i: (g, b, 0, 0)),
                  pl.BlockSpec((1, 1, HEAD_DIM, ncp), lambda b, g, i: (g, b, 0, 0)),
                  keys(S_KSLC), values(T_VSLC), keys(S_KWIN), values(T_VWIN),
                  pl.BlockSpec((1, GATE_ROWS, tq), lambda b, g, i: (g, 0, b * nq + i)),
                  pl.BlockSpec((T, HEAD_DIM), lambda b, g, i: (0, 0)),
                  pl.BlockSpec((1, R, HEAD_DIM), lambda b, g, i: (g, 0, 0))] + cast_specs,
        out_specs=[pl.BlockSpec((tq, R * HEAD_DIM), lambda b, g, i: (b * nq + i, g))] + cast_specs,
        out_shape=[jax.ShapeDtypeStruct((B * T, NSA_HEADS * HEAD_DIM), BF16)] + cast_shapes,
        scratch_shapes=[pltpu.VMEM((1, R * tq), F32),
                        pltpu.VMEM((HEAD_DIM + ONES_ROWS, R * tq), F32), pltpu.VMEM((HEAD_DIM, R * tq), F32),
                        pltpu.VMEM((3, 2 * HEAD_DIM, R * tq), BF16),
                        pltpu.VMEM((SLC_TKS, R * tq), F32), pltpu.VMEM((SLC_TKS, R * tq), F32),
                        pltpu.VMEM((tq // CHUNK, SUBLANES + ncp, CHUNK), F32)],
        compiler_params=_cparams(("parallel", "parallel", "arbitrary")),
        name="nsa",
    )(yT, kcvc, vcT, ystd, yT, ystd, yT, gT, e_slc, gains, *[w for w, _ in casts])


def _moba_kernel(qT_ref, k_ref, vT_ref, e_ref, gain_ref, o_ref, m_scr, acc_scr, kmean_scr, qaug_scr,
                 s_scr, *, T):
    tq, tk = MOBA_TQ, MOBA_TK
    nb = T // MOBA_BLOCK
    nbp = kmean_scr.shape[1]
    i = pl.program_id(2)
    t0 = i * tq

    @pl.when(i == 0)
    def _():
        kmean_scr[...] = jnp.zeros(kmean_scr.shape, F32)
        for a in range(MOBA_HB):
            kb = k_ref[a].astype(F32).reshape(nb, MOBA_BLOCK, HEAD_DIM)
            kmean_scr[a, 0:nb, :] = jnp.mean(kb, axis=1)

    n_idx = lax.broadcasted_iota(jnp.int32, (nbp, tq), 0)
    cur = (t0 + lax.broadcasted_iota(jnp.int32, (nbp, tq), 1)) // MOBA_BLOCK
    past = n_idx < cur
    causal = lax.broadcasted_iota(jnp.int32, (tk, tq), 0) <= lax.broadcasted_iota(jnp.int32, (tk, tq), 1)

    qTs = [jnp.concatenate([qT_ref[a, c] for c in range(tq // CHUNK)], axis=1) for a in range(MOBA_HB)]
    for a in range(MOBA_HB):
        qT = qTs[a]
        gate = jnp.zeros((nbp, tq), F32)
        for part in _split3(kmean_scr[a]):
            gate = gate + jnp.dot(part, qT, preferred_element_type=F32)
        sel = _rank_select(jnp.where(past, gate, NEG), min(MOBA_TOPK, nb)) & past
        for d in range(2):
            qaug_scr[a, d, 0:HEAD_DIM, :] = qT
        qaug_scr[a, 0, HEAD_DIM:, :] = _pad_rows(jnp.where(sel, 0.0, NEG), HEAD_DIM).astype(BF16)
        qaug_scr[a, 1, HEAD_DIM:, :] = jnp.full((HEAD_DIM, tq), NEG, BF16)

    s_own = []
    for a in range(MOBA_HB):
        m_scr[a] = jnp.full(m_scr.shape[1:], M_INIT, F32)
        acc_scr[a] = jnp.zeros(acc_scr.shape[1:], F32)
        k_own = k_ref[a, pl.ds(pl.multiple_of(t0, tk), tk), :]
        s_own.append(jnp.where(causal, jnp.dot(k_own, qTs[a], preferred_element_type=F32), NEG))

    chains = [(k_ref.at[a], e_ref, vT_ref.at[a], qaug_scr.at[a], (s_scr.at[a, 0], s_scr.at[a, 1]),
               m_scr.at[a], acc_scr.at[a]) for a in range(MOBA_HB)]
    n_steps = (t0 + MOBA_TKS - 1) // MOBA_TKS
    _pipelined_attention(chains, n_steps, T // MOBA_TKS, MOBA_TKS, CHUNK)

    own = tq // CHUNK
    for a in range(MOBA_HB):
        v_own = jnp.concatenate([vT_ref[a, i * own + c] for c in range(own)], axis=1)
        _online_step(s_own[a], v_own, m_scr.at[a], acc_scr.at[a])
        oT = _normalized(acc_scr[a])
        ms = jnp.mean(oT * oT, axis=0, keepdims=True)
        on = (oT * lax.rsqrt(ms + EPS)).T * gain_ref[a]
        o_ref[:, a * HEAD_DIM:(a + 1) * HEAD_DIM] = on.astype(BF16)


def _moba(yT, ystd, e_moba, gains, casts, B, T):
    H, HB, tq = MOBA_HEADS, MOBA_HB, MOBA_TQ
    nq = T // tq
    nb = T // MOBA_BLOCK
    nbp = max(16, nb)
    grid = (B, H // HB, nq)
    cast_specs, cast_shapes = _cast_specs(casts, grid)
    kern = _carry_casts(functools.partial(_moba_kernel, T=T), 5, 1, len(casts))
    return pl.pallas_call(
        kern,
        grid=grid,
        in_specs=[pl.BlockSpec((HB, tq // CHUNK, HEAD_DIM, CHUNK),
                               lambda b, h, i: (T_MOBA_Q // HB + h, b * nq + i, 0, 0)),
                  pl.BlockSpec((HB, T, HEAD_DIM), lambda b, h, i: (S_MOBA_K // HB + h, b, 0)),
                  pl.BlockSpec((HB, T // CHUNK, HEAD_DIM, CHUNK), lambda b, h, i: (T_MOBA_V // HB + h, b, 0, 0)),
                  pl.BlockSpec((T, HEAD_DIM), lambda b, h, i: (0, 0)),
                  pl.BlockSpec((HB, 1, HEAD_DIM), lambda b, h, i: (h, 0, 0))] + cast_specs,
        out_specs=[pl.BlockSpec((tq, HB * HEAD_DIM), lambda b, h, i: (b * nq + i, h))] + cast_specs,
        out_shape=[jax.ShapeDtypeStruct((B * T, H * HEAD_DIM), BF16)] + cast_shapes,
        scratch_shapes=[pltpu.VMEM((HB, 1, tq), F32),
                        pltpu.VMEM((HB, HEAD_DIM + ONES_ROWS, tq), F32), pltpu.VMEM((HB, nbp, HEAD_DIM), F32),
                        pltpu.VMEM((HB, 2, 2 * HEAD_DIM, tq), BF16),
                        pltpu.VMEM((HB, 2, MOBA_TKS, tq), F32)],
        compiler_params=_cparams(("parallel", "parallel", "arbitrary")),
        name="moba",
    )(yT, ystd, yT, e_moba, gains, *[w for w, _ in casts])


def _outproj_kernel(on_ref, om_ref, w_ref, x_ref, mod_ref, o_ref):
    half = on_ref.shape[1]
    acc = jnp.dot(on_ref[...], w_ref[0:half, :], preferred_element_type=F32)
    acc = acc + jnp.dot(om_ref[...], w_ref[half:, :], preferred_element_type=F32)
    o_ref[...] = x_ref[...] + mod_ref[0, 2:3, :] * acc


def _out_proj(o_nsa, o_moba, w_out, x2, mod3, T):
    BT, D = x2.shape
    tm = min(512, T)
    tpb = T // tm
    half = o_nsa.shape[1]
    return pl.pallas_call(
        _outproj_kernel,
        grid=(BT // tm,),
        in_specs=[pl.BlockSpec((tm, half), lambda i: (i, 0)),
                  pl.BlockSpec((tm, half), lambda i: (i, 0)),
                  pl.BlockSpec((D, D), lambda i: (0, 0)),
                  pl.BlockSpec((tm, D), lambda i: (i, 0)),
                  pl.BlockSpec((1, 6, D), lambda i: (i // tpb, 0, 0))],
        out_specs=pl.BlockSpec((tm, D), lambda i: (i, 0)),
        out_shape=jax.ShapeDtypeStruct((BT, D), F32),
        compiler_params=_cparams(("parallel",)),
        name="out_proj",
    )(o_nsa, o_moba, w_out, x2, mod3)


def _ffn_kernel(x_ref, mod_ref, wg_ref, wu_ref, wo_ref, o_ref, h_scr):
    j = pl.program_id(1)

    @pl.when(j == 0)
    def _():
        x = x_ref[...]
        ms = jnp.mean(x * x, axis=-1, keepdims=True)
        h = x * lax.rsqrt(ms + EPS) * (1.0 + mod_ref[0, 4:5, :]) + mod_ref[0, 3:4, :]
        h_scr[...] = h.astype(BF16)
        o_ref[...] = jnp.zeros(o_ref.shape, F32)

    hb = h_scr[...]
    gate = jnp.dot(hb, wg_ref[...], preferred_element_type=F32)
    up = jnp.dot(hb, wu_ref[...], preferred_element_type=F32)
    act = (gate * jax.nn.sigmoid(gate) * up).astype(BF16)
    o_ref[...] += jnp.dot(act, wo_ref[...], preferred_element_type=F32)

    @pl.when(j == pl.num_programs(1) - 1)
    def _():
        o_ref[...] = x_ref[...] + mod_ref[0, 5:6, :] * o_ref[...]


def _ffn(x1, mod3, w_in, w_out, T):
    BT, D = x1.shape
    Fh = w_out.shape[0]
    tm = min(512, T)
    tf = 512
    tpb = T // tm
    nf = Fh // tf
    return pl.pallas_call(
        _ffn_kernel,
        grid=(BT // tm, nf),
        in_specs=[pl.BlockSpec((tm, D), lambda i, j: (i, 0)),
                  pl.BlockSpec((1, 6, D), lambda i, j: (i // tpb, 0, 0)),
                  pl.BlockSpec((D, tf), lambda i, j: (0, j)),
                  pl.BlockSpec((D, tf), lambda i, j: (0, nf + j)),
                  pl.BlockSpec((tf, D), lambda i, j: (j, 0))],
        out_specs=pl.BlockSpec((tm, D), lambda i, j: (i, 0)),
        out_shape=jax.ShapeDtypeStruct((BT, D), F32),
        scratch_shapes=[pltpu.VMEM((tm, D), BF16)],
        compiler_params=_cparams(("parallel", "arbitrary")),
        name="ffn",
    )(x1, mod3, w_in, w_in, w_out)


def _rope_tables(T):
    inv = ROPE_THETA ** (-jnp.arange(0, ROPE_DIMS, 2, dtype=F32) / ROPE_DIMS)
    ang = jnp.arange(T).astype(F32)[:, None] * inv[None, :]
    cos, sin = jnp.cos(ang), jnp.sin(ang)
    rest = HEAD_DIM - ROPE_DIMS
    cc = jnp.concatenate([cos, cos, jnp.ones((T, rest), F32)], axis=1)
    sa = jnp.concatenate([-sin, jnp.zeros((T, HEAD_DIM - ROPE_HALF), F32)], axis=1)
    sb = jnp.concatenate([jnp.zeros((T, ROPE_HALF), F32), sin, jnp.zeros((T, rest), F32)], axis=1)
    return cc, sa, sb, cos.T, sin.T


def _block_onehot(T, block):
    return (jnp.arange(T)[:, None] // block == jnp.arange(HEAD_DIM)[None, :]).astype(BF16)


def _layer(x2, c, B, T, w_ada, b_ada, w_in, nsa_q_norm, nsa_k_norm, moba_q_norm, moba_k_norm,
           cmp_pe_k, cmp_w1_k, cmp_w2_k, cmp_pe_v, cmp_w1_v, cmp_w2_v, out_norm, w_out,
           w_ffn_in, w_ffn_out):
    D = x2.shape[1]
    G = NSA_KV_HEADS
    scale = HEAD_DIM ** -0.5 * LOG2E
    assert T % MOBA_BLOCK == 0 and T % SLC_TKS == 0 and T % MOBA_TKS == 0 and T % NSA_TQ == 0 and T >= WIN_SPAN
    assert T // SLC_BLOCK <= HEAD_DIM and T // MOBA_BLOCK <= HEAD_DIM

    mod3 = _adaln(c, w_ada, b_ada)

    nsa_w = NSA_HEADS * HEAD_DIM + 6 * G * HEAD_DIM
    gw = NSA_HEADS * 3
    w_nsa = w_in.astype(BF16)
    w_moba = w_nsa[:, nsa_w + gw:]
    wg = w_in[:, nsa_w:nsa_w + gw].reshape(D, G, NSA_GROUP * 3)
    w_gate = jnp.pad(wg, ((0, 0), (0, 0), (0, GATE_LANES - NSA_GROUP * 3))).reshape(D, G * GATE_LANES)
    w_gate = w_gate.astype(BF16)
    across = lambda g_: jnp.broadcast_to(g_[:, None], (HEAD_DIM, CHUNK))
    gainsT = jnp.stack([across(nsa_q_norm * scale), across(moba_q_norm * scale),
                        across(nsa_k_norm[1]), across(nsa_k_norm[2]), across(moba_k_norm)])
    cc, sa, sb, cosT, sinT = _rope_tables(T)

    yT, ystd, hc, gT = _in_proj(x2, mod3, w_nsa, w_moba, w_gate, gainsT, cosT, sinT, T)

    ncp = T // CMP_STRIDE
    half = CMP_STRIDE * HEAD_DIM
    w1 = jnp.stack([cmp_w1_k, cmp_w1_v]).astype(BF16)
    w1cat = jnp.concatenate([w1[:, :half], w1[:, half:]], axis=2)
    pe8 = jnp.broadcast_to(jnp.stack([cmp_pe_k, cmp_pe_v]).reshape(2, 1, 2 * half), (2, 8, 2 * half))
    w2 = jnp.stack([cmp_w2_k, cmp_w2_v]).astype(BF16)
    cmp_rows = lambda t: jnp.pad(t[CMP_BLOCK - 1::CMP_STRIDE], ((0, 1), (0, 0)))
    kcvc = _compress(hc.reshape(2 * G, B, ncp, half), w1cat, pe8.astype(BF16), w1, w2,
                     nsa_k_norm[0].reshape(1, HEAD_DIM),
                     cmp_rows(cc), cmp_rows(sa), cmp_rows(sb))
    vcT = kcvc[G:].transpose(0, 1, 3, 2)

    on = out_norm.reshape(N_HEADS, HEAD_DIM)
    o_nsa, w_ffn_out_b, w_out_b = _nsa(yT, ystd, kcvc, vcT, gT, _block_onehot(T, SLC_BLOCK),
                                       on[:NSA_HEADS].reshape(G, NSA_GROUP, HEAD_DIM),
                                       [(w_ffn_out, 128), (w_out, 128)], B, T)
    o_moba, w_ffn_in_b = _moba(yT, ystd, _block_onehot(T, MOBA_BLOCK),
                               on[NSA_HEADS:].reshape(MOBA_HEADS, 1, HEAD_DIM), [(w_ffn_in, 32)], B, T)

    x1 = _out_proj(o_nsa, o_moba, w_out_b, x2, mod3, T)
    return _ffn(x1, mod3, w_ffn_in_b, w_ffn_out_b, T)


def kernel(x, c, w_ada, b_ada, w_in, nsa_q_norm, nsa_k_norm, moba_q_norm, moba_k_norm, cmp_pe_k, cmp_w1_k, cmp_w2_k, cmp_pe_v, cmp_w1_v, cmp_w2_v, out_norm, w_out, w_ffn_in, w_ffn_out):
    B, T, D = x.shape
    x2 = x.reshape(B * T, D)
    for l in range(w_ada.shape[0]):
        x2 = _layer(x2, c, B, T, w_ada[l], b_ada[l], w_in[l], nsa_q_norm[l], nsa_k_norm[l],
                    moba_q_norm[l], moba_k_norm[l], cmp_pe_k[l], cmp_w1_k[l], cmp_w2_k[l],
                    cmp_pe_v[l], cmp_w1_v[l], cmp_w2_v[l], out_norm[l], w_out[l],
                    w_ffn_in[l], w_ffn_out[l])
    return x2.reshape(B, T, D)
```

```python
import functools

import jax
import jax.numpy as jnp
from jax import lax
from jax.experimental import pallas as pl
from jax.experimental.pallas import tpu as pltpu

F32 = jnp.float32
BF16 = jnp.bfloat16

HEAD_DIM = 128
SUBLANES = 8
NSA_HEADS = 8
NSA_KV_HEADS = 2
NSA_GROUP = NSA_HEADS // NSA_KV_HEADS
MOBA_HEADS = 8
N_HEADS = NSA_HEADS + MOBA_HEADS
CMP_BLOCK = 32
CMP_STRIDE = 16
SLC_BLOCK = 64
SLC_TOPK = 16
WINDOW = 512
FORCE_BONUS = 1e4
MOBA_BLOCK = 256
MOBA_TOPK = 3
ROPE_THETA = 500000.0
ROPE_DIMS = HEAD_DIM // 4
ROPE_HALF = ROPE_DIMS // 2
EPS = 1e-6
LOG2E = 1.4426950408889634
ONES_ROWS = 16
NEG = -1e30
M_INIT = -1e29

V7X_VMEM_BYTES = 64 * 1024 * 1024
VMEM_LIMIT = V7X_VMEM_BYTES - 8 * 1024 * 1024

HEADS_PER_TILE = 4
PROJ_TILES = ("nsa_q", "nsa_q", "kv_cmp", "kv_slc", "kv_win",
              "moba_q", "moba_q", "moba_k", "moba_k", "moba_v", "moba_v")
N_NSA_TILES = 5
GAIN_KINDS = ("nsa_q", "moba_q", "k_slc", "k_win", "moba_k")
PROJ_GAIN = {"nsa_q": "nsa_q", "moba_q": "moba_q", "kv_slc": "k_slc", "kv_win": "k_win", "moba_k": "moba_k"}
YT_ADVANCE = (1, 3, 5, 6, 9, 10)
YSTD_ADVANCE = (7, 8)
T_NSA_Q, T_VSLC, T_VWIN, T_MOBA_Q, T_MOBA_V = 0, 8, 10, 12, 20
S_KSLC, S_KWIN, S_MOBA_K = 0, 2, 4
N_T_HEADS, N_STD_HEADS, N_CMP_HEADS = 28, 12, 4
CHUNK = 128
GATE_LANES = 128
GATE_ROWS = 16

ATTN_TQ = 256
ATTN_TKS = 256
ATTN_BODY_PAIRS = (4, 2, 1)
NSA_TQ = ATTN_TQ
SLC_TKS = ATTN_TKS
WIN_SPAN = WINDOW + NSA_TQ
MOBA_TQ = ATTN_TQ
MOBA_TK = MOBA_BLOCK
MOBA_TKS = ATTN_TKS
MOBA_HB = 4


def _cparams(sem):
    return pltpu.CompilerParams(dimension_semantics=sem, vmem_limit_bytes=VMEM_LIMIT)


def _split3(a):
    hi = a.astype(BF16)
    r1 = a - hi.astype(F32)
    mid = r1.astype(BF16)
    lo = (r1 - mid.astype(F32)).astype(BF16)
    return hi, mid, lo


def _adaln_kernel(c_ref, w_ref, b_ref, o_ref):
    cv = c_ref[...]
    s = cv * jax.nn.sigmoid(cv)
    w = w_ref[...].astype(BF16)
    acc = b_ref[...] + jnp.zeros(o_ref.shape, F32)
    for part in _split3(s)[:2]:
        acc = acc + jnp.dot(part, w, preferred_element_type=F32)
    o_ref[...] = acc


def _adaln(c, w_ada, b_ada):
    B, D = c.shape
    N = w_ada.shape[1]
    tn = 1024
    c8 = jnp.zeros((8, D), F32).at[:B].set(c)
    out = pl.pallas_call(
        _adaln_kernel,
        grid=(N // tn,),
        in_specs=[pl.BlockSpec((8, D), lambda j: (0, 0)),
                  pl.BlockSpec((D, tn), lambda j: (0, j)),
                  pl.BlockSpec((1, tn), lambda j: (0, j))],
        out_specs=pl.BlockSpec((8, tn), lambda j: (0, j)),
        out_shape=jax.ShapeDtypeStruct((8, N), F32),
        compiler_params=_cparams(("arbitrary",)),
        name="adaln",
    )(c8, w_ada, b_ada.reshape(1, N))
    return out[:B].reshape(B, 6, D)


def _rope(y, cc, sa, sb):
    return (y * cc + pltpu.roll(y, HEAD_DIM - ROPE_HALF, 1) * sa + pltpu.roll(y, ROPE_HALF, 1) * sb)


def _inproj_kernel(x_ref, mod_ref, wa_ref, wb_ref, wg_ref, gainT_ref, cosT_ref, sinT_ref,
                   yT_ref, ystd_ref, hc_ref, gT_ref, h_scr, rows_scr):
    j = pl.program_id(1)
    tm = x_ref.shape[0]
    n_chunks = tm // CHUNK
    pair = 2 * HEAD_DIM

    @pl.when(j == 0)
    def _():
        x = x_ref[...]
        ms = jnp.mean(x * x, axis=-1, keepdims=True)
        h = x * lax.rsqrt(ms + EPS) * (1.0 + mod_ref[0, 1:2, :]) + mod_ref[0, 0:1, :]
        hb = h.astype(BF16)
        h_scr[...] = hb
        g = jax.nn.sigmoid(jnp.dot(hb, wg_ref[...], preferred_element_type=F32))
        for grp in range(NSA_KV_HEADS):
            for c in range(n_chunks):
                blk = g[c * CHUNK:(c + 1) * CHUNK, grp * GATE_LANES:(grp + 1) * GATE_LANES].T
                gT_ref[grp, :, c * CHUNK:(c + 1) * CHUNK] = blk[0:GATE_ROWS, :]

    def head_pairs(w_ref):
        for half in range(HEADS_PER_TILE // 2):
            acc = jnp.dot(h_scr[...], w_ref[:, half * pair:(half + 1) * pair], preferred_element_type=F32)
            for h2 in range(2):
                yield 2 * half + h2, acc[:, h2 * HEAD_DIM:(h2 + 1) * HEAD_DIM]

    def qk_chunk(t, c):
        ms = jnp.mean(t * t, axis=0, keepdims=True)
        tn = t * lax.rsqrt(ms + EPS) * gainT_ref[0]
        cs = cosT_ref[:, c * CHUNK:(c + 1) * CHUNK]
        sn = sinT_ref[:, c * CHUNK:(c + 1) * CHUNK]
        a, b = tn[0:ROPE_HALF], tn[ROPE_HALF:ROPE_DIMS]
        return jnp.concatenate([a * cs - b * sn, b * cs + a * sn, tn[ROPE_DIMS:]], axis=0)

    def store_T(yh, slot, treated):
        for c in range(n_chunks):
            t = yh[c * CHUNK:(c + 1) * CHUNK, :].T
            yT_ref[slot, c] = (qk_chunk(t, c) if treated else t).astype(BF16)

    def store_rows(yh, slot):
        for c in range(n_chunks):
            rows = slice(c * CHUNK, (c + 1) * CHUNK)
            ystd_ref[slot, rows, :] = qk_chunk(yh[rows, :].T, c).T.astype(BF16)

    def when_kind(kind):
        steps = [t for t, name in enumerate(PROJ_TILES) if name == kind]
        return pl.when(functools.reduce(jnp.logical_or, [j == t for t in steps]))

    for kind, w_ref in (("nsa_q", wa_ref), ("moba_q", wb_ref)):
        @when_kind(kind)
        def _(w_ref=w_ref):
            for hh, yh in head_pairs(w_ref):
                store_T(yh, hh, True)

    @when_kind("moba_v")
    def _():
        for hh, yh in head_pairs(wb_ref):
            store_T(yh, hh, False)

    @when_kind("moba_k")
    def _():
        for hh, yh in head_pairs(wb_ref):
            store_rows(yh, hh)

    for kind, base in (("kv_slc", 0), ("kv_win", NSA_KV_HEADS)):
        @when_kind(kind)
        def _(base=base):
            for hh, yh in head_pairs(wa_ref):
                if hh < NSA_KV_HEADS:
                    store_rows(yh, base + hh)
                else:
                    store_T(yh, base + hh - NSA_KV_HEADS, False)

    @when_kind("kv_cmp")
    def _():
        for hh, yh in head_pairs(wa_ref):
            rows_scr[...] = yh
            flat = [rows_scr[pl.ds(l, tm // CMP_STRIDE, stride=CMP_STRIDE), :] for l in range(CMP_STRIDE)]
            hc_ref[hh] = jnp.concatenate(flat, axis=1).astype(BF16)


def _in_proj(x2, mod3, w_nsa, w_moba, w_gate, gainsT, cosT, sinT, T):
    BT, D = x2.shape
    G = NSA_KV_HEADS
    tm = min(1024, T)
    tn = HEADS_PER_TILE * HEAD_DIM
    tpb = T // tm
    advance = lambda j, steps: sum((j >= t).astype(jnp.int32) for t in steps)
    gain_kind = lambda j: sum((j == t).astype(jnp.int32) * GAIN_KINDS.index(PROJ_GAIN[name])
                              for t, name in enumerate(PROJ_TILES) if name in PROJ_GAIN)
    return pl.pallas_call(
        _inproj_kernel,
        grid=(BT // tm, len(PROJ_TILES)),
        in_specs=[pl.BlockSpec((tm, D), lambda i, j: (i, 0)),
                  pl.BlockSpec((1, 6, D), lambda i, j: (i // tpb, 0, 0)),
                  pl.BlockSpec((D, tn), lambda i, j: (0, jnp.minimum(j, N_NSA_TILES - 1))),
                  pl.BlockSpec((D, tn), lambda i, j: (0, jnp.maximum(j - N_NSA_TILES, 0))),
                  pl.BlockSpec((D, G * GATE_LANES), lambda i, j: (0, 0)),
                  pl.BlockSpec((1, HEAD_DIM, CHUNK), lambda i, j: (gain_kind(j), 0, 0)),
                  pl.BlockSpec((ROPE_HALF, tm), lambda i, j: (0, i % tpb)),
                  pl.BlockSpec((ROPE_HALF, tm), lambda i, j: (0, i % tpb))],
        out_specs=[pl.BlockSpec((HEADS_PER_TILE, tm // CHUNK, HEAD_DIM, CHUNK),
                                lambda i, j: (advance(j, YT_ADVANCE), i, 0, 0)),
                   pl.BlockSpec((HEADS_PER_TILE, tm, HEAD_DIM), lambda i, j: (advance(j, YSTD_ADVANCE), i, 0)),
                   pl.BlockSpec((N_CMP_HEADS, tm // CMP_STRIDE, CMP_STRIDE * HEAD_DIM), lambda i, j: (0, i, 0)),
                   pl.BlockSpec((G, GATE_ROWS, tm), lambda i, j: (0, 0, i))],
        out_shape=[jax.ShapeDtypeStruct((N_T_HEADS, BT // CHUNK, HEAD_DIM, CHUNK), BF16),
                   jax.ShapeDtypeStruct((N_STD_HEADS, BT, HEAD_DIM), BF16),
                   jax.ShapeDtypeStruct((N_CMP_HEADS, BT // CMP_STRIDE, CMP_STRIDE * HEAD_DIM), BF16),
                   jax.ShapeDtypeStruct((G, GATE_ROWS, BT), F32)],
        scratch_shapes=[pltpu.VMEM((tm, D), BF16), pltpu.VMEM((tm, HEAD_DIM), F32)],
        compiler_params=_cparams(("parallel", "arbitrary")),
        name="in_proj",
    )(x2, mod3, w_nsa, w_moba, w_gate, gainsT, cosT, sinT)


def _compress_kernel(h_ref, w1c_ref, pe_ref, w1_ref, w2_ref, gain_ref, cc_ref, sa_ref, sb_ref, o_ref):
    a = pl.program_id(0)
    ncp = h_ref.shape[2]
    z = jnp.dot(h_ref[0, 0], w1c_ref[0], preferred_element_type=F32)
    top = z[:, :HEAD_DIM]
    bot = pltpu.roll(z[:, HEAD_DIM:], ncp - 1, 0)
    pe_term = jnp.dot(pe_ref[0], w1_ref[0], preferred_element_type=F32)[0:1, :]
    pre = top + bot + pe_term
    act = pre * jax.nn.sigmoid(pre)
    out = jnp.dot(act.astype(BF16), w2_ref[0], preferred_element_type=F32)
    live = lax.broadcasted_iota(jnp.int32, out.shape, 0) < ncp - 1
    out = jnp.where(live, out, 0.0)

    @pl.when(a < NSA_KV_HEADS)
    def _():
        ms = jnp.mean(out * out, axis=-1, keepdims=True)
        yn = out * lax.rsqrt(ms + EPS) * gain_ref[...]
        o_ref[0, 0] = _rope(yn, cc_ref[...], sa_ref[...], sb_ref[...]).astype(BF16)

    @pl.when(a >= NSA_KV_HEADS)
    def _():
        o_ref[0, 0] = out.astype(BF16)


def _compress(hc, w1cat, pe8, w1, w2, gain, cc, sa, sb):
    A, B, ncp, K = hc.shape
    G = NSA_KV_HEADS
    return pl.pallas_call(
        _compress_kernel,
        grid=(A, B),
        in_specs=[pl.BlockSpec((1, 1, ncp, K), lambda a, b: (a, b, 0, 0)),
                  pl.BlockSpec((1, K, 2 * HEAD_DIM), lambda a, b: (a // G, 0, 0)),
                  pl.BlockSpec((1, 8, 2 * K), lambda a, b: (a // G, 0, 0)),
                  pl.BlockSpec((1, 2 * K, HEAD_DIM), lambda a, b: (a // G, 0, 0)),
                  pl.BlockSpec((1, HEAD_DIM, HEAD_DIM), lambda a, b: (a // G, 0, 0)),
                  pl.BlockSpec((1, HEAD_DIM), lambda a, b: (0, 0)),
                  pl.BlockSpec((ncp, HEAD_DIM), lambda a, b: (0, 0)),
                  pl.BlockSpec((ncp, HEAD_DIM), lambda a, b: (0, 0)),
                  pl.BlockSpec((ncp, HEAD_DIM), lambda a, b: (0, 0))],
        out_specs=pl.BlockSpec((1, 1, ncp, HEAD_DIM), lambda a, b: (a, b, 0, 0)),
        out_shape=jax.ShapeDtypeStruct((A, B, ncp, HEAD_DIM), BF16),
        compiler_params=_cparams(("arbitrary", "arbitrary")),
        name="compress",
    )(hc, w1cat, pe8, w1, w2, gain, cc, sa, sb)


def _first_step(s, vT, m_scr, acc_scr):
    m = jnp.max(s, axis=0, keepdims=True)
    p = jnp.exp2(s - m)
    acc_scr[...] = jnp.dot(_with_ones(vT), p.astype(BF16), preferred_element_type=F32)
    m_scr[...] = m


def _with_ones(vT):
    return jnp.concatenate([vT, jnp.ones((ONES_ROWS, vT.shape[1]), vT.dtype)], axis=0)


def _normalized(acc):
    return acc[0:HEAD_DIM] / acc[HEAD_DIM:HEAD_DIM + 1]


def _online_step(s, vT, m_scr, acc_scr):
    m_prev = m_scr[...]
    m_new = jnp.maximum(m_prev, jnp.max(s, axis=0, keepdims=True))
    alpha = jnp.exp2(m_prev - m_new)
    p = jnp.exp2(s - m_new)
    acc_scr[...] = alpha * acc_scr[...] + jnp.dot(_with_ones(vT), p.astype(BF16), preferred_element_type=F32)
    m_scr[...] = m_new


def _rank_select(score, k):
    n = score.shape[0]
    rank = jnp.zeros(score.shape, F32)
    for m in range(n):
        sm = score[m:m + 1, :]
        lo = (m // SUBLANES) * SUBLANES
        hi = min(lo + SUBLANES, n)
        parts = []
        if lo > 0:
            parts.append(jnp.where(sm > score[:lo], 1.0, 0.0))
        gt =jnp.where(sm > score[lo:hi], 1.0, 0.0)
        ge = jnp.where(sm >= score[lo:hi], 1.0, 0.0)
        below = lax.broadcasted_iota(jnp.int32, gt.shape, 0) > m - lo
        parts.append(jnp.where(below, ge, gt))
        if hi < n:
            parts.append(jnp.where(sm >= score[hi:], 1.0, 0.0))
        rank = rank + (jnp.concatenate(parts, axis=0) if len(parts) > 1 else parts[0])
    return rank < k


def _pad_rows(a, rows):
    return jnp.concatenate([a, jnp.zeros((rows - a.shape[0], a.shape[1]), a.dtype)], axis=0)


def _masked_scores(k_ref, e_ref, start, size, q_aug):
    rows = pl.ds(pl.multiple_of(start, 128), size)
    k_aug = jnp.concatenate([k_ref[rows, :], e_ref[rows, :]], axis=1)
    return jnp.dot(k_aug, q_aug, preferred_element_type=F32)


def _pipelined_attention(chains, n_steps, n_max, tks, tk, body_pairs):
    per = tks // tk

    def scores(step, buf):
        st = jnp.minimum(step, n_max - 1)
        dead = jnp.where(step < n_steps, 0, 1)
        for k_ref, e_ref, _, qaug_ref, bufs, _, _ in chains:
            bufs[buf][...] = _masked_scores(k_ref, e_ref, st * tks, tks, qaug_ref[dead])

    def consume(step, buf):
        st = jnp.minimum(step, n_max - 1)
        for _, _, vT_ref, _, bufs, m_scr, acc_scr in chains:
            vT = jnp.concatenate([vT_ref[st * per + a] for a in range(per)], axis=1)
            _online_step(bufs[buf][...], vT, m_scr, acc_scr)

    def pairs(first_step, n_pairs_in_body):
        def body(it, carry):
            base = first_step + it * 2 * n_pairs_in_body
            for p in range(n_pairs_in_body):
                scores(base + 2 * p + 1, 1)
                consume(base + 2 * p, 0)
                scores(base + 2 * p + 2, 0)
                consume(base + 2 * p + 1, 1)
            return carry
        return body

    scores(0, 0)
    done = 0
    for n_body in body_pairs[:-1]:
        n_iter = (n_steps - done) // (2 * n_body)
        lax.fori_loop(0, n_iter, pairs(done, n_body), 0)
        done = done + n_iter * 2 * n_body
    lax.fori_loop(0, (n_steps - done + 1) // 2, pairs(done, 1), 0)


def _carry_casts(kernel_fn, n_in, n_out, n_cast):
    def wrapped(*refs):
        ins, cast_in = refs[:n_in], refs[n_in:n_in + n_cast]
        outs = refs[n_in + n_cast:n_in + n_cast + n_out]
        cast_out = refs[n_in + n_cast + n_out:n_in + n_cast + n_out + n_cast]
        for src, dst in zip(cast_in, cast_out):
            dst[...] = src[...].astype(BF16)
        kernel_fn(*ins, *outs, *refs[n_in + n_cast + n_out + n_cast:])
    return wrapped


def _cast_specs(weights, grid):
    n_steps = grid[0] * grid[1] * grid[2]
    specs, shapes = [], []
    for w, rows in weights:
        while w.shape[0] // rows > n_steps:
            rows *= 2
        n_blk = w.shape[0] // rows
        assert w.shape[0] % rows == 0
        index = lambda a, b, c, n_blk=n_blk: (jnp.minimum((a * grid[1] + b) * grid[2] + c, n_blk - 1), 0)
        specs.append(pl.BlockSpec((rows, w.shape[1]), index))
        shapes.append(jax.ShapeDtypeStruct(w.shape, BF16))
    return specs, shapes


def _nsa_body(qT_ref, kc_ref, vcT_ref, ks_ref, vsT_ref, kw_ref, vwT_ref, gT_ref, e_ref,
              gain_ref, o_ref, m_scr, acc_scr, comb_scr, qaug_scr, s0_scr, s1_scr, psum_scr, *, T):
    R, tq = NSA_GROUP, NSA_TQ
    i = pl.program_id(2)
    t0 = i * tq
    ncp = T // CMP_STRIDE
    nsb = T // SLC_BLOCK
    qT = jnp.concatenate([qT_ref[r, c] for r in range(R) for c in range(tq // CHUNK)], axis=1)
    gT = gT_ref[0]

    def per_head(row0):
        return jnp.concatenate([gT[row0 + 3 * r:row0 + 3 * r + 1, :] for r in range(R)], axis=1)

    def lanes_x_heads(a):
        return jnp.concatenate([a] * R, axis=1)

    s = jnp.dot(kc_ref[0, 0], qT, preferred_element_type=F32)
    c_idx = lax.broadcasted_iota(jnp.int32, (ncp, tq), 0)
    t_c = t0 + lax.broadcasted_iota(jnp.int32, (ncp, tq), 1)
    vis = lanes_x_heads((c_idx * CMP_STRIDE + (CMP_BLOCK - 1) <= t_c) & (c_idx < ncp - 1))
    s = jnp.where(vis, s, NEG)
    m = jnp.maximum(jnp.max(s, axis=0, keepdims=True), M_INIT)
    p = jnp.exp2(s - m)
    l = jnp.sum(p, axis=0, keepdims=True)
    p = p / jnp.where(l > 0.0, l, 1.0)
    o_cmp = jnp.dot(vcT_ref[0, 0], p.astype(BF16), preferred_element_type=F32)
    comb_scr[...] = o_cmp * per_head(0)

    psum = p[:, 0:tq]
    for r in range(1, R):
        psum = psum + p[:, r * tq:(r + 1) * tq]
    per = SLC_BLOCK // CMP_STRIDE
    imp_chunks = []
    for c in range(tq // CHUNK):
        psum_scr[c, 0:SUBLANES, :] = jnp.zeros((SUBLANES, CHUNK), F32)
        psum_scr[c, SUBLANES:, :] = psum[:, c * CHUNK:(c + 1) * CHUNK]
        acc = jnp.zeros((nsb, CHUNK), F32)
        for k in range(1 - CMP_BLOCK // CMP_STRIDE, per):
            acc = acc + psum_scr[c, pl.ds(SUBLANES + k, nsb, stride=per), :]
        imp_chunks.append(acc)
    imp = jnp.concatenate(imp_chunks, axis=1)
    n_idx = lax.broadcasted_iota(jnp.int32, (nsb, tq), 0)
    cur = (t0 + lax.broadcasted_iota(jnp.int32, (nsb, tq), 1)) // SLC_BLOCK
    forced = (n_idx == 0) | (n_idx == cur) | (n_idx == cur - 1)
    valid = n_idx <= cur
    score = jnp.where(valid, imp + jnp.where(forced, FORCE_BONUS, 0.0), NEG)
    sel = _rank_select(score, min(SLC_TOPK, nsb)) & valid
    sel_past = sel & (n_idx * SLC_BLOCK < t0)
    bias = _pad_rows(jnp.where(sel_past, 0.0, NEG), HEAD_DIM).astype(BF16)
    bias_own = _pad_rows(jnp.where(sel, 0.0, NEG), HEAD_DIM).astype(BF16)
    for d in range(3):
        qaug_scr[d, 0:HEAD_DIM, :] = qT
    qaug_scr[0, HEAD_DIM:, :] = lanes_x_heads(bias)
    qaug_scr[1, HEAD_DIM:, :] = jnp.full((HEAD_DIM, R * tq), NEG, BF16)
    qaug_scr[2, HEAD_DIM:, :] = lanes_x_heads(bias_own)

    w0 = jnp.maximum(t0 + tq - WIN_SPAN, 0)
    k_w = kw_ref[0, pl.ds(pl.multiple_of(w0, CHUNK), WIN_SPAN), :]
    s_w = jnp.dot(k_w, qT, preferred_element_type=F32)
    kpos = w0 + lax.broadcasted_iota(jnp.int32, (WIN_SPAN, tq), 0)
    tpos = t0 + lax.broadcasted_iota(jnp.int32, (WIN_SPAN, tq), 1)
    ok = lanes_x_heads((kpos <= tpos) & (tpos - kpos < WINDOW))
    s_w = jnp.where(ok, s_w, NEG)
    m_w = jnp.max(s_w, axis=0, keepdims=True)
    p_w = jnp.exp2(s_w - m_w)
    jw = w0 // CHUNK
    v_w = jnp.concatenate([vwT_ref[0, jw + a] for a in range(WIN_SPAN // CHUNK)], axis=1)
    o_w = jnp.dot(_with_ones(v_w), p_w.astype(BF16), preferred_element_type=F32)
    comb_scr[...] += _normalized(o_w) * per_head(2)

    tri =lax.broadcasted_iota(jnp.int32, (tq, tq), 0) <= lax.broadcasted_iota(jnp.int32, (tq, tq), 1)
    s_d = jnp.where(lanes_x_heads(tri), _masked_scores(ks_ref.at[0], e_ref, t0, tq, qaug_scr[2]), NEG)
    own = tq // CHUNK
    v_d = jnp.concatenate([vsT_ref[0, i * own + a] for a in range(own)], axis=1)
    _first_step(s_d, v_d, m_scr, acc_scr)
    yield [(ks_ref.at[0], e_ref, vsT_ref.at[0], qaug_scr, (s0_scr, s1_scr), m_scr, acc_scr)]
    comb =comb_scr[...] + _normalized(acc_scr[...]) * per_head(1)

    for r in range(R):
        oT = comb[:, r * tq:(r + 1) * tq]
        ms = jnp.mean(oT * oT, axis=0, keepdims=True)
        on = (oT * lax.rsqrt(ms + EPS)).T * gain_ref[0, r:r + 1, :]
        o_ref[:, r * HEAD_DIM:(r + 1) * HEAD_DIM] = on.astype(BF16)


def _moba_body(qT_ref, k_ref, vT_ref, e_ref, gain_ref, o_ref, m_scr, acc_scr, kmean_scr, qaug_scr,
               s_scr, *, T):
    tq, tk = MOBA_TQ, MOBA_TK
    nb = T // MOBA_BLOCK
    nbp = kmean_scr.shape[1]
    i = pl.program_id(2)
    t0 = i * tq

    @pl.when(i == 0)
    def _():
        kmean_scr[...] = jnp.zeros(kmean_scr.shape, F32)
        for a in range(MOBA_HB):
            kb = k_ref[a].astype(F32).reshape(nb, MOBA_BLOCK, HEAD_DIM)
            kmean_scr[a, 0:nb, :] = jnp.mean(kb, axis=1)

    n_idx = lax.broadcasted_iota(jnp.int32, (nbp, tq), 0)
    cur = (t0 + lax.broadcasted_iota(jnp.int32, (nbp, tq), 1)) // MOBA_BLOCK
    past = n_idx < cur
    causal = lax.broadcasted_iota(jnp.int32, (tk, tq), 0) <= lax.broadcasted_iota(jnp.int32, (tk, tq), 1)

    qTs = [jnp.concatenate([qT_ref[a, c] for c in range(tq // CHUNK)], axis=1) for a in range(MOBA_HB)]
    for a in range(MOBA_HB):
        qT = qTs[a]
        gate = jnp.zeros((nbp, tq), F32)
        for part in _split3(kmean_scr[a]):
            gate = gate + jnp.dot(part, qT, preferred_element_type=F32)
        sel = _rank_select(jnp.where(past, gate, NEG), min(MOBA_TOPK, nb)) & past
        for d in range(2):
            qaug_scr[a, d, 0:HEAD_DIM, :] = qT
        qaug_scr[a, 0, HEAD_DIM:, :] = _pad_rows(jnp.where(sel, 0.0, NEG), HEAD_DIM).astype(BF16)
        qaug_scr[a, 1, HEAD_DIM:, :] = jnp.full((HEAD_DIM, tq), NEG, BF16)

    s_own = []
    for a in range(MOBA_HB):
        m_scr[a] = jnp.full(m_scr.shape[1:], M_INIT, F32)
        acc_scr[a] = jnp.zeros(acc_scr.shape[1:], F32)
        k_own = k_ref[a, pl.ds(pl.multiple_of(t0, tk), tk), :]
        s_own.append(jnp.where(causal, jnp.dot(k_own, qTs[a], preferred_element_type=F32), NEG))

    yield [(k_ref.at[a], e_ref, vT_ref.at[a], qaug_scr.at[a], (s_scr.at[a, 0], s_scr.at[a, 1]),
            m_scr.at[a], acc_scr.at[a]) for a in range(MOBA_HB)]

    own = tq // CHUNK
    for a in range(MOBA_HB):
        v_own = jnp.concatenate([vT_ref[a, i * own + c] for c in range(own)], axis=1)
        _online_step(s_own[a], v_own, m_scr.at[a], acc_scr.at[a])
        oT = _normalized(acc_scr[a])
        ms = jnp.mean(oT * oT, axis=0, keepdims=True)
        on = (oT * lax.rsqrt(ms + EPS)).T * gain_ref[a]
        o_ref[:, a * HEAD_DIM:(a + 1) * HEAD_DIM] = on.astype(BF16)


N_NSA_IN, N_NSA_SCRATCH, N_MOBA_IN = 10, 7, 5


def _attention_kernel(*refs, T):
    nsa_in, moba_in = refs[:N_NSA_IN], refs[N_NSA_IN:N_NSA_IN + N_MOBA_IN]
    o_nsa, o_moba = refs[N_NSA_IN + N_MOBA_IN:N_NSA_IN + N_MOBA_IN + 2]
    scratch = refs[N_NSA_IN + N_MOBA_IN + 2:]
    moba = _moba_body(*moba_in, o_moba, *scratch[N_NSA_SCRATCH:], T=T)
    nsa = _nsa_body(*nsa_in, o_nsa, *scratch[:N_NSA_SCRATCH], T=T)
    chains = next(moba) + next(nsa)
    n_steps = pl.program_id(2) * ATTN_TQ // ATTN_TKS
    _pipelined_attention(chains, n_steps, T // ATTN_TKS, ATTN_TKS, CHUNK, ATTN_BODY_PAIRS)
    for rest in (nsa, moba):
        assert next(rest, None) is None


def _attention(yT, ystd, kcvc, vcT, gT, e_slc, e_moba, nsa_gains, moba_gains, casts, B, T):
    G, R, tq = NSA_KV_HEADS, NSA_GROUP, ATTN_TQ
    H, HB = MOBA_HEADS, MOBA_HB
    assert H // HB == G and NSA_TQ == MOBA_TQ == tq and SLC_TKS == MOBA_TKS == ATTN_TKS and tq % ATTN_TKS == 0
    nq = T // tq
    ncp = T // CMP_STRIDE
    nb = T // MOBA_BLOCK
    nbp = max(16, nb)
    grid = (B, G, nq)
    cast_specs, cast_shapes = _cast_specs(casts, grid)
    kern = _carry_casts(functools.partial(_attention_kernel, T=T), N_NSA_IN + N_MOBA_IN, 2, len(casts))
    keys = lambda head0: pl.BlockSpec((1, T, HEAD_DIM), lambda b, g, i: (head0 + g, b, 0))
    values = lambda head0: pl.BlockSpec((1, T // CHUNK, HEAD_DIM, CHUNK), lambda b, g, i: (head0 + g, b, 0, 0))
    onehot = pl.BlockSpec((T, HEAD_DIM), lambda b, g, i: (0, 0))
    nsa_specs = [pl.BlockSpec((R, tq // CHUNK, HEAD_DIM, CHUNK), lambda b, g, i: (T_NSA_Q // R + g, b * nq + i, 0, 0)),
                 pl.BlockSpec((1, 1, ncp, HEAD_DIM), lambda b, g, i: (g, b, 0, 0)),
                 pl.BlockSpec((1, 1, HEAD_DIM, ncp), lambda b, g, i: (g, b, 0, 0)),
                 keys(S_KSLC), values(T_VSLC), keys(S_KWIN), values(T_VWIN),
                 pl.BlockSpec((1, GATE_ROWS, tq), lambda b, g, i: (g, 0, b * nq + i)),
                 onehot,
                 pl.BlockSpec((1, R, HEAD_DIM), lambda b, g, i: (g, 0, 0))]
    moba_specs = [pl.BlockSpec((HB, tq // CHUNK, HEAD_DIM, CHUNK), lambda b, h, i: (T_MOBA_Q // HB + h, b * nq + i, 0, 0)),
                  pl.BlockSpec((HB, T, HEAD_DIM), lambda b, h, i: (S_MOBA_K // HB + h, b, 0)),
                  pl.BlockSpec((HB, T // CHUNK, HEAD_DIM, CHUNK), lambda b, h, i: (T_MOBA_V // HB + h, b, 0, 0)),
                  onehot,
                  pl.BlockSpec((HB, 1, HEAD_DIM), lambda b, h, i: (h, 0, 0))]
    nsa_scratch = [pltpu.VMEM((1, R * tq), F32),
                   pltpu.VMEM((HEAD_DIM + ONES_ROWS, R * tq), F32), pltpu.VMEM((HEAD_DIM, R * tq), F32),
                   pltpu.VMEM((3, 2 * HEAD_DIM, R * tq), BF16),
                   pltpu.VMEM((ATTN_TKS, R * tq), F32), pltpu.VMEM((ATTN_TKS, R * tq), F32),
                   pltpu.VMEM((tq // CHUNK, SUBLANES + ncp, CHUNK), F32)]
    moba_scratch = [pltpu.VMEM((HB, 1, tq), F32),
                    pltpu.VMEM((HB, HEAD_DIM + ONES_ROWS, tq), F32), pltpu.VMEM((HB, nbp, HEAD_DIM), F32),
                    pltpu.VMEM((HB, 2, 2 * HEAD_DIM, tq), BF16),
                    pltpu.VMEM((HB, 2, ATTN_TKS, tq), F32)]
    assert len(nsa_specs) == N_NSA_IN and len(moba_specs) == N_MOBA_IN and len(nsa_scratch) == N_NSA_SCRATCH
    return pl.pallas_call(
        kern,
        grid=grid,
        in_specs=nsa_specs + moba_specs + cast_specs,
        out_specs=[pl.BlockSpec((tq, R * HEAD_DIM), lambda b, g, i: (b * nq + i, g)),
                   pl.BlockSpec((tq, HB * HEAD_DIM), lambda b, h, i: (b * nq + i, h))] + cast_specs,
        out_shape=[jax.ShapeDtypeStruct((B * T, NSA_HEADS * HEAD_DIM), BF16),
                   jax.ShapeDtypeStruct((B * T, H * HEAD_DIM), BF16)] + cast_shapes,
        scratch_shapes=nsa_scratch + moba_scratch,
        compiler_params=_cparams(("parallel", "parallel", "arbitrary")),
        name="attention",
    )(yT, kcvc, vcT, ystd, yT, ystd, yT, gT, e_slc, nsa_gains,
      yT, ystd, yT, e_moba, moba_gains, *[w for w, _ in casts])


def _outproj_kernel(on_ref, om_ref, w_ref, x_ref, mod_ref, o_ref):
    half = on_ref.shape[1]
    acc = jnp.dot(on_ref[...], w_ref[0:half, :], preferred_element_type=F32)
    acc = acc + jnp.dot(om_ref[...], w_ref[half:, :], preferred_element_type=F32)
    o_ref[...] = x_ref[...] + mod_ref[0, 2:3, :] * acc


def _out_proj(o_nsa, o_moba, w_out, x2, mod3, T):
    BT, D = x2.shape
    tm = min(512, T)
    tpb = T // tm
    half = o_nsa.shape[1]
    return pl.pallas_call(
        _outproj_kernel,
        grid=(BT // tm,),
        in_specs=[pl.BlockSpec((tm, half), lambda i: (i, 0)),
                  pl.BlockSpec((tm, half), lambda i: (i, 0)),
                  pl.BlockSpec((D, D), lambda i: (0, 0)),
                  pl.BlockSpec((tm, D), lambda i: (i, 0)),
                  pl.BlockSpec((1, 6, D), lambda i: (i // tpb, 0, 0))],
        out_specs=pl.BlockSpec((tm, D), lambda i: (i, 0)),
        out_shape=jax.ShapeDtypeStruct((BT, D), F32),
        compiler_params=_cparams(("parallel",)),
        name="out_proj",
    )(o_nsa, o_moba, w_out, x2, mod3)


def _ffn_kernel(x_ref, mod_ref, wg_ref, wu_ref, wo_ref, o_ref, h_scr):
    j = pl.program_id(1)

    @pl.when(j == 0)
    def _():
        x = x_ref[...]
        ms = jnp.mean(x * x, axis=-1, keepdims=True)
        h = x * lax.rsqrt(ms + EPS) * (1.0 + mod_ref[0, 4:5, :]) + mod_ref[0, 3:4, :]
        h_scr[...] = h.astype(BF16)
        o_ref[...] = jnp.zeros(o_ref.shape, F32)

    hb = h_scr[...]
    half = wg_ref.shape[1] // 2
    down = None
    for c in range(2):
        cols = slice(c * half, (c + 1) * half)
        gate = jnp.dot(hb, wg_ref[:, cols], preferred_element_type=F32)
        up = jnp.dot(hb, wu_ref[:, cols], preferred_element_type=F32)
        act = (gate * jax.nn.sigmoid(gate) * up).astype(BF16)
        part = jnp.dot(act, wo_ref[cols, :], preferred_element_type=F32)
        down = part if down is None else down + part
    o_ref[...] += down

    @pl.when(j == pl.num_programs(1) - 1)
    def _():
        o_ref[...] = x_ref[...] + mod_ref[0, 5:6, :] * o_ref[...]


def _ffn(x1, mod3, w_in, w_out, T):
    BT, D = x1.shape
    Fh = w_out.shape[0]
    tm = min(512, T)
    tf = 512
    tpb = T // tm
    nf = Fh // tf
    return pl.pallas_call(
        _ffn_kernel,
        grid=(BT // tm, nf),
        in_specs=[pl.BlockSpec((tm, D), lambda i, j: (i, 0)),
                  pl.BlockSpec((1, 6, D), lambda i, j: (i // tpb, 0, 0)),
                  pl.BlockSpec((D, tf), lambda i, j: (0, j)),
                  pl.BlockSpec((D, tf), lambda i, j: (0, nf + j)),
                  pl.BlockSpec((tf, D), lambda i, j: (j, 0))],
        out_specs=pl.BlockSpec((tm, D), lambda i, j: (i, 0)),
        out_shape=jax.ShapeDtypeStruct((BT, D), F32),
        scratch_shapes=[pltpu.VMEM((tm, D), BF16)],
        compiler_params=_cparams(("parallel", "arbitrary")),
        name="ffn",
    )(x1, mod3, w_in, w_in, w_out)


def _rope_tables(T):
    inv = ROPE_THETA ** (-jnp.arange(0, ROPE_DIMS, 2, dtype=F32) / ROPE_DIMS)
    ang = jnp.arange(T).astype(F32)[:, None] * inv[None, :]
    cos, sin = jnp.cos(ang), jnp.sin(ang)
    rest = HEAD_DIM - ROPE_DIMS
    cc = jnp.concatenate([cos, cos, jnp.ones((T, rest), F32)], axis=1)
    sa = jnp.concatenate([-sin, jnp.zeros((T, HEAD_DIM - ROPE_HALF), F32)], axis=1)
    sb = jnp.concatenate([jnp.zeros((T, ROPE_HALF), F32), sin, jnp.zeros((T, rest), F32)], axis=1)
    return cc, sa, sb, cos.T, sin.T


def _block_onehot(T, block):
    return (jnp.arange(T)[:, None] // block == jnp.arange(HEAD_DIM)[None, :]).astype(BF16)


def _layer(x2, c, B, T, w_ada, b_ada, w_in, nsa_q_norm, nsa_k_norm, moba_q_norm, moba_k_norm,
           cmp_pe_k, cmp_w1_k, cmp_w2_k, cmp_pe_v, cmp_w1_v, cmp_w2_v, out_norm, w_out,
           w_ffn_in, w_ffn_out):
    D = x2.shape[1]
    G = NSA_KV_HEADS
    scale = HEAD_DIM ** -0.5 * LOG2E
    assert T % MOBA_BLOCK == 0 and T % SLC_TKS == 0 and T % MOBA_TKS == 0 and T % NSA_TQ == 0 and T >= WIN_SPAN
    assert T // SLC_BLOCK <= HEAD_DIM and T // MOBA_BLOCK <= HEAD_DIM

    nsa_w = NSA_HEADS * HEAD_DIM + 6 * G * HEAD_DIM
    gw = NSA_HEADS * 3
    mod3 = _adaln(c, w_ada, b_ada)
    w_nsa = w_in.astype(BF16)
    w_moba = w_nsa[:, nsa_w + gw:]

    wg = w_in[:, nsa_w:nsa_w + gw].reshape(D, G, NSA_GROUP * 3)
    w_gate = jnp.pad(wg, ((0, 0), (0, 0), (0, GATE_LANES - NSA_GROUP * 3))).reshape(D, G * GATE_LANES)
    w_gate = w_gate.astype(BF16)
    across = lambda g_: jnp.broadcast_to(g_[:, None], (HEAD_DIM, CHUNK))
    gainsT = jnp.stack([across(nsa_q_norm * scale), across(moba_q_norm * scale),
                        across(nsa_k_norm[1]), across(nsa_k_norm[2]), across(moba_k_norm)])
    cc, sa, sb, cosT, sinT = _rope_tables(T)

    yT, ystd, hc, gT = _in_proj(x2, mod3, w_nsa, w_moba, w_gate, gainsT, cosT, sinT, T)

    ncp = T // CMP_STRIDE
    half = CMP_STRIDE * HEAD_DIM
    w1 = jnp.stack([cmp_w1_k, cmp_w1_v]).astype(BF16)
    w1cat = jnp.concatenate([w1[:, :half], w1[:, half:]], axis=2)
    pe8 = jnp.broadcast_to(jnp.stack([cmp_pe_k, cmp_pe_v]).reshape(2, 1, 2 * half), (2, 8, 2 * half))
    w2 = jnp.stack([cmp_w2_k, cmp_w2_v]).astype(BF16)
    cmp_rows = lambda t: jnp.pad(t[CMP_BLOCK - 1::CMP_STRIDE], ((0, 1), (0, 0)))
    kcvc = _compress(hc.reshape(2 * G, B, ncp, half), w1cat, pe8.astype(BF16), w1, w2,
                     nsa_k_norm[0].reshape(1, HEAD_DIM),
                     cmp_rows(cc), cmp_rows(sa), cmp_rows(sb))
    vcT = kcvc[G:].transpose(0, 1, 3, 2)

    on = out_norm.reshape(N_HEADS, HEAD_DIM)
    o_nsa, o_moba, w_ffn_out_b, w_out_b, w_ffn_in_b = _attention(
        yT, ystd, kcvc, vcT, gT, _block_onehot(T, SLC_BLOCK), _block_onehot(T, MOBA_BLOCK),
        on[:NSA_HEADS].reshape(G, NSA_GROUP, HEAD_DIM), on[NSA_HEADS:].reshape(MOBA_HEADS, 1, HEAD_DIM),
        [(w_ffn_out, 128), (w_out, 128), (w_ffn_in, 32)], B, T)

    x1 = _out_proj(o_nsa, o_moba, w_out_b, x2, mod3, T)
    return _ffn(x1, mod3, w_ffn_in_b, w_ffn_out_b, T)


def kernel(x, c, w_ada, b_ada, w_in, nsa_q_norm, nsa_k_norm, moba_q_norm, moba_k_norm, cmp_pe_k, cmp_w1_k, cmp_w2_k, cmp_pe_v, cmp_w1_v, cmp_w2_v, out_norm, w_out, w_ffn_in, w_ffn_out):
    B, T, D = x.shape
    x2 = x.reshape(B * T, D)
    for l in range(w_ada.shape[0]):
        x2 = _layer(x2, c, B, T, w_ada[l], b_ada[l], w_in[l], nsa_q_norm[l], nsa_k_norm[l],
                    moba_q_norm[l], moba_k_norm[l], cmp_pe_k[l], cmp_w1_k[l], cmp_w2_k[l],
                    cmp_pe_v[l], cmp_w1_v[l], cmp_w2_v[l], out_norm[l], w_out[l],
                    w_ffn_in[l], w_ffn_out[l])
    return x2.reshape(B, T, D)
```

```python
import functools

import jax
import jax.numpy as jnp
from jax import lax
from jax.experimental import pallas as pl
from jax.experimental.pallas import tpu as pltpu

F32 = jnp.float32
BF16 = jnp.bfloat16

HEAD_DIM = 128
SUBLANES = 8
NSA_HEADS = 8
NSA_KV_HEADS = 2
NSA_GROUP = NSA_HEADS // NSA_KV_HEADS
MOBA_HEADS = 8
N_HEADS = NSA_HEADS + MOBA_HEADS
CMP_BLOCK = 32
CMP_STRIDE = 16
SLC_BLOCK = 64
SLC_TOPK = 16
WINDOW = 512
FORCE_BONUS = 1e4
MOBA_BLOCK = 256
MOBA_TOPK = 3
ROPE_THETA = 500000.0
ROPE_DIMS = HEAD_DIM // 4
ROPE_HALF = ROPE_DIMS // 2
ROPE_SPLIT = 64
EPS = 1e-6
LOG2E = 1.4426950408889634
ONES_ROWS = 16
NEG = -1e30
M_INIT = -1e29

V7X_VMEM_BYTES = 64 * 1024 * 1024
VMEM_LIMIT = V7X_VMEM_BYTES - 8 * 1024 * 1024

HEADS_PER_TILE = 4
PROJ_TILES = ("nsa_q", "nsa_q", "kv_cmp", "kv_slc", "kv_win",
              "moba_q", "moba_q", "moba_k", "moba_k", "moba_v", "moba_v")
N_NSA_TILES = 5
GAIN_KINDS = ("nsa_q", "moba_q", "k_slc", "k_win", "moba_k")
PROJ_GAIN = {"nsa_q": "nsa_q", "moba_q": "moba_q", "kv_slc": "k_slc", "kv_win": "k_win", "moba_k": "moba_k"}
YT_ADVANCE = (1, 3, 5, 6, 9, 10)
YSTD_ADVANCE = (7, 8)
T_NSA_Q, T_VSLC, T_VWIN, T_MOBA_Q, T_MOBA_V = 0, 8, 10, 12, 20
S_KSLC, S_KWIN, S_MOBA_K = 0, 2, 4
N_T_HEADS, N_STD_HEADS, N_CMP_HEADS = 28, 12, 4
CHUNK = 128
GATE_LANES = 128
GATE_ROWS = 16

ATTN_TQ = 256
ATTN_TKS = 256
ATTN_BODY_PAIRS = (4, 2, 1)
NSA_TQ = ATTN_TQ
SLC_TKS = ATTN_TKS
WIN_SPAN = WINDOW + NSA_TQ
MOBA_TQ = ATTN_TQ
MOBA_TK = MOBA_BLOCK
MOBA_TKS = ATTN_TKS
MOBA_HB = 4


def _cparams(sem):
    return pltpu.CompilerParams(dimension_semantics=sem, vmem_limit_bytes=VMEM_LIMIT)


def _split3(a):
    hi = a.astype(BF16)
    r1 = a - hi.astype(F32)
    mid = r1.astype(BF16)
    lo = (r1 - mid.astype(F32)).astype(BF16)
    return hi, mid, lo


def _adaln_kernel(c_ref, w_ref, b_ref, o_ref):
    cv = c_ref[...]
    s = cv * jax.nn.sigmoid(cv)
    w = w_ref[...].astype(BF16)
    acc = b_ref[...] + jnp.zeros(o_ref.shape, F32)
    for part in _split3(s)[:2]:
        acc = acc + jnp.dot(part, w, preferred_element_type=F32)
    o_ref[...] = acc


def _adaln(c, w_ada, b_ada):
    B, D = c.shape
    N = w_ada.shape[1]
    tn = 1024
    c8 = jnp.zeros((8, D), F32).at[:B].set(c)
    out = pl.pallas_call(
        _adaln_kernel,
        grid=(N // tn,),
        in_specs=[pl.BlockSpec((8, D), lambda j: (0, 0)),
                  pl.BlockSpec((D, tn), lambda j: (0, j)),
                  pl.BlockSpec((1, tn), lambda j: (0, j))],
        out_specs=pl.BlockSpec((8, tn), lambda j: (0, j)),
        out_shape=jax.ShapeDtypeStruct((8, N), F32),
        compiler_params=_cparams(("arbitrary",)),
        name="adaln",
    )(c8, w_ada, b_ada.reshape(1, N))
    return out[:B].reshape(B, 6, D)


def _rope(y, cc, sa, sb):
    return (y * cc + pltpu.roll(y, HEAD_DIM - ROPE_HALF, 1) * sa + pltpu.roll(y, ROPE_HALF, 1) * sb)


def _inproj_kernel(x_ref, mod_ref, wa_ref, wb_ref, wg_ref, gainT_ref, cosT_ref, sinT_ref,
                   yT_ref, ystd_ref, hc_ref, gT_ref, h_scr, rows_scr):
    j = pl.program_id(1)
    tm = x_ref.shape[0]
    n_chunks = tm // CHUNK
    pair = 2 * HEAD_DIM

    @pl.when(j == 0)
    def _():
        x = x_ref[...]
        ms = jnp.mean(x * x, axis=-1, keepdims=True)
        h = x * lax.rsqrt(ms + EPS) * (1.0 + mod_ref[0, 1:2, :]) + mod_ref[0, 0:1, :]
        hb = h.astype(BF16)
        h_scr[...] = hb
        g = jax.nn.sigmoid(jnp.dot(hb, wg_ref[...], preferred_element_type=F32))
        for grp in range(NSA_KV_HEADS):
            for c in range(n_chunks):
                blk = g[c * CHUNK:(c + 1) * CHUNK, grp * GATE_LANES:(grp + 1) * GATE_LANES].T
                gT_ref[grp, :, c * CHUNK:(c + 1) * CHUNK] = blk[0:GATE_ROWS, :]

    def head_pairs(w_ref):
        for half in range(HEADS_PER_TILE // 2):
            acc = jnp.dot(h_scr[...], w_ref[:, half * pair:(half + 1) * pair], preferred_element_type=F32)
            for h2 in range(2):
                yield 2 * half + h2, acc[:, h2 * HEAD_DIM:(h2 + 1) * HEAD_DIM]

    def qk_chunk(t, c):
        ms = jnp.mean(t * t, axis=0, keepdims=True)
        tn = t * lax.rsqrt(ms + EPS) * gainT_ref[0]
        cs = cosT_ref[:, c * CHUNK:(c + 1) * CHUNK]
        sn = sinT_ref[:, c * CHUNK:(c + 1) * CHUNK]
        a, b = tn[0:ROPE_HALF], tn[ROPE_HALF:ROPE_DIMS]
        return jnp.concatenate([a * cs - b * sn, b * cs + a * sn, tn[ROPE_DIMS:]], axis=0)

    def store_T(yh, slot, treated):
        for c in range(n_chunks):
            chunk = yh[c * CHUNK:(c + 1) * CHUNK, :]
            if treated:
                yT_ref[slot, c] = qk_chunk(chunk.T, c).astype(BF16)
            else:
                yT_ref[slot, c] = chunk.astype(BF16).T

    def store_rows(yh, slot):
        for c in range(n_chunks):
            rows = slice(c * CHUNK, (c + 1) * CHUNK)
            ystd_ref[slot, rows, :] = qk_chunk(yh[rows, :].T, c).astype(BF16).T

    def when_kind(kind):
        steps = [t for t, name in enumerate(PROJ_TILES) if name == kind]
        return pl.when(functools.reduce(jnp.logical_or, [j == t for t in steps]))

    for kind, w_ref in (("nsa_q", wa_ref), ("moba_q", wb_ref)):
        @when_kind(kind)
        def _(w_ref=w_ref):
            for hh, yh in head_pairs(w_ref):
                store_T(yh, hh, True)

    @when_kind("moba_v")
    def _():
        for hh, yh in head_pairs(wb_ref):
            store_T(yh, hh, False)

    @when_kind("moba_k")
    def _():
        for hh, yh in head_pairs(wb_ref):
            store_rows(yh, hh)

    for kind, base in (("kv_slc", 0), ("kv_win", NSA_KV_HEADS)):
        @when_kind(kind)
        def _(base=base):
            for hh, yh in head_pairs(wa_ref):
                if hh < NSA_KV_HEADS:
                    store_rows(yh, base + hh)
                else:
                    store_T(yh, base + hh - NSA_KV_HEADS, False)

    @when_kind("kv_cmp")
    def _():
        for hh, yh in head_pairs(wa_ref):
            rows_scr[...] = yh
            flat = [rows_scr[pl.ds(l, tm // CMP_STRIDE, stride=CMP_STRIDE), :] for l in range(CMP_STRIDE)]
            hc_ref[hh] = jnp.concatenate(flat, axis=1).astype(BF16)


def _in_proj(x2, mod3, w_nsa, w_moba, w_gate, gainsT, cosT, sinT, T):
    BT, D = x2.shape
    G = NSA_KV_HEADS
    tm = min(1024, T)
    tn = HEADS_PER_TILE * HEAD_DIM
    tpb = T // tm
    advance = lambda j, steps: sum((j >= t).astype(jnp.int32) for t in steps)
    gain_kind = lambda j: sum((j == t).astype(jnp.int32) * GAIN_KINDS.index(PROJ_GAIN[name])
                              for t, name in enumerate(PROJ_TILES) if name in PROJ_GAIN)
    return pl.pallas_call(
        _inproj_kernel,
        grid=(BT // tm, len(PROJ_TILES)),
        in_specs=[pl.BlockSpec((tm, D), lambda i, j: (i, 0)),
                  pl.BlockSpec((1, 6, D), lambda i, j: (i // tpb, 0, 0)),
                  pl.BlockSpec((D, tn), lambda i, j: (0, jnp.minimum(j, N_NSA_TILES - 1))),
                  pl.BlockSpec((D, tn), lambda i, j: (0, jnp.maximum(j - N_NSA_TILES, 0))),
                  pl.BlockSpec((D, G * GATE_LANES), lambda i, j: (0, 0)),
                  pl.BlockSpec((1, HEAD_DIM, CHUNK), lambda i, j: (gain_kind(j), 0, 0)),
                  pl.BlockSpec((ROPE_HALF, tm), lambda i, j: (0, i % tpb)),
                  pl.BlockSpec((ROPE_HALF, tm), lambda i, j: (0, i % tpb))],
        out_specs=[pl.BlockSpec((HEADS_PER_TILE, tm // CHUNK, HEAD_DIM, CHUNK),
                                lambda i, j: (advance(j, YT_ADVANCE), i, 0, 0)),
                   pl.BlockSpec((HEADS_PER_TILE, tm, HEAD_DIM), lambda i, j: (advance(j, YSTD_ADVANCE), i, 0)),
                   pl.BlockSpec((N_CMP_HEADS, tm // CMP_STRIDE, CMP_STRIDE * HEAD_DIM), lambda i, j: (0, i, 0)),
                   pl.BlockSpec((G, GATE_ROWS, tm), lambda i, j: (0, 0, i))],
        out_shape=[jax.ShapeDtypeStruct((N_T_HEADS, BT // CHUNK, HEAD_DIM, CHUNK), BF16),
                   jax.ShapeDtypeStruct((N_STD_HEADS, BT, HEAD_DIM), BF16),
                   jax.ShapeDtypeStruct((N_CMP_HEADS, BT // CMP_STRIDE, CMP_STRIDE * HEAD_DIM), BF16),
                   jax.ShapeDtypeStruct((G, GATE_ROWS, BT), F32)],
        scratch_shapes=[pltpu.VMEM((tm, D), BF16), pltpu.VMEM((tm, HEAD_DIM), F32)],
        compiler_params=_cparams(("parallel", "arbitrary")),
        name="in_proj",
    )(x2, mod3, w_nsa, w_moba, w_gate, gainsT, cosT, sinT)


def _compress_kernel(h_ref, w1c_ref, pe_ref, w1_ref, w2_ref, gain_ref, cc_ref, sa_ref, sb_ref, o_ref):
    a = pl.program_id(0)
    ncp = h_ref.shape[2]
    z = jnp.dot(h_ref[0, 0], w1c_ref[0], preferred_element_type=F32)
    top = z[:, :HEAD_DIM]
    bot = pltpu.roll(z[:, HEAD_DIM:], ncp - 1, 0)
    pe_term = jnp.dot(pe_ref[0], w1_ref[0], preferred_element_type=F32)[0:1, :]
    pre = top + bot + pe_term
    act = pre * jax.nn.sigmoid(pre)
    out = jnp.dot(act.astype(BF16), w2_ref[0], preferred_element_type=F32)
    live = lax.broadcasted_iota(jnp.int32, out.shape, 0) < ncp - 1
    out = jnp.where(live, out, 0.0)

    @pl.when(a < NSA_KV_HEADS)
    def _():
        ms = jnp.mean(out * out, axis=-1, keepdims=True)
        yn = out * lax.rsqrt(ms + EPS) * gain_ref[...]
        o_ref[0, 0] = _rope(yn, cc_ref[...], sa_ref[...], sb_ref[...]).astype(BF16)

    @pl.when(a >= NSA_KV_HEADS)
    def _():
        o_ref[0, 0] = out.astype(BF16)


def _compress(hc, w1cat, pe8, w1, w2, gain, cc, sa, sb):
    A, B, ncp, K = hc.shape
    G = NSA_KV_HEADS
    return pl.pallas_call(
        _compress_kernel,
        grid=(A, B),
        in_specs=[pl.BlockSpec((1, 1, ncp, K), lambda a, b: (a, b, 0, 0)),
                  pl.BlockSpec((1, K, 2 * HEAD_DIM), lambda a, b: (a // G, 0, 0)),
                  pl.BlockSpec((1, 8, 2 * K), lambda a, b: (a // G, 0, 0)),
                  pl.BlockSpec((1, 2 * K, HEAD_DIM), lambda a, b: (a // G, 0, 0)),
                  pl.BlockSpec((1, HEAD_DIM, HEAD_DIM), lambda a, b: (a // G, 0, 0)),
                  pl.BlockSpec((1, HEAD_DIM), lambda a, b: (0, 0)),
                  pl.BlockSpec((ncp, HEAD_DIM), lambda a, b: (0, 0)),
                  pl.BlockSpec((ncp, HEAD_DIM), lambda a, b: (0, 0)),
                  pl.BlockSpec((ncp, HEAD_DIM), lambda a, b: (0, 0))],
        out_specs=pl.BlockSpec((1, 1, ncp, HEAD_DIM), lambda a, b: (a, b, 0, 0)),
        out_shape=jax.ShapeDtypeStruct((A, B, ncp, HEAD_DIM), BF16),
        compiler_params=_cparams(("arbitrary", "arbitrary")),
        name="compress",
    )(hc, w1cat, pe8, w1, w2, gain, cc, sa, sb)


def _first_step(s, vT, m_scr, acc_scr):
    m = jnp.max(s, axis=0, keepdims=True)
    p = jnp.exp2(s - m)
    acc_scr[...] = jnp.dot(_with_ones(vT), p.astype(BF16), preferred_element_type=F32)
    m_scr[...] = m


def _with_ones(vT):
    return jnp.concatenate([vT, jnp.ones((ONES_ROWS, vT.shape[1]), vT.dtype)], axis=0)


def _normalized(acc):
    return acc[0:HEAD_DIM] / acc[HEAD_DIM:HEAD_DIM + 1]


def _online_step(s, vT, m_scr, acc_scr):
    m_prev = m_scr[...]
    m_new = jnp.maximum(m_prev, jnp.max(s, axis=0, keepdims=True))
    alpha = jnp.exp2(m_prev - m_new)
    p = jnp.exp2(s - m_new)
    acc_scr[...] = alpha * acc_scr[...] + jnp.dot(_with_ones(vT), p.astype(BF16), preferred_element_type=F32)
    m_scr[...] = m_new


def _rank_select(score, k):
    n = score.shape[0]
    rank = jnp.zeros(score.shape, F32)
    for m in range(n):
        sm = score[m:m + 1, :]
        lo = (m // SUBLANES) * SUBLANES
        hi = min(lo + SUBLANES, n)
        parts = []
        if lo > 0:
            parts.append(jnp.where(sm > score[:lo], 1.0, 0.0))
        gt =jnp.where(sm > score[lo:hi], 1.0, 0.0)
        ge = jnp.where(sm >= score[lo:hi], 1.0, 0.0)
        below = lax.broadcasted_iota(jnp.int32, gt.shape, 0) > m - lo
        parts.append(jnp.where(below, ge, gt))
        if hi < n:
            parts.append(jnp.where(sm >= score[hi:], 1.0, 0.0))
        rank = rank + (jnp.concatenate(parts, axis=0) if len(parts) > 1 else parts[0])
    return rank < k


def _pad_rows(a, rows):
    return jnp.concatenate([a, jnp.zeros((rows - a.shape[0], a.shape[1]), a.dtype)], axis=0)


def _masked_scores(k_ref, e_ref, start, size, q_aug):
    rows = pl.ds(pl.multiple_of(start, 128), size)
    k_aug = jnp.concatenate([k_ref[rows, :], e_ref[rows, :]], axis=1)
    return jnp.dot(k_aug, q_aug, preferred_element_type=F32)


def _pipelined_attention(chains, n_steps, n_max, tks, tk, body_pairs):
    per = tks // tk

    def scores(step, buf):
        st = jnp.minimum(step, n_max - 1)
        dead = jnp.where(step < n_steps, 0, 1)
        for k_ref, e_ref, _, qaug_ref, bufs, _, _ in chains:
            bufs[buf][...] = _masked_scores(k_ref, e_ref, st * tks, tks, qaug_ref[dead])

    def consume(step, buf):
        st = jnp.minimum(step, n_max - 1)
        for _, _, vT_ref, _, bufs, m_scr, acc_scr in chains:
            vT = jnp.concatenate([vT_ref[st * per + a] for a in range(per)], axis=1)
            _online_step(bufs[buf][...], vT, m_scr, acc_scr)

    def pairs(first_step, n_pairs_in_body):
        def body(it, carry):
            base = first_step + it * 2 * n_pairs_in_body
            for p in range(n_pairs_in_body):
                scores(base + 2 * p + 1, 1)
                consume(base + 2 * p, 0)
                scores(base + 2 * p + 2, 0)
                consume(base + 2 * p + 1, 1)
            return carry
        return body

    scores(0, 0)
    done = 0
    for n_body in body_pairs[:-1]:
        n_iter = (n_steps - done) // (2 * n_body)
        lax.fori_loop(0, n_iter, pairs(done, n_body), 0)
        done = done + n_iter * 2 * n_body
    lax.fori_loop(0, (n_steps - done + 1) // 2, pairs(done, 1), 0)


def _carry_casts(kernel_fn, n_in, n_out, n_cast):
    def wrapped(*refs):
        ins, cast_in = refs[:n_in], refs[n_in:n_in + n_cast]
        outs = refs[n_in + n_cast:n_in + n_cast + n_out]
        cast_out = refs[n_in + n_cast + n_out:n_in + n_cast + n_out + n_cast]
        for src, dst in zip(cast_in, cast_out):
            dst[...] = src[...].astype(BF16)
        kernel_fn(*ins, *outs, *refs[n_in + n_cast + n_out + n_cast:])
    return wrapped


def _cast_specs(weights, grid):
    n_steps = grid[0] * grid[1] * grid[2]
    specs, shapes = [], []
    for w, rows in weights:
        while w.shape[0] // rows > n_steps:
            rows *= 2
        n_blk = w.shape[0] // rows
        assert w.shape[0] % rows == 0
        index = lambda a, b, c, n_blk=n_blk: (jnp.minimum((a * grid[1] + b) * grid[2] + c, n_blk - 1), 0)
        specs.append(pl.BlockSpec((rows, w.shape[1]), index))
        shapes.append(jax.ShapeDtypeStruct(w.shape, BF16))
    return specs, shapes


def _nsa_body(qT_ref, kc_ref, vcT_ref, ks_ref, vsT_ref, kw_ref, vwT_ref, gT_ref, e_ref,
              gain_ref, o_ref, m_scr, acc_scr, comb_scr, qaug_scr, s0_scr, s1_scr, psum_scr, *, T):
    R, tq = NSA_GROUP, NSA_TQ
    i = pl.program_id(2)
    t0 = i * tq
    ncp = T // CMP_STRIDE
    nsb = T // SLC_BLOCK
    qT = jnp.concatenate([qT_ref[r, c] for r in range(R) for c in range(tq // CHUNK)], axis=1)
    gT = gT_ref[0]

    def per_head(row0):
        return jnp.concatenate([gT[row0 + 3 * r:row0 + 3 * r + 1, :] for r in range(R)], axis=1)

    def lanes_x_heads(a):
        return jnp.concatenate([a] * R, axis=1)

    s = jnp.dot(kc_ref[0, 0], qT, preferred_element_type=F32)
    c_idx = lax.broadcasted_iota(jnp.int32, (ncp, tq), 0)
    t_c = t0 + lax.broadcasted_iota(jnp.int32, (ncp, tq), 1)
    vis = lanes_x_heads((c_idx * CMP_STRIDE + (CMP_BLOCK - 1) <= t_c) & (c_idx < ncp - 1))
    s = jnp.where(vis, s, NEG)
    m = jnp.maximum(jnp.max(s, axis=0, keepdims=True), M_INIT)
    p = jnp.exp2(s - m)
    l = jnp.sum(p, axis=0, keepdims=True)
    p = p / jnp.where(l > 0.0, l, 1.0)
    o_cmp = jnp.dot(vcT_ref[0, 0], p.astype(BF16), preferred_element_type=F32)
    comb_scr[...] = o_cmp * per_head(0)

    psum = p[:, 0:tq]
    for r in range(1, R):
        psum = psum + p[:, r * tq:(r + 1) * tq]
    per = SLC_BLOCK // CMP_STRIDE
    imp_chunks = []
    for c in range(tq // CHUNK):
        psum_scr[c, 0:SUBLANES, :] = jnp.zeros((SUBLANES, CHUNK), F32)
        psum_scr[c, SUBLANES:, :] = psum[:, c * CHUNK:(c + 1) * CHUNK]
        acc = jnp.zeros((nsb, CHUNK), F32)
        for k in range(1 - CMP_BLOCK // CMP_STRIDE, per):
            acc = acc + psum_scr[c, pl.ds(SUBLANES + k, nsb, stride=per), :]
        imp_chunks.append(acc)
    imp = jnp.concatenate(imp_chunks, axis=1)
    n_idx = lax.broadcasted_iota(jnp.int32, (nsb, tq), 0)
    cur = (t0 + lax.broadcasted_iota(jnp.int32, (nsb, tq), 1)) // SLC_BLOCK
    forced = (n_idx == 0) | (n_idx == cur) | (n_idx == cur - 1)
    valid = n_idx <= cur
    score = jnp.where(valid, imp + jnp.where(forced, FORCE_BONUS, 0.0), NEG)
    sel = _rank_select(score, min(SLC_TOPK, nsb)) & valid
    sel_past = sel & (n_idx * SLC_BLOCK < t0)
    bias = _pad_rows(jnp.where(sel_past, 0.0, NEG), HEAD_DIM).astype(BF16)
    bias_own = _pad_rows(jnp.where(sel, 0.0, NEG), HEAD_DIM).astype(BF16)
    for d in range(3):
        qaug_scr[d, 0:HEAD_DIM, :] = qT
    qaug_scr[0, HEAD_DIM:, :] = lanes_x_heads(bias)
    qaug_scr[1, HEAD_DIM:, :] = jnp.full((HEAD_DIM, R * tq), NEG, BF16)
    qaug_scr[2, HEAD_DIM:, :] = lanes_x_heads(bias_own)

    w0 = jnp.maximum(t0 + tq - WIN_SPAN, 0)
    k_w = kw_ref[0, pl.ds(pl.multiple_of(w0, CHUNK), WIN_SPAN), :]
    s_w = jnp.dot(k_w, qT, preferred_element_type=F32)
    kpos = w0 + lax.broadcasted_iota(jnp.int32, (WIN_SPAN, tq), 0)
    tpos = t0 + lax.broadcasted_iota(jnp.int32, (WIN_SPAN, tq), 1)
    ok = lanes_x_heads((kpos <= tpos) & (tpos - kpos < WINDOW))
    s_w = jnp.where(ok, s_w, NEG)
    m_w = jnp.max(s_w, axis=0, keepdims=True)
    p_w = jnp.exp2(s_w - m_w)
    jw = w0 // CHUNK
    v_w = jnp.concatenate([vwT_ref[0, jw + a] for a in range(WIN_SPAN // CHUNK)], axis=1)
    o_w = jnp.dot(_with_ones(v_w), p_w.astype(BF16), preferred_element_type=F32)
    comb_scr[...] += _normalized(o_w) * per_head(2)

    tri =lax.broadcasted_iota(jnp.int32, (tq, tq), 0) <= lax.broadcasted_iota(jnp.int32, (tq, tq), 1)
    s_d = jnp.where(lanes_x_heads(tri), _masked_scores(ks_ref.at[0], e_ref, t0, tq, qaug_scr[2]), NEG)
    own = tq // CHUNK
    v_d = jnp.concatenate([vsT_ref[0, i * own + a] for a in range(own)], axis=1)
    _first_step(s_d, v_d, m_scr, acc_scr)
    yield [(ks_ref.at[0], e_ref, vsT_ref.at[0], qaug_scr, (s0_scr, s1_scr), m_scr, acc_scr)]
    comb =comb_scr[...] + _normalized(acc_scr[...]) * per_head(1)

    for r in range(R):
        oT = comb[:, r * tq:(r + 1) * tq]
        ms = jnp.mean(oT * oT, axis=0, keepdims=True)
        on = (oT * lax.rsqrt(ms + EPS)).T * gain_ref[0, r:r + 1, :]
        o_ref[:, r * HEAD_DIM:(r + 1) * HEAD_DIM] = on.astype(BF16)


def _moba_body(qT_ref, k_ref, vT_ref, e_ref, gain_ref, o_ref, m_scr, acc_scr, kmean_scr, qaug_scr,
               s_scr, *, T):
    tq, tk = MOBA_TQ, MOBA_TK
    nb = T // MOBA_BLOCK
    nbp = kmean_scr.shape[1]
    i = pl.program_id(2)
    t0 = i * tq

    @pl.when(i == 0)
    def _():
        kmean_scr[...] = jnp.zeros(kmean_scr.shape, F32)
        for a in range(MOBA_HB):
            kb = k_ref[a].astype(F32).reshape(nb, MOBA_BLOCK, HEAD_DIM)
            kmean_scr[a, 0:nb, :] = jnp.mean(kb, axis=1)

    n_idx = lax.broadcasted_iota(jnp.int32, (nbp, tq), 0)
    cur = (t0 + lax.broadcasted_iota(jnp.int32, (nbp, tq), 1)) // MOBA_BLOCK
    past = n_idx < cur
    causal = lax.broadcasted_iota(jnp.int32, (tk, tq), 0) <= lax.broadcasted_iota(jnp.int32, (tk, tq), 1)

    qTs = [jnp.concatenate([qT_ref[a, c] for c in range(tq // CHUNK)], axis=1) for a in range(MOBA_HB)]
    for a in range(MOBA_HB):
        qT = qTs[a]
        gate = jnp.zeros((nbp, tq), F32)
        for part in _split3(kmean_scr[a]):
            gate = gate + jnp.dot(part, qT, preferred_element_type=F32)
        sel = _rank_select(jnp.where(past, gate, NEG), min(MOBA_TOPK, nb)) & past
        for d in range(2):
            qaug_scr[a, d, 0:HEAD_DIM, :] = qT
        qaug_scr[a, 0, HEAD_DIM:, :] = _pad_rows(jnp.where(sel, 0.0, NEG), HEAD_DIM).astype(BF16)
        qaug_scr[a, 1, HEAD_DIM:, :] = jnp.full((HEAD_DIM, tq), NEG, BF16)

    s_own = []
    for a in range(MOBA_HB):
        m_scr[a] = jnp.full(m_scr.shape[1:], M_INIT, F32)
        acc_scr[a] = jnp.zeros(acc_scr.shape[1:], F32)
        k_own = k_ref[a, pl.ds(pl.multiple_of(t0, tk), tk), :]
        s_own.append(jnp.where(causal, jnp.dot(k_own, qTs[a], preferred_element_type=F32), NEG))

    yield [(k_ref.at[a], e_ref, vT_ref.at[a], qaug_scr.at[a], (s_scr.at[a, 0], s_scr.at[a, 1]),
            m_scr.at[a], acc_scr.at[a]) for a in range(MOBA_HB)]

    own = tq // CHUNK
    for a in range(MOBA_HB):
        v_own = jnp.concatenate([vT_ref[a, i * own + c] for c in range(own)], axis=1)
        _online_step(s_own[a], v_own, m_scr.at[a], acc_scr.at[a])
        oT = _normalized(acc_scr[a])
        ms = jnp.mean(oT * oT, axis=0, keepdims=True)
        on = (oT * lax.rsqrt(ms + EPS)).T * gain_ref[a]
        o_ref[:, a * HEAD_DIM:(a + 1) * HEAD_DIM] = on.astype(BF16)


N_NSA_IN, N_NSA_SCRATCH, N_MOBA_IN = 10, 7, 5


def _attention_kernel(*refs, T):
    nsa_in, moba_in = refs[:N_NSA_IN], refs[N_NSA_IN:N_NSA_IN + N_MOBA_IN]
    o_nsa, o_moba = refs[N_NSA_IN + N_MOBA_IN:N_NSA_IN + N_MOBA_IN + 2]
    scratch = refs[N_NSA_IN + N_MOBA_IN + 2:]
    moba = _moba_body(*moba_in, o_moba, *scratch[N_NSA_SCRATCH:], T=T)
    nsa = _nsa_body(*nsa_in, o_nsa, *scratch[:N_NSA_SCRATCH], T=T)
    chains = next(moba) + next(nsa)
    n_steps = pl.program_id(2) * ATTN_TQ // ATTN_TKS
    _pipelined_attention(chains, n_steps, T // ATTN_TKS, ATTN_TKS, CHUNK, ATTN_BODY_PAIRS)
    for rest in (nsa, moba):
        assert next(rest, None) is None


def _attention(yT, ystd, kcvc, vcT, gT, e_slc, e_moba, nsa_gains, moba_gains, casts, B, T):
    G, R, tq = NSA_KV_HEADS, NSA_GROUP, ATTN_TQ
    H, HB = MOBA_HEADS, MOBA_HB
    assert H // HB == G and NSA_TQ == MOBA_TQ == tq and SLC_TKS == MOBA_TKS == ATTN_TKS and tq % ATTN_TKS == 0
    nq = T // tq
    ncp = T // CMP_STRIDE
    nb = T // MOBA_BLOCK
    nbp = max(16, nb)
    grid = (B, G, nq)
    cast_specs, cast_shapes = _cast_specs(casts, grid)
    kern = _carry_casts(functools.partial(_attention_kernel, T=T), N_NSA_IN + N_MOBA_IN, 2, len(casts))
    keys = lambda head0: pl.BlockSpec((1, T, HEAD_DIM), lambda b, g, i: (head0 + g, b, 0))
    values = lambda head0: pl.BlockSpec((1, T // CHUNK, HEAD_DIM, CHUNK), lambda b, g, i: (head0 + g, b, 0, 0))
    onehot = pl.BlockSpec((T, HEAD_DIM), lambda b, g, i: (0, 0))
    nsa_specs = [pl.BlockSpec((R, tq // CHUNK, HEAD_DIM, CHUNK), lambda b, g, i: (T_NSA_Q // R + g, b * nq + i, 0, 0)),
                 pl.BlockSpec((1, 1, ncp, HEAD_DIM), lambda b, g, i: (g, b, 0, 0)),
                 pl.BlockSpec((1, 1, HEAD_DIM, ncp), lambda b, g, i: (g, b, 0, 0)),
                 keys(S_KSLC), values(T_VSLC), keys(S_KWIN), values(T_VWIN),
                 pl.BlockSpec((1, GATE_ROWS, tq), lambda b, g, i: (g, 0, b * nq + i)),
                 onehot,
                 pl.BlockSpec((1, R, HEAD_DIM), lambda b, g, i: (g, 0, 0))]
    moba_specs = [pl.BlockSpec((HB, tq // CHUNK, HEAD_DIM, CHUNK), lambda b, h, i: (T_MOBA_Q // HB + h, b * nq + i, 0, 0)),
                  pl.BlockSpec((HB, T, HEAD_DIM), lambda b, h, i: (S_MOBA_K // HB + h, b, 0)),
                  pl.BlockSpec((HB, T // CHUNK, HEAD_DIM, CHUNK), lambda b, h, i: (T_MOBA_V // HB + h, b, 0, 0)),
                  onehot,
                  pl.BlockSpec((HB, 1, HEAD_DIM), lambda b, h, i: (h, 0, 0))]
    nsa_scratch = [pltpu.VMEM((1, R * tq), F32),
                   pltpu.VMEM((HEAD_DIM + ONES_ROWS, R * tq), F32), pltpu.VMEM((HEAD_DIM, R * tq), F32),
                   pltpu.VMEM((3, 2 * HEAD_DIM, R * tq), BF16),
                   pltpu.VMEM((ATTN_TKS, R * tq), F32), pltpu.VMEM((ATTN_TKS, R * tq), F32),
                   pltpu.VMEM((tq // CHUNK, SUBLANES + ncp, CHUNK), F32)]
    moba_scratch = [pltpu.VMEM((HB, 1, tq), F32),
                    pltpu.VMEM((HB, HEAD_DIM + ONES_ROWS, tq), F32), pltpu.VMEM((HB, nbp, HEAD_DIM), F32),
                    pltpu.VMEM((HB, 2, 2 * HEAD_DIM, tq), BF16),
                    pltpu.VMEM((HB, 2, ATTN_TKS, tq), F32)]
    assert len(nsa_specs) == N_NSA_IN and len(moba_specs) == N_MOBA_IN and len(nsa_scratch) == N_NSA_SCRATCH
    return pl.pallas_call(
        kern,
        grid=grid,
        in_specs=nsa_specs + moba_specs + cast_specs,
        out_specs=[pl.BlockSpec((tq, R * HEAD_DIM), lambda b, g, i: (b * nq + i, g)),
                   pl.BlockSpec((tq, HB * HEAD_DIM), lambda b, h, i: (b * nq + i, h))] + cast_specs,
        out_shape=[jax.ShapeDtypeStruct((B * T, NSA_HEADS * HEAD_DIM), BF16),
                   jax.ShapeDtypeStruct((B * T, H * HEAD_DIM), BF16)] + cast_shapes,
        scratch_shapes=nsa_scratch + moba_scratch,
        compiler_params=_cparams(("parallel", "parallel", "arbitrary")),
        name="attention",
    )(yT, kcvc, vcT, ystd, yT, ystd, yT, gT, e_slc, nsa_gains,
      yT, ystd, yT, e_moba, moba_gains, *[w for w, _ in casts])


def _outproj_kernel(on_ref, om_ref, w_ref, x_ref, mod_ref, o_ref):
    half = on_ref.shape[1]
    acc = jnp.dot(on_ref[...], w_ref[0:half, :], preferred_element_type=F32)
    acc = acc + jnp.dot(om_ref[...], w_ref[half:, :], preferred_element_type=F32)
    o_ref[...] = x_ref[...] + mod_ref[0, 2:3, :] * acc


def _out_proj(o_nsa, o_moba, w_out, x2, mod3, T):
    BT, D = x2.shape
    tm = min(512, T)
    tpb = T // tm
    half = o_nsa.shape[1]
    return pl.pallas_call(
        _outproj_kernel,
        grid=(BT // tm,),
        in_specs=[pl.BlockSpec((tm, half), lambda i: (i, 0)),
                  pl.BlockSpec((tm, half), lambda i: (i, 0)),
                  pl.BlockSpec((D, D), lambda i: (0, 0)),
                  pl.BlockSpec((tm, D), lambda i: (i, 0)),
                  pl.BlockSpec((1, 6, D), lambda i: (i // tpb, 0, 0))],
        out_specs=pl.BlockSpec((tm, D), lambda i: (i, 0)),
        out_shape=jax.ShapeDtypeStruct((BT, D), F32),
        compiler_params=_cparams(("parallel",)),
        name="out_proj",
    )(o_nsa, o_moba, w_out, x2, mod3)


def _ffn_kernel(x_ref, mod_ref, wg_ref, wu_ref, wo_ref, o_ref, h_scr):
    j = pl.program_id(1)

    @pl.when(j == 0)
    def _():
        x = x_ref[...]
        ms = jnp.mean(x * x, axis=-1, keepdims=True)
        h = x * lax.rsqrt(ms + EPS) * (1.0 + mod_ref[0, 4:5, :]) + mod_ref[0, 3:4, :]
        h_scr[...] = h.astype(BF16)
        o_ref[...] = jnp.zeros(o_ref.shape, F32)

    hb = h_scr[...]
    half = wg_ref.shape[1] // 2
    down = None
    for c in range(2):
        cols = slice(c * half, (c + 1) * half)
        gate = jnp.dot(hb, wg_ref[:, cols], preferred_element_type=F32)
        up = jnp.dot(hb, wu_ref[:, cols], preferred_element_type=F32)
        act = (gate * jax.nn.sigmoid(gate) * up).astype(BF16)
        part = jnp.dot(act, wo_ref[cols, :], preferred_element_type=F32)
        down = part if down is None else down + part
    o_ref[...] += down

    @pl.when(j == pl.num_programs(1) - 1)
    def _():
        o_ref[...] = x_ref[...] + mod_ref[0, 5:6, :] * o_ref[...]


def _ffn(x1, mod3, w_in, w_out, T):
    BT, D = x1.shape
    Fh = w_out.shape[0]
    tm = min(512, T)
    tf = 512
    tpb = T // tm
    nf = Fh // tf
    return pl.pallas_call(
        _ffn_kernel,
        grid=(BT // tm, nf),
        in_specs=[pl.BlockSpec((tm, D), lambda i, j: (i, 0)),
                  pl.BlockSpec((1, 6, D), lambda i, j: (i // tpb, 0, 0)),
                  pl.BlockSpec((D, tf), lambda i, j: (0, j)),
                  pl.BlockSpec((D, tf), lambda i, j: (0, nf + j)),
                  pl.BlockSpec((tf, D), lambda i, j: (j, 0))],
        out_specs=pl.BlockSpec((tm, D), lambda i, j: (i, 0)),
        out_shape=jax.ShapeDtypeStruct((BT, D), F32),
        scratch_shapes=[pltpu.VMEM((tm, D), BF16)],
        compiler_params=_cparams(("parallel", "arbitrary")),
        name="ffn",
    )(x1, mod3, w_in, w_in, w_out)


def _rope_cos_sin(pos):
    inv = ROPE_THETA ** (-jnp.arange(0, ROPE_DIMS, 2, dtype=F32) / ROPE_DIMS)
    ang = pos.astype(F32)[:, None] * inv[None, :]
    return jnp.cos(ang), jnp.sin(ang)


def _rope_tables(T):
    c_hi, s_hi = _rope_cos_sin(jnp.arange(0, T, ROPE_SPLIT))
    c_lo, s_lo = _rope_cos_sin(jnp.arange(ROPE_SPLIT))
    cos = (c_hi[:, None] * c_lo[None] - s_hi[:, None] * s_lo[None]).reshape(T, ROPE_HALF)
    sin = (s_hi[:, None] * c_lo[None] + c_hi[:, None] * s_lo[None]).reshape(T, ROPE_HALF)
    cos_c, sin_c = _rope_cos_sin(jnp.arange(CMP_BLOCK - 1, T, CMP_STRIDE))
    n, rest = cos_c.shape[0], HEAD_DIM - ROPE_DIMS
    pad = lambda t: jnp.pad(t, ((0, 1), (0, 0)))
    cc = pad(jnp.concatenate([cos_c, cos_c, jnp.ones((n, rest), F32)], axis=1))
    sa = pad(jnp.concatenate([-sin_c, jnp.zeros((n, HEAD_DIM - ROPE_HALF), F32)], axis=1))
    sb = pad(jnp.concatenate([jnp.zeros((n, ROPE_HALF), F32), sin_c, jnp.zeros((n, rest), F32)], axis=1))
    return cc, sa, sb, cos.T, sin.T


def _block_onehot(T, block):
    return (jnp.arange(T)[:, None] // block == jnp.arange(HEAD_DIM)[None, :]).astype(BF16)


def _layer(x2, c, B, T, w_ada, b_ada, w_in, nsa_q_norm, nsa_k_norm, moba_q_norm, moba_k_norm,
           cmp_pe_k, cmp_w1_k, cmp_w2_k, cmp_pe_v, cmp_w1_v, cmp_w2_v, out_norm, w_out,
           w_ffn_in, w_ffn_out):
    D = x2.shape[1]
    G = NSA_KV_HEADS
    scale = HEAD_DIM ** -0.5 * LOG2E
    assert T % MOBA_BLOCK == 0 and T % SLC_TKS == 0 and T % MOBA_TKS == 0 and T % NSA_TQ == 0 and T >= WIN_SPAN
    assert T // SLC_BLOCK <= HEAD_DIM and T // MOBA_BLOCK <= HEAD_DIM

    nsa_w = NSA_HEADS * HEAD_DIM + 6 * G * HEAD_DIM
    gw = NSA_HEADS * 3
    mod3 = _adaln(c, w_ada, b_ada)
    w_nsa = w_in.astype(BF16)
    w_moba = w_nsa[:, nsa_w + gw:]

    wg = w_in[:, nsa_w:nsa_w + gw].reshape(D, G, NSA_GROUP * 3)
    w_gate = jnp.pad(wg, ((0, 0), (0, 0), (0, GATE_LANES - NSA_GROUP * 3))).reshape(D, G * GATE_LANES)
    w_gate = w_gate.astype(BF16)
    across = lambda g_: jnp.broadcast_to(g_[:, None], (HEAD_DIM, CHUNK))
    gainsT = jnp.stack([across(nsa_q_norm * scale), across(moba_q_norm * scale),
                        across(nsa_k_norm[1]), across(nsa_k_norm[2]), across(moba_k_norm)])
    cc, sa, sb, cosT, sinT = _rope_tables(T)

    yT, ystd, hc, gT = _in_proj(x2, mod3, w_nsa, w_moba, w_gate, gainsT, cosT, sinT, T)

    ncp = T // CMP_STRIDE
    half = CMP_STRIDE * HEAD_DIM
    w1 = jnp.stack([cmp_w1_k, cmp_w1_v]).astype(BF16)
    w1cat = jnp.concatenate([w1[:, :half], w1[:, half:]], axis=2)
    pe8 = jnp.broadcast_to(jnp.stack([cmp_pe_k, cmp_pe_v]).reshape(2, 1, 2 * half), (2, 8, 2 * half))
    w2 = jnp.stack([cmp_w2_k, cmp_w2_v]).astype(BF16)
    kcvc = _compress(hc.reshape(2 * G, B, ncp, half), w1cat, pe8.astype(BF16), w1, w2,
                     nsa_k_norm[0].reshape(1, HEAD_DIM), cc, sa, sb)
    vcT = kcvc[G:].transpose(0, 1, 3, 2)

    on = out_norm.reshape(N_HEADS, HEAD_DIM)
    o_nsa, o_moba, w_ffn_out_b, w_out_b, w_ffn_in_b = _attention(
        yT, ystd, kcvc, vcT, gT, _block_onehot(T, SLC_BLOCK), _block_onehot(T, MOBA_BLOCK),
        on[:NSA_HEADS].reshape(G, NSA_GROUP, HEAD_DIM), on[NSA_HEADS:].reshape(MOBA_HEADS, 1, HEAD_DIM),
        [(w_ffn_out, 128), (w_out, 128), (w_ffn_in, 32)], B, T)

    x1 = _out_proj(o_nsa, o_moba, w_out_b, x2, mod3, T)
    return _ffn(x1, mod3, w_ffn_in_b, w_ffn_out_b, T)


def kernel(x, c, w_ada, b_ada, w_in, nsa_q_norm, nsa_k_norm, moba_q_norm, moba_k_norm, cmp_pe_k, cmp_w1_k, cmp_w2_k, cmp_pe_v, cmp_w1_v, cmp_w2_v, out_norm, w_out, w_ffn_in, w_ffn_out):
    B, T, D = x.shape
    x2 = x.reshape(B * T, D)
    for l in range(w_ada.shape[0]):
        x2 = _layer(x2, c, B, T, w_ada[l], b_ada[l], w_in[l], nsa_q_norm[l], nsa_k_norm[l],
                    moba_q_norm[l], moba_k_norm[l], cmp_pe_k[l], cmp_w1_k[l], cmp_w2_k[l],
                    cmp_pe_v[l], cmp_w1_v[l], cmp_w2_v[l], out_norm[l], w_out[l],
                    w_ffn_in[l], w_ffn_out[l])
    return x2.reshape(B, T, D)
```

```python
import functools

import jax
import jax.numpy as jnp
from jax import lax
from jax.experimental import pallas as pl
from jax.experimental.pallas import tpu as pltpu

F32 = jnp.float32
BF16 = jnp.bfloat16

HEAD_DIM = 128
SUBLANES = 8
NSA_HEADS = 8
NSA_KV_HEADS = 2
NSA_GROUP = NSA_HEADS // NSA_KV_HEADS
MOBA_HEADS = 8
N_HEADS = NSA_HEADS + MOBA_HEADS
CMP_BLOCK = 32
CMP_STRIDE = 16
SLC_BLOCK = 64
SLC_TOPK = 16
WINDOW = 512
FORCE_BONUS = 1e4
MOBA_BLOCK = 256
MOBA_TOPK = 3
ROPE_THETA = 500000.0
ROPE_DIMS = HEAD_DIM // 4
ROPE_HALF = ROPE_DIMS // 2
ROPE_SPLIT = 64
EPS = 1e-6
LOG2E = 1.4426950408889634
ONES_ROWS = 16
NEG = -1e30
M_INIT = -1e29

V7X_VMEM_BYTES = 64 * 1024 * 1024
VMEM_LIMIT = V7X_VMEM_BYTES - 8 * 1024 * 1024

HEADS_PER_TILE = 4
PROJ_TILES = ("nsa_q", "nsa_q", "kv_cmp", "kv_slc", "kv_win",
              "moba_q", "moba_q", "moba_k", "moba_k", "moba_v", "moba_v")
N_NSA_TILES = 5
GAIN_KINDS = ("nsa_q", "moba_q", "k_slc", "k_win", "moba_k")
PROJ_GAIN = {"nsa_q": "nsa_q", "moba_q": "moba_q", "kv_slc": "k_slc", "kv_win": "k_win", "moba_k": "moba_k"}
YT_ADVANCE = (1, 3, 5, 6, 9, 10)
YSTD_ADVANCE = (7, 8)
T_NSA_Q, T_VSLC, T_VWIN, T_MOBA_Q, T_MOBA_V = 0, 8, 10, 12, 20
S_KSLC, S_KWIN, S_MOBA_K = 0, 2, 4
N_T_HEADS, N_STD_HEADS, N_CMP_HEADS = 28, 12, 4
CHUNK = 128
GATE_LANES = 128
GATE_ROWS = 16

ATTN_TQ = 256
ATTN_TKS = 256
ATTN_BODY_PAIRS = (4, 2, 1)
NSA_TQ = ATTN_TQ
SLC_TKS = ATTN_TKS
WIN_SPAN = WINDOW + NSA_TQ
MOBA_TQ = ATTN_TQ
MOBA_TK = MOBA_BLOCK
MOBA_TKS = ATTN_TKS
MOBA_HB = 4


def _cparams(sem):
    return pltpu.CompilerParams(dimension_semantics=sem, vmem_limit_bytes=VMEM_LIMIT)


def _split3(a):
    hi = a.astype(BF16)
    r1 = a - hi.astype(F32)
    mid = r1.astype(BF16)
    lo = (r1 - mid.astype(F32)).astype(BF16)
    return hi, mid, lo


def _adaln_kernel(c_ref, w_ref, b_ref, o_ref):
    cv = c_ref[...]
    s = cv * jax.nn.sigmoid(cv)
    w = w_ref[...].astype(BF16)
    acc = b_ref[...] + jnp.zeros(o_ref.shape, F32)
    for part in _split3(s)[:2]:
        acc = acc + jnp.dot(part, w, preferred_element_type=F32)
    o_ref[...] = acc


def _adaln(c, w_ada, b_ada):
    B, D = c.shape
    N = w_ada.shape[1]
    tn = 1024
    c8 = jnp.zeros((8, D), F32).at[:B].set(c)
    out = pl.pallas_call(
        _adaln_kernel,
        grid=(N // tn,),
        in_specs=[pl.BlockSpec((8, D), lambda j: (0, 0)),
                  pl.BlockSpec((D, tn), lambda j: (0, j)),
                  pl.BlockSpec((1, tn), lambda j: (0, j))],
        out_specs=pl.BlockSpec((8, tn), lambda j: (0, j)),
        out_shape=jax.ShapeDtypeStruct((8, N), F32),
        compiler_params=_cparams(("arbitrary",)),
        name="adaln",
    )(c8, w_ada, b_ada.reshape(1, N))
    return out[:B].reshape(B, 6, D)


def _rope(y, cc, sa, sb):
    return (y * cc + pltpu.roll(y, HEAD_DIM - ROPE_HALF, 1) * sa + pltpu.roll(y, ROPE_HALF, 1) * sb)


def _inproj_kernel(x_ref, mod_ref, wa_ref, wb_ref, wg_ref, gainT_ref, cosT_ref, sinT_ref,
                   yT_ref, ystd_ref, hc_ref, gT_ref, h_scr, rows_scr):
    j = pl.program_id(1)
    tm = x_ref.shape[0]
    n_chunks = tm // CHUNK
    pair = 2 * HEAD_DIM

    @pl.when(j == 0)
    def _():
        x = x_ref[...]
        ms = jnp.mean(x * x, axis=-1, keepdims=True)
        h = x * lax.rsqrt(ms + EPS) * (1.0 + mod_ref[0, 1:2, :]) + mod_ref[0, 0:1, :]
        hb = h.astype(BF16)
        h_scr[...] = hb
        g = jax.nn.sigmoid(jnp.dot(hb, wg_ref[...], preferred_element_type=F32))
        for grp in range(NSA_KV_HEADS):
            for c in range(n_chunks):
                blk = g[c * CHUNK:(c + 1) * CHUNK, grp * GATE_LANES:(grp + 1) * GATE_LANES].T
                gT_ref[grp, :, c * CHUNK:(c + 1) * CHUNK] = blk[0:GATE_ROWS, :]

    def head_pairs(w_ref):
        for half in range(HEADS_PER_TILE // 2):
            acc = jnp.dot(h_scr[...], w_ref[:, half * pair:(half + 1) * pair], preferred_element_type=F32)
            for h2 in range(2):
                yield 2 * half + h2, acc[:, h2 * HEAD_DIM:(h2 + 1) * HEAD_DIM]

    def qk_chunk(t, c):
        ms = jnp.mean(t * t, axis=0, keepdims=True)
        tn = t * lax.rsqrt(ms + EPS) * gainT_ref[0]
        cs = cosT_ref[:, c * CHUNK:(c + 1) * CHUNK]
        sn = sinT_ref[:, c * CHUNK:(c + 1) * CHUNK]
        a, b = tn[0:ROPE_HALF], tn[ROPE_HALF:ROPE_DIMS]
        return jnp.concatenate([a * cs - b * sn, b * cs + a * sn, tn[ROPE_DIMS:]], axis=0)

    def store_T(yh, slot, treated):
        for c in range(n_chunks):
            chunk = yh[c * CHUNK:(c + 1) * CHUNK, :]
            if treated:
                yT_ref[slot, c] = qk_chunk(chunk.T, c).astype(BF16)
            else:
                yT_ref[slot, c] = chunk.astype(BF16).T

    def store_rows(yh, slot):
        for c in range(n_chunks):
            rows = slice(c * CHUNK, (c + 1) * CHUNK)
            ystd_ref[slot, rows, :] = qk_chunk(yh[rows, :].T, c).astype(BF16).T

    def when_kind(kind):
        steps = [t for t, name in enumerate(PROJ_TILES) if name == kind]
        return pl.when(functools.reduce(jnp.logical_or, [j == t for t in steps]))

    for kind, w_ref in (("nsa_q", wa_ref), ("moba_q", wb_ref)):
        @when_kind(kind)
        def _(w_ref=w_ref):
            for hh, yh in head_pairs(w_ref):
                store_T(yh, hh, True)

    @when_kind("moba_v")
    def _():
        for hh, yh in head_pairs(wb_ref):
            store_T(yh, hh, False)

    @when_kind("moba_k")
    def _():
        for hh, yh in head_pairs(wb_ref):
            store_rows(yh, hh)

    for kind, base in (("kv_slc", 0), ("kv_win", NSA_KV_HEADS)):
        @when_kind(kind)
        def _(base=base):
            for hh, yh in head_pairs(wa_ref):
                if hh < NSA_KV_HEADS:
                    store_rows(yh, base + hh)
                else:
                    store_T(yh, base + hh - NSA_KV_HEADS, False)

    @when_kind("kv_cmp")
    def _():
        for hh, yh in head_pairs(wa_ref):
            rows_scr[...] = yh
            flat = [rows_scr[pl.ds(l, tm // CMP_STRIDE, stride=CMP_STRIDE), :] for l in range(CMP_STRIDE)]
            hc_ref[hh] = jnp.concatenate(flat, axis=1).astype(BF16)


def _in_proj(x2, mod3, w_nsa, w_moba, w_gate, gainsT, cosT, sinT, T):
    BT, D = x2.shape
    G = NSA_KV_HEADS
    tm = min(1024, T)
    tn = HEADS_PER_TILE * HEAD_DIM
    tpb = T // tm
    advance = lambda j, steps: sum((j >= t).astype(jnp.int32) for t in steps)
    gain_kind = lambda j: sum((j == t).astype(jnp.int32) * GAIN_KINDS.index(PROJ_GAIN[name])
                              for t, name in enumerate(PROJ_TILES) if name in PROJ_GAIN)
    return pl.pallas_call(
        _inproj_kernel,
        grid=(BT // tm, len(PROJ_TILES)),
        in_specs=[pl.BlockSpec((tm, D), lambda i, j: (i, 0)),
                  pl.BlockSpec((1, 6, D), lambda i, j: (i // tpb, 0, 0)),
                  pl.BlockSpec((D, tn), lambda i, j: (0, jnp.minimum(j, N_NSA_TILES - 1))),
                  pl.BlockSpec((D, tn), lambda i, j: (0, jnp.maximum(j - N_NSA_TILES, 0))),
                  pl.BlockSpec((D, G * GATE_LANES), lambda i, j: (0, 0)),
                  pl.BlockSpec((1, HEAD_DIM, CHUNK), lambda i, j: (gain_kind(j), 0, 0)),
                  pl.BlockSpec((ROPE_HALF, tm), lambda i, j: (0, i % tpb)),
                  pl.BlockSpec((ROPE_HALF, tm), lambda i, j: (0, i % tpb))],
        out_specs=[pl.BlockSpec((HEADS_PER_TILE, tm // CHUNK, HEAD_DIM, CHUNK),
                                lambda i, j: (advance(j, YT_ADVANCE), i, 0, 0)),
                   pl.BlockSpec((HEADS_PER_TILE, tm, HEAD_DIM), lambda i, j: (advance(j, YSTD_ADVANCE), i, 0)),
                   pl.BlockSpec((N_CMP_HEADS, tm // CMP_STRIDE, CMP_STRIDE * HEAD_DIM), lambda i, j: (0, i, 0)),
                   pl.BlockSpec((G, GATE_ROWS, tm), lambda i, j: (0, 0, i))],
        out_shape=[jax.ShapeDtypeStruct((N_T_HEADS, BT // CHUNK, HEAD_DIM, CHUNK), BF16),
                   jax.ShapeDtypeStruct((N_STD_HEADS, BT, HEAD_DIM), BF16),
                   jax.ShapeDtypeStruct((N_CMP_HEADS, BT // CMP_STRIDE, CMP_STRIDE * HEAD_DIM), BF16),
                   jax.ShapeDtypeStruct((G, GATE_ROWS, BT), F32)],
        scratch_shapes=[pltpu.VMEM((tm, D), BF16), pltpu.VMEM((tm, HEAD_DIM), F32)],
        compiler_params=_cparams(("parallel", "arbitrary")),
        name="in_proj",
    )(x2, mod3, w_nsa, w_moba, w_gate, gainsT, cosT, sinT)


def _compress_kernel(h_ref, w1c_ref, pe_ref, w1_ref, w2_ref, gain_ref, cc_ref, sa_ref, sb_ref, o_ref):
    a = pl.program_id(0)
    ncp = h_ref.shape[2]
    z = jnp.dot(h_ref[0, 0], w1c_ref[0], preferred_element_type=F32)
    top = z[:, :HEAD_DIM]
    bot = pltpu.roll(z[:, HEAD_DIM:], ncp - 1, 0)
    pe_term = jnp.dot(pe_ref[0], w1_ref[0], preferred_element_type=F32)[0:1, :]
    pre = top + bot + pe_term
    act = pre * jax.nn.sigmoid(pre)
    out = jnp.dot(act.astype(BF16), w2_ref[0], preferred_element_type=F32)
    live = lax.broadcasted_iota(jnp.int32, out.shape, 0) < ncp - 1
    out = jnp.where(live, out, 0.0)

    @pl.when(a < NSA_KV_HEADS)
    def _():
        ms = jnp.mean(out * out, axis=-1, keepdims=True)
        yn = out * lax.rsqrt(ms + EPS) * gain_ref[...]
        o_ref[0, 0] = _rope(yn, cc_ref[...], sa_ref[...], sb_ref[...]).astype(BF16)

    @pl.when(a >= NSA_KV_HEADS)
    def _():
        o_ref[0, 0] = out.astype(BF16)


def _compress(hc, w1cat, pe8, w1, w2, gain, cc, sa, sb):
    A, B, ncp, K = hc.shape
    G = NSA_KV_HEADS
    return pl.pallas_call(
        _compress_kernel,
        grid=(A, B),
        in_specs=[pl.BlockSpec((1, 1, ncp, K), lambda a, b: (a, b, 0, 0)),
                  pl.BlockSpec((1, K, 2 * HEAD_DIM), lambda a, b: (a // G, 0, 0)),
                  pl.BlockSpec((1, 8, 2 * K), lambda a, b: (a // G, 0, 0)),
                  pl.BlockSpec((1, 2 * K, HEAD_DIM), lambda a, b: (a // G, 0, 0)),
                  pl.BlockSpec((1, HEAD_DIM, HEAD_DIM), lambda a, b: (a // G, 0, 0)),
                  pl.BlockSpec((1, HEAD_DIM), lambda a, b: (0, 0)),
                  pl.BlockSpec((ncp, HEAD_DIM), lambda a, b: (0, 0)),
                  pl.BlockSpec((ncp, HEAD_DIM), lambda a, b: (0, 0)),
                  pl.BlockSpec((ncp, HEAD_DIM), lambda a, b: (0, 0))],
        out_specs=pl.BlockSpec((1, 1, ncp, HEAD_DIM), lambda a, b: (a, b, 0, 0)),
        out_shape=jax.ShapeDtypeStruct((A, B, ncp, HEAD_DIM), BF16),
        compiler_params=_cparams(("arbitrary", "arbitrary")),
        name="compress",
    )(hc, w1cat, pe8, w1, w2, gain, cc, sa, sb)


def _first_step(s, vT, m_scr, acc_scr):
    m = jnp.max(s, axis=0, keepdims=True)
    p = jnp.exp2(s - m)
    acc_scr[...] = jnp.dot(_with_ones(vT), p.astype(BF16), preferred_element_type=F32)
    m_scr[...] = m


def _with_ones(vT):
    return jnp.concatenate([vT, jnp.ones((ONES_ROWS, vT.shape[1]), vT.dtype)], axis=0)


def _normalized(acc):
    return acc[0:HEAD_DIM] / acc[HEAD_DIM:HEAD_DIM + 1]


def _online_step(s, vT, m_scr, acc_scr):
    m_prev = m_scr[...]
    m_new = jnp.maximum(m_prev, jnp.max(s, axis=0, keepdims=True))
    alpha = jnp.exp2(m_prev - m_new)
    p = jnp.exp2(s - m_new)
    acc_scr[...] = alpha * acc_scr[...] + jnp.dot(_with_ones(vT), p.astype(BF16), preferred_element_type=F32)
    m_scr[...] = m_new


def _rank_select(score, k):
    n = score.shape[0]
    rank = jnp.zeros(score.shape, F32)
    for m in range(n):
        sm = score[m:m + 1, :]
        lo = (m // SUBLANES) * SUBLANES
        hi = min(lo + SUBLANES, n)
        parts = []
        if lo > 0:
            parts.append(jnp.where(sm > score[:lo], 1.0, 0.0))
        gt =jnp.where(sm > score[lo:hi], 1.0, 0.0)
        ge = jnp.where(sm >= score[lo:hi], 1.0, 0.0)
        below = lax.broadcasted_iota(jnp.int32, gt.shape, 0) > m - lo
        parts.append(jnp.where(below, ge, gt))
        if hi < n:
            parts.append(jnp.where(sm >= score[hi:], 1.0, 0.0))
        rank = rank + (jnp.concatenate(parts, axis=0) if len(parts) > 1 else parts[0])
    return rank < k


def _pad_rows(a, rows):
    return jnp.concatenate([a, jnp.zeros((rows - a.shape[0], a.shape[1]), a.dtype)], axis=0)


def _masked_scores(k_ref, e_ref, start, size, q_aug):
    rows = pl.ds(pl.multiple_of(start, 128), size)
    k_aug = jnp.concatenate([k_ref[rows, :], e_ref[rows, :]], axis=1)
    return jnp.dot(k_aug, q_aug, preferred_element_type=F32)


def _pipelined_attention(chains, n_steps, n_max, tks, tk, body_pairs):
    per = tks // tk

    def scores(step, buf):
        st = jnp.minimum(step, n_max - 1)
        dead = jnp.where(step < n_steps, 0, 1)
        for k_ref, e_ref, _, qaug_ref, bufs, _, _ in chains:
            bufs[buf][...] = _masked_scores(k_ref, e_ref, st * tks, tks, qaug_ref[dead])

    def consume(step, buf):
        st = jnp.minimum(step, n_max - 1)
        for _, _, vT_ref, _, bufs, m_scr, acc_scr in chains:
            vT = jnp.concatenate([vT_ref[st * per + a] for a in range(per)], axis=1)
            _online_step(bufs[buf][...], vT, m_scr, acc_scr)

    def pairs(first_step, n_pairs_in_body):
        def body(it, carry):
            base = first_step + it * 2 * n_pairs_in_body
            for p in range(n_pairs_in_body):
                scores(base + 2 * p + 1, 1)
                consume(base + 2 * p, 0)
                scores(base + 2 * p + 2, 0)
                consume(base + 2 * p + 1, 1)
            return carry
        return body

    scores(0, 0)
    done = 0
    for n_body in body_pairs[:-1]:
        n_iter = (n_steps - done) // (2 * n_body)
        lax.fori_loop(0, n_iter, pairs(done, n_body), 0)
        done = done + n_iter * 2 * n_body
    lax.fori_loop(0, (n_steps - done + 1) // 2, pairs(done, 1), 0)


def _carry_casts(kernel_fn, n_in, n_out, n_cast):
    def wrapped(*refs):
        ins, cast_in = refs[:n_in], refs[n_in:n_in + n_cast]
        outs = refs[n_in + n_cast:n_in + n_cast + n_out]
        cast_out = refs[n_in + n_cast + n_out:n_in + n_cast + n_out + n_cast]
        for src, dst in zip(cast_in, cast_out):
            dst[...] = src[...].astype(BF16)
        kernel_fn(*ins, *outs, *refs[n_in + n_cast + n_out + n_cast:])
    return wrapped


def _cast_specs(weights, grid):
    n_steps = grid[0] * grid[1] * grid[2]
    specs, shapes = [], []
    for w, rows in weights:
        while w.shape[0] // rows > n_steps:
            rows *= 2
        n_blk = w.shape[0] // rows
        assert w.shape[0] % rows == 0
        index = lambda a, b, c, n_blk=n_blk: (jnp.minimum((a * grid[1] + b) * grid[2] + c, n_blk - 1), 0)
        specs.append(pl.BlockSpec((rows, w.shape[1]), index))
        shapes.append(jax.ShapeDtypeStruct(w.shape, BF16))
    return specs, shapes


def _nsa_body(qT_ref, kc_ref, vcT_ref, ks_ref, vsT_ref, kw_ref, vwT_ref, gT_ref, e_ref,
              gain_ref, o_ref, m_scr, acc_scr, comb_scr, qaug_scr, s0_scr, s1_scr, psum_scr, *, T):
    R, tq = NSA_GROUP, NSA_TQ
    i = pl.program_id(2)
    t0 = i * tq
    ncp = T // CMP_STRIDE
    nsb = T // SLC_BLOCK
    qT = jnp.concatenate([qT_ref[r, c] for r in range(R) for c in range(tq // CHUNK)], axis=1)
    gT = gT_ref[0]

    def per_head(row0):
        return jnp.concatenate([gT[row0 + 3 * r:row0 + 3 * r + 1, :] for r in range(R)], axis=1)

    def lanes_x_heads(a):
        return jnp.concatenate([a] * R, axis=1)

    s = jnp.dot(kc_ref[0, 0], qT, preferred_element_type=F32)
    c_idx = lax.broadcasted_iota(jnp.int32, (ncp, tq), 0)
    t_c = t0 + lax.broadcasted_iota(jnp.int32, (ncp, tq), 1)
    vis = lanes_x_heads((c_idx * CMP_STRIDE + (CMP_BLOCK - 1) <= t_c) & (c_idx < ncp - 1))
    s = jnp.where(vis, s, NEG)
    m = jnp.maximum(jnp.max(s, axis=0, keepdims=True), M_INIT)
    p = jnp.exp2(s - m)
    l = jnp.sum(p, axis=0, keepdims=True)
    p = p / jnp.where(l > 0.0, l, 1.0)
    o_cmp = jnp.dot(vcT_ref[0, 0], p.astype(BF16), preferred_element_type=F32)
    comb_scr[...] = o_cmp * per_head(0)

    psum = p[:, 0:tq]
    for r in range(1, R):
        psum = psum + p[:, r * tq:(r + 1) * tq]
    per = SLC_BLOCK // CMP_STRIDE
    imp_chunks = []
    for c in range(tq // CHUNK):
        psum_scr[c, 0:SUBLANES, :] = jnp.zeros((SUBLANES, CHUNK), F32)
        psum_scr[c, SUBLANES:, :] = psum[:, c * CHUNK:(c + 1) * CHUNK]
        acc = jnp.zeros((nsb, CHUNK), F32)
        for k in range(1 - CMP_BLOCK // CMP_STRIDE, per):
            acc = acc + psum_scr[c, pl.ds(SUBLANES + k, nsb, stride=per), :]
        imp_chunks.append(acc)
    imp = jnp.concatenate(imp_chunks, axis=1)
    n_idx = lax.broadcasted_iota(jnp.int32, (nsb, tq), 0)
    cur = (t0 + lax.broadcasted_iota(jnp.int32, (nsb, tq), 1)) // SLC_BLOCK
    forced = (n_idx == 0) | (n_idx == cur) | (n_idx == cur - 1)
    valid = n_idx <= cur
    score = jnp.where(valid, imp + jnp.where(forced, FORCE_BONUS, 0.0), NEG)
    sel = _rank_select(score, min(SLC_TOPK, nsb)) & valid
    sel_past = sel & (n_idx * SLC_BLOCK < t0)
    bias = _pad_rows(jnp.where(sel_past, 0.0, NEG), HEAD_DIM).astype(BF16)
    bias_own = _pad_rows(jnp.where(sel, 0.0, NEG), HEAD_DIM).astype(BF16)
    for d in range(3):
        qaug_scr[d, 0:HEAD_DIM, :] = qT
    qaug_scr[0, HEAD_DIM:, :] = lanes_x_heads(bias)
    qaug_scr[1, HEAD_DIM:, :] = jnp.full((HEAD_DIM, R * tq), NEG, BF16)
    qaug_scr[2, HEAD_DIM:, :] = lanes_x_heads(bias_own)

    w0 = jnp.maximum(t0 + tq - WIN_SPAN, 0)
    k_w = kw_ref[0, pl.ds(pl.multiple_of(w0, CHUNK), WIN_SPAN), :]
    s_w = jnp.dot(k_w, qT, preferred_element_type=F32)
    dist = (t0 - w0) + (lax.broadcasted_iota(jnp.int32, (WIN_SPAN, tq), 1)
                        - lax.broadcasted_iota(jnp.int32, (WIN_SPAN, tq), 0))
    ok = lanes_x_heads(dist.astype(jnp.uint32) < WINDOW)
    s_w = jnp.where(ok, s_w, NEG)
    m_w = jnp.max(s_w, axis=0, keepdims=True)
    p_w = jnp.exp2(s_w - m_w)
    jw = w0 // CHUNK
    v_w = jnp.concatenate([vwT_ref[0, jw + a] for a in range(WIN_SPAN // CHUNK)], axis=1)
    o_w = jnp.dot(_with_ones(v_w), p_w.astype(BF16), preferred_element_type=F32)
    comb_scr[...] += _normalized(o_w) * per_head(2)

    tri =lax.broadcasted_iota(jnp.int32, (tq, tq), 0) <= lax.broadcasted_iota(jnp.int32, (tq, tq), 1)
    s_d = jnp.where(lanes_x_heads(tri), _masked_scores(ks_ref.at[0], e_ref, t0, tq, qaug_scr[2]), NEG)
    own = tq // CHUNK
    v_d = jnp.concatenate([vsT_ref[0, i * own + a] for a in range(own)], axis=1)
    _first_step(s_d, v_d, m_scr, acc_scr)
    yield [(ks_ref.at[0], e_ref, vsT_ref.at[0], qaug_scr, (s0_scr, s1_scr), m_scr, acc_scr)]
    comb =comb_scr[...] + _normalized(acc_scr[...]) * per_head(1)

    for r in range(R):
        oT = comb[:, r * tq:(r + 1) * tq]
        ms = jnp.mean(oT * oT, axis=0, keepdims=True)
        on = (oT * lax.rsqrt(ms + EPS)).T * gain_ref[0, r:r + 1, :]
        o_ref[:, r * HEAD_DIM:(r + 1) * HEAD_DIM] = on.astype(BF16)


def _moba_body(qT_ref, k_ref, vT_ref, e_ref, gain_ref, o_ref, m_scr, acc_scr, kmean_scr, qaug_scr,
               s_scr, *, T):
    tq, tk = MOBA_TQ, MOBA_TK
    nb = T // MOBA_BLOCK
    nbp = kmean_scr.shape[1]
    i = pl.program_id(2)
    t0 = i * tq

    @pl.when(i == 0)
    def _():
        kmean_scr[...] = jnp.zeros(kmean_scr.shape, F32)
        for a in range(MOBA_HB):
            kb = k_ref[a].astype(F32).reshape(nb, MOBA_BLOCK, HEAD_DIM)
            kmean_scr[a, 0:nb, :] = jnp.mean(kb, axis=1)

    n_idx = lax.broadcasted_iota(jnp.int32, (nbp, tq), 0)
    cur = (t0 + lax.broadcasted_iota(jnp.int32, (nbp, tq), 1)) // MOBA_BLOCK
    past = n_idx < cur
    causal = lax.broadcasted_iota(jnp.int32, (tk, tq), 0) <= lax.broadcasted_iota(jnp.int32, (tk, tq), 1)

    qTs = [jnp.concatenate([qT_ref[a, c] for c in range(tq // CHUNK)], axis=1) for a in range(MOBA_HB)]
    for a in range(MOBA_HB):
        qT = qTs[a]
        gate = jnp.zeros((nbp, tq), F32)
        for part in _split3(kmean_scr[a]):
            gate = gate + jnp.dot(part, qT, preferred_element_type=F32)
        sel = _rank_select(jnp.where(past, gate, NEG), min(MOBA_TOPK, nb)) & past
        for d in range(2):
            qaug_scr[a, d, 0:HEAD_DIM, :] = qT
        qaug_scr[a, 0, HEAD_DIM:, :] = _pad_rows(jnp.where(sel, 0.0, NEG), HEAD_DIM).astype(BF16)
        qaug_scr[a, 1, HEAD_DIM:, :] = jnp.full((HEAD_DIM, tq), NEG, BF16)

    s_own = []
    for a in range(MOBA_HB):
        m_scr[a] = jnp.full(m_scr.shape[1:], M_INIT, F32)
        acc_scr[a] = jnp.zeros(acc_scr.shape[1:], F32)
        k_own = k_ref[a, pl.ds(pl.multiple_of(t0, tk), tk), :]
        s_own.append(jnp.where(causal, jnp.dot(k_own, qTs[a], preferred_element_type=F32), NEG))

    yield [(k_ref.at[a], e_ref, vT_ref.at[a], qaug_scr.at[a], (s_scr.at[a, 0], s_scr.at[a, 1]),
            m_scr.at[a], acc_scr.at[a]) for a in range(MOBA_HB)]

    own = tq // CHUNK
    for a in range(MOBA_HB):
        v_own = jnp.concatenate([vT_ref[a, i * own + c] for c in range(own)], axis=1)
        _online_step(s_own[a], v_own, m_scr.at[a], acc_scr.at[a])
        oT = _normalized(acc_scr[a])
        ms = jnp.mean(oT * oT, axis=0, keepdims=True)
        on = (oT * lax.rsqrt(ms + EPS)).T * gain_ref[a]
        o_ref[:, a * HEAD_DIM:(a + 1) * HEAD_DIM] = on.astype(BF16)


N_NSA_IN, N_NSA_SCRATCH, N_MOBA_IN = 10, 7, 5


def _attention_kernel(*refs, T):
    nsa_in, moba_in = refs[:N_NSA_IN], refs[N_NSA_IN:N_NSA_IN + N_MOBA_IN]
    o_nsa, o_moba = refs[N_NSA_IN + N_MOBA_IN:N_NSA_IN + N_MOBA_IN + 2]
    scratch = refs[N_NSA_IN + N_MOBA_IN + 2:]
    moba = _moba_body(*moba_in, o_moba, *scratch[N_NSA_SCRATCH:], T=T)
    nsa = _nsa_body(*nsa_in, o_nsa, *scratch[:N_NSA_SCRATCH], T=T)
    chains = next(moba) + next(nsa)
    n_steps = pl.program_id(2) * ATTN_TQ // ATTN_TKS
    _pipelined_attention(chains, n_steps, T // ATTN_TKS, ATTN_TKS, CHUNK, ATTN_BODY_PAIRS)
    for rest in (nsa, moba):
        assert next(rest, None) is None


def _attention(yT, ystd, kcvc, vcT, gT, e_slc, e_moba, nsa_gains, moba_gains, casts, B, T):
    G, R, tq = NSA_KV_HEADS, NSA_GROUP, ATTN_TQ
    H, HB = MOBA_HEADS, MOBA_HB
    assert H // HB == G and NSA_TQ == MOBA_TQ == tq and SLC_TKS == MOBA_TKS == ATTN_TKS and tq % ATTN_TKS == 0
    nq = T // tq
    ncp = T // CMP_STRIDE
    nb = T // MOBA_BLOCK
    nbp = max(16, nb)
    grid = (B, G, nq)
    cast_specs, cast_shapes = _cast_specs(casts, grid)
    kern = _carry_casts(functools.partial(_attention_kernel, T=T), N_NSA_IN + N_MOBA_IN, 2, len(casts))
    keys = lambda head0: pl.BlockSpec((1, T, HEAD_DIM), lambda b, g, i: (head0 + g, b, 0))
    values = lambda head0: pl.BlockSpec((1, T // CHUNK, HEAD_DIM, CHUNK), lambda b, g, i: (head0 + g, b, 0, 0))
    onehot = pl.BlockSpec((T, HEAD_DIM), lambda b, g, i: (0, 0))
    nsa_specs = [pl.BlockSpec((R, tq // CHUNK, HEAD_DIM, CHUNK), lambda b, g, i: (T_NSA_Q // R + g, b * nq + i, 0, 0)),
                 pl.BlockSpec((1, 1, ncp, HEAD_DIM), lambda b, g, i: (g, b, 0, 0)),
                 pl.BlockSpec((1, 1, HEAD_DIM, ncp), lambda b, g, i: (g, b, 0, 0)),
                 keys(S_KSLC), values(T_VSLC), keys(S_KWIN), values(T_VWIN),
                 pl.BlockSpec((1, GATE_ROWS, tq), lambda b, g, i: (g, 0, b * nq + i)),
                 onehot,
                 pl.BlockSpec((1, R, HEAD_DIM), lambda b, g, i: (g, 0, 0))]
    moba_specs = [pl.BlockSpec((HB, tq // CHUNK, HEAD_DIM, CHUNK), lambda b, h, i: (T_MOBA_Q // HB + h, b * nq + i, 0, 0)),
                  pl.BlockSpec((HB, T, HEAD_DIM), lambda b, h, i: (S_MOBA_K // HB + h, b, 0)),
                  pl.BlockSpec((HB, T // CHUNK, HEAD_DIM, CHUNK), lambda b, h, i: (T_MOBA_V // HB + h, b, 0, 0)),
                  onehot,
                  pl.BlockSpec((HB, 1, HEAD_DIM), lambda b, h, i: (h, 0, 0))]
    nsa_scratch = [pltpu.VMEM((1, R * tq), F32),
                   pltpu.VMEM((HEAD_DIM + ONES_ROWS, R * tq), F32), pltpu.VMEM((HEAD_DIM, R * tq), F32),
                   pltpu.VMEM((3, 2 * HEAD_DIM, R * tq), BF16),
                   pltpu.VMEM((ATTN_TKS, R * tq), F32), pltpu.VMEM((ATTN_TKS, R * tq), F32),
                   pltpu.VMEM((tq // CHUNK, SUBLANES + ncp, CHUNK), F32)]
    moba_scratch = [pltpu.VMEM((HB, 1, tq), F32),
                    pltpu.VMEM((HB, HEAD_DIM + ONES_ROWS, tq), F32), pltpu.VMEM((HB, nbp, HEAD_DIM), F32),
                    pltpu.VMEM((HB, 2, 2 * HEAD_DIM, tq), BF16),
                    pltpu.VMEM((HB, 2, ATTN_TKS, tq), F32)]
    assert len(nsa_specs) == N_NSA_IN and len(moba_specs) == N_MOBA_IN and len(nsa_scratch) == N_NSA_SCRATCH
    return pl.pallas_call(
        kern,
        grid=grid,
        in_specs=nsa_specs + moba_specs + cast_specs,
        out_specs=[pl.BlockSpec((tq, R * HEAD_DIM), lambda b, g, i: (b * nq + i, g)),
                   pl.BlockSpec((tq, HB * HEAD_DIM), lambda b, h, i: (b * nq + i, h))] + cast_specs,
        out_shape=[jax.ShapeDtypeStruct((B * T, NSA_HEADS * HEAD_DIM), BF16),
                   jax.ShapeDtypeStruct((B * T, H * HEAD_DIM), BF16)] + cast_shapes,
        scratch_shapes=nsa_scratch + moba_scratch,
        compiler_params=_cparams(("parallel", "parallel", "arbitrary")),
        name="attention",
    )(yT, kcvc, vcT, ystd, yT, ystd, yT, gT, e_slc, nsa_gains,
      yT, ystd, yT, e_moba, moba_gains, *[w for w, _ in casts])


def _outproj_kernel(on_ref, om_ref, w_ref, x_ref, mod_ref, o_ref):
    half = on_ref.shape[1]
    acc = jnp.dot(on_ref[...], w_ref[0:half, :], preferred_element_type=F32)
    acc = acc + jnp.dot(om_ref[...], w_ref[half:, :], preferred_element_type=F32)
    o_ref[...] = x_ref[...] + mod_ref[0, 2:3, :] * acc


def _out_proj(o_nsa, o_moba, w_out, x2, mod3, T):
    BT, D = x2.shape
    tm = min(512, T)
    tpb = T // tm
    half = o_nsa.shape[1]
    return pl.pallas_call(
        _outproj_kernel,
        grid=(BT // tm,),
        in_specs=[pl.BlockSpec((tm, half), lambda i: (i, 0)),
                  pl.BlockSpec((tm, half), lambda i: (i, 0)),
                  pl.BlockSpec((D, D), lambda i: (0, 0)),
                  pl.BlockSpec((tm, D), lambda i: (i, 0)),
                  pl.BlockSpec((1, 6, D), lambda i: (i // tpb, 0, 0))],
        out_specs=pl.BlockSpec((tm, D), lambda i: (i, 0)),
        out_shape=jax.ShapeDtypeStruct((BT, D), F32),
        compiler_params=_cparams(("parallel",)),
        name="out_proj",
    )(o_nsa, o_moba, w_out, x2, mod3)


def _ffn_kernel(x_ref, mod_ref, wg_ref, wu_ref, wo_ref, o_ref, h_scr):
    j = pl.program_id(1)

    @pl.when(j == 0)
    def _():
        x = x_ref[...]
        ms = jnp.mean(x * x, axis=-1, keepdims=True)
        h = x * lax.rsqrt(ms + EPS) * (1.0 + mod_ref[0, 4:5, :]) + mod_ref[0, 3:4, :]
        h_scr[...] = h.astype(BF16)
        o_ref[...] = jnp.zeros(o_ref.shape, F32)

    hb = h_scr[...]
    half = wg_ref.shape[1] // 2
    down = None
    for c in range(2):
        cols = slice(c * half, (c + 1) * half)
        gate = jnp.dot(hb, wg_ref[:, cols], preferred_element_type=F32)
        up = jnp.dot(hb, wu_ref[:, cols], preferred_element_type=F32)
        act = (gate * jax.nn.sigmoid(gate) * up).astype(BF16)
        part = jnp.dot(act, wo_ref[cols, :], preferred_element_type=F32)
        down = part if down is None else down + part
    o_ref[...] += down

    @pl.when(j == pl.num_programs(1) - 1)
    def _():
        o_ref[...] = x_ref[...] + mod_ref[0, 5:6, :] * o_ref[...]


def _ffn(x1, mod3, w_in, w_out, T):
    BT, D = x1.shape
    Fh = w_out.shape[0]
    tm = min(512, T)
    tf = 512
    tpb = T // tm
    nf = Fh // tf
    return pl.pallas_call(
        _ffn_kernel,
        grid=(BT // tm, nf),
        in_specs=[pl.BlockSpec((tm, D), lambda i, j: (i, 0)),
                  pl.BlockSpec((1, 6, D), lambda i, j: (i // tpb, 0, 0)),
                  pl.BlockSpec((D, tf), lambda i, j: (0, j)),
                  pl.BlockSpec((D, tf), lambda i, j: (0, nf + j)),
                  pl.BlockSpec((tf, D), lambda i, j: (j, 0))],
        out_specs=pl.BlockSpec((tm, D), lambda i, j: (i, 0)),
        out_shape=jax.ShapeDtypeStruct((BT, D), F32),
        scratch_shapes=[pltpu.VMEM((tm, D), BF16)],
        compiler_params=_cparams(("parallel", "arbitrary")),
        name="ffn",
    )(x1, mod3, w_in, w_in, w_out)


def _rope_cos_sin(pos):
    inv = ROPE_THETA ** (-jnp.arange(0, ROPE_DIMS, 2, dtype=F32) / ROPE_DIMS)
    ang = pos.astype(F32)[:, None] * inv[None, :]
    return jnp.cos(ang), jnp.sin(ang)


def _rope_tables(T):
    c_hi, s_hi = _rope_cos_sin(jnp.arange(0, T, ROPE_SPLIT))
    c_lo, s_lo = _rope_cos_sin(jnp.arange(ROPE_SPLIT))
    cos = (c_hi[:, None] * c_lo[None] - s_hi[:, None] * s_lo[None]).reshape(T, ROPE_HALF)
    sin = (s_hi[:, None] * c_lo[None] + c_hi[:, None] * s_lo[None]).reshape(T, ROPE_HALF)
    cos_c, sin_c = _rope_cos_sin(jnp.arange(CMP_BLOCK - 1, T, CMP_STRIDE))
    n, rest = cos_c.shape[0], HEAD_DIM - ROPE_DIMS
    pad = lambda t: jnp.pad(t, ((0, 1), (0, 0)))
    cc = pad(jnp.concatenate([cos_c, cos_c, jnp.ones((n, rest), F32)], axis=1))
    sa = pad(jnp.concatenate([-sin_c, jnp.zeros((n, HEAD_DIM - ROPE_HALF), F32)], axis=1))
    sb = pad(jnp.concatenate([jnp.zeros((n, ROPE_HALF), F32), sin_c, jnp.zeros((n, rest), F32)], axis=1))
    return cc, sa, sb, cos.T, sin.T


def _block_onehot(T, block):
    return (jnp.arange(T)[:, None] // block == jnp.arange(HEAD_DIM)[None, :]).astype(BF16)


def _layer(x2, c, B, T, w_ada, b_ada, w_in, nsa_q_norm, nsa_k_norm, moba_q_norm, moba_k_norm,
           cmp_pe_k, cmp_w1_k, cmp_w2_k, cmp_pe_v, cmp_w1_v, cmp_w2_v, out_norm, w_out,
           w_ffn_in, w_ffn_out):
    D = x2.shape[1]
    G = NSA_KV_HEADS
    scale = HEAD_DIM ** -0.5 * LOG2E
    assert T % MOBA_BLOCK == 0 and T % SLC_TKS == 0 and T % MOBA_TKS == 0 and T % NSA_TQ == 0 and T >= WIN_SPAN
    assert T // SLC_BLOCK <= HEAD_DIM and T // MOBA_BLOCK <= HEAD_DIM

    nsa_w = NSA_HEADS * HEAD_DIM + 6 * G * HEAD_DIM
    gw = NSA_HEADS * 3
    mod3 = _adaln(c, w_ada, b_ada)
    w_nsa = w_in.astype(BF16)
    w_moba = w_nsa[:, nsa_w + gw:]

    wg = w_in[:, nsa_w:nsa_w + gw].reshape(D, G, NSA_GROUP * 3)
    w_gate = jnp.pad(wg, ((0, 0), (0, 0), (0, GATE_LANES - NSA_GROUP * 3))).reshape(D, G * GATE_LANES)
    w_gate = w_gate.astype(BF16)
    across = lambda g_: jnp.broadcast_to(g_[:, None], (HEAD_DIM, CHUNK))
    gainsT = jnp.stack([across(nsa_q_norm * scale), across(moba_q_norm * scale),
                        across(nsa_k_norm[1]), across(nsa_k_norm[2]), across(moba_k_norm)])
    cc, sa, sb, cosT, sinT = _rope_tables(T)

    yT, ystd, hc, gT = _in_proj(x2, mod3, w_nsa, w_moba, w_gate, gainsT, cosT, sinT, T)

    ncp = T // CMP_STRIDE
    half = CMP_STRIDE * HEAD_DIM
    w1 = jnp.stack([cmp_w1_k, cmp_w1_v]).astype(BF16)
    w1cat = jnp.concatenate([w1[:, :half], w1[:, half:]], axis=2)
    pe8 = jnp.broadcast_to(jnp.stack([cmp_pe_k, cmp_pe_v]).reshape(2, 1, 2 * half), (2, 8, 2 * half))
    w2 = jnp.stack([cmp_w2_k, cmp_w2_v]).astype(BF16)
    kcvc = _compress(hc.reshape(2 * G, B, ncp, half), w1cat, pe8.astype(BF16), w1, w2,
                     nsa_k_norm[0].reshape(1, HEAD_DIM), cc, sa, sb)
    vcT = kcvc[G:].transpose(0, 1, 3, 2)

    on = out_norm.reshape(N_HEADS, HEAD_DIM)
    o_nsa, o_moba, w_ffn_out_b, w_out_b, w_ffn_in_b = _attention(
        yT, ystd, kcvc, vcT, gT, _block_onehot(T, SLC_BLOCK), _block_onehot(T, MOBA_BLOCK),
        on[:NSA_HEADS].reshape(G, NSA_GROUP, HEAD_DIM), on[NSA_HEADS:].reshape(MOBA_HEADS, 1, HEAD_DIM),
        [(w_ffn_out, 128), (w_out, 128), (w_ffn_in, 32)], B, T)

    x1 = _out_proj(o_nsa, o_moba, w_out_b, x2, mod3, T)
    return _ffn(x1, mod3, w_ffn_in_b, w_ffn_out_b, T)


def kernel(x, c, w_ada, b_ada, w_in, nsa_q_norm, nsa_k_norm, moba_q_norm, moba_k_norm, cmp_pe_k, cmp_w1_k, cmp_w2_k, cmp_pe_v, cmp_w1_v, cmp_w2_v, out_norm, w_out, w_ffn_in, w_ffn_out):
    B, T, D = x.shape
    x2 = x.reshape(B * T, D)
    for l in range(w_ada.shape[0]):
        x2 = _layer(x2, c, B, T, w_ada[l], b_ada[l], w_in[l], nsa_q_norm[l], nsa_k_norm[l],
                    moba_q_norm[l], moba_k_norm[l], cmp_pe_k[l], cmp_w1_k[l], cmp_w2_k[l],
                    cmp_pe_v[l], cmp_w1_v[l], cmp_w2_v[l], out_norm[l], w_out[l],
                    w_ffn_in[l], w_ffn_out[l])
    return x2.reshape(B, T, D)
```

```python
import functools

import jax
import jax.numpy as jnp
from jax import lax
from jax.experimental import pallas as pl
from jax.experimental.pallas import tpu as pltpu

F32 = jnp.float32
BF16 = jnp.bfloat16

HEAD_DIM = 128
SUBLANES = 8
NSA_HEADS = 8
NSA_KV_HEADS = 2
NSA_GROUP = NSA_HEADS // NSA_KV_HEADS
MOBA_HEADS = 8
N_HEADS = NSA_HEADS + MOBA_HEADS
CMP_BLOCK = 32
CMP_STRIDE = 16
SLC_BLOCK = 64
SLC_TOPK = 16
WINDOW = 512
FORCE_BONUS = 1e4
MOBA_BLOCK = 256
MOBA_TOPK = 3
ROPE_THETA = 500000.0
ROPE_DIMS = HEAD_DIM // 4
ROPE_HALF = ROPE_DIMS // 2
ROPE_SPLIT = 64
EPS = 1e-6
LOG2E = 1.4426950408889634
ONES_ROWS = 16
NEG = -1e30
M_INIT = -1e29

V7X_VMEM_BYTES = 64 * 1024 * 1024
VMEM_LIMIT = V7X_VMEM_BYTES - 8 * 1024 * 1024

HEADS_PER_TILE = 4
PROJ_TILES = ("nsa_q", "nsa_q", "kv_cmp", "kv_slc", "kv_win",
              "moba_q", "moba_q", "moba_k", "moba_k", "moba_v", "moba_v")
N_NSA_TILES = 5
GAIN_KINDS = ("nsa_q", "moba_q", "k_slc", "k_win", "moba_k")
PROJ_GAIN = {"nsa_q": "nsa_q", "moba_q": "moba_q", "kv_slc": "k_slc", "kv_win": "k_win", "moba_k": "moba_k"}
YT_ADVANCE = (1, 3, 5, 6, 9, 10)
YSTD_ADVANCE = (7, 8)
T_NSA_Q, T_VSLC, T_VWIN, T_MOBA_Q, T_MOBA_V = 0, 8, 10, 12, 20
S_KSLC, S_KWIN, S_MOBA_K = 0, 2, 4
N_T_HEADS, N_STD_HEADS, N_CMP_HEADS = 28, 12, 4
CHUNK = 128
GATE_LANES = 128
GATE_ROWS = 16

ATTN_TQ = 256
ATTN_TKS = 256
ATTN_BODY_PAIRS = (4, 2, 1)
NSA_TQ = ATTN_TQ
SLC_TKS = ATTN_TKS
WIN_SPAN = WINDOW + NSA_TQ
MOBA_TQ = ATTN_TQ
MOBA_TK = MOBA_BLOCK
MOBA_TKS = ATTN_TKS
MOBA_HB = 4


def _cparams(sem):
    return pltpu.CompilerParams(dimension_semantics=sem, vmem_limit_bytes=VMEM_LIMIT)


def _split3(a):
    hi = a.astype(BF16)
    r1 = a - hi.astype(F32)
    mid = r1.astype(BF16)
    lo = (r1 - mid.astype(F32)).astype(BF16)
    return hi, mid, lo


def _adaln_kernel(c_ref, w_ref, b_ref, o_ref):
    cv = c_ref[...]
    s = cv * jax.nn.sigmoid(cv)
    w = w_ref[...].astype(BF16)
    acc = b_ref[...] + jnp.zeros(o_ref.shape, F32)
    for part in _split3(s)[:2]:
        acc = acc + jnp.dot(part, w, preferred_element_type=F32)
    o_ref[...] = acc


def _adaln(c, w_ada, b_ada):
    B, D = c.shape
    N = w_ada.shape[1]
    tn = 1024
    c8 = jnp.zeros((8, D), F32).at[:B].set(c)
    out = pl.pallas_call(
        _adaln_kernel,
        grid=(N // tn,),
        in_specs=[pl.BlockSpec((8, D), lambda j: (0, 0)),
                  pl.BlockSpec((D, tn), lambda j: (0, j)),
                  pl.BlockSpec((1, tn), lambda j: (0, j))],
        out_specs=pl.BlockSpec((8, tn), lambda j: (0, j)),
        out_shape=jax.ShapeDtypeStruct((8, N), F32),
        compiler_params=_cparams(("arbitrary",)),
        name="adaln",
    )(c8, w_ada, b_ada.reshape(1, N))
    return out[:B].reshape(B, 6, D)


def _rope(y, cc, sa, sb):
    return (y * cc + pltpu.roll(y, HEAD_DIM - ROPE_HALF, 1) * sa + pltpu.roll(y, ROPE_HALF, 1) * sb)


def _inproj_kernel(x_ref, mod_ref, wa_ref, wb_ref, wg_ref, gainT_ref, cosT_ref, sinT_ref,
                   yT_ref, ystd_ref, hc_ref, gT_ref, h_scr, rows_scr):
    j = pl.program_id(1)
    tm = x_ref.shape[0]
    n_chunks = tm // CHUNK
    pair = 2 * HEAD_DIM

    @pl.when(j == 0)
    def _():
        x = x_ref[...]
        ms = jnp.mean(x * x, axis=-1, keepdims=True)
        h = x * lax.rsqrt(ms + EPS) * (1.0 + mod_ref[0, 1:2, :]) + mod_ref[0, 0:1, :]
        hb = h.astype(BF16)
        h_scr[...] = hb
        g = jax.nn.sigmoid(jnp.dot(hb, wg_ref[...], preferred_element_type=F32))
        for grp in range(NSA_KV_HEADS):
            for c in range(n_chunks):
                blk = g[c * CHUNK:(c + 1) * CHUNK, grp * GATE_LANES:(grp + 1) * GATE_LANES].T
                gT_ref[grp, :, c * CHUNK:(c + 1) * CHUNK] = blk[0:GATE_ROWS, :]

    def head_pairs(w_ref):
        for half in range(HEADS_PER_TILE // 2):
            acc = jnp.dot(h_scr[...], w_ref[:, half * pair:(half + 1) * pair], preferred_element_type=F32)
            for h2 in range(2):
                yield 2 * half + h2, acc[:, h2 * HEAD_DIM:(h2 + 1) * HEAD_DIM]

    def qk_chunk(t, c):
        ms = jnp.mean(t * t, axis=0, keepdims=True)
        tn = t * lax.rsqrt(ms + EPS) * gainT_ref[0]
        cs = cosT_ref[:, c * CHUNK:(c + 1) * CHUNK]
        sn = sinT_ref[:, c * CHUNK:(c + 1) * CHUNK]
        a, b = tn[0:ROPE_HALF], tn[ROPE_HALF:ROPE_DIMS]
        return jnp.concatenate([a * cs - b * sn, b * cs + a * sn, tn[ROPE_DIMS:]], axis=0)

    def store_T(yh, slot, treated):
        for c in range(n_chunks):
            chunk = yh[c * CHUNK:(c + 1) * CHUNK, :]
            if treated:
                yT_ref[slot, c] = qk_chunk(chunk.T, c).astype(BF16)
            else:
                yT_ref[slot, c] = chunk.astype(BF16).T

    def store_rows(yh, slot):
        for c in range(n_chunks):
            rows = slice(c * CHUNK, (c + 1) * CHUNK)
            ystd_ref[slot, rows, :] = qk_chunk(yh[rows, :].T, c).astype(BF16).T

    def when_kind(kind):
        steps = [t for t, name in enumerate(PROJ_TILES) if name == kind]
        return pl.when(functools.reduce(jnp.logical_or, [j == t for t in steps]))

    for kind, w_ref in (("nsa_q", wa_ref), ("moba_q", wb_ref)):
        @when_kind(kind)
        def _(w_ref=w_ref):
            for hh, yh in head_pairs(w_ref):
                store_T(yh, hh, True)

    @when_kind("moba_v")
    def _():
        for hh, yh in head_pairs(wb_ref):
            store_T(yh, hh, False)

    @when_kind("moba_k")
    def _():
        for hh, yh in head_pairs(wb_ref):
            store_rows(yh, hh)

    for kind, base in (("kv_slc", 0), ("kv_win", NSA_KV_HEADS)):
        @when_kind(kind)
        def _(base=base):
            for hh, yh in head_pairs(wa_ref):
                if hh < NSA_KV_HEADS:
                    store_rows(yh, base + hh)
                else:
                    store_T(yh, base + hh - NSA_KV_HEADS, False)

    @when_kind("kv_cmp")
    def _():
        for hh, yh in head_pairs(wa_ref):
            rows_scr[...] = yh
            flat = [rows_scr[pl.ds(l, tm // CMP_STRIDE, stride=CMP_STRIDE), :] for l in range(CMP_STRIDE)]
            hc_ref[hh] = jnp.concatenate(flat, axis=1).astype(BF16)


def _in_proj(x2, mod3, w_nsa, w_moba, w_gate, gainsT, cosT, sinT, T):
    BT, D = x2.shape
    G = NSA_KV_HEADS
    tm = min(1024, T)
    tn = HEADS_PER_TILE * HEAD_DIM
    tpb = T // tm
    advance = lambda j, steps: sum((j >= t).astype(jnp.int32) for t in steps)
    gain_kind = lambda j: sum((j == t).astype(jnp.int32) * GAIN_KINDS.index(PROJ_GAIN[name])
                              for t, name in enumerate(PROJ_TILES) if name in PROJ_GAIN)
    return pl.pallas_call(
        _inproj_kernel,
        grid=(BT // tm, len(PROJ_TILES)),
        in_specs=[pl.BlockSpec((tm, D), lambda i, j: (i, 0)),
                  pl.BlockSpec((1, 6, D), lambda i, j: (i // tpb, 0, 0)),
                  pl.BlockSpec((D, tn), lambda i, j: (0, jnp.minimum(j, N_NSA_TILES - 1))),
                  pl.BlockSpec((D, tn), lambda i, j: (0, jnp.maximum(j - N_NSA_TILES, 0))),
                  pl.BlockSpec((D, G * GATE_LANES), lambda i, j: (0, 0)),
                  pl.BlockSpec((1, HEAD_DIM, CHUNK), lambda i, j: (gain_kind(j), 0, 0)),
                  pl.BlockSpec((ROPE_HALF, tm), lambda i, j: (0, i % tpb)),
                  pl.BlockSpec((ROPE_HALF, tm), lambda i, j: (0, i % tpb))],
        out_specs=[pl.BlockSpec((HEADS_PER_TILE, tm // CHUNK, HEAD_DIM, CHUNK),
                                lambda i, j: (advance(j, YT_ADVANCE), i, 0, 0)),
                   pl.BlockSpec((HEADS_PER_TILE, tm, HEAD_DIM), lambda i, j: (advance(j, YSTD_ADVANCE), i, 0)),
                   pl.BlockSpec((N_CMP_HEADS, tm // CMP_STRIDE, CMP_STRIDE * HEAD_DIM), lambda i, j: (0, i, 0)),
                   pl.BlockSpec((G, GATE_ROWS, tm), lambda i, j: (0, 0, i))],
        out_shape=[jax.ShapeDtypeStruct((N_T_HEADS, BT // CHUNK, HEAD_DIM, CHUNK), BF16),
                   jax.ShapeDtypeStruct((N_STD_HEADS, BT, HEAD_DIM), BF16),
                   jax.ShapeDtypeStruct((N_CMP_HEADS, BT // CMP_STRIDE, CMP_STRIDE * HEAD_DIM), BF16),
                   jax.ShapeDtypeStruct((G, GATE_ROWS, BT), F32)],
        scratch_shapes=[pltpu.VMEM((tm, D), BF16), pltpu.VMEM((tm, HEAD_DIM), F32)],
        compiler_params=_cparams(("parallel", "arbitrary")),
        name="in_proj",
    )(x2, mod3, w_nsa, w_moba, w_gate, gainsT, cosT, sinT)


def _compress_kernel(h_ref, w1c_ref, pe_ref, w1_ref, w2_ref, gain_ref, cc_ref, sa_ref, sb_ref, o_ref):
    a = pl.program_id(0)
    ncp = h_ref.shape[2]
    z = jnp.dot(h_ref[0, 0], w1c_ref[0], preferred_element_type=F32)
    top = z[:, :HEAD_DIM]
    bot = pltpu.roll(z[:, HEAD_DIM:], ncp - 1, 0)
    pe_term = jnp.dot(pe_ref[0], w1_ref[0], preferred_element_type=F32)[0:1, :]
    pre = top + bot + pe_term
    act = pre * jax.nn.sigmoid(pre)
    out = jnp.dot(act.astype(BF16), w2_ref[0], preferred_element_type=F32)
    live = lax.broadcasted_iota(jnp.int32, out.shape, 0) < ncp - 1
    out = jnp.where(live, out, 0.0)

    @pl.when(a < NSA_KV_HEADS)
    def _():
        ms = jnp.mean(out * out, axis=-1, keepdims=True)
        yn = out * lax.rsqrt(ms + EPS) * gain_ref[...]
        o_ref[0, 0] = _rope(yn, cc_ref[...], sa_ref[...], sb_ref[...]).astype(BF16)

    @pl.when(a >= NSA_KV_HEADS)
    def _():
        o_ref[0, 0] = out.astype(BF16)


def _compress(hc, w1cat, pe8, w1, w2, gain, cc, sa, sb):
    A, B, ncp, K = hc.shape
    G = NSA_KV_HEADS
    return pl.pallas_call(
        _compress_kernel,
        grid=(A, B),
        in_specs=[pl.BlockSpec((1, 1, ncp, K), lambda a, b: (a, b, 0, 0)),
                  pl.BlockSpec((1, K, 2 * HEAD_DIM), lambda a, b: (a // G, 0, 0)),
                  pl.BlockSpec((1, 8, 2 * K), lambda a, b: (a // G, 0, 0)),
                  pl.BlockSpec((1, 2 * K, HEAD_DIM), lambda a, b: (a // G, 0, 0)),
                  pl.BlockSpec((1, HEAD_DIM, HEAD_DIM), lambda a, b: (a // G, 0, 0)),
                  pl.BlockSpec((1, HEAD_DIM), lambda a, b: (0, 0)),
                  pl.BlockSpec((ncp, HEAD_DIM), lambda a, b: (0, 0)),
                  pl.BlockSpec((ncp, HEAD_DIM), lambda a, b: (0, 0)),
                  pl.BlockSpec((ncp, HEAD_DIM), lambda a, b: (0, 0))],
        out_specs=pl.BlockSpec((1, 1, ncp, HEAD_DIM), lambda a, b: (a, b, 0, 0)),
        out_shape=jax.ShapeDtypeStruct((A, B, ncp, HEAD_DIM), BF16),
        compiler_params=_cparams(("arbitrary", "arbitrary")),
        name="compress",
    )(hc, w1cat, pe8, w1, w2, gain, cc, sa, sb)


def _first_step(s, vT, m_scr, acc_scr):
    m = jnp.max(s, axis=0, keepdims=True)
    p = jnp.exp2(s - m)
    acc_scr[...] = jnp.dot(_with_ones(vT), p.astype(BF16), preferred_element_type=F32)
    m_scr[...] = m


def _with_ones(vT):
    return jnp.concatenate([vT, jnp.ones((ONES_ROWS, vT.shape[1]), vT.dtype)], axis=0)


def _normalized(acc):
    return acc[0:HEAD_DIM] / acc[HEAD_DIM:HEAD_DIM + 1]


def _online_step(s, vT, m_scr, acc_scr):
    m_prev = m_scr[...]
    m_new = jnp.maximum(m_prev, jnp.max(s, axis=0, keepdims=True))
    alpha = jnp.exp2(m_prev - m_new)
    p = jnp.exp2(s - m_new)
    acc_scr[...] = alpha * acc_scr[...] + jnp.dot(_with_ones(vT), p.astype(BF16), preferred_element_type=F32)
    m_scr[...] = m_new


def _rank_select(score, k):
    n = score.shape[0]
    rank = jnp.zeros(score.shape, F32)
    for m in range(n):
        sm = score[m:m + 1, :]
        lo = (m // SUBLANES) * SUBLANES
        hi = min(lo + SUBLANES, n)
        parts = []
        if lo > 0:
            parts.append(jnp.where(sm > score[:lo], 1.0, 0.0))
        gt = jnp.where(sm > score[lo:hi], 1.0, 0.0)
        ge = jnp.where(sm >= score[lo:hi], 1.0, 0.0)
        below = lax.broadcasted_iota(jnp.int32, gt.shape, 0) > m - lo
        parts.append(jnp.where(below, ge, gt))
        if hi < n:
            parts.append(jnp.where(sm >= score[hi:], 1.0, 0.0))
        rank = rank + (jnp.concatenate(parts, axis=0) if len(parts) > 1 else parts[0])
    return rank < k


def _pad_rows(a, rows):
    return jnp.concatenate([a, jnp.zeros((rows - a.shape[0], a.shape[1]), a.dtype)], axis=0)


def _masked_scores(k_ref, e_ref, start, size, q_aug):
    rows = pl.ds(pl.multiple_of(start, 128), size)
    k_aug = jnp.concatenate([k_ref[rows, :], e_ref[rows, :]], axis=1)
    return jnp.dot(k_aug, q_aug, preferred_element_type=F32)


def _pipelined_attention(chains, n_steps, n_max, tks, tk, body_pairs):
    per = tks // tk

    def scores(step, buf):
        st = jnp.minimum(step, n_max - 1)
        dead = jnp.where(step < n_steps, 0, 1)
        for k_ref, e_ref, _, qaug_ref, bufs, _, _ in chains:
            bufs[buf][...] = _masked_scores(k_ref, e_ref, st * tks, tks, qaug_ref[dead])

    def consume(step, buf):
        st = jnp.minimum(step, n_max - 1)
        for _, _, vT_ref, _, bufs, m_scr, acc_scr in chains:
            vT = jnp.concatenate([vT_ref[st * per + a] for a in range(per)], axis=1)
            _online_step(bufs[buf][...], vT, m_scr, acc_scr)

    def pairs(first_step, n_pairs_in_body):
        def body(it, carry):
            base = first_step + it * 2 * n_pairs_in_body
            for p in range(n_pairs_in_body):
                scores(base + 2 * p + 1, 1)
                consume(base + 2 * p, 0)
                scores(base + 2 * p + 2, 0)
                consume(base + 2 * p + 1, 1)
            return carry
        return body

    scores(0, 0)
    done = 0
    for n_body in body_pairs[:-1]:
        n_iter = (n_steps - done) // (2 * n_body)
        lax.fori_loop(0, n_iter, pairs(done, n_body), 0)
        done = done + n_iter * 2 * n_body
    lax.fori_loop(0, (n_steps - done + 1) // 2, pairs(done, 1), 0)


def _carry_casts(kernel_fn, n_in, n_out, n_cast):
    def wrapped(*refs):
        ins, cast_in = refs[:n_in], refs[n_in:n_in + n_cast]
        outs = refs[n_in + n_cast:n_in + n_cast + n_out]
        cast_out = refs[n_in + n_cast + n_out:n_in + n_cast + n_out + n_cast]
        for src, dst in zip(cast_in, cast_out):
            dst[...] = src[...].astype(BF16)
        kernel_fn(*ins, *outs, *refs[n_in + n_cast + n_out + n_cast:])
    return wrapped


def _cast_specs(weights, grid):
    n_steps = grid[0] * grid[1] * grid[2]
    specs, shapes = [], []
    for w, rows in weights:
        while w.shape[0] // rows > n_steps:
            rows *= 2
        n_blk = w.shape[0] // rows
        assert w.shape[0] % rows == 0
        index = lambda a, b, c, n_blk=n_blk: (jnp.minimum((a * grid[1] + b) * grid[2] + c, n_blk - 1), 0)
        specs.append(pl.BlockSpec((rows, w.shape[1]), index))
        shapes.append(jax.ShapeDtypeStruct(w.shape, BF16))
    return specs, shapes


def _nsa_body(qT_ref, kc_ref, vcT_ref, ks_ref, vsT_ref, kw_ref, vwT_ref, gT_ref, e_ref,
              gain_ref, o_ref, m_scr, acc_scr, comb_scr, qaug_scr, s0_scr, s1_scr, psum_scr, *, T):
    R, tq = NSA_GROUP, NSA_TQ
    i = pl.program_id(2)
    t0 = i * tq
    ncp = T // CMP_STRIDE
    nsb = T // SLC_BLOCK
    qT = jnp.concatenate([qT_ref[r, c] for r in range(R) for c in range(tq // CHUNK)], axis=1)
    gT = gT_ref[0]

    def per_head(row0):
        return jnp.concatenate([gT[row0 + 3 * r:row0 + 3 * r + 1, :] for r in range(R)], axis=1)

    def lanes_x_heads(a):
        return jnp.concatenate([a] * R, axis=1)

    s = jnp.dot(kc_ref[0, 0], qT, preferred_element_type=F32)
    c_idx = lax.broadcasted_iota(jnp.int32, (ncp, tq), 0)
    t_c = t0 + lax.broadcasted_iota(jnp.int32, (ncp, tq), 1)
    vis = lanes_x_heads((c_idx * CMP_STRIDE + (CMP_BLOCK - 1) <= t_c) & (c_idx < ncp - 1))
    s = jnp.where(vis, s, NEG)
    m = jnp.maximum(jnp.max(s, axis=0, keepdims=True), M_INIT)
    p = jnp.exp2(s - m)
    l = jnp.sum(p, axis=0, keepdims=True)
    p = p / jnp.where(l > 0.0, l, 1.0)
    o_cmp = jnp.dot(vcT_ref[0, 0], p.astype(BF16), preferred_element_type=F32)
    comb_scr[...] = o_cmp * per_head(0)

    psum = p[:, 0:tq]
    for r in range(1, R):
        psum = psum + p[:, r * tq:(r + 1) * tq]
    per = SLC_BLOCK // CMP_STRIDE
    imp_chunks = []
    for c in range(tq // CHUNK):
        psum_scr[c, 0:SUBLANES, :] = jnp.zeros((SUBLANES, CHUNK), F32)
        psum_scr[c, SUBLANES:, :] = psum[:, c * CHUNK:(c + 1) * CHUNK]
        acc = jnp.zeros((nsb, CHUNK), F32)
        for k in range(1 - CMP_BLOCK // CMP_STRIDE, per):
            acc = acc + psum_scr[c, pl.ds(SUBLANES + k, nsb, stride=per), :]
        imp_chunks.append(acc)
    imp = jnp.concatenate(imp_chunks, axis=1)
    n_idx = lax.broadcasted_iota(jnp.int32, (nsb, tq), 0)
    cur = (t0 + lax.broadcasted_iota(jnp.int32, (nsb, tq), 1)) // SLC_BLOCK
    forced = (n_idx == 0) | (n_idx == cur) | (n_idx == cur - 1)
    valid = n_idx <= cur
    score = jnp.where(valid, imp + jnp.where(forced, FORCE_BONUS, 0.0), NEG)
    sel = _rank_select(score, min(SLC_TOPK, nsb)) & valid
    sel_past = sel & (n_idx * SLC_BLOCK < t0)
    bias = _pad_rows(jnp.where(sel_past, 0.0, NEG), HEAD_DIM).astype(BF16)
    bias_own = _pad_rows(jnp.where(sel, 0.0, NEG), HEAD_DIM).astype(BF16)
    for d in range(3):
        qaug_scr[d, 0:HEAD_DIM, :] = qT
    qaug_scr[0, HEAD_DIM:, :] = lanes_x_heads(bias)
    qaug_scr[1, HEAD_DIM:, :] = jnp.full((HEAD_DIM, R * tq), NEG, BF16)
    qaug_scr[2, HEAD_DIM:, :] = lanes_x_heads(bias_own)

    w0 = jnp.maximum(t0 + tq - WIN_SPAN, 0)
    k_w = kw_ref[0, pl.ds(pl.multiple_of(w0, CHUNK), WIN_SPAN), :]
    s_w = jnp.dot(k_w, qT, preferred_element_type=F32)
    kpos = w0 + lax.broadcasted_iota(jnp.int32, (WIN_SPAN, tq), 0)
    tpos = t0 + lax.broadcasted_iota(jnp.int32, (WIN_SPAN, tq), 1)
    ok = lanes_x_heads((kpos <= tpos) & (tpos - kpos < WINDOW))
    s_w = jnp.where(ok, s_w, NEG)
    m_w = jnp.max(s_w, axis=0, keepdims=True)
    p_w = jnp.exp2(s_w - m_w)
    jw = w0 // CHUNK
    v_w = jnp.concatenate([vwT_ref[0, jw + a] for a in range(WIN_SPAN // CHUNK)], axis=1)
    o_w = jnp.dot(_with_ones(v_w), p_w.astype(BF16), preferred_element_type=F32)
    comb_scr[...] += _normalized(o_w) * per_head(2)

    tri =lax.broadcasted_iota(jnp.int32, (tq, tq), 0) <= lax.broadcasted_iota(jnp.int32, (tq, tq), 1)
    s_d = jnp.where(lanes_x_heads(tri), _masked_scores(ks_ref.at[0], e_ref, t0, tq, qaug_scr[2]), NEG)
    own = tq // CHUNK
    v_d = jnp.concatenate([vsT_ref[0, i * own + a] for a in range(own)], axis=1)
    _first_step(s_d, v_d, m_scr, acc_scr)
    yield [(ks_ref.at[0], e_ref, vsT_ref.at[0], qaug_scr, (s0_scr, s1_scr), m_scr, acc_scr)]
    comb =comb_scr[...] + _normalized(acc_scr[...]) * per_head(1)

    for r in range(R):
        oT = comb[:, r * tq:(r + 1) * tq]
        ms = jnp.mean(oT * oT, axis=0, keepdims=True)
        on = (oT * lax.rsqrt(ms + EPS)).T * gain_ref[0, r:r + 1, :]
        o_ref[:, r * HEAD_DIM:(r + 1) * HEAD_DIM] = on.astype(BF16)


def _moba_body(qT_ref, k_ref, vT_ref, e_ref, gain_ref, o_ref, m_scr, acc_scr, kmean_scr, qaug_scr,
               s_scr, *, T):
    tq, tk = MOBA_TQ, MOBA_TK
    nb = T // MOBA_BLOCK
    nbp = kmean_scr.shape[1]
    i = pl.program_id(2)
    t0 = i * tq

    @pl.when(i == 0)
    def _():
        kmean_scr[...] = jnp.zeros(kmean_scr.shape, F32)
        for a in range(MOBA_HB):
            kb = k_ref[a].astype(F32).reshape(nb, MOBA_BLOCK, HEAD_DIM)
            kmean_scr[a, 0:nb, :] = jnp.mean(kb, axis=1)

    n_idx = lax.broadcasted_iota(jnp.int32, (nbp, tq), 0)
    cur = (t0 + lax.broadcasted_iota(jnp.int32, (nbp, tq), 1)) // MOBA_BLOCK
    past = n_idx < cur
    causal = lax.broadcasted_iota(jnp.int32, (tk, tq), 0) <= lax.broadcasted_iota(jnp.int32, (tk, tq), 1)

    qTs = [jnp.concatenate([qT_ref[a, c] for c in range(tq // CHUNK)], axis=1) for a in range(MOBA_HB)]
    for a in range(MOBA_HB):
        qT = qTs[a]
        gate = jnp.zeros((nbp, tq), F32)
        for part in _split3(kmean_scr[a]):
            gate = gate + jnp.dot(part, qT, preferred_element_type=F32)
        sel = _rank_select(jnp.where(past, gate, NEG), min(MOBA_TOPK, nb)) & past
        for d in range(2):
            qaug_scr[a, d, 0:HEAD_DIM, :] = qT
        qaug_scr[a, 0, HEAD_DIM:, :] = _pad_rows(jnp.where(sel, 0.0, NEG), HEAD_DIM).astype(BF16)
        qaug_scr[a, 1, HEAD_DIM:, :] = jnp.full((HEAD_DIM, tq), NEG, BF16)

    s_own = []
    for a in range(MOBA_HB):
        m_scr[a] = jnp.full(m_scr.shape[1:], M_INIT, F32)
        acc_scr[a] = jnp.zeros(acc_scr.shape[1:], F32)
        k_own = k_ref[a, pl.ds(pl.multiple_of(t0, tk), tk), :]
        s_own.append(jnp.where(causal, jnp.dot(k_own, qTs[a], preferred_element_type=F32), NEG))

    yield [(k_ref.at[a], e_ref, vT_ref.at[a], qaug_scr.at[a], (s_scr.at[a, 0], s_scr.at[a, 1]),
            m_scr.at[a], acc_scr.at[a]) for a in range(MOBA_HB)]

    own = tq // CHUNK
    for a in range(MOBA_HB):
        v_own = jnp.concatenate([vT_ref[a, i * own + c] for c in range(own)], axis=1)
        _online_step(s_own[a], v_own, m_scr.at[a], acc_scr.at[a])
        oT = _normalized(acc_scr[a])
        ms = jnp.mean(oT * oT, axis=0, keepdims=True)
        on = (oT * lax.rsqrt(ms + EPS)).T * gain_ref[a]
        o_ref[:, a * HEAD_DIM:(a + 1) * HEAD_DIM] = on.astype(BF16)


N_NSA_IN, N_NSA_SCRATCH, N_MOBA_IN = 10, 7, 5


def _attention_kernel(*refs, T):
    nsa_in, moba_in = refs[:N_NSA_IN], refs[N_NSA_IN:N_NSA_IN + N_MOBA_IN]
    o_nsa, o_moba = refs[N_NSA_IN + N_MOBA_IN:N_NSA_IN + N_MOBA_IN + 2]
    scratch = refs[N_NSA_IN + N_MOBA_IN + 2:]
    moba = _moba_body(*moba_in, o_moba, *scratch[N_NSA_SCRATCH:], T=T)
    nsa = _nsa_body(*nsa_in, o_nsa, *scratch[:N_NSA_SCRATCH], T=T)
    chains = next(moba) + next(nsa)
    n_steps = pl.program_id(2) * ATTN_TQ // ATTN_TKS
    _pipelined_attention(chains, n_steps, T // ATTN_TKS, ATTN_TKS, CHUNK, ATTN_BODY_PAIRS)
    for rest in (nsa, moba):
        assert next(rest, None) is None


def _attention(yT, ystd, kcvc, vcT, gT, e_slc, e_moba, nsa_gains, moba_gains, casts, B, T):
    G, R, tq = NSA_KV_HEADS, NSA_GROUP, ATTN_TQ
    H, HB = MOBA_HEADS, MOBA_HB
    assert H // HB == G and NSA_TQ == MOBA_TQ == tq and SLC_TKS == MOBA_TKS == ATTN_TKS and tq % ATTN_TKS == 0
    nq = T // tq
    ncp = T // CMP_STRIDE
    nb = T // MOBA_BLOCK
    nbp = max(16, nb)
    grid = (B, G, nq)
    cast_specs, cast_shapes = _cast_specs(casts, grid)
    kern = _carry_casts(functools.partial(_attention_kernel, T=T), N_NSA_IN + N_MOBA_IN, 2, len(casts))
    keys = lambda head0: pl.BlockSpec((1, T, HEAD_DIM), lambda b, g, i: (head0 + g, b, 0))
    values = lambda head0: pl.BlockSpec((1, T // CHUNK, HEAD_DIM, CHUNK), lambda b, g, i: (head0 + g, b, 0, 0))
    onehot = pl.BlockSpec((T, HEAD_DIM), lambda b, g, i: (0, 0))
    nsa_specs = [pl.BlockSpec((R, tq // CHUNK, HEAD_DIM, CHUNK), lambda b, g, i: (T_NSA_Q // R + g, b * nq + i, 0, 0)),
                 pl.BlockSpec((1, 1, ncp, HEAD_DIM), lambda b, g, i: (g, b, 0, 0)),
                 pl.BlockSpec((1, 1, HEAD_DIM, ncp), lambda b, g, i: (g, b, 0, 0)),
                 keys(S_KSLC), values(T_VSLC), keys(S_KWIN), values(T_VWIN),
                 pl.BlockSpec((1, GATE_ROWS, tq), lambda b, g, i: (g, 0, b * nq + i)),
                 onehot,
                 pl.BlockSpec((1, R, HEAD_DIM), lambda b, g, i: (g, 0, 0))]
    moba_specs = [pl.BlockSpec((HB, tq // CHUNK, HEAD_DIM, CHUNK), lambda b, h, i: (T_MOBA_Q // HB + h, b * nq + i, 0, 0)),
                  pl.BlockSpec((HB, T, HEAD_DIM), lambda b, h, i: (S_MOBA_K // HB + h, b, 0)),
                  pl.BlockSpec((HB, T // CHUNK, HEAD_DIM, CHUNK), lambda b, h, i: (T_MOBA_V // HB + h, b, 0, 0)),
                  onehot,
                  pl.BlockSpec((HB, 1, HEAD_DIM), lambda b, h, i: (h, 0, 0))]
    nsa_scratch = [pltpu.VMEM((1, R * tq), F32),
                   pltpu.VMEM((HEAD_DIM + ONES_ROWS, R * tq), F32), pltpu.VMEM((HEAD_DIM, R * tq), F32),
                   pltpu.VMEM((3, 2 * HEAD_DIM, R * tq), BF16),
                   pltpu.VMEM((ATTN_TKS, R * tq), F32), pltpu.VMEM((ATTN_TKS, R * tq), F32),
                   pltpu.VMEM((tq // CHUNK, SUBLANES + ncp, CHUNK), F32)]
    moba_scratch = [pltpu.VMEM((HB, 1, tq), F32),
                    pltpu.VMEM((HB, HEAD_DIM + ONES_ROWS, tq), F32), pltpu.VMEM((HB, nbp, HEAD_DIM), F32),
                    pltpu.VMEM((HB, 2, 2 * HEAD_DIM, tq), BF16),
                    pltpu.VMEM((HB, 2, ATTN_TKS, tq), F32)]
    assert len(nsa_specs) == N_NSA_IN and len(moba_specs) == N_MOBA_IN and len(nsa_scratch) == N_NSA_SCRATCH
    return pl.pallas_call(
        kern,
        grid=grid,
        in_specs=nsa_specs + moba_specs + cast_specs,
        out_specs=[pl.BlockSpec((tq, R * HEAD_DIM), lambda b, g, i: (b * nq + i, g)),
                   pl.BlockSpec((tq, HB * HEAD_DIM), lambda b, h, i: (b * nq + i, h))] + cast_specs,
        out_shape=[jax.ShapeDtypeStruct((B * T, NSA_HEADS * HEAD_DIM), BF16),
                   jax.ShapeDtypeStruct((B * T, H * HEAD_DIM), BF16)] + cast_shapes,
        scratch_shapes=nsa_scratch + moba_scratch,
        compiler_params=_cparams(("parallel", "parallel", "arbitrary")),
        name="attention",
    )(yT, kcvc, vcT, ystd, yT, ystd, yT, gT, e_slc, nsa_gains,
      yT, ystd, yT, e_moba, moba_gains, *[w for w, _ in casts])


def _outproj_kernel(on_ref, om_ref, w_ref, x_ref, mod_ref, o_ref):
    half = on_ref.shape[1]
    acc = jnp.dot(on_ref[...], w_ref[0:half, :], preferred_element_type=F32)
    acc = acc + jnp.dot(om_ref[...], w_ref[half:, :], preferred_element_type=F32)
    o_ref[...] = x_ref[...] + mod_ref[0, 2:3, :] * acc


def _out_proj(o_nsa, o_moba, w_out, x2, mod3, T):
    BT, D = x2.shape
    tm = min(512, T)
    tpb = T // tm
    half = o_nsa.shape[1]
    return pl.pallas_call(
        _outproj_kernel,
        grid=(BT // tm,),
        in_specs=[pl.BlockSpec((tm, half), lambda i: (i, 0)),
                  pl.BlockSpec((tm, half), lambda i: (i, 0)),
                  pl.BlockSpec((D, D), lambda i: (0, 0)),
                  pl.BlockSpec((tm, D), lambda i: (i, 0)),
                  pl.BlockSpec((1, 6, D), lambda i: (i // tpb, 0, 0))],
        out_specs=pl.BlockSpec((tm, D), lambda i: (i, 0)),
        out_shape=jax.ShapeDtypeStruct((BT, D), F32),
        compiler_params=_cparams(("parallel",)),
        name="out_proj",
    )(o_nsa, o_moba, w_out, x2, mod3)


def _ffn_kernel(x_ref, mod_ref, wg_ref, wu_ref, wo_ref, o_ref, h_scr):
    j = pl.program_id(1)

    @pl.when(j == 0)
    def _():
        x = x_ref[...]
        ms = jnp.mean(x * x, axis=-1, keepdims=True)
        h = x * lax.rsqrt(ms + EPS) * (1.0 + mod_ref[0, 4:5, :]) + mod_ref[0, 3:4, :]
        h_scr[...] = h.astype(BF16)
        o_ref[...] = jnp.zeros(o_ref.shape, F32)

    hb = h_scr[...]
    half = wg_ref.shape[1] // 2
    down = None
    for c in range(2):
        cols = slice(c * half, (c + 1) * half)
        gate = jnp.dot(hb, wg_ref[:, cols], preferred_element_type=F32)
        up = jnp.dot(hb, wu_ref[:, cols], preferred_element_type=F32)
        act = (gate * jax.nn.sigmoid(gate) * up).astype(BF16)
        part = jnp.dot(act, wo_ref[cols, :], preferred_element_type=F32)
        down = part if down is None else down + part
    o_ref[...] += down

    @pl.when(j == pl.num_programs(1) - 1)
    def _():
        o_ref[...] = x_ref[...] + mod_ref[0, 5:6, :] * o_ref[...]


def _ffn(x1, mod3, w_in, w_out, T):
    BT, D = x1.shape
    Fh = w_out.shape[0]
    tm = min(512, T)
    tf = 512
    tpb = T // tm
    nf = Fh // tf
    return pl.pallas_call(
        _ffn_kernel,
        grid=(BT // tm, nf),
        in_specs=[pl.BlockSpec((tm, D), lambda i, j: (i, 0)),
                  pl.BlockSpec((1, 6, D), lambda i, j: (i // tpb, 0, 0)),
                  pl.BlockSpec((D, tf), lambda i, j: (0, j)),
                  pl.BlockSpec((D, tf), lambda i, j: (0, nf + j)),
                  pl.BlockSpec((tf, D), lambda i, j: (j, 0))],
        out_specs=pl.BlockSpec((tm, D), lambda i, j: (i, 0)),
        out_shape=jax.ShapeDtypeStruct((BT, D), F32),
        scratch_shapes=[pltpu.VMEM((tm, D), BF16)],
        compiler_params=_cparams(("parallel", "arbitrary")),
        name="ffn",
    )(x1, mod3, w_in, w_in, w_out)


def _rope_cos_sin(pos):
    inv = ROPE_THETA ** (-jnp.arange(0, ROPE_DIMS, 2, dtype=F32) / ROPE_DIMS)
    ang = pos.astype(F32)[:, None] * inv[None, :]
    return jnp.cos(ang), jnp.sin(ang)


def _rope_tables(T):
    c_hi, s_hi = _rope_cos_sin(jnp.arange(0, T, ROPE_SPLIT))
    c_lo, s_lo = _rope_cos_sin(jnp.arange(ROPE_SPLIT))
    cos = (c_hi[:, None] * c_lo[None] - s_hi[:, None] * s_lo[None]).reshape(T, ROPE_HALF)
    sin = (s_hi[:, None] * c_lo[None] + c_hi[:, None] * s_lo[None]).reshape(T, ROPE_HALF)
    cos_c, sin_c = _rope_cos_sin(jnp.arange(CMP_BLOCK - 1, T, CMP_STRIDE))
    n, rest = cos_c.shape[0], HEAD_DIM - ROPE_DIMS
    pad = lambda t: jnp.pad(t, ((0, 1), (0, 0)))
    cc = pad(jnp.concatenate([cos_c, cos_c, jnp.ones((n, rest), F32)], axis=1))
    sa = pad(jnp.concatenate([-sin_c, jnp.zeros((n, HEAD_DIM - ROPE_HALF), F32)], axis=1))
    sb = pad(jnp.concatenate([jnp.zeros((n, ROPE_HALF), F32), sin_c, jnp.zeros((n, rest), F32)], axis=1))
    return cc, sa, sb, cos.T, sin.T


def _block_onehot(T, block):
    return (jnp.arange(T)[:, None] // block == jnp.arange(HEAD_DIM)[None, :]).astype(BF16)


def _layer(x2, c, B, T, w_ada, b_ada, w_in, nsa_q_norm, nsa_k_norm, moba_q_norm, moba_k_norm,
           cmp_pe_k, cmp_w1_k, cmp_w2_k, cmp_pe_v, cmp_w1_v, cmp_w2_v, out_norm, w_out,
           w_ffn_in, w_ffn_out):
    D = x2.shape[1]
    G = NSA_KV_HEADS
    scale = HEAD_DIM ** -0.5 * LOG2E
    assert T % MOBA_BLOCK == 0 and T % SLC_TKS == 0 and T % MOBA_TKS == 0 and T % NSA_TQ == 0 and T >= WIN_SPAN
    assert T // SLC_BLOCK <= HEAD_DIM and T // MOBA_BLOCK <= HEAD_DIM

    nsa_w = NSA_HEADS * HEAD_DIM + 6 * G * HEAD_DIM
    gw = NSA_HEADS * 3
    mod3 = _adaln(c, w_ada, b_ada)
    w_nsa = w_in.astype(BF16)
    w_moba = w_nsa[:, nsa_w + gw:]

    wg = w_in[:, nsa_w:nsa_w + gw].reshape(D, G, NSA_GROUP * 3)
    w_gate = jnp.pad(wg, ((0, 0), (0, 0), (0, GATE_LANES - NSA_GROUP * 3))).reshape(D, G * GATE_LANES)
    w_gate = w_gate.astype(BF16)
    across = lambda g_: jnp.broadcast_to(g_[:, None], (HEAD_DIM, CHUNK))
    gainsT = jnp.stack([across(nsa_q_norm * scale), across(moba_q_norm * scale),
                        across(nsa_k_norm[1]), across(nsa_k_norm[2]), across(moba_k_norm)])
    cc, sa, sb, cosT, sinT = _rope_tables(T)

    yT, ystd, hc, gT = _in_proj(x2, mod3, w_nsa, w_moba, w_gate, gainsT, cosT, sinT, T)

    ncp = T // CMP_STRIDE
    half = CMP_STRIDE * HEAD_DIM
    w1 = jnp.stack([cmp_w1_k, cmp_w1_v]).astype(BF16)
    w1cat = jnp.concatenate([w1[:, :half], w1[:, half:]], axis=2)
    pe8 = jnp.broadcast_to(jnp.stack([cmp_pe_k, cmp_pe_v]).reshape(2, 1, 2 * half), (2, 8, 2 * half))
    w2 = jnp.stack([cmp_w2_k, cmp_w2_v]).astype(BF16)
    kcvc = _compress(hc.reshape(2 * G, B, ncp, half), w1cat, pe8.astype(BF16), w1, w2,
                     nsa_k_norm[0].reshape(1, HEAD_DIM), cc, sa, sb)
    vcT = kcvc[G:].transpose(0, 1, 3, 2)

    on = out_norm.reshape(N_HEADS, HEAD_DIM)
    o_nsa, o_moba, w_ffn_out_b, w_out_b, w_ffn_in_b = _attention(
        yT, ystd, kcvc, vcT, gT, _block_onehot(T, SLC_BLOCK), _block_onehot(T, MOBA_BLOCK),
        on[:NSA_HEADS].reshape(G, NSA_GROUP, HEAD_DIM), on[NSA_HEADS:].reshape(MOBA_HEADS, 1, HEAD_DIM),
        [(w_ffn_out, 128), (w_out, 128), (w_ffn_in, 32)], B, T)

    x1 = _out_proj(o_nsa, o_moba, w_out_b, x2, mod3, T)
    return _ffn(x1, mod3, w_ffn_in_b, w_ffn_out_b, T)


def kernel(x, c, w_ada, b_ada, w_in, nsa_q_norm, nsa_k_norm, moba_q_norm, moba_k_norm, cmp_pe_k, cmp_w1_k, cmp_w2_k, cmp_pe_v, cmp_w1_v, cmp_w2_v, out_norm, w_out, w_ffn_in, w_ffn_out):
    B, T, D = x.shape
    x2 = x.reshape(B * T, D)
    for l in range(w_ada.shape[0]):
        x2 = _layer(x2, c, B, T, w_ada[l], b_ada[l], w_in[l], nsa_q_norm[l], nsa_k_norm[l],
                    moba_q_norm[l], moba_k_norm[l], cmp_pe_k[l], cmp_w1_k[l], cmp_w2_k[l],
                    cmp_pe_v[l], cmp_w1_v[l], cmp_w2_v[l], out_norm[l], w_out[l],
                    w_ffn_in[l], w_ffn_out[l])
    return x2.reshape(B, T, D)
```

```python
import functools

import jax
import jax.numpy as jnp
from jax import lax
from jax.experimental import pallas as pl
from jax.experimental.pallas import tpu as pltpu

F32 = jnp.float32
BF16 = jnp.bfloat16

HEAD_DIM = 128
SUBLANES = 8
NSA_HEADS = 8
NSA_KV_HEADS = 2
NSA_GROUP = NSA_HEADS // NSA_KV_HEADS
MOBA_HEADS = 8
N_HEADS = NSA_HEADS + MOBA_HEADS
CMP_BLOCK = 32
CMP_STRIDE = 16
SLC_BLOCK = 64
SLC_TOPK = 16
WINDOW = 512
FORCE_BONUS = 1e4
MOBA_BLOCK = 256
MOBA_TOPK = 3
ROPE_THETA = 500000.0
ROPE_DIMS = HEAD_DIM // 4
ROPE_HALF = ROPE_DIMS // 2
ROPE_SPLIT = 64
EPS = 1e-6
LOG2E = 1.4426950408889634
ONES_ROWS = 16
NEG = -1e30
M_INIT = -1e29

V7X_VMEM_BYTES = 64 * 1024 * 1024
VMEM_LIMIT = V7X_VMEM_BYTES - 8 * 1024 * 1024

PROJ_STEPS = ("nsa_q", "kv_cmp", "kv_slc", "kv_win", "moba_q", "moba_k", "moba_v")
WIDE_HEADS, NARROW_HEADS = 8, 4
OUT_BLOCK_HEADS = 8
YT_BLOCK = (0, 0, 3, 3, 1, 1, 2)
YSTD_BLOCK = (1, 1, 1, 1, 0, 0, 0)
GAIN_KINDS = ("nsa_q", "moba_q", "k_slc", "k_win", "moba_k")
PROJ_GAIN = {"nsa_q": "nsa_q", "moba_q": "moba_q", "kv_slc": "k_slc", "kv_win": "k_win", "moba_k": "moba_k"}
T_NSA_Q, T_MOBA_Q, T_MOBA_V, T_VSLC, T_VWIN = 0, 8, 16, 24, 26
S_MOBA_K, S_KSLC, S_KWIN = 0, 8, 10
N_T_HEADS, N_STD_HEADS, N_CMP_HEADS = 32, 16, 4
CHUNK = 128
GATE_LANES = 128
GATE_ROWS = 16

ATTN_TQ = 256
ATTN_TKS = 256
ATTN_BODY_PAIRS = (4, 2, 1)
NSA_TQ = ATTN_TQ
SLC_TKS = ATTN_TKS
WIN_SPAN = WINDOW + NSA_TQ
MOBA_TQ = ATTN_TQ
MOBA_TK = MOBA_BLOCK
MOBA_TKS = ATTN_TKS
MOBA_HB = 4


def _cparams(sem):
    return pltpu.CompilerParams(dimension_semantics=sem, vmem_limit_bytes=VMEM_LIMIT)


def _split3(a):
    hi = a.astype(BF16)
    r1 = a - hi.astype(F32)
    mid = r1.astype(BF16)
    lo = (r1 - mid.astype(F32)).astype(BF16)
    return hi, mid, lo


def _adaln_kernel(c_ref, w_ref, b_ref, o_ref):
    cv = c_ref[...]
    s = cv * jax.nn.sigmoid(cv)
    w = w_ref[...].astype(BF16)
    acc = b_ref[...] + jnp.zeros(o_ref.shape, F32)
    for part in _split3(s)[:2]:
        acc = acc + jnp.dot(part, w, preferred_element_type=F32)
    o_ref[...] = acc


def _adaln(c, w_ada, b_ada):
    B, D = c.shape
    N = w_ada.shape[1]
    tn = 1024
    c8 = jnp.zeros((8, D), F32).at[:B].set(c)
    out = pl.pallas_call(
        _adaln_kernel,
        grid=(N // tn,),
        in_specs=[pl.BlockSpec((8, D), lambda j: (0, 0)),
                  pl.BlockSpec((D, tn), lambda j: (0, j)),
                  pl.BlockSpec((1, tn), lambda j: (0, j))],
        out_specs=pl.BlockSpec((8, tn), lambda j: (0, j)),
        out_shape=jax.ShapeDtypeStruct((8, N), F32),
        compiler_params=_cparams(("arbitrary",)),
        name="adaln",
    )(c8, w_ada, b_ada.reshape(1, N))
    return out[:B].reshape(B, 6, D)


def _rope(y, cc, sa, sb):
    return (y * cc + pltpu.roll(y, HEAD_DIM - ROPE_HALF, 1) * sa + pltpu.roll(y, ROPE_HALF, 1) * sb)


def _inproj_kernel(x_ref, mod_ref, wq_ref, wkv_ref, wm_ref, wg_ref, gainT_ref, cosT_ref, sinT_ref,
                   yT_ref, ystd_ref, hc_ref, gT_ref, h_scr, rows_scr):
    j = pl.program_id(1)
    tm = x_ref.shape[0]
    n_chunks = tm // CHUNK
    pair = 2 * HEAD_DIM

    @pl.when(j == 0)
    def _():
        x = x_ref[...]
        ms = jnp.mean(x * x, axis=-1, keepdims=True)
        h = x * lax.rsqrt(ms + EPS) * (1.0 + mod_ref[0, 1:2, :]) + mod_ref[0, 0:1, :]
        hb = h.astype(BF16)
        h_scr[...] = hb
        g = jax.nn.sigmoid(jnp.dot(hb, wg_ref[...], preferred_element_type=F32))
        for grp in range(NSA_KV_HEADS):
            for c in range(n_chunks):
                blk = g[c * CHUNK:(c + 1) * CHUNK, grp * GATE_LANES:(grp + 1) * GATE_LANES].T
                gT_ref[grp, :, c * CHUNK:(c + 1) * CHUNK] = blk[0:GATE_ROWS, :]

    def head_pairs(w_ref, n_heads):
        for half in range(n_heads // 2):
            acc = jnp.dot(h_scr[...], w_ref[:, half * pair:(half + 1) * pair], preferred_element_type=F32)
            for h2 in range(2):
                yield 2 * half + h2, acc[:, h2 * HEAD_DIM:(h2 + 1) * HEAD_DIM]

    def qk_chunk(t, c):
        ms = jnp.mean(t * t, axis=0, keepdims=True)
        tn = t * lax.rsqrt(ms + EPS) * gainT_ref[0]
        cs = cosT_ref[:, c * CHUNK:(c + 1) * CHUNK]
        sn = sinT_ref[:, c * CHUNK:(c + 1) * CHUNK]
        a, b = tn[0:ROPE_HALF], tn[ROPE_HALF:ROPE_DIMS]
        return jnp.concatenate([a * cs - b * sn, b * cs + a * sn, tn[ROPE_DIMS:]], axis=0)

    def store_T(yh, slot, treated):
        for c in range(n_chunks):
            chunk = yh[c * CHUNK:(c + 1) * CHUNK, :]
            if treated:
                yT_ref[slot, c] = qk_chunk(chunk.T, c).astype(BF16)
            else:
                yT_ref[slot, c] = chunk.astype(BF16).T

    def store_rows(yh, slot):
        for c in range(n_chunks):
            rows = slice(c * CHUNK, (c + 1) * CHUNK)
            ystd_ref[slot, rows, :] = qk_chunk(yh[rows, :].T, c).astype(BF16).T

    def when_step(kind):
        return pl.when(j == PROJ_STEPS.index(kind))

    for kind, w_ref in (("nsa_q", wq_ref), ("moba_q", wm_ref)):
        @when_step(kind)
        def _(w_ref=w_ref):
            for hh, yh in head_pairs(w_ref, WIDE_HEADS):
                store_T(yh, hh, True)

    @when_step("moba_v")
    def _():
        for hh, yh in head_pairs(wm_ref, WIDE_HEADS):
            store_T(yh, hh, False)

    @when_step("moba_k")
    def _():
        for hh, yh in head_pairs(wm_ref, WIDE_HEADS):
            store_rows(yh, hh)

    for kind, base in (("kv_slc", 0), ("kv_win", NSA_KV_HEADS)):
        @when_step(kind)
        def _(base=base):
            if base == 0:
                for slot in range(2 * NSA_KV_HEADS, OUT_BLOCK_HEADS):
                    yT_ref[slot] = jnp.zeros(yT_ref.shape[1:], BF16)
                    ystd_ref[slot] = jnp.zeros(ystd_ref.shape[1:], BF16)
            for hh, yh in head_pairs(wkv_ref, NARROW_HEADS):
                if hh < NSA_KV_HEADS:
                    store_rows(yh, base + hh)
                else:
                    store_T(yh, base + hh - NSA_KV_HEADS, False)

    @when_step("kv_cmp")
    def _():
        for hh, yh in head_pairs(wkv_ref, NARROW_HEADS):
            rows_scr[...] = yh
            flat = [rows_scr[pl.ds(l, tm // CMP_STRIDE, stride=CMP_STRIDE), :] for l in range(CMP_STRIDE)]
            hc_ref[hh] = jnp.concatenate(flat, axis=1).astype(BF16)


def _in_proj(x2, mod3, w_nsa, w_moba, w_gate, gainsT, cosT, sinT, T):
    BT, D = x2.shape
    G = NSA_KV_HEADS
    tm = min(1024, T)
    wide, narrow = WIDE_HEADS * HEAD_DIM, NARROW_HEADS * HEAD_DIM
    tpb = T // tm
    pick = lambda j, table: sum((j == t).astype(jnp.int32) * v for t, v in enumerate(table))
    gain_kind = lambda j: pick(j, [GAIN_KINDS.index(PROJ_GAIN[s]) if s in PROJ_GAIN else 0 for s in PROJ_STEPS])
    first_kv, last_kv = PROJ_STEPS.index("kv_cmp"), PROJ_STEPS.index("kv_win")
    kv_block0 = NSA_HEADS * HEAD_DIM // narrow
    first_moba = PROJ_STEPS.index("moba_q")
    return pl.pallas_call(
        _inproj_kernel,
        grid=(BT // tm, len(PROJ_STEPS)),
        in_specs=[pl.BlockSpec((tm, D), lambda i, j: (i, 0)),
                  pl.BlockSpec((1, 6, D), lambda i, j: (i // tpb, 0, 0)),
                  pl.BlockSpec((D, wide), lambda i, j: (0, 0)),
                  pl.BlockSpec((D, narrow), lambda i, j: (0, kv_block0 + jnp.clip(j - first_kv, 0, last_kv - first_kv))),
                  pl.BlockSpec((D, wide), lambda i, j: (0, jnp.clip(j - first_moba, 0, 2))),
                  pl.BlockSpec((D, G * GATE_LANES), lambda i, j: (0, 0)),
                  pl.BlockSpec((1, HEAD_DIM, CHUNK), lambda i, j: (gain_kind(j), 0, 0)),
                  pl.BlockSpec((ROPE_HALF, tm), lambda i, j: (0, i % tpb)),
                  pl.BlockSpec((ROPE_HALF, tm), lambda i, j: (0, i % tpb))],
        out_specs=[pl.BlockSpec((OUT_BLOCK_HEADS, tm // CHUNK, HEAD_DIM, CHUNK), lambda i, j: (pick(j, YT_BLOCK), i, 0, 0)),
                   pl.BlockSpec((OUT_BLOCK_HEADS, tm, HEAD_DIM), lambda i, j: (pick(j, YSTD_BLOCK), i, 0)),
                   pl.BlockSpec((N_CMP_HEADS, tm // CMP_STRIDE, CMP_STRIDE * HEAD_DIM), lambda i, j: (0, i, 0)),
                   pl.BlockSpec((G, GATE_ROWS, tm), lambda i, j: (0, 0, i))],
        out_shape=[jax.ShapeDtypeStruct((N_T_HEADS, BT // CHUNK, HEAD_DIM, CHUNK), BF16),
                   jax.ShapeDtypeStruct((N_STD_HEADS, BT, HEAD_DIM), BF16),
                   jax.ShapeDtypeStruct((N_CMP_HEADS, BT // CMP_STRIDE, CMP_STRIDE * HEAD_DIM), BF16),
                   jax.ShapeDtypeStruct((G, GATE_ROWS, BT), F32)],
        scratch_shapes=[pltpu.VMEM((tm, D), BF16), pltpu.VMEM((tm, HEAD_DIM), F32)],
        compiler_params=_cparams(("parallel", "arbitrary")),
        name="in_proj",
    )(x2, mod3, w_nsa, w_nsa, w_moba, w_gate, gainsT, cosT, sinT)


def _compress_kernel(h_ref, w1c_ref, pe_ref, w1_ref, w2_ref, gain_ref, cc_ref, sa_ref, sb_ref, o_ref):
    a = pl.program_id(0)
    ncp = h_ref.shape[2]
    z = jnp.dot(h_ref[0, 0], w1c_ref[0], preferred_element_type=F32)
    top = z[:, :HEAD_DIM]
    bot = pltpu.roll(z[:, HEAD_DIM:], ncp - 1, 0)
    pe_term = jnp.dot(pe_ref[0], w1_ref[0], preferred_element_type=F32)[0:1, :]
    pre = top + bot + pe_term
    act = pre * jax.nn.sigmoid(pre)
    out = jnp.dot(act.astype(BF16), w2_ref[0], preferred_element_type=F32)
    live = lax.broadcasted_iota(jnp.int32, out.shape, 0) < ncp - 1
    out = jnp.where(live, out, 0.0)

    @pl.when(a < NSA_KV_HEADS)
    def _():
        ms = jnp.mean(out * out, axis=-1, keepdims=True)
        yn = out * lax.rsqrt(ms + EPS) * gain_ref[...]
        o_ref[0, 0] = _rope(yn, cc_ref[...], sa_ref[...], sb_ref[...]).astype(BF16)

    @pl.when(a >= NSA_KV_HEADS)
    def _():
        o_ref[0, 0] = out.astype(BF16)


def _compress(hc, w1cat, pe8, w1, w2, gain, cc, sa, sb):
    A, B, ncp, K = hc.shape
    G = NSA_KV_HEADS
    return pl.pallas_call(
        _compress_kernel,
        grid=(A, B),
        in_specs=[pl.BlockSpec((1, 1, ncp, K), lambda a, b: (a, b, 0, 0)),
                  pl.BlockSpec((1, K, 2 * HEAD_DIM), lambda a, b: (a // G, 0, 0)),
                  pl.BlockSpec((1, 8, 2 * K), lambda a, b: (a // G, 0, 0)),
                  pl.BlockSpec((1, 2 * K, HEAD_DIM), lambda a, b: (a // G, 0, 0)),
                  pl.BlockSpec((1, HEAD_DIM, HEAD_DIM), lambda a, b: (a // G, 0, 0)),
                  pl.BlockSpec((1, HEAD_DIM), lambda a, b: (0, 0)),
                  pl.BlockSpec((ncp, HEAD_DIM), lambda a, b: (0, 0)),
                  pl.BlockSpec((ncp, HEAD_DIM), lambda a, b: (0, 0)),
                  pl.BlockSpec((ncp, HEAD_DIM), lambda a, b: (0, 0))],
        out_specs=pl.BlockSpec((1, 1, ncp, HEAD_DIM), lambda a, b: (a, b, 0, 0)),
        out_shape=jax.ShapeDtypeStruct((A, B, ncp, HEAD_DIM), BF16),
        compiler_params=_cparams(("arbitrary", "arbitrary")),
        name="compress",
    )(hc, w1cat, pe8, w1, w2, gain, cc, sa, sb)


def _first_step(s, vT, m_scr, acc_scr):
    m = jnp.max(s, axis=0, keepdims=True)
    p = jnp.exp2(s - m)
    acc_scr[...] = jnp.dot(_with_ones(vT), p.astype(BF16), preferred_element_type=F32)
    m_scr[...] = m


def _with_ones(vT):
    return jnp.concatenate([vT, jnp.ones((ONES_ROWS, vT.shape[1]), vT.dtype)], axis=0)


def _normalized(acc):
    return acc[0:HEAD_DIM] / acc[HEAD_DIM:HEAD_DIM + 1]


def _online_step(s, vT, m_scr, acc_scr):
    m_prev = m_scr[...]
    m_new = jnp.maximum(m_prev, jnp.max(s, axis=0, keepdims=True))
    alpha = jnp.exp2(m_prev - m_new)
    p = jnp.exp2(s - m_new)
    acc_scr[...] = alpha * acc_scr[...] + jnp.dot(_with_ones(vT), p.astype(BF16), preferred_element_type=F32)
    m_scr[...] = m_new


def _rank_select(score, k):
    n = score.shape[0]
    rank = jnp.zeros(score.shape, F32)
    for m in range(n):
        sm = score[m:m + 1, :]
        lo = (m // SUBLANES) * SUBLANES
        hi = min(lo + SUBLANES, n)
        parts = []
        if lo > 0:
            parts.append(jnp.where(sm > score[:lo], 1.0, 0.0))
        gt = jnp.where(sm > score[lo:hi], 1.0, 0.0)
        ge = jnp.where(sm >= score[lo:hi], 1.0, 0.0)
        below = lax.broadcasted_iota(jnp.int32, gt.shape, 0) > m - lo
        parts.append(jnp.where(below, ge, gt))
        if hi < n:
            parts.append(jnp.where(sm >= score[hi:], 1.0, 0.0))
        rank = rank + (jnp.concatenate(parts, axis=0) if len(parts) > 1 else parts[0])
    return rank < k


def _pad_rows(a, rows):
    return jnp.concatenate([a, jnp.zeros((rows - a.shape[0], a.shape[1]), a.dtype)], axis=0)


def _masked_scores(k_ref, e_ref, start, size, q_aug):
    rows = pl.ds(pl.multiple_of(start, 128), size)
    k_aug = jnp.concatenate([k_ref[rows, :], e_ref[rows, :]], axis=1)
    return jnp.dot(k_aug, q_aug, preferred_element_type=F32)


def _pipelined_attention(chains, n_steps, n_max, tks, tk, body_pairs):
    per = tks // tk

    def scores(step, buf):
        st = jnp.minimum(step, n_max - 1)
        dead = jnp.where(step < n_steps, 0, 1)
        for k_ref, e_ref, _, qaug_ref, bufs, _, _ in chains:
            bufs[buf][...] = _masked_scores(k_ref, e_ref, st * tks, tks, qaug_ref[dead])

    def consume(step, buf):
        st = jnp.minimum(step, n_max - 1)
        for _, _, vT_ref, _, bufs, m_scr, acc_scr in chains:
            vT = jnp.concatenate([vT_ref[st * per + a] for a in range(per)], axis=1)
            _online_step(bufs[buf][...], vT, m_scr, acc_scr)

    def pairs(first_step, n_pairs_in_body):
        def body(it, carry):
            base = first_step + it * 2 * n_pairs_in_body
            for p in range(n_pairs_in_body):
                scores(base + 2 * p + 1, 1)
                consume(base + 2 * p, 0)
                scores(base + 2 * p + 2, 0)
                consume(base + 2 * p + 1, 1)
            return carry
        return body

    scores(0, 0)
    done = 0
    for n_body in body_pairs[:-1]:
        n_iter = (n_steps - done) // (2 * n_body)
        lax.fori_loop(0, n_iter, pairs(done, n_body), 0)
        done = done + n_iter * 2 * n_body
    lax.fori_loop(0, (n_steps - done + 1) // 2, pairs(done, 1), 0)


def _carry_casts(kernel_fn, n_in, n_out, n_cast):
    def wrapped(*refs):
        ins, cast_in = refs[:n_in], refs[n_in:n_in + n_cast]
        outs = refs[n_in + n_cast:n_in + n_cast + n_out]
        cast_out = refs[n_in + n_cast + n_out:n_in + n_cast + n_out + n_cast]
        for src, dst in zip(cast_in, cast_out):
            dst[...] = src[...].astype(BF16)
        kernel_fn(*ins, *outs, *refs[n_in + n_cast + n_out + n_cast:])
    return wrapped


def _cast_specs(weights, grid):
    n_steps = grid[0] * grid[1] * grid[2]
    specs, shapes = [], []
    for w, rows in weights:
        while w.shape[0] // rows > n_steps:
            rows *= 2
        n_blk = w.shape[0] // rows
        assert w.shape[0] % rows == 0
        index = lambda a, b, c, n_blk=n_blk: (jnp.minimum((a * grid[1] + b) * grid[2] + c, n_blk - 1), 0)
        specs.append(pl.BlockSpec((rows, w.shape[1]), index))
        shapes.append(jax.ShapeDtypeStruct(w.shape, BF16))
    return specs, shapes


def _nsa_body(qT_ref, kc_ref, vcT_ref, ks_ref, vsT_ref, kw_ref, vwT_ref, gT_ref, e_ref,
              gain_ref, o_ref, m_scr, acc_scr, comb_scr, qaug_scr, s0_scr, s1_scr, psum_scr, *, T):
    R, tq = NSA_GROUP, NSA_TQ
    i = pl.program_id(2)
    t0 = i * tq
    ncp = T // CMP_STRIDE
    nsb = T // SLC_BLOCK
    qT = jnp.concatenate([qT_ref[r, c] for r in range(R) for c in range(tq // CHUNK)], axis=1)
    gT = gT_ref[0]

    def per_head(row0):
        return jnp.concatenate([gT[row0 + 3 * r:row0 + 3 * r + 1, :] for r in range(R)], axis=1)

    def lanes_x_heads(a):
        return jnp.concatenate([a] * R, axis=1)

    s = jnp.dot(kc_ref[0, 0], qT, preferred_element_type=F32)
    c_idx = lax.broadcasted_iota(jnp.int32, (ncp, tq), 0)
    t_c = t0 + lax.broadcasted_iota(jnp.int32, (ncp, tq), 1)
    vis = lanes_x_heads((c_idx * CMP_STRIDE + (CMP_BLOCK - 1) <= t_c) & (c_idx < ncp - 1))
    s = jnp.where(vis, s, NEG)
    m = jnp.maximum(jnp.max(s, axis=0, keepdims=True), M_INIT)
    p = jnp.exp2(s - m)
    l = jnp.sum(p, axis=0, keepdims=True)
    p = p / jnp.where(l > 0.0, l, 1.0)
    o_cmp = jnp.dot(vcT_ref[0, 0], p.astype(BF16), preferred_element_type=F32)
    comb_scr[...] = o_cmp * per_head(0)

    psum = p[:, 0:tq]
    for r in range(1, R):
        psum = psum + p[:, r * tq:(r + 1) * tq]
    per = SLC_BLOCK // CMP_STRIDE
    imp_chunks = []
    for c in range(tq // CHUNK):
        psum_scr[c, 0:SUBLANES, :] = jnp.zeros((SUBLANES, CHUNK), F32)
        psum_scr[c, SUBLANES:, :] = psum[:, c * CHUNK:(c + 1) * CHUNK]
        acc = jnp.zeros((nsb, CHUNK), F32)
        for k in range(1 - CMP_BLOCK // CMP_STRIDE, per):
            acc = acc + psum_scr[c, pl.ds(SUBLANES + k, nsb, stride=per), :]
        imp_chunks.append(acc)
    imp = jnp.concatenate(imp_chunks, axis=1)
    n_idx = lax.broadcasted_iota(jnp.int32, (nsb, tq), 0)
    cur = (t0 + lax.broadcasted_iota(jnp.int32, (nsb, tq), 1)) // SLC_BLOCK
    forced = (n_idx == 0) | (n_idx == cur) | (n_idx == cur - 1)
    valid = n_idx <= cur
    score = jnp.where(valid, imp + jnp.where(forced, FORCE_BONUS, 0.0), NEG)
    sel = _rank_select(score, min(SLC_TOPK, nsb)) & valid
    sel_past = sel & (n_idx * SLC_BLOCK < t0)
    bias = _pad_rows(jnp.where(sel_past, 0.0, NEG), HEAD_DIM).astype(BF16)
    bias_own = _pad_rows(jnp.where(sel, 0.0, NEG), HEAD_DIM).astype(BF16)
    for d in range(3):
        qaug_scr[d, 0:HEAD_DIM, :] = qT
    qaug_scr[0, HEAD_DIM:, :] = lanes_x_heads(bias)
    qaug_scr[1, HEAD_DIM:, :] = jnp.full((HEAD_DIM, R * tq), NEG, BF16)
    qaug_scr[2, HEAD_DIM:, :] = lanes_x_heads(bias_own)

    w0 = jnp.maximum(t0 + tq - WIN_SPAN, 0)
    k_w = kw_ref[0, pl.ds(pl.multiple_of(w0, CHUNK), WIN_SPAN), :]
    s_w = jnp.dot(k_w, qT, preferred_element_type=F32)
    kpos = w0 + lax.broadcasted_iota(jnp.int32, (WIN_SPAN, tq), 0)
    tpos = t0 + lax.broadcasted_iota(jnp.int32, (WIN_SPAN, tq), 1)
    ok = lanes_x_heads((kpos <= tpos) & (tpos - kpos < WINDOW))
    s_w = jnp.where(ok, s_w, NEG)
    m_w = jnp.max(s_w, axis=0, keepdims=True)
    p_w = jnp.exp2(s_w - m_w)
    jw = w0 // CHUNK
    v_w = jnp.concatenate([vwT_ref[0, jw + a] for a in range(WIN_SPAN // CHUNK)], axis=1)
    o_w = jnp.dot(_with_ones(v_w), p_w.astype(BF16), preferred_element_type=F32)
    comb_scr[...] += _normalized(o_w) * per_head(2)

    tri =lax.broadcasted_iota(jnp.int32, (tq, tq), 0) <= lax.broadcasted_iota(jnp.int32, (tq, tq), 1)
    s_d = jnp.where(lanes_x_heads(tri), _masked_scores(ks_ref.at[0], e_ref, t0, tq, qaug_scr[2]), NEG)
    own = tq // CHUNK
    v_d = jnp.concatenate([vsT_ref[0, i * own + a] for a in range(own)], axis=1)
    _first_step(s_d, v_d, m_scr, acc_scr)
    yield [(ks_ref.at[0], e_ref, vsT_ref.at[0], qaug_scr, (s0_scr, s1_scr), m_scr, acc_scr)]
    comb =comb_scr[...] + _normalized(acc_scr[...]) * per_head(1)

    for r in range(R):
        oT = comb[:, r * tq:(r + 1) * tq]
        ms = jnp.mean(oT * oT, axis=0, keepdims=True)
        on = (oT * lax.rsqrt(ms + EPS)).T * gain_ref[0, r:r + 1, :]
        o_ref[:, r * HEAD_DIM:(r + 1) * HEAD_DIM] = on.astype(BF16)


def _moba_body(qT_ref, k_ref, vT_ref, e_ref, gain_ref, o_ref, m_scr, acc_scr, kmean_scr, qaug_scr,
               s_scr, *, T):
    tq, tk = MOBA_TQ, MOBA_TK
    nb = T // MOBA_BLOCK
    nbp = kmean_scr.shape[1]
    i = pl.program_id(2)
    t0 = i * tq

    @pl.when(i == 0)
    def _():
        kmean_scr[...] = jnp.zeros(kmean_scr.shape, F32)
        for a in range(MOBA_HB):
            kb = k_ref[a].astype(F32).reshape(nb, MOBA_BLOCK, HEAD_DIM)
            kmean_scr[a, 0:nb, :] = jnp.mean(kb, axis=1)

    n_idx = lax.broadcasted_iota(jnp.int32, (nbp, tq), 0)
    cur = (t0 + lax.broadcasted_iota(jnp.int32, (nbp, tq), 1)) // MOBA_BLOCK
    past = n_idx < cur
    causal = lax.broadcasted_iota(jnp.int32, (tk, tq), 0) <= lax.broadcasted_iota(jnp.int32, (tk, tq), 1)

    qTs = [jnp.concatenate([qT_ref[a, c] for c in range(tq // CHUNK)], axis=1) for a in range(MOBA_HB)]
    for a in range(MOBA_HB):
        qT = qTs[a]
        gate = jnp.zeros((nbp, tq), F32)
        for part in _split3(kmean_scr[a]):
            gate = gate + jnp.dot(part, qT, preferred_element_type=F32)
        sel = _rank_select(jnp.where(past, gate, NEG), min(MOBA_TOPK, nb)) & past
        for d in range(2):
            qaug_scr[a, d, 0:HEAD_DIM, :] = qT
        qaug_scr[a, 0, HEAD_DIM:, :] = _pad_rows(jnp.where(sel, 0.0, NEG), HEAD_DIM).astype(BF16)
        qaug_scr[a, 1, HEAD_DIM:, :] = jnp.full((HEAD_DIM, tq), NEG, BF16)

    s_own = []
    for a in range(MOBA_HB):
        m_scr[a] = jnp.full(m_scr.shape[1:], M_INIT, F32)
        acc_scr[a] = jnp.zeros(acc_scr.shape[1:], F32)
        k_own = k_ref[a, pl.ds(pl.multiple_of(t0, tk), tk), :]
        s_own.append(jnp.where(causal, jnp.dot(k_own, qTs[a], preferred_element_type=F32), NEG))

    yield [(k_ref.at[a], e_ref, vT_ref.at[a], qaug_scr.at[a], (s_scr.at[a, 0], s_scr.at[a, 1]),
            m_scr.at[a], acc_scr.at[a]) for a in range(MOBA_HB)]

    own = tq // CHUNK
    for a in range(MOBA_HB):
        v_own = jnp.concatenate([vT_ref[a, i * own + c] for c in range(own)], axis=1)
        _online_step(s_own[a], v_own, m_scr.at[a], acc_scr.at[a])
        oT = _normalized(acc_scr[a])
        ms = jnp.mean(oT * oT, axis=0, keepdims=True)
        on = (oT * lax.rsqrt(ms + EPS)).T * gain_ref[a]
        o_ref[:, a * HEAD_DIM:(a + 1) * HEAD_DIM] = on.astype(BF16)


N_NSA_IN, N_NSA_SCRATCH, N_MOBA_IN = 10, 7, 5


def _attention_kernel(*refs, T):
    nsa_in, moba_in = refs[:N_NSA_IN], refs[N_NSA_IN:N_NSA_IN + N_MOBA_IN]
    o_nsa, o_moba = refs[N_NSA_IN + N_MOBA_IN:N_NSA_IN + N_MOBA_IN + 2]
    scratch = refs[N_NSA_IN + N_MOBA_IN + 2:]
    moba = _moba_body(*moba_in, o_moba, *scratch[N_NSA_SCRATCH:], T=T)
    nsa = _nsa_body(*nsa_in, o_nsa, *scratch[:N_NSA_SCRATCH], T=T)
    chains = next(moba) + next(nsa)
    n_steps = pl.program_id(2) * ATTN_TQ // ATTN_TKS
    _pipelined_attention(chains, n_steps, T // ATTN_TKS, ATTN_TKS, CHUNK, ATTN_BODY_PAIRS)
    for rest in (nsa, moba):
        assert next(rest, None) is None


def _attention(yT, ystd, kcvc, vcT, gT, e_slc, e_moba, nsa_gains, moba_gains, casts, B, T):
    G, R, tq = NSA_KV_HEADS, NSA_GROUP, ATTN_TQ
    H, HB = MOBA_HEADS, MOBA_HB
    assert H // HB == G and NSA_TQ == MOBA_TQ == tq and SLC_TKS == MOBA_TKS == ATTN_TKS and tq % ATTN_TKS == 0
    nq = T // tq
    ncp = T // CMP_STRIDE
    nb = T // MOBA_BLOCK
    nbp = max(16, nb)
    grid = (B, G, nq)
    cast_specs, cast_shapes = _cast_specs(casts, grid)
    kern = _carry_casts(functools.partial(_attention_kernel, T=T), N_NSA_IN + N_MOBA_IN, 2, len(casts))
    keys = lambda head0: pl.BlockSpec((1, T, HEAD_DIM), lambda b, g, i: (head0 + g, b, 0))
    values = lambda head0: pl.BlockSpec((1, T // CHUNK, HEAD_DIM, CHUNK), lambda b, g, i: (head0 + g, b, 0, 0))
    onehot = pl.BlockSpec((T, HEAD_DIM), lambda b, g, i: (0, 0))
    nsa_specs = [pl.BlockSpec((R, tq // CHUNK, HEAD_DIM, CHUNK), lambda b, g, i: (T_NSA_Q // R + g, b * nq + i, 0, 0)),
                 pl.BlockSpec((1, 1, ncp, HEAD_DIM), lambda b, g, i: (g, b, 0, 0)),
                 pl.BlockSpec((1, 1, HEAD_DIM, ncp), lambda b, g, i: (g, b, 0, 0)),
                 keys(S_KSLC), values(T_VSLC), keys(S_KWIN), values(T_VWIN),
                 pl.BlockSpec((1, GATE_ROWS, tq), lambda b, g, i: (g, 0, b * nq + i)),
                 onehot,
                 pl.BlockSpec((1, R, HEAD_DIM), lambda b, g, i: (g, 0, 0))]
    moba_specs = [pl.BlockSpec((HB, tq // CHUNK, HEAD_DIM, CHUNK), lambda b, h, i: (T_MOBA_Q // HB + h, b * nq + i, 0, 0)),
                  pl.BlockSpec((HB, T, HEAD_DIM), lambda b, h, i: (S_MOBA_K // HB + h, b, 0)),
                  pl.BlockSpec((HB, T // CHUNK, HEAD_DIM, CHUNK), lambda b, h, i: (T_MOBA_V // HB + h, b, 0, 0)),
                  onehot,
                  pl.BlockSpec((HB, 1, HEAD_DIM), lambda b, h, i: (h, 0, 0))]
    nsa_scratch = [pltpu.VMEM((1, R * tq), F32),
                   pltpu.VMEM((HEAD_DIM + ONES_ROWS, R * tq), F32), pltpu.VMEM((HEAD_DIM, R * tq), F32),
                   pltpu.VMEM((3, 2 * HEAD_DIM, R * tq), BF16),
                   pltpu.VMEM((ATTN_TKS, R * tq), F32), pltpu.VMEM((ATTN_TKS, R * tq), F32),
                   pltpu.VMEM((tq // CHUNK, SUBLANES + ncp, CHUNK), F32)]
    moba_scratch = [pltpu.VMEM((HB, 1, tq), F32),
                    pltpu.VMEM((HB, HEAD_DIM + ONES_ROWS, tq), F32), pltpu.VMEM((HB, nbp, HEAD_DIM), F32),
                    pltpu.VMEM((HB, 2, 2 * HEAD_DIM, tq), BF16),
                    pltpu.VMEM((HB, 2, ATTN_TKS, tq), F32)]
    assert len(nsa_specs) == N_NSA_IN and len(moba_specs) == N_MOBA_IN and len(nsa_scratch) == N_NSA_SCRATCH
    return pl.pallas_call(
        kern,
        grid=grid,
        in_specs=nsa_specs + moba_specs + cast_specs,
        out_specs=[pl.BlockSpec((tq, R * HEAD_DIM), lambda b, g, i: (b * nq + i, g)),
                   pl.BlockSpec((tq, HB * HEAD_DIM), lambda b, h, i: (b * nq + i, h))] + cast_specs,
        out_shape=[jax.ShapeDtypeStruct((B * T, NSA_HEADS * HEAD_DIM), BF16),
                   jax.ShapeDtypeStruct((B * T, H * HEAD_DIM), BF16)] + cast_shapes,
        scratch_shapes=nsa_scratch + moba_scratch,
        compiler_params=_cparams(("parallel", "parallel", "arbitrary")),
        name="attention",
    )(yT, kcvc, vcT, ystd, yT, ystd, yT, gT, e_slc, nsa_gains,
      yT, ystd, yT, e_moba, moba_gains, *[w for w, _ in casts])


def _outproj_kernel(on_ref, om_ref, w_ref, x_ref, mod_ref, o_ref):
    half = on_ref.shape[1]
    acc = jnp.dot(on_ref[...], w_ref[0:half, :], preferred_element_type=F32)
    acc = acc + jnp.dot(om_ref[...], w_ref[half:, :], preferred_element_type=F32)
    o_ref[...] = x_ref[...] + mod_ref[0, 2:3, :] * acc


def _out_proj(o_nsa, o_moba, w_out, x2, mod3, T):
    BT, D = x2.shape
    tm = min(512, T)
    tpb = T // tm
    half = o_nsa.shape[1]
    return pl.pallas_call(
        _outproj_kernel,
        grid=(BT // tm,),
        in_specs=[pl.BlockSpec((tm, half), lambda i: (i, 0)),
                  pl.BlockSpec((tm, half), lambda i: (i, 0)),
                  pl.BlockSpec((D, D), lambda i: (0, 0)),
                  pl.BlockSpec((tm, D), lambda i: (i, 0)),
                  pl.BlockSpec((1, 6, D), lambda i: (i // tpb, 0, 0))],
        out_specs=pl.BlockSpec((tm, D), lambda i: (i, 0)),
        out_shape=jax.ShapeDtypeStruct((BT, D), F32),
        compiler_params=_cparams(("parallel",)),
        name="out_proj",
    )(o_nsa, o_moba, w_out, x2, mod3)


def _ffn_kernel(x_ref, mod_ref, wg_ref, wu_ref, wo_ref, o_ref, h_scr):
    j = pl.program_id(1)

    @pl.when(j == 0)
    def _():
        x = x_ref[...]
        ms = jnp.mean(x * x, axis=-1, keepdims=True)
        h = x * lax.rsqrt(ms + EPS) * (1.0 + mod_ref[0, 4:5, :]) + mod_ref[0, 3:4, :]
        h_scr[...] = h.astype(BF16)
        o_ref[...] = jnp.zeros(o_ref.shape, F32)

    hb = h_scr[...]
    half = wg_ref.shape[1] // 2
    down = None
    for c in range(2):
        cols = slice(c * half, (c + 1) * half)
        gate = jnp.dot(hb, wg_ref[:, cols], preferred_element_type=F32)
        up = jnp.dot(hb, wu_ref[:, cols], preferred_element_type=F32)
        act = (gate * jax.nn.sigmoid(gate) * up).astype(BF16)
        part = jnp.dot(act, wo_ref[cols, :], preferred_element_type=F32)
        down = part if down is None else down + part
    o_ref[...] += down

    @pl.when(j == pl.num_programs(1) - 1)
    def _():
        o_ref[...] = x_ref[...] + mod_ref[0, 5:6, :] * o_ref[...]


def _ffn(x1, mod3, w_in, w_out, T):
    BT, D = x1.shape
    Fh = w_out.shape[0]
    tm = min(512, T)
    tf = 512
    tpb = T // tm
    nf = Fh // tf
    return pl.pallas_call(
        _ffn_kernel,
        grid=(BT // tm, nf),
        in_specs=[pl.BlockSpec((tm, D), lambda i, j: (i, 0)),
                  pl.BlockSpec((1, 6, D), lambda i, j: (i // tpb, 0, 0)),
                  pl.BlockSpec((D, tf), lambda i, j: (0, j)),
                  pl.BlockSpec((D, tf), lambda i, j: (0, nf + j)),
                  pl.BlockSpec((tf, D), lambda i, j: (j, 0))],
        out_specs=pl.BlockSpec((tm, D), lambda i, j: (i, 0)),
        out_shape=jax.ShapeDtypeStruct((BT, D), F32),
        scratch_shapes=[pltpu.VMEM((tm, D), BF16)],
        compiler_params=_cparams(("parallel", "arbitrary")),
        name="ffn",
    )(x1, mod3, w_in, w_in, w_out)


def _rope_cos_sin(pos):
    inv = ROPE_THETA ** (-jnp.arange(0, ROPE_DIMS, 2, dtype=F32) / ROPE_DIMS)
    ang = pos.astype(F32)[:, None] * inv[None, :]
    return jnp.cos(ang), jnp.sin(ang)


def _rope_tables(T):
    c_hi, s_hi = _rope_cos_sin(jnp.arange(0, T, ROPE_SPLIT))
    c_lo, s_lo = _rope_cos_sin(jnp.arange(ROPE_SPLIT))
    cos = (c_hi[:, None] * c_lo[None] - s_hi[:, None] * s_lo[None]).reshape(T, ROPE_HALF)
    sin = (s_hi[:, None] * c_lo[None] + c_hi[:, None] * s_lo[None]).reshape(T, ROPE_HALF)
    cos_c, sin_c = _rope_cos_sin(jnp.arange(CMP_BLOCK - 1, T, CMP_STRIDE))
    n, rest = cos_c.shape[0], HEAD_DIM - ROPE_DIMS
    pad = lambda t: jnp.pad(t, ((0, 1), (0, 0)))
    cc = pad(jnp.concatenate([cos_c, cos_c, jnp.ones((n, rest), F32)], axis=1))
    sa = pad(jnp.concatenate([-sin_c, jnp.zeros((n, HEAD_DIM - ROPE_HALF), F32)], axis=1))
    sb = pad(jnp.concatenate([jnp.zeros((n, ROPE_HALF), F32), sin_c, jnp.zeros((n, rest), F32)], axis=1))
    return cc, sa, sb, cos.T, sin.T


def _block_onehot(T, block):
    return (jnp.arange(T)[:, None] // block == jnp.arange(HEAD_DIM)[None, :]).astype(BF16)


def _layer(x2, c, B, T, w_ada, b_ada, w_in, nsa_q_norm, nsa_k_norm, moba_q_norm, moba_k_norm,
           cmp_pe_k, cmp_w1_k, cmp_w2_k, cmp_pe_v, cmp_w1_v, cmp_w2_v, out_norm, w_out,
           w_ffn_in, w_ffn_out):
    D = x2.shape[1]
    G = NSA_KV_HEADS
    scale = HEAD_DIM ** -0.5 * LOG2E
    assert T % MOBA_BLOCK == 0 and T % SLC_TKS == 0 and T % MOBA_TKS == 0 and T % NSA_TQ == 0 and T >= WIN_SPAN
    assert T // SLC_BLOCK <= HEAD_DIM and T // MOBA_BLOCK <= HEAD_DIM

    nsa_w = NSA_HEADS * HEAD_DIM + 6 * G * HEAD_DIM
    gw = NSA_HEADS * 3
    mod3 = _adaln(c, w_ada, b_ada)
    w_nsa = w_in.astype(BF16)
    w_moba = w_nsa[:, nsa_w + gw:]

    wg = w_in[:, nsa_w:nsa_w + gw].reshape(D, G, NSA_GROUP * 3)
    w_gate = jnp.pad(wg, ((0, 0), (0, 0), (0, GATE_LANES - NSA_GROUP * 3))).reshape(D, G * GATE_LANES)
    w_gate = w_gate.astype(BF16)
    across = lambda g_: jnp.broadcast_to(g_[:, None], (HEAD_DIM, CHUNK))
    gainsT = jnp.stack([across(nsa_q_norm * scale), across(moba_q_norm * scale),
                        across(nsa_k_norm[1]), across(nsa_k_norm[2]), across(moba_k_norm)])
    cc, sa, sb, cosT, sinT = _rope_tables(T)

    yT, ystd, hc, gT = _in_proj(x2, mod3, w_nsa, w_moba, w_gate, gainsT, cosT, sinT, T)

    ncp = T // CMP_STRIDE
    half = CMP_STRIDE * HEAD_DIM
    w1 = jnp.stack([cmp_w1_k, cmp_w1_v]).astype(BF16)
    w1cat = jnp.concatenate([w1[:, :half], w1[:, half:]], axis=2)
    pe8 = jnp.broadcast_to(jnp.stack([cmp_pe_k, cmp_pe_v]).reshape(2, 1, 2 * half), (2, 8, 2 * half))
    w2 = jnp.stack([cmp_w2_k, cmp_w2_v]).astype(BF16)
    kcvc = _compress(hc.reshape(2 * G, B, ncp, half), w1cat, pe8.astype(BF16), w1, w2,
                     nsa_k_norm[0].reshape(1, HEAD_DIM), cc, sa, sb)
    vcT = kcvc[G:].transpose(0, 1, 3, 2)

    on = out_norm.reshape(N_HEADS, HEAD_DIM)
    o_nsa, o_moba, w_ffn_out_b, w_out_b, w_ffn_in_b = _attention(
        yT, ystd, kcvc, vcT, gT, _block_onehot(T, SLC_BLOCK), _block_onehot(T, MOBA_BLOCK),
        on[:NSA_HEADS].reshape(G, NSA_GROUP, HEAD_DIM), on[NSA_HEADS:].reshape(MOBA_HEADS, 1, HEAD_DIM),
        [(w_ffn_out, 128), (w_out, 128), (w_ffn_in, 32)], B, T)

    x1 = _out_proj(o_nsa, o_moba, w_out_b, x2, mod3, T)
    return _ffn(x1, mod3, w_ffn_in_b, w_ffn_out_b, T)


def kernel(x, c, w_ada, b_ada, w_in, nsa_q_norm, nsa_k_norm, moba_q_norm, moba_k_norm, cmp_pe_k, cmp_w1_k, cmp_w2_k, cmp_pe_v, cmp_w1_v, cmp_w2_v, out_norm, w_out, w_ffn_in, w_ffn_out):
    B, T, D = x.shape
    x2 = x.reshape(B * T, D)
    for l in range(w_ada.shape[0]):
        x2 = _layer(x2, c, B, T, w_ada[l], b_ada[l], w_in[l], nsa_q_norm[l], nsa_k_norm[l],
                    moba_q_norm[l], moba_k_norm[l], cmp_pe_k[l], cmp_w1_k[l], cmp_w2_k[l],
                    cmp_pe_v[l], cmp_w1_v[l], cmp_w2_v[l], out_norm[l], w_out[l],
                    w_ffn_in[l], w_ffn_out[l])
    return x2.reshape(B, T, D)
```

```python
import functools

import jax
import jax.numpy as jnp
from jax import lax
from jax.experimental import pallas as pl
from jax.experimental.pallas import tpu as pltpu

F32 = jnp.float32
BF16 = jnp.bfloat16

HEAD_DIM = 128
SUBLANES = 8
NSA_HEADS = 8
NSA_KV_HEADS = 2
NSA_GROUP = NSA_HEADS // NSA_KV_HEADS
MOBA_HEADS = 8
N_HEADS = NSA_HEADS + MOBA_HEADS
CMP_BLOCK = 32
CMP_STRIDE = 16
SLC_BLOCK = 64
SLC_TOPK = 16
WINDOW = 512
FORCE_BONUS = 1e4
MOBA_BLOCK = 256
MOBA_TOPK = 3
ROPE_THETA = 500000.0
ROPE_DIMS = HEAD_DIM // 4
ROPE_HALF = ROPE_DIMS // 2
ROPE_SPLIT = 64
EPS = 1e-6
LOG2E = 1.4426950408889634
ONES_ROWS = 16
NEG = -1e30
M_INIT = -1e29

V7X_VMEM_BYTES = 64 * 1024 * 1024
VMEM_LIMIT = V7X_VMEM_BYTES - 8 * 1024 * 1024

PROJ_STEPS = ("nsa_q", "kv_cmp_slc", "kv_win", "moba_q", "moba_k", "moba_v")
WIDE_HEADS, NARROW_HEADS = 8, 4
OUT_BLOCK_HEADS = 8
YT_BLOCK = (0, 3, 3, 1, 1, 2)
YSTD_BLOCK = (1, 1, 1, 0, 0, 0)
GAIN_KINDS = ("nsa_q", "moba_q", "k_slc", "k_win", "moba_k")
PROJ_GAIN = {"nsa_q": "nsa_q", "moba_q": "moba_q", "kv_cmp_slc": "k_slc", "kv_win": "k_win", "moba_k": "moba_k"}
T_NSA_Q, T_MOBA_Q, T_MOBA_V, T_VSLC, T_VWIN = 0, 8, 16, 24, 26
S_MOBA_K, S_KSLC, S_KWIN = 0, 8, 10
N_T_HEADS, N_STD_HEADS, N_CMP_HEADS = 32, 16, 4
CHUNK = 128
GATE_LANES = 128
GATE_ROWS = 16

ATTN_TQ = 256
ATTN_TKS = 256
ATTN_BODY_PAIRS = (4, 2, 1)
NSA_TQ = ATTN_TQ
SLC_TKS = ATTN_TKS
WIN_SPAN = WINDOW + NSA_TQ
MOBA_TQ = ATTN_TQ
MOBA_TK = MOBA_BLOCK
MOBA_TKS = ATTN_TKS
MOBA_HB = 4


def _cparams(sem):
    return pltpu.CompilerParams(dimension_semantics=sem, vmem_limit_bytes=VMEM_LIMIT)


def _split3(a):
    hi = a.astype(BF16)
    r1 = a - hi.astype(F32)
    mid = r1.astype(BF16)
    lo = (r1 - mid.astype(F32)).astype(BF16)
    return hi, mid, lo


def _adaln_kernel(c_ref, w_ref, b_ref, o_ref):
    cv = c_ref[...]
    s = cv * jax.nn.sigmoid(cv)
    w = w_ref[...].astype(BF16)
    acc = b_ref[...] + jnp.zeros(o_ref.shape, F32)
    for part in _split3(s)[:2]:
        acc = acc + jnp.dot(part, w, preferred_element_type=F32)
    o_ref[...] = acc


def _adaln(c, w_ada, b_ada):
    B, D = c.shape
    N = w_ada.shape[1]
    tn = 1024
    c8 = jnp.zeros((8, D), F32).at[:B].set(c)
    out = pl.pallas_call(
        _adaln_kernel,
        grid=(N // tn,),
        in_specs=[pl.BlockSpec((8, D), lambda j: (0, 0)),
                  pl.BlockSpec((D, tn), lambda j: (0, j)),
                  pl.BlockSpec((1, tn), lambda j: (0, j))],
        out_specs=pl.BlockSpec((8, tn), lambda j: (0, j)),
        out_shape=jax.ShapeDtypeStruct((8, N), F32),
        compiler_params=_cparams(("arbitrary",)),
        name="adaln",
    )(c8, w_ada, b_ada.reshape(1, N))
    return out[:B].reshape(B, 6, D)


def _rope(y, cc, sa, sb):
    return (y * cc + pltpu.roll(y, HEAD_DIM - ROPE_HALF, 1) * sa + pltpu.roll(y, ROPE_HALF, 1) * sb)


def _inproj_kernel(x_ref, mod_ref, wq_ref, wkv_ref, wm_ref, wg_ref, gainT_ref, cosT_ref, sinT_ref,
                   yT_ref, ystd_ref, hc_ref, gT_ref, h_scr, rows_scr):
    j = pl.program_id(1)
    tm = x_ref.shape[0]
    n_chunks = tm // CHUNK
    pair = 2 * HEAD_DIM

    @pl.when(j == 0)
    def _():
        x = x_ref[...]
        ms = jnp.mean(x * x, axis=-1, keepdims=True)
        h = x * lax.rsqrt(ms + EPS) * (1.0 + mod_ref[0, 1:2, :]) + mod_ref[0, 0:1, :]
        hb = h.astype(BF16)
        h_scr[...] = hb
        g = jax.nn.sigmoid(jnp.dot(hb, wg_ref[...], preferred_element_type=F32))
        for grp in range(NSA_KV_HEADS):
            for c in range(n_chunks):
                blk = g[c * CHUNK:(c + 1) * CHUNK, grp * GATE_LANES:(grp + 1) * GATE_LANES].T
                gT_ref[grp, :, c * CHUNK:(c + 1) * CHUNK] = blk[0:GATE_ROWS, :]

    def head_pairs(w_ref, n_heads):
        for half in range(n_heads // 2):
            acc = jnp.dot(h_scr[...], w_ref[:, half * pair:(half + 1) * pair], preferred_element_type=F32)
            for h2 in range(2):
                yield 2 * half + h2, acc[:, h2 * HEAD_DIM:(h2 + 1) * HEAD_DIM]

    def qk_chunk(t, c):
        ms = jnp.mean(t * t, axis=0, keepdims=True)
        tn = t * lax.rsqrt(ms + EPS) * gainT_ref[0]
        cs = cosT_ref[:, c * CHUNK:(c + 1) * CHUNK]
        sn = sinT_ref[:, c * CHUNK:(c + 1) * CHUNK]
        a, b = tn[0:ROPE_HALF], tn[ROPE_HALF:ROPE_DIMS]
        return jnp.concatenate([a * cs - b * sn, b * cs + a * sn, tn[ROPE_DIMS:]], axis=0)

    def store_T(yh, slot, treated):
        for c in range(n_chunks):
            chunk = yh[c * CHUNK:(c + 1) * CHUNK, :]
            if treated:
                yT_ref[slot, c] = qk_chunk(chunk.T, c).astype(BF16)
            else:
                yT_ref[slot, c] = chunk.astype(BF16).T

    def store_rows(yh, slot):
        for c in range(n_chunks):
            rows = slice(c * CHUNK, (c + 1) * CHUNK)
            ystd_ref[slot, rows, :] = qk_chunk(yh[rows, :].T, c).astype(BF16).T

    def when_step(kind):
        return pl.when(j == PROJ_STEPS.index(kind))

    for kind, w_ref in (("nsa_q", wq_ref), ("moba_q", wm_ref)):
        @when_step(kind)
        def _(w_ref=w_ref):
            for hh, yh in head_pairs(w_ref, WIDE_HEADS):
                store_T(yh, hh, True)

    @when_step("moba_v")
    def _():
        for hh, yh in head_pairs(wm_ref, WIDE_HEADS):
            store_T(yh, hh, False)

    @when_step("moba_k")
    def _():
        for hh, yh in head_pairs(wm_ref, WIDE_HEADS):
            store_rows(yh, hh)

    def flatten_rows(yh, slot):
        rows_scr[...] = yh
        flat = [rows_scr[pl.ds(l, tm // CMP_STRIDE, stride=CMP_STRIDE), :] for l in range(CMP_STRIDE)]
        hc_ref[slot] = jnp.concatenate(flat, axis=1).astype(BF16)

    @when_step("kv_cmp_slc")
    def _():
        for slot in range(2 * NSA_KV_HEADS, OUT_BLOCK_HEADS):
            yT_ref[slot] = jnp.zeros(yT_ref.shape[1:], BF16)
            ystd_ref[slot] = jnp.zeros(ystd_ref.shape[1:], BF16)
        for hh, yh in head_pairs(wq_ref, WIDE_HEADS):
            if hh < N_CMP_HEADS:
                flatten_rows(yh, hh)
            elif hh < N_CMP_HEADS + NSA_KV_HEADS:
                store_rows(yh, hh - N_CMP_HEADS)
            else:
                store_T(yh, hh - N_CMP_HEADS - NSA_KV_HEADS, False)

    @when_step("kv_win")
    def _():
        for hh, yh in head_pairs(wkv_ref, NARROW_HEADS):
            if hh < NSA_KV_HEADS:
                store_rows(yh, NSA_KV_HEADS + hh)
            else:
                store_T(yh, hh, False)


def _in_proj(x2, mod3, w_nsa, w_moba, w_gate, gainsT, cosT, sinT, T):
    BT, D = x2.shape
    G = NSA_KV_HEADS
    tm = min(1024, T)
    wide, narrow = WIDE_HEADS * HEAD_DIM, NARROW_HEADS * HEAD_DIM
    tpb = T // tm
    pick = lambda j, table: sum((j == t).astype(jnp.int32) * v for t, v in enumerate(table))
    gain_kind = lambda j: pick(j, [GAIN_KINDS.index(PROJ_GAIN[s]) if s in PROJ_GAIN else 0 for s in PROJ_STEPS])
    win_block = (NSA_HEADS + N_CMP_HEADS + 2 * NSA_KV_HEADS) * HEAD_DIM // narrow
    first_moba = PROJ_STEPS.index("moba_q")
    return pl.pallas_call(
        _inproj_kernel,
        grid=(BT // tm, len(PROJ_STEPS)),
        in_specs=[pl.BlockSpec((tm, D), lambda i, j: (i, 0)),
                  pl.BlockSpec((1, 6, D), lambda i, j: (i // tpb, 0, 0)),
                  pl.BlockSpec((D, wide), lambda i, j: (0, jnp.minimum(j, 1))),
                  pl.BlockSpec((D, narrow), lambda i, j: (0, win_block)),
                  pl.BlockSpec((D, wide), lambda i, j: (0, jnp.clip(j - first_moba, 0, 2))),
                  pl.BlockSpec((D, G * GATE_LANES), lambda i, j: (0, 0)),
                  pl.BlockSpec((1, HEAD_DIM, CHUNK), lambda i, j: (gain_kind(j), 0, 0)),
                  pl.BlockSpec((ROPE_HALF, tm), lambda i, j: (0, i % tpb)),
                  pl.BlockSpec((ROPE_HALF, tm), lambda i, j: (0, i % tpb))],
        out_specs=[pl.BlockSpec((OUT_BLOCK_HEADS, tm // CHUNK, HEAD_DIM, CHUNK), lambda i, j: (pick(j, YT_BLOCK), i, 0, 0)),
                   pl.BlockSpec((OUT_BLOCK_HEADS, tm, HEAD_DIM), lambda i, j: (pick(j, YSTD_BLOCK), i, 0)),
                   pl.BlockSpec((N_CMP_HEADS, tm // CMP_STRIDE, CMP_STRIDE * HEAD_DIM), lambda i, j: (0, i, 0)),
                   pl.BlockSpec((G, GATE_ROWS, tm), lambda i, j: (0, 0, i))],
        out_shape=[jax.ShapeDtypeStruct((N_T_HEADS, BT // CHUNK, HEAD_DIM, CHUNK), BF16),
                   jax.ShapeDtypeStruct((N_STD_HEADS, BT, HEAD_DIM), BF16),
                   jax.ShapeDtypeStruct((N_CMP_HEADS, BT // CMP_STRIDE, CMP_STRIDE * HEAD_DIM), BF16),
                   jax.ShapeDtypeStruct((G, GATE_ROWS, BT), F32)],
        scratch_shapes=[pltpu.VMEM((tm, D), BF16), pltpu.VMEM((tm, HEAD_DIM), F32)],
        compiler_params=_cparams(("parallel", "arbitrary")),
        name="in_proj",
    )(x2, mod3, w_nsa, w_nsa, w_moba, w_gate, gainsT, cosT, sinT)


def _compress_kernel(h_ref, w1c_ref, pe_ref, w1_ref, w2_ref, gain_ref, cc_ref, sa_ref, sb_ref, o_ref):
    a = pl.program_id(0)
    ncp = h_ref.shape[2]
    z = jnp.dot(h_ref[0, 0], w1c_ref[0], preferred_element_type=F32)
    top = z[:, :HEAD_DIM]
    bot = pltpu.roll(z[:, HEAD_DIM:], ncp - 1, 0)
    pe_term = jnp.dot(pe_ref[0], w1_ref[0], preferred_element_type=F32)[0:1, :]
    pre = top + bot + pe_term
    act = pre * jax.nn.sigmoid(pre)
    out = jnp.dot(act.astype(BF16), w2_ref[0], preferred_element_type=F32)
    live = lax.broadcasted_iota(jnp.int32, out.shape, 0) < ncp - 1
    out = jnp.where(live, out, 0.0)

    @pl.when(a < NSA_KV_HEADS)
    def _():
        ms = jnp.mean(out * out, axis=-1, keepdims=True)
        yn = out * lax.rsqrt(ms + EPS) * gain_ref[...]
        o_ref[0, 0] = _rope(yn, cc_ref[...], sa_ref[...], sb_ref[...]).astype(BF16)

    @pl.when(a >= NSA_KV_HEADS)
    def _():
        o_ref[0, 0] = out.astype(BF16)


def _compress(hc, w1cat, pe8, w1, w2, gain, cc, sa, sb):
    A, B, ncp, K = hc.shape
    G = NSA_KV_HEADS
    return pl.pallas_call(
        _compress_kernel,
        grid=(A, B),
        in_specs=[pl.BlockSpec((1, 1, ncp, K), lambda a, b: (a, b, 0, 0)),
                  pl.BlockSpec((1, K, 2 * HEAD_DIM), lambda a, b: (a // G, 0, 0)),
                  pl.BlockSpec((1, 8, 2 * K), lambda a, b: (a // G, 0, 0)),
                  pl.BlockSpec((1, 2 * K, HEAD_DIM), lambda a, b: (a // G, 0, 0)),
                  pl.BlockSpec((1, HEAD_DIM, HEAD_DIM), lambda a, b: (a // G, 0, 0)),
                  pl.BlockSpec((1, HEAD_DIM), lambda a, b: (0, 0)),
                  pl.BlockSpec((ncp, HEAD_DIM), lambda a, b: (0, 0)),
                  pl.BlockSpec((ncp, HEAD_DIM), lambda a, b: (0, 0)),
                  pl.BlockSpec((ncp, HEAD_DIM), lambda a, b: (0, 0))],
        out_specs=pl.BlockSpec((1, 1, ncp, HEAD_DIM), lambda a, b: (a, b, 0, 0)),
        out_shape=jax.ShapeDtypeStruct((A, B, ncp, HEAD_DIM), BF16),
        compiler_params=_cparams(("arbitrary", "arbitrary")),
        name="compress",
    )(hc, w1cat, pe8, w1, w2, gain, cc, sa, sb)


def _first_step(s, vT, m_scr, acc_scr):
    m = jnp.max(s, axis=0, keepdims=True)
    p = jnp.exp2(s - m)
    acc_scr[...] = jnp.dot(_with_ones(vT), p.astype(BF16), preferred_element_type=F32)
    m_scr[...] = m


def _with_ones(vT):
    return jnp.concatenate([vT, jnp.ones((ONES_ROWS, vT.shape[1]), vT.dtype)], axis=0)


def _normalized(acc):
    return acc[0:HEAD_DIM] / acc[HEAD_DIM:HEAD_DIM + 1]


def _online_step(s, vT, m_scr, acc_scr):
    m_prev = m_scr[...]
    m_new = jnp.maximum(m_prev, jnp.max(s, axis=0, keepdims=True))
    alpha = jnp.exp2(m_prev - m_new)
    p = jnp.exp2(s - m_new)
    acc_scr[...] = alpha * acc_scr[...] + jnp.dot(_with_ones(vT), p.astype(BF16), preferred_element_type=F32)
    m_scr[...] = m_new


def _rank_select(score, k):
    n = score.shape[0]
    rank = jnp.zeros(score.shape, F32)
    for m in range(n):
        sm = score[m:m + 1, :]
        lo = (m // SUBLANES) * SUBLANES
        hi = min(lo + SUBLANES, n)
        parts = []
        if lo > 0:
            parts.append(jnp.where(sm > score[:lo], 1.0, 0.0))
        gt = jnp.where(sm > score[lo:hi], 1.0, 0.0)
        ge = jnp.where(sm >= score[lo:hi], 1.0, 0.0)
        below = lax.broadcasted_iota(jnp.int32, gt.shape, 0) > m - lo
        parts.append(jnp.where(below, ge, gt))
        if hi < n:
            parts.append(jnp.where(sm >= score[hi:], 1.0, 0.0))
        rank = rank + (jnp.concatenate(parts, axis=0) if len(parts) > 1 else parts[0])
    return rank < k


def _pad_rows(a, rows):
    return jnp.concatenate([a, jnp.zeros((rows - a.shape[0], a.shape[1]), a.dtype)], axis=0)


def _masked_scores(k_ref, e_ref, start, size, q_aug):
    rows = pl.ds(pl.multiple_of(start, 128), size)
    k_aug = jnp.concatenate([k_ref[rows, :], e_ref[rows, :]], axis=1)
    return jnp.dot(k_aug, q_aug, preferred_element_type=F32)


def _pipelined_attention(chains, n_steps, n_max, tks, tk, body_pairs):
    per = tks // tk

    def scores(step, buf):
        st = jnp.minimum(step, n_max - 1)
        dead = jnp.where(step < n_steps, 0, 1)
        for k_ref, e_ref, _, qaug_ref, bufs, _, _ in chains:
            bufs[buf][...] = _masked_scores(k_ref, e_ref, st * tks, tks, qaug_ref[dead])

    def consume(step, buf):
        st = jnp.minimum(step, n_max - 1)
        for _, _, vT_ref, _, bufs, m_scr, acc_scr in chains:
            vT = jnp.concatenate([vT_ref[st * per + a] for a in range(per)], axis=1)
            _online_step(bufs[buf][...], vT, m_scr, acc_scr)

    def pairs(first_step, n_pairs_in_body):
        def body(it, carry):
            base = first_step + it * 2 * n_pairs_in_body
            for p in range(n_pairs_in_body):
                scores(base + 2 * p + 1, 1)
                consume(base + 2 * p, 0)
                scores(base + 2 * p + 2, 0)
                consume(base + 2 * p + 1, 1)
            return carry
        return body

    scores(0, 0)
    done = 0
    for n_body in body_pairs[:-1]:
        n_iter = (n_steps - done) // (2 * n_body)
        lax.fori_loop(0, n_iter, pairs(done, n_body), 0)
        done = done + n_iter * 2 * n_body
    lax.fori_loop(0, (n_steps - done + 1) // 2, pairs(done, 1), 0)


def _carry_casts(kernel_fn, n_in, n_out, n_cast):
    def wrapped(*refs):
        ins, cast_in = refs[:n_in], refs[n_in:n_in + n_cast]
        outs = refs[n_in + n_cast:n_in + n_cast + n_out]
        cast_out = refs[n_in + n_cast + n_out:n_in + n_cast + n_out + n_cast]
        for src, dst in zip(cast_in, cast_out):
            dst[...] = src[...].astype(BF16)
        kernel_fn(*ins, *outs, *refs[n_in + n_cast + n_out + n_cast:])
    return wrapped


def _cast_specs(weights, grid):
    n_steps = grid[0] * grid[1] * grid[2]
    specs, shapes = [], []
    for w, rows in weights:
        while w.shape[0] // rows > n_steps:
            rows *= 2
        n_blk = w.shape[0] // rows
        assert w.shape[0] % rows == 0
        index = lambda a, b, c, n_blk=n_blk: (jnp.minimum((a * grid[1] + b) * grid[2] + c, n_blk - 1), 0)
        specs.append(pl.BlockSpec((rows, w.shape[1]), index))
        shapes.append(jax.ShapeDtypeStruct(w.shape, BF16))
    return specs, shapes


def _nsa_body(qT_ref, kc_ref, vcT_ref, ks_ref, vsT_ref, kw_ref, vwT_ref, gT_ref, e_ref,
              gain_ref, o_ref, m_scr, acc_scr, comb_scr, qaug_scr, s0_scr, s1_scr, psum_scr, *, T):
    R, tq = NSA_GROUP, NSA_TQ
    i = pl.program_id(2)
    t0 = i * tq
    ncp = T // CMP_STRIDE
    nsb = T // SLC_BLOCK
    qT = jnp.concatenate([qT_ref[r, c] for r in range(R) for c in range(tq // CHUNK)], axis=1)
    gT = gT_ref[0]

    def per_head(row0):
        return jnp.concatenate([gT[row0 + 3 * r:row0 + 3 * r + 1, :] for r in range(R)], axis=1)

    def lanes_x_heads(a):
        return jnp.concatenate([a] * R, axis=1)

    s = jnp.dot(kc_ref[0, 0], qT, preferred_element_type=F32)
    c_idx = lax.broadcasted_iota(jnp.int32, (ncp, tq), 0)
    t_c = t0 + lax.broadcasted_iota(jnp.int32, (ncp, tq), 1)
    vis = lanes_x_heads((c_idx * CMP_STRIDE + (CMP_BLOCK - 1) <= t_c) & (c_idx < ncp - 1))
    s = jnp.where(vis, s, NEG)
    m = jnp.maximum(jnp.max(s, axis=0, keepdims=True), M_INIT)
    p = jnp.exp2(s - m)
    l = jnp.sum(p, axis=0, keepdims=True)
    p = p / jnp.where(l > 0.0, l, 1.0)
    o_cmp = jnp.dot(vcT_ref[0, 0], p.astype(BF16), preferred_element_type=F32)
    comb_scr[...] = o_cmp * per_head(0)

    psum = p[:, 0:tq]
    for r in range(1, R):
        psum = psum + p[:, r * tq:(r + 1) * tq]
    per = SLC_BLOCK // CMP_STRIDE
    imp_chunks = []
    for c in range(tq // CHUNK):
        psum_scr[c, 0:SUBLANES, :] = jnp.zeros((SUBLANES, CHUNK), F32)
        psum_scr[c, SUBLANES:, :] = psum[:, c * CHUNK:(c + 1) * CHUNK]
        acc = jnp.zeros((nsb, CHUNK), F32)
        for k in range(1 - CMP_BLOCK // CMP_STRIDE, per):
            acc = acc + psum_scr[c, pl.ds(SUBLANES + k, nsb, stride=per), :]
        imp_chunks.append(acc)
    imp = jnp.concatenate(imp_chunks, axis=1)
    n_idx = lax.broadcasted_iota(jnp.int32, (nsb, tq), 0)
    cur = (t0 + lax.broadcasted_iota(jnp.int32, (nsb, tq), 1)) // SLC_BLOCK
    forced = (n_idx == 0) | (n_idx == cur) | (n_idx == cur - 1)
    valid = n_idx <= cur
    score = jnp.where(valid, imp + jnp.where(forced, FORCE_BONUS, 0.0), NEG)
    sel = _rank_select(score, min(SLC_TOPK, nsb)) & valid
    sel_past = sel & (n_idx * SLC_BLOCK < t0)
    bias = _pad_rows(jnp.where(sel_past, 0.0, NEG), HEAD_DIM).astype(BF16)
    bias_own = _pad_rows(jnp.where(sel, 0.0, NEG), HEAD_DIM).astype(BF16)
    for d in range(3):
        qaug_scr[d, 0:HEAD_DIM, :] = qT
    qaug_scr[0, HEAD_DIM:, :] = lanes_x_heads(bias)
    qaug_scr[1, HEAD_DIM:, :] = jnp.full((HEAD_DIM, R * tq), NEG, BF16)
    qaug_scr[2, HEAD_DIM:, :] = lanes_x_heads(bias_own)

    w0 = jnp.maximum(t0 + tq - WIN_SPAN, 0)
    k_w = kw_ref[0, pl.ds(pl.multiple_of(w0, CHUNK), WIN_SPAN), :]
    s_w = jnp.dot(k_w, qT, preferred_element_type=F32)
    kpos = w0 + lax.broadcasted_iota(jnp.int32, (WIN_SPAN, tq), 0)
    tpos = t0 + lax.broadcasted_iota(jnp.int32, (WIN_SPAN, tq), 1)
    ok = lanes_x_heads((kpos <= tpos) & (tpos - kpos < WINDOW))
    s_w = jnp.where(ok, s_w, NEG)
    m_w = jnp.max(s_w, axis=0, keepdims=True)
    p_w = jnp.exp2(s_w - m_w)
    jw = w0 // CHUNK
    v_w = jnp.concatenate([vwT_ref[0, jw + a] for a in range(WIN_SPAN // CHUNK)], axis=1)
    o_w = jnp.dot(_with_ones(v_w), p_w.astype(BF16), preferred_element_type=F32)
    comb_scr[...] += _normalized(o_w) * per_head(2)

    tri =lax.broadcasted_iota(jnp.int32, (tq, tq), 0) <= lax.broadcasted_iota(jnp.int32, (tq, tq), 1)
    s_d = jnp.where(lanes_x_heads(tri), _masked_scores(ks_ref.at[0], e_ref, t0, tq, qaug_scr[2]), NEG)
    own = tq // CHUNK
    v_d = jnp.concatenate([vsT_ref[0, i * own + a] for a in range(own)], axis=1)
    _first_step(s_d, v_d, m_scr, acc_scr)
    yield [(ks_ref.at[0], e_ref, vsT_ref.at[0], qaug_scr, (s0_scr, s1_scr), m_scr, acc_scr)]
    comb =comb_scr[...] + _normalized(acc_scr[...]) * per_head(1)

    for r in range(R):
        oT = comb[:, r * tq:(r + 1) * tq]
        ms = jnp.mean(oT * oT, axis=0, keepdims=True)
        on = (oT * lax.rsqrt(ms + EPS)).T * gain_ref[0, r:r + 1, :]
        o_ref[:, r * HEAD_DIM:(r + 1) * HEAD_DIM] = on.astype(BF16)


def _moba_body(qT_ref, k_ref, vT_ref, e_ref, gain_ref, o_ref, m_scr, acc_scr, kmean_scr, qaug_scr,
               s_scr, *, T):
    tq, tk = MOBA_TQ, MOBA_TK
    nb = T // MOBA_BLOCK
    nbp = kmean_scr.shape[1]
    i = pl.program_id(2)
    t0 = i * tq

    @pl.when(i == 0)
    def _():
        kmean_scr[...] = jnp.zeros(kmean_scr.shape, F32)
        for a in range(MOBA_HB):
            kb = k_ref[a].astype(F32).reshape(nb, MOBA_BLOCK, HEAD_DIM)
            kmean_scr[a, 0:nb, :] = jnp.mean(kb, axis=1)

    n_idx = lax.broadcasted_iota(jnp.int32, (nbp, tq), 0)
    cur = (t0 + lax.broadcasted_iota(jnp.int32, (nbp, tq), 1)) // MOBA_BLOCK
    past = n_idx < cur
    causal = lax.broadcasted_iota(jnp.int32, (tk, tq), 0) <= lax.broadcasted_iota(jnp.int32, (tk, tq), 1)

    qTs = [jnp.concatenate([qT_ref[a, c] for c in range(tq // CHUNK)], axis=1) for a in range(MOBA_HB)]
    for a in range(MOBA_HB):
        qT = qTs[a]
        gate = jnp.zeros((nbp, tq), F32)
        for part in _split3(kmean_scr[a]):
            gate = gate + jnp.dot(part, qT, preferred_element_type=F32)
        sel = _rank_select(jnp.where(past, gate, NEG), min(MOBA_TOPK, nb)) & past
        for d in range(2):
            qaug_scr[a, d, 0:HEAD_DIM, :] = qT
        qaug_scr[a, 0, HEAD_DIM:, :] = _pad_rows(jnp.where(sel, 0.0, NEG), HEAD_DIM).astype(BF16)
        qaug_scr[a, 1, HEAD_DIM:, :] = jnp.full((HEAD_DIM, tq), NEG, BF16)

    s_own = []
    for a in range(MOBA_HB):
        m_scr[a] = jnp.full(m_scr.shape[1:], M_INIT, F32)
        acc_scr[a] = jnp.zeros(acc_scr.shape[1:], F32)
        k_own = k_ref[a, pl.ds(pl.multiple_of(t0, tk), tk), :]
        s_own.append(jnp.where(causal, jnp.dot(k_own, qTs[a], preferred_element_type=F32), NEG))

    yield [(k_ref.at[a], e_ref, vT_ref.at[a], qaug_scr.at[a], (s_scr.at[a, 0], s_scr.at[a, 1]),
            m_scr.at[a], acc_scr.at[a]) for a in range(MOBA_HB)]

    own = tq // CHUNK
    for a in range(MOBA_HB):
        v_own = jnp.concatenate([vT_ref[a, i * own + c] for c in range(own)], axis=1)
        _online_step(s_own[a], v_own, m_scr.at[a], acc_scr.at[a])
        oT = _normalized(acc_scr[a])
        ms = jnp.mean(oT * oT, axis=0, keepdims=True)
        on = (oT * lax.rsqrt(ms + EPS)).T * gain_ref[a]
        o_ref[:, a * HEAD_DIM:(a + 1) * HEAD_DIM] = on.astype(BF16)


N_NSA_IN, N_NSA_SCRATCH, N_MOBA_IN = 10, 7, 5


def _attention_kernel(*refs, T):
    nsa_in, moba_in = refs[:N_NSA_IN], refs[N_NSA_IN:N_NSA_IN + N_MOBA_IN]
    o_nsa, o_moba = refs[N_NSA_IN + N_MOBA_IN:N_NSA_IN + N_MOBA_IN + 2]
    scratch = refs[N_NSA_IN + N_MOBA_IN + 2:]
    moba = _moba_body(*moba_in, o_moba, *scratch[N_NSA_SCRATCH:], T=T)
    nsa = _nsa_body(*nsa_in, o_nsa, *scratch[:N_NSA_SCRATCH], T=T)
    chains = next(moba) + next(nsa)
    n_steps = pl.program_id(2) * ATTN_TQ // ATTN_TKS
    _pipelined_attention(chains, n_steps, T // ATTN_TKS, ATTN_TKS, CHUNK, ATTN_BODY_PAIRS)
    for rest in (nsa, moba):
        assert next(rest, None) is None


def _attention(yT, ystd, kcvc, vcT, gT, e_slc, e_moba, nsa_gains, moba_gains, casts, B, T):
    G, R, tq = NSA_KV_HEADS, NSA_GROUP, ATTN_TQ
    H, HB = MOBA_HEADS, MOBA_HB
    assert H // HB == G and NSA_TQ == MOBA_TQ == tq and SLC_TKS == MOBA_TKS == ATTN_TKS and tq % ATTN_TKS == 0
    nq = T // tq
    ncp = T // CMP_STRIDE
    nb = T // MOBA_BLOCK
    nbp = max(16, nb)
    grid = (B, G, nq)
    cast_specs, cast_shapes = _cast_specs(casts, grid)
    kern = _carry_casts(functools.partial(_attention_kernel, T=T), N_NSA_IN + N_MOBA_IN, 2, len(casts))
    keys = lambda head0: pl.BlockSpec((1, T, HEAD_DIM), lambda b, g, i: (head0 + g, b, 0))
    values = lambda head0: pl.BlockSpec((1, T // CHUNK, HEAD_DIM, CHUNK), lambda b, g, i: (head0 + g, b, 0, 0))
    onehot = pl.BlockSpec((T, HEAD_DIM), lambda b, g, i: (0, 0))
    nsa_specs = [pl.BlockSpec((R, tq // CHUNK, HEAD_DIM, CHUNK), lambda b, g, i: (T_NSA_Q // R + g, b * nq + i, 0, 0)),
                 pl.BlockSpec((1, 1, ncp, HEAD_DIM), lambda b, g, i: (g, b, 0, 0)),
                 pl.BlockSpec((1, 1, HEAD_DIM, ncp), lambda b, g, i: (g, b, 0, 0)),
                 keys(S_KSLC), values(T_VSLC), keys(S_KWIN), values(T_VWIN),
                 pl.BlockSpec((1, GATE_ROWS, tq), lambda b, g, i: (g, 0, b * nq + i)),
                 onehot,
                 pl.BlockSpec((1, R, HEAD_DIM), lambda b, g, i: (g, 0, 0))]
    moba_specs = [pl.BlockSpec((HB, tq // CHUNK, HEAD_DIM, CHUNK), lambda b, h, i: (T_MOBA_Q // HB + h, b * nq + i, 0, 0)),
                  pl.BlockSpec((HB, T, HEAD_DIM), lambda b, h, i: (S_MOBA_K // HB + h, b, 0)),
                  pl.BlockSpec((HB, T // CHUNK, HEAD_DIM, CHUNK), lambda b, h, i: (T_MOBA_V // HB + h, b, 0, 0)),
                  onehot,
                  pl.BlockSpec((HB, 1, HEAD_DIM), lambda b, h, i: (h, 0, 0))]
    nsa_scratch = [pltpu.VMEM((1, R * tq), F32),
                   pltpu.VMEM((HEAD_DIM + ONES_ROWS, R * tq), F32), pltpu.VMEM((HEAD_DIM, R * tq), F32),
                   pltpu.VMEM((3, 2 * HEAD_DIM, R * tq), BF16),
                   pltpu.VMEM((ATTN_TKS, R * tq), F32), pltpu.VMEM((ATTN_TKS, R * tq), F32),
                   pltpu.VMEM((tq // CHUNK, SUBLANES + ncp, CHUNK), F32)]
    moba_scratch = [pltpu.VMEM((HB, 1, tq), F32),
                    pltpu.VMEM((HB, HEAD_DIM + ONES_ROWS, tq), F32), pltpu.VMEM((HB, nbp, HEAD_DIM), F32),
                    pltpu.VMEM((HB, 2, 2 * HEAD_DIM, tq), BF16),
                    pltpu.VMEM((HB, 2, ATTN_TKS, tq), F32)]
    assert len(nsa_specs) == N_NSA_IN and len(moba_specs) == N_MOBA_IN and len(nsa_scratch) == N_NSA_SCRATCH
    return pl.pallas_call(
        kern,
        grid=grid,
        in_specs=nsa_specs + moba_specs + cast_specs,
        out_specs=[pl.BlockSpec((tq, R * HEAD_DIM), lambda b, g, i: (b * nq + i, g)),
                   pl.BlockSpec((tq, HB * HEAD_DIM), lambda b, h, i: (b * nq + i, h))] + cast_specs,
        out_shape=[jax.ShapeDtypeStruct((B * T, NSA_HEADS * HEAD_DIM), BF16),
                   jax.ShapeDtypeStruct((B * T, H * HEAD_DIM), BF16)] + cast_shapes,
        scratch_shapes=nsa_scratch + moba_scratch,
        compiler_params=_cparams(("parallel", "parallel", "arbitrary")),
        name="attention",
    )(yT, kcvc, vcT, ystd, yT, ystd, yT, gT, e_slc, nsa_gains,
      yT, ystd, yT, e_moba, moba_gains, *[w for w, _ in casts])


def _outproj_kernel(on_ref, om_ref, w_ref, x_ref, mod_ref, o_ref):
    half = on_ref.shape[1]
    acc = jnp.dot(on_ref[...], w_ref[0:half, :], preferred_element_type=F32)
    acc = acc + jnp.dot(om_ref[...], w_ref[half:, :], preferred_element_type=F32)
    o_ref[...] = x_ref[...] + mod_ref[0, 2:3, :] * acc


def _out_proj(o_nsa, o_moba, w_out, x2, mod3, T):
    BT, D = x2.shape
    tm = min(512, T)
    tpb = T // tm
    half = o_nsa.shape[1]
    return pl.pallas_call(
        _outproj_kernel,
        grid=(BT // tm,),
        in_specs=[pl.BlockSpec((tm, half), lambda i: (i, 0)),
                  pl.BlockSpec((tm, half), lambda i: (i, 0)),
                  pl.BlockSpec((D, D), lambda i: (0, 0)),
                  pl.BlockSpec((tm, D), lambda i: (i, 0)),
                  pl.BlockSpec((1, 6, D), lambda i: (i // tpb, 0, 0))],
        out_specs=pl.BlockSpec((tm, D), lambda i: (i, 0)),
        out_shape=jax.ShapeDtypeStruct((BT, D), F32),
        compiler_params=_cparams(("parallel",)),
        name="out_proj",
    )(o_nsa, o_moba, w_out, x2, mod3)


def _ffn_kernel(x_ref, mod_ref, wg_ref, wu_ref, wo_ref, o_ref, h_scr):
    j = pl.program_id(1)

    @pl.when(j == 0)
    def _():
        x = x_ref[...]
        ms = jnp.mean(x * x, axis=-1, keepdims=True)
        h = x * lax.rsqrt(ms + EPS) * (1.0 + mod_ref[0, 4:5, :]) + mod_ref[0, 3:4, :]
        h_scr[...] = h.astype(BF16)
        o_ref[...] = jnp.zeros(o_ref.shape, F32)

    hb = h_scr[...]
    half = wg_ref.shape[1] // 2
    down = None
    for c in range(2):
        cols = slice(c * half, (c + 1) * half)
        gate = jnp.dot(hb, wg_ref[:, cols], preferred_element_type=F32)
        up = jnp.dot(hb, wu_ref[:, cols], preferred_element_type=F32)
        act = (gate * jax.nn.sigmoid(gate) * up).astype(BF16)
        part = jnp.dot(act, wo_ref[cols, :], preferred_element_type=F32)
        down = part if down is None else down + part
    o_ref[...] += down

    @pl.when(j == pl.num_programs(1) - 1)
    def _():
        o_ref[...] = x_ref[...] + mod_ref[0, 5:6, :] * o_ref[...]


def _ffn(x1, mod3, w_in, w_out, T):
    BT, D = x1.shape
    Fh = w_out.shape[0]
    tm = min(512, T)
    tf = 512
    tpb = T // tm
    nf = Fh // tf
    return pl.pallas_call(
        _ffn_kernel,
        grid=(BT // tm, nf),
        in_specs=[pl.BlockSpec((tm, D), lambda i, j: (i, 0)),
                  pl.BlockSpec((1, 6, D), lambda i, j: (i // tpb, 0, 0)),
                  pl.BlockSpec((D, tf), lambda i, j: (0, j)),
                  pl.BlockSpec((D, tf), lambda i, j: (0, nf + j)),
                  pl.BlockSpec((tf, D), lambda i, j: (j, 0))],
        out_specs=pl.BlockSpec((tm, D), lambda i, j: (i, 0)),
        out_shape=jax.ShapeDtypeStruct((BT, D), F32),
        scratch_shapes=[pltpu.VMEM((tm, D), BF16)],
        compiler_params=_cparams(("parallel", "arbitrary")),
        name="ffn",
    )(x1, mod3, w_in, w_in, w_out)


def _rope_cos_sin(pos):
    inv = ROPE_THETA ** (-jnp.arange(0, ROPE_DIMS, 2, dtype=F32) / ROPE_DIMS)
    ang = pos.astype(F32)[:, None] * inv[None, :]
    return jnp.cos(ang), jnp.sin(ang)


def _rope_tables(T):
    c_hi, s_hi = _rope_cos_sin(jnp.arange(0, T, ROPE_SPLIT))
    c_lo, s_lo = _rope_cos_sin(jnp.arange(ROPE_SPLIT))
    cos = (c_hi[:, None] * c_lo[None] - s_hi[:, None] * s_lo[None]).reshape(T, ROPE_HALF)
    sin = (s_hi[:, None] * c_lo[None] + c_hi[:, None] * s_lo[None]).reshape(T, ROPE_HALF)
    cos_c, sin_c = _rope_cos_sin(jnp.arange(CMP_BLOCK - 1, T, CMP_STRIDE))
    n, rest = cos_c.shape[0], HEAD_DIM - ROPE_DIMS
    pad = lambda t: jnp.pad(t, ((0, 1), (0, 0)))
    cc = pad(jnp.concatenate([cos_c, cos_c, jnp.ones((n, rest), F32)], axis=1))
    sa = pad(jnp.concatenate([-sin_c, jnp.zeros((n, HEAD_DIM - ROPE_HALF), F32)], axis=1))
    sb = pad(jnp.concatenate([jnp.zeros((n, ROPE_HALF), F32), sin_c, jnp.zeros((n, rest), F32)], axis=1))
    return cc, sa, sb, cos.T, sin.T


def _block_onehot(T, block):
    return (jnp.arange(T)[:, None] // block == jnp.arange(HEAD_DIM)[None, :]).astype(BF16)


def _layer(x2, c, B, T, w_ada, b_ada, w_in, nsa_q_norm, nsa_k_norm, moba_q_norm, moba_k_norm,
           cmp_pe_k, cmp_w1_k, cmp_w2_k, cmp_pe_v, cmp_w1_v, cmp_w2_v, out_norm, w_out,
           w_ffn_in, w_ffn_out):
    D = x2.shape[1]
    G = NSA_KV_HEADS
    scale = HEAD_DIM ** -0.5 * LOG2E
    assert T % MOBA_BLOCK == 0 and T % SLC_TKS == 0 and T % MOBA_TKS == 0 and T % NSA_TQ == 0 and T >= WIN_SPAN
    assert T // SLC_BLOCK <= HEAD_DIM and T // MOBA_BLOCK <= HEAD_DIM

    nsa_w = NSA_HEADS * HEAD_DIM + 6 * G * HEAD_DIM
    gw = NSA_HEADS * 3
    mod3 = _adaln(c, w_ada, b_ada)
    w_nsa = w_in.astype(BF16)
    w_moba = w_nsa[:, nsa_w + gw:]

    wg = w_in[:, nsa_w:nsa_w + gw].reshape(D, G, NSA_GROUP * 3)
    w_gate = jnp.pad(wg, ((0, 0), (0, 0), (0, GATE_LANES - NSA_GROUP * 3))).reshape(D, G * GATE_LANES)
    w_gate = w_gate.astype(BF16)
    across = lambda g_: jnp.broadcast_to(g_[:, None], (HEAD_DIM, CHUNK))
    gainsT = jnp.stack([across(nsa_q_norm * scale), across(moba_q_norm * scale),
                        across(nsa_k_norm[1]), across(nsa_k_norm[2]), across(moba_k_norm)])
    cc, sa, sb, cosT, sinT = _rope_tables(T)

    yT, ystd, hc, gT = _in_proj(x2, mod3, w_nsa, w_moba, w_gate, gainsT, cosT, sinT, T)

    ncp = T // CMP_STRIDE
    half = CMP_STRIDE * HEAD_DIM
    w1 = jnp.stack([cmp_w1_k, cmp_w1_v]).astype(BF16)
    w1cat = jnp.concatenate([w1[:, :half], w1[:, half:]], axis=2)
    pe8 = jnp.broadcast_to(jnp.stack([cmp_pe_k, cmp_pe_v]).reshape(2, 1, 2 * half), (2, 8, 2 * half))
    w2 = jnp.stack([cmp_w2_k, cmp_w2_v]).astype(BF16)
    kcvc = _compress(hc.reshape(2 * G, B, ncp, half), w1cat, pe8.astype(BF16), w1, w2,
                     nsa_k_norm[0].reshape(1, HEAD_DIM), cc, sa, sb)
    vcT = kcvc[G:].transpose(0, 1, 3, 2)

    on = out_norm.reshape(N_HEADS, HEAD_DIM)
    o_nsa, o_moba, w_ffn_out_b, w_out_b, w_ffn_in_b = _attention(
        yT, ystd, kcvc, vcT, gT, _block_onehot(T, SLC_BLOCK), _block_onehot(T, MOBA_BLOCK),
        on[:NSA_HEADS].reshape(G, NSA_GROUP, HEAD_DIM), on[NSA_HEADS:].reshape(MOBA_HEADS, 1, HEAD_DIM),
        [(w_ffn_out, 128), (w_out, 128), (w_ffn_in, 32)], B, T)

    x1 = _out_proj(o_nsa, o_moba, w_out_b, x2, mod3, T)
    return _ffn(x1, mod3, w_ffn_in_b, w_ffn_out_b, T)


def kernel(x, c, w_ada, b_ada, w_in, nsa_q_norm, nsa_k_norm, moba_q_norm, moba_k_norm, cmp_pe_k, cmp_w1_k, cmp_w2_k, cmp_pe_v, cmp_w1_v, cmp_w2_v, out_norm, w_out, w_ffn_in, w_ffn_out):
    B, T, D = x.shape
    x2 = x.reshape(B * T, D)
    for l in range(w_ada.shape[0]):
        x2 = _layer(x2, c, B, T, w_ada[l], b_ada[l], w_in[l], nsa_q_norm[l], nsa_k_norm[l],
                    moba_q_norm[l], moba_k_norm[l], cmp_pe_k[l], cmp_w1_k[l], cmp_w2_k[l],
                    cmp_pe_v[l], cmp_w1_v[l], cmp_w2_v[l], out_norm[l], w_out[l],
                    w_ffn_in[l], w_ffn_out[l])
    return x2.reshape(B, T, D)
```

```python
import functools

import jax
import jax.numpy as jnp
from jax import lax
from jax.experimental import pallas as pl
from jax.experimental.pallas import tpu as pltpu

F32 = jnp.float32
BF16 = jnp.bfloat16

HEAD_DIM = 128
SUBLANES = 8
NSA_HEADS = 8
NSA_KV_HEADS = 2
NSA_GROUP = NSA_HEADS // NSA_KV_HEADS
MOBA_HEADS = 8
N_HEADS = NSA_HEADS + MOBA_HEADS
CMP_BLOCK = 32
CMP_STRIDE = 16
SLC_BLOCK = 64
SLC_TOPK = 16
WINDOW = 512
FORCE_BONUS = 1e4
MOBA_BLOCK = 256
MOBA_TOPK = 3
ROPE_THETA = 500000.0
ROPE_DIMS = HEAD_DIM // 4
ROPE_HALF = ROPE_DIMS // 2
ROPE_SPLIT = 64
EPS = 1e-6
LOG2E = 1.4426950408889634
ONES_ROWS = 16
NEG = -1e30
M_INIT = -1e29

V7X_VMEM_BYTES = 64 * 1024 * 1024
VMEM_LIMIT = V7X_VMEM_BYTES - 8 * 1024 * 1024

PROJ_STEPS = ("nsa_q", "kv_cmp_slc", "kv_win", "moba_q", "moba_k", "moba_v")
WIDE_HEADS, NARROW_HEADS = 8, 4
OUT_BLOCK_HEADS = 8
YT_BLOCK = (0, 3, 3, 1, 1, 2)
YSTD_BLOCK = (1, 1, 1, 0, 0, 0)
GAIN_KINDS = ("nsa_q", "moba_q", "k_slc", "k_win", "moba_k")
PROJ_GAIN = {"nsa_q": "nsa_q", "moba_q": "moba_q", "kv_cmp_slc": "k_slc", "kv_win": "k_win", "moba_k": "moba_k"}
T_NSA_Q, T_MOBA_Q, T_MOBA_V, T_VSLC, T_VWIN = 0, 8, 16, 24, 26
S_MOBA_K, S_KSLC, S_KWIN = 0, 8, 10
N_T_HEADS, N_STD_HEADS, N_CMP_HEADS = 32, 16, 4
CHUNK = 128
GATE_LANES = 128
GATE_ROWS = 16

ATTN_TQ = 256
ATTN_TKS = 256
ATTN_BODY_PAIRS = (4, 2, 1)
NSA_TQ = ATTN_TQ
SLC_TKS = ATTN_TKS
WIN_SPAN = WINDOW + NSA_TQ
MOBA_TQ = ATTN_TQ
MOBA_TK = MOBA_BLOCK
MOBA_TKS = ATTN_TKS
MOBA_HB = 4


def _cparams(sem):
    return pltpu.CompilerParams(dimension_semantics=sem, vmem_limit_bytes=VMEM_LIMIT)


def _split3(a):
    hi = a.astype(BF16)
    r1 = a - hi.astype(F32)
    mid = r1.astype(BF16)
    lo = (r1 - mid.astype(F32)).astype(BF16)
    return hi, mid, lo


def _adaln_kernel(c_ref, w_ref, b_ref, o_ref):
    cv = c_ref[...]
    s = cv * jax.nn.sigmoid(cv)
    w = w_ref[...].astype(BF16)
    acc = b_ref[...] + jnp.zeros(o_ref.shape, F32)
    for part in _split3(s)[:2]:
        acc = acc + jnp.dot(part, w, preferred_element_type=F32)
    o_ref[...] = acc


def _adaln(c, w_ada, b_ada):
    B, D = c.shape
    N = w_ada.shape[1]
    tn = 1024
    c8 = jnp.zeros((8, D), F32).at[:B].set(c)
    out = pl.pallas_call(
        _adaln_kernel,
        grid=(N // tn,),
        in_specs=[pl.BlockSpec((8, D), lambda j: (0, 0)),
                  pl.BlockSpec((D, tn), lambda j: (0, j)),
                  pl.BlockSpec((1, tn), lambda j: (0, j))],
        out_specs=pl.BlockSpec((8, tn), lambda j: (0, j)),
        out_shape=jax.ShapeDtypeStruct((8, N), F32),
        compiler_params=_cparams(("arbitrary",)),
        name="adaln",
    )(c8, w_ada, b_ada.reshape(1, N))
    return out[:B].reshape(B, 6, D)


def _rope(y, cc, sa, sb):
    return (y * cc + pltpu.roll(y, HEAD_DIM - ROPE_HALF, 1) * sa + pltpu.roll(y, ROPE_HALF, 1) * sb)


def _inproj_kernel(x_ref, mod_ref, wq_ref, wkv_ref, wm_ref, wg_ref, gainT_ref, cosT_ref, sinT_ref,
                   yT_ref, ystd_ref, hc_ref, gT_ref, h_scr, rows_scr):
    j = pl.program_id(1)
    tm = x_ref.shape[0]
    n_chunks = tm // CHUNK
    pair = 2 * HEAD_DIM

    @pl.when(j == 0)
    def _():
        x = x_ref[...]
        ms = jnp.mean(x * x, axis=-1, keepdims=True)
        h = x * lax.rsqrt(ms + EPS) * (1.0 + mod_ref[0, 1:2, :]) + mod_ref[0, 0:1, :]
        hb = h.astype(BF16)
        h_scr[...] = hb
        g = jax.nn.sigmoid(jnp.dot(hb, wg_ref[...], preferred_element_type=F32))
        for grp in range(NSA_KV_HEADS):
            for c in range(n_chunks):
                blk = g[c * CHUNK:(c + 1) * CHUNK, grp * GATE_LANES:(grp + 1) * GATE_LANES].T
                gT_ref[grp, :, c * CHUNK:(c + 1) * CHUNK] = blk[0:GATE_ROWS, :]

    def head_pairs(w_ref, n_heads):
        for half in range(n_heads // 2):
            acc = jnp.dot(h_scr[...], w_ref[:, half * pair:(half + 1) * pair], preferred_element_type=F32)
            for h2 in range(2):
                yield 2 * half + h2, acc[:, h2 * HEAD_DIM:(h2 + 1) * HEAD_DIM]

    def qk_chunk(t, c):
        ms = jnp.mean(t * t, axis=0, keepdims=True)
        tn = t * lax.rsqrt(ms + EPS) * gainT_ref[0]
        cs = cosT_ref[:, c * CHUNK:(c + 1) * CHUNK]
        sn = sinT_ref[:, c * CHUNK:(c + 1) * CHUNK]
        a, b = tn[0:ROPE_HALF], tn[ROPE_HALF:ROPE_DIMS]
        return jnp.concatenate([a * cs - b * sn, b * cs + a * sn, tn[ROPE_DIMS:]], axis=0)

    def store_T(yh, slot, treated):
        for c in range(n_chunks):
            chunk = yh[c * CHUNK:(c + 1) * CHUNK, :]
            if treated:
                yT_ref[slot, c] = qk_chunk(chunk.T, c).astype(BF16)
            else:
                yT_ref[slot, c] = chunk.astype(BF16).T

    def store_rows(yh, slot):
        for c in range(n_chunks):
            rows = slice(c * CHUNK, (c + 1) * CHUNK)
            ystd_ref[slot, rows, :] = qk_chunk(yh[rows, :].T, c).astype(BF16).T

    def when_step(kind):
        return pl.when(j == PROJ_STEPS.index(kind))

    for kind, w_ref in (("nsa_q", wq_ref), ("moba_q", wm_ref)):
        @when_step(kind)
        def _(w_ref=w_ref):
            for hh, yh in head_pairs(w_ref, WIDE_HEADS):
                store_T(yh, hh, True)

    @when_step("moba_v")
    def _():
        for hh, yh in head_pairs(wm_ref, WIDE_HEADS):
            store_T(yh, hh, False)

    @when_step("moba_k")
    def _():
        for hh, yh in head_pairs(wm_ref, WIDE_HEADS):
            store_rows(yh, hh)

    def flatten_rows(yh, slot):
        rows_scr[...] = yh
        flat = [rows_scr[pl.ds(l, tm // CMP_STRIDE, stride=CMP_STRIDE), :] for l in range(CMP_STRIDE)]
        hc_ref[slot] = jnp.concatenate(flat, axis=1).astype(BF16)

    @when_step("kv_cmp_slc")
    def _():
        for slot in range(2 * NSA_KV_HEADS, OUT_BLOCK_HEADS):
            yT_ref[slot] = jnp.zeros(yT_ref.shape[1:], BF16)
            ystd_ref[slot] = jnp.zeros(ystd_ref.shape[1:], BF16)
        for hh, yh in head_pairs(wq_ref, WIDE_HEADS):
            if hh < N_CMP_HEADS:
                flatten_rows(yh, hh)
            elif hh < N_CMP_HEADS + NSA_KV_HEADS:
                store_rows(yh, hh - N_CMP_HEADS)
            else:
                store_T(yh, hh - N_CMP_HEADS - NSA_KV_HEADS, False)

    @when_step("kv_win")
    def _():
        for hh, yh in head_pairs(wkv_ref, NARROW_HEADS):
            if hh < NSA_KV_HEADS:
                store_rows(yh, NSA_KV_HEADS + hh)
            else:
                store_T(yh, hh, False)


def _in_proj(x2, mod3, w_nsa, w_moba, w_gate, gainsT, cosT, sinT, T):
    BT, D = x2.shape
    G = NSA_KV_HEADS
    tm = min(1024, T)
    wide, narrow = WIDE_HEADS * HEAD_DIM, NARROW_HEADS * HEAD_DIM
    tpb = T // tm
    pick = lambda j, table: sum((j == t).astype(jnp.int32) * v for t, v in enumerate(table))
    gain_kind = lambda j: pick(j, [GAIN_KINDS.index(PROJ_GAIN[s]) if s in PROJ_GAIN else 0 for s in PROJ_STEPS])
    win_block = (NSA_HEADS + N_CMP_HEADS + 2 * NSA_KV_HEADS) * HEAD_DIM // narrow
    first_moba = PROJ_STEPS.index("moba_q")
    return pl.pallas_call(
        _inproj_kernel,
        grid=(BT // tm, len(PROJ_STEPS)),
        in_specs=[pl.BlockSpec((tm, D), lambda i, j: (i, 0)),
                  pl.BlockSpec((1, 6, D), lambda i, j: (i // tpb, 0, 0)),
                  pl.BlockSpec((D, wide), lambda i, j: (0, jnp.minimum(j, 1))),
                  pl.BlockSpec((D, narrow), lambda i, j: (0, win_block)),
                  pl.BlockSpec((D, wide), lambda i, j: (0, jnp.clip(j - first_moba, 0, 2))),
                  pl.BlockSpec((D, G * GATE_LANES), lambda i, j: (0, 0)),
                  pl.BlockSpec((1, HEAD_DIM, CHUNK), lambda i, j: (gain_kind(j), 0, 0)),
                  pl.BlockSpec((ROPE_HALF, tm), lambda i, j: (0, i % tpb)),
                  pl.BlockSpec((ROPE_HALF, tm), lambda i, j: (0, i % tpb))],
        out_specs=[pl.BlockSpec((OUT_BLOCK_HEADS, tm // CHUNK, HEAD_DIM, CHUNK), lambda i, j: (pick(j, YT_BLOCK), i, 0, 0)),
                   pl.BlockSpec((OUT_BLOCK_HEADS, tm, HEAD_DIM), lambda i, j: (pick(j, YSTD_BLOCK), i, 0)),
                   pl.BlockSpec((N_CMP_HEADS, tm // CMP_STRIDE, CMP_STRIDE * HEAD_DIM), lambda i, j: (0, i, 0)),
                   pl.BlockSpec((G, GATE_ROWS, tm), lambda i, j: (0, 0, i))],
        out_shape=[jax.ShapeDtypeStruct((N_T_HEADS, BT // CHUNK, HEAD_DIM, CHUNK), BF16),
                   jax.ShapeDtypeStruct((N_STD_HEADS, BT, HEAD_DIM), BF16),
                   jax.ShapeDtypeStruct((N_CMP_HEADS, BT // CMP_STRIDE, CMP_STRIDE * HEAD_DIM), BF16),
                   jax.ShapeDtypeStruct((G, GATE_ROWS, BT), F32)],
        scratch_shapes=[pltpu.VMEM((tm, D), BF16), pltpu.VMEM((tm, HEAD_DIM), F32)],
        compiler_params=_cparams(("parallel", "arbitrary")),
        name="in_proj",
    )(x2, mod3, w_nsa, w_nsa, w_moba, w_gate, gainsT, cosT, sinT)


def _compress_kernel(h_ref, w1c_ref, pe_ref, w1_ref, w2_ref, gain_ref, cc_ref, sa_ref, sb_ref, o_ref):
    a = pl.program_id(0)
    ncp = h_ref.shape[2]
    z = jnp.dot(h_ref[0, 0], w1c_ref[0], preferred_element_type=F32)
    top = z[:, :HEAD_DIM]
    bot = pltpu.roll(z[:, HEAD_DIM:], ncp - 1, 0)
    pe_term = jnp.dot(pe_ref[0], w1_ref[0], preferred_element_type=F32)[0:1, :]
    pre = top + bot + pe_term
    act = pre * jax.nn.sigmoid(pre)
    out = jnp.dot(act.astype(BF16), w2_ref[0], preferred_element_type=F32)
    live = lax.broadcasted_iota(jnp.int32, out.shape, 0) < ncp - 1
    out = jnp.where(live, out, 0.0)

    @pl.when(a < NSA_KV_HEADS)
    def _():
        ms = jnp.mean(out * out, axis=-1, keepdims=True)
        yn = out * lax.rsqrt(ms + EPS) * gain_ref[...]
        o_ref[0, 0] = _rope(yn, cc_ref[...], sa_ref[...], sb_ref[...]).astype(BF16)

    @pl.when(a >= NSA_KV_HEADS)
    def _():
        o_ref[0, 0] = out.astype(BF16)


def _compress(hc, w1cat, pe8, w1, w2, gain, cc, sa, sb):
    A, B, ncp, K = hc.shape
    G = NSA_KV_HEADS
    return pl.pallas_call(
        _compress_kernel,
        grid=(A, B),
        in_specs=[pl.BlockSpec((1, 1, ncp, K), lambda a, b: (a, b, 0, 0)),
                  pl.BlockSpec((1, K, 2 * HEAD_DIM), lambda a, b: (a // G, 0, 0)),
                  pl.BlockSpec((1, 8, 2 * K), lambda a, b: (a // G, 0, 0)),
                  pl.BlockSpec((1, 2 * K, HEAD_DIM), lambda a, b: (a // G, 0, 0)),
                  pl.BlockSpec((1, HEAD_DIM, HEAD_DIM), lambda a, b: (a // G, 0, 0)),
                  pl.BlockSpec((1, HEAD_DIM), lambda a, b: (0, 0)),
                  pl.BlockSpec((ncp, HEAD_DIM), lambda a, b: (0, 0)),
                  pl.BlockSpec((ncp, HEAD_DIM), lambda a, b: (0, 0)),
                  pl.BlockSpec((ncp, HEAD_DIM), lambda a, b: (0, 0))],
        out_specs=pl.BlockSpec((1, 1, ncp, HEAD_DIM), lambda a, b: (a, b, 0, 0)),
        out_shape=jax.ShapeDtypeStruct((A, B, ncp, HEAD_DIM), BF16),
        compiler_params=_cparams(("arbitrary", "arbitrary")),
        name="compress",
    )(hc, w1cat, pe8, w1, w2, gain, cc, sa, sb)


def _first_step(s, vT, m_scr, acc_scr):
    m = jnp.max(s, axis=0, keepdims=True)
    p = jnp.exp2(s - m)
    acc_scr[...] = jnp.dot(_with_ones(vT), p.astype(BF16), preferred_element_type=F32)
    m_scr[...] = m


def _with_ones(vT):
    return jnp.concatenate([vT, jnp.ones((ONES_ROWS, vT.shape[1]), vT.dtype)], axis=0)


def _normalized(acc):
    return acc[0:HEAD_DIM] / acc[HEAD_DIM:HEAD_DIM + 1]


def _online_step(s, vT, m_scr, acc_scr):
    m_prev = m_scr[...]
    m_new = jnp.maximum(m_prev, jnp.max(s, axis=0, keepdims=True))
    alpha = jnp.exp2(m_prev - m_new)
    p = jnp.exp2(s - m_new)
    acc_scr[...] = alpha * acc_scr[...] + jnp.dot(_with_ones(vT), p.astype(BF16), preferred_element_type=F32)
    m_scr[...] = m_new


def _rank_select(score, k):
    n = score.shape[0]
    rank = jnp.zeros(score.shape, F32)
    for m in range(n):
        sm = score[m:m + 1, :]
        lo = (m // SUBLANES) * SUBLANES
        hi = min(lo + SUBLANES, n)
        parts = []
        if lo > 0:
            parts.append(jnp.where(sm > score[:lo], 1.0, 0.0))
        gt = jnp.where(sm > score[lo:hi], 1.0, 0.0)
        ge = jnp.where(sm >= score[lo:hi], 1.0, 0.0)
        below = lax.broadcasted_iota(jnp.int32, gt.shape, 0) > m - lo
        parts.append(jnp.where(below, ge, gt))
        if hi < n:
            parts.append(jnp.where(sm >= score[hi:], 1.0, 0.0))
        rank = rank + (jnp.concatenate(parts, axis=0) if len(parts) > 1 else parts[0])
    return rank < k


def _pad_rows(a, rows):
    return jnp.concatenate([a, jnp.zeros((rows - a.shape[0], a.shape[1]), a.dtype)], axis=0)


def _masked_scores(k_ref, e_ref, start, size, q_aug):
    rows = pl.ds(pl.multiple_of(start, 128), size)
    k_aug = jnp.concatenate([k_ref[rows, :], e_ref[rows, :]], axis=1)
    return jnp.dot(k_aug, q_aug, preferred_element_type=F32)


def _pipelined_attention(chains, n_steps, n_max, tks, tk, body_pairs):
    per = tks // tk

    def scores(step, buf):
        st = jnp.minimum(step, n_max - 1)
        dead = jnp.where(step < n_steps, 0, 1)
        for k_ref, e_ref, _, qaug_ref, bufs, _, _ in chains:
            bufs[buf][...] = _masked_scores(k_ref, e_ref, st * tks, tks, qaug_ref[dead])

    def consume(step, buf):
        st = jnp.minimum(step, n_max - 1)
        for _, _, vT_ref, _, bufs, m_scr, acc_scr in chains:
            vT = jnp.concatenate([vT_ref[st * per + a] for a in range(per)], axis=1)
            _online_step(bufs[buf][...], vT, m_scr, acc_scr)

    def pairs(first_step, n_pairs_in_body):
        def body(it, carry):
            base = first_step + it * 2 * n_pairs_in_body
            for p in range(n_pairs_in_body):
                scores(base + 2 * p + 1, 1)
                consume(base + 2 * p, 0)
                scores(base + 2 * p + 2, 0)
                consume(base + 2 * p + 1, 1)
            return carry
        return body

    scores(0, 0)
    done = 0
    for n_body in body_pairs[:-1]:
        n_iter = (n_steps - done) // (2 * n_body)
        lax.fori_loop(0, n_iter, pairs(done, n_body), 0)
        done = done + n_iter * 2 * n_body
    lax.fori_loop(0, (n_steps - done + 1) // 2, pairs(done, 1), 0)


def _carry_casts(kernel_fn, n_in, n_out, n_cast):
    def wrapped(*refs):
        ins, cast_in = refs[:n_in], refs[n_in:n_in + n_cast]
        outs = refs[n_in + n_cast:n_in + n_cast + n_out]
        cast_out = refs[n_in + n_cast + n_out:n_in + n_cast + n_out + n_cast]
        for src, dst in zip(cast_in, cast_out):
            dst[...] = src[...].astype(BF16)
        kernel_fn(*ins, *outs, *refs[n_in + n_cast + n_out + n_cast:])
    return wrapped


def _cast_specs(weights, grid):
    n_steps = grid[0] * grid[1] * grid[2]
    specs, shapes = [], []
    for w, rows in weights:
        while w.shape[0] // rows > n_steps:
            rows *= 2
        n_blk = w.shape[0] // rows
        assert w.shape[0] % rows == 0
        index = lambda a, b, c, n_blk=n_blk: (jnp.minimum((a * grid[1] + b) * grid[2] + c, n_blk - 1), 0)
        specs.append(pl.BlockSpec((rows, w.shape[1]), index))
        shapes.append(jax.ShapeDtypeStruct(w.shape, BF16))
    return specs, shapes


def _nsa_body(qT_ref, kc_ref, vcT_ref, ks_ref, vsT_ref, kw_ref, vwT_ref, gT_ref, e_ref,
              gain_ref, o_ref, m_scr, acc_scr, comb_scr, qaug_scr, s0_scr, s1_scr, psum_scr, *, T):
    R, tq = NSA_GROUP, NSA_TQ
    i = pl.program_id(2)
    t0 = i * tq
    ncp = T // CMP_STRIDE
    nsb = T // SLC_BLOCK
    qT = jnp.concatenate([qT_ref[r, c] for r in range(R) for c in range(tq // CHUNK)], axis=1)
    gT = gT_ref[0]

    def per_head(row0):
        return jnp.concatenate([gT[row0 + 3 * r:row0 + 3 * r + 1, :] for r in range(R)], axis=1)

    def lanes_x_heads(a):
        return jnp.concatenate([a] * R, axis=1)

    s = jnp.dot(kc_ref[0, 0], qT, preferred_element_type=F32)
    c_idx = lax.broadcasted_iota(jnp.int32, (ncp, tq), 0)
    t_c = t0 + lax.broadcasted_iota(jnp.int32, (ncp, tq), 1)
    vis = lanes_x_heads((c_idx * CMP_STRIDE + (CMP_BLOCK - 1) <= t_c) & (c_idx < ncp - 1))
    s = jnp.where(vis, s, NEG)
    m = jnp.maximum(jnp.max(s, axis=0, keepdims=True), M_INIT)
    p = jnp.exp2(s - m)
    l = jnp.sum(p, axis=0, keepdims=True)
    p = p / jnp.where(l > 0.0, l, 1.0)
    o_cmp = jnp.dot(vcT_ref[0, 0], p.astype(BF16), preferred_element_type=F32)
    comb_scr[...] = o_cmp * per_head(0)

    psum = p[:, 0:tq]
    for r in range(1, R):
        psum = psum + p[:, r * tq:(r + 1) * tq]
    per = SLC_BLOCK // CMP_STRIDE
    imp_chunks = []
    for c in range(tq // CHUNK):
        psum_scr[c, 0:SUBLANES, :] = jnp.zeros((SUBLANES, CHUNK), F32)
        psum_scr[c, SUBLANES:, :] = psum[:, c * CHUNK:(c + 1) * CHUNK]
        acc = jnp.zeros((nsb, CHUNK), F32)
        for k in range(1 - CMP_BLOCK // CMP_STRIDE, per):
            acc = acc + psum_scr[c, pl.ds(SUBLANES + k, nsb, stride=per), :]
        imp_chunks.append(acc)
    imp = jnp.concatenate(imp_chunks, axis=1)
    n_idx = lax.broadcasted_iota(jnp.int32, (nsb, tq), 0)
    cur = (t0 + lax.broadcasted_iota(jnp.int32, (nsb, tq), 1)) // SLC_BLOCK
    forced = (n_idx == 0) | (n_idx == cur) | (n_idx == cur - 1)
    valid = n_idx <= cur
    score = jnp.where(valid, imp + jnp.where(forced, FORCE_BONUS, 0.0), NEG)
    sel = _rank_select(score, min(SLC_TOPK, nsb)) & valid
    sel_past = sel & (n_idx * SLC_BLOCK < t0)
    bias = _pad_rows(jnp.where(sel_past, 0.0, NEG), HEAD_DIM).astype(BF16)
    bias_own = _pad_rows(jnp.where(sel, 0.0, NEG), HEAD_DIM).astype(BF16)
    for d in range(3):
        qaug_scr[d, 0:HEAD_DIM, :] = qT
    qaug_scr[0, HEAD_DIM:, :] = lanes_x_heads(bias)
    qaug_scr[1, HEAD_DIM:, :] = jnp.full((HEAD_DIM, R * tq), NEG, BF16)
    qaug_scr[2, HEAD_DIM:, :] = lanes_x_heads(bias_own)

    w0 = jnp.maximum(t0 + tq - WIN_SPAN, 0)
    k_w = kw_ref[0, pl.ds(pl.multiple_of(w0, CHUNK), WIN_SPAN), :]
    s_w = jnp.dot(k_w, qT, preferred_element_type=F32)
    kpos = w0 + lax.broadcasted_iota(jnp.int32, (WIN_SPAN, tq), 0)
    tpos = t0 + lax.broadcasted_iota(jnp.int32, (WIN_SPAN, tq), 1)
    ok = lanes_x_heads((kpos <= tpos) & (tpos - kpos < WINDOW))
    s_w = jnp.where(ok, s_w, NEG)
    m_w = jnp.max(s_w, axis=0, keepdims=True)
    p_w = jnp.exp2(s_w - m_w)
    jw = w0 // CHUNK
    v_w = jnp.concatenate([vwT_ref[0, jw + a] for a in range(WIN_SPAN // CHUNK)], axis=1)
    o_w = jnp.dot(_with_ones(v_w), p_w.astype(BF16), preferred_element_type=F32)
    comb_scr[...] += _normalized(o_w) * per_head(2)

    tri =lax.broadcasted_iota(jnp.int32, (tq, tq), 0) <= lax.broadcasted_iota(jnp.int32, (tq, tq), 1)
    s_d = jnp.where(lanes_x_heads(tri), _masked_scores(ks_ref.at[0], e_ref, t0, tq, qaug_scr[2]), NEG)
    own = tq // CHUNK
    v_d = jnp.concatenate([vsT_ref[0, i * own + a] for a in range(own)], axis=1)
    _first_step(s_d, v_d, m_scr, acc_scr)
    halves = [slice(h * (R // 2) * tq, (h + 1) * (R // 2) * tq) for h in range(2)]
    yield [(ks_ref.at[0], e_ref, vsT_ref.at[0], qaug_scr.at[:, :, ln], (s0_scr.at[:, ln], s1_scr.at[:, ln]),
            m_scr.at[:, ln], acc_scr.at[:, ln]) for ln in halves]
    comb = comb_scr[...] + _normalized(acc_scr[...]) * per_head(1)

    for r in range(R):
        oT = comb[:, r * tq:(r + 1) * tq]
        ms = jnp.mean(oT * oT, axis=0, keepdims=True)
        on = (oT * lax.rsqrt(ms + EPS)).T * gain_ref[0, r:r + 1, :]
        o_ref[:, r * HEAD_DIM:(r + 1) * HEAD_DIM] = on.astype(BF16)


def _moba_body(qT_ref, k_ref, vT_ref, e_ref, gain_ref, o_ref, m_scr, acc_scr, kmean_scr, qaug_scr,
               s_scr, *, T):
    tq, tk = MOBA_TQ, MOBA_TK
    nb = T // MOBA_BLOCK
    nbp = kmean_scr.shape[1]
    i = pl.program_id(2)
    t0 = i * tq

    @pl.when(i == 0)
    def _():
        kmean_scr[...] = jnp.zeros(kmean_scr.shape, F32)
        for a in range(MOBA_HB):
            kb = k_ref[a].astype(F32).reshape(nb, MOBA_BLOCK, HEAD_DIM)
            kmean_scr[a, 0:nb, :] = jnp.mean(kb, axis=1)

    n_idx = lax.broadcasted_iota(jnp.int32, (nbp, tq), 0)
    cur = (t0 + lax.broadcasted_iota(jnp.int32, (nbp, tq), 1)) // MOBA_BLOCK
    past = n_idx < cur
    causal = lax.broadcasted_iota(jnp.int32, (tk, tq), 0) <= lax.broadcasted_iota(jnp.int32, (tk, tq), 1)

    qTs = [jnp.concatenate([qT_ref[a, c] for c in range(tq // CHUNK)], axis=1) for a in range(MOBA_HB)]
    for a in range(MOBA_HB):
        qT = qTs[a]
        gate = jnp.zeros((nbp, tq), F32)
        for part in _split3(kmean_scr[a]):
            gate = gate + jnp.dot(part, qT, preferred_element_type=F32)
        sel = _rank_select(jnp.where(past, gate, NEG), min(MOBA_TOPK, nb)) & past
        for d in range(2):
            qaug_scr[a, d, 0:HEAD_DIM, :] = qT
        qaug_scr[a, 0, HEAD_DIM:, :] = _pad_rows(jnp.where(sel, 0.0, NEG), HEAD_DIM).astype(BF16)
        qaug_scr[a, 1, HEAD_DIM:, :] = jnp.full((HEAD_DIM, tq), NEG, BF16)

    s_own = []
    for a in range(MOBA_HB):
        m_scr[a] = jnp.full(m_scr.shape[1:], M_INIT, F32)
        acc_scr[a] = jnp.zeros(acc_scr.shape[1:], F32)
        k_own = k_ref[a, pl.ds(pl.multiple_of(t0, tk), tk), :]
        s_own.append(jnp.where(causal, jnp.dot(k_own, qTs[a], preferred_element_type=F32), NEG))

    yield [(k_ref.at[a], e_ref, vT_ref.at[a], qaug_scr.at[a], (s_scr.at[a, 0], s_scr.at[a, 1]),
            m_scr.at[a], acc_scr.at[a]) for a in range(MOBA_HB)]

    own = tq // CHUNK
    for a in range(MOBA_HB):
        v_own = jnp.concatenate([vT_ref[a, i * own + c] for c in range(own)], axis=1)
        _online_step(s_own[a], v_own, m_scr.at[a], acc_scr.at[a])
        oT = _normalized(acc_scr[a])
        ms = jnp.mean(oT * oT, axis=0, keepdims=True)
        on = (oT * lax.rsqrt(ms + EPS)).T * gain_ref[a]
        o_ref[:, a * HEAD_DIM:(a + 1) * HEAD_DIM] = on.astype(BF16)


N_NSA_IN, N_NSA_SCRATCH, N_MOBA_IN = 10, 7, 5


def _attention_kernel(*refs, T):
    nsa_in, moba_in = refs[:N_NSA_IN], refs[N_NSA_IN:N_NSA_IN + N_MOBA_IN]
    o_nsa, o_moba = refs[N_NSA_IN + N_MOBA_IN:N_NSA_IN + N_MOBA_IN + 2]
    scratch = refs[N_NSA_IN + N_MOBA_IN + 2:]
    moba = _moba_body(*moba_in, o_moba, *scratch[N_NSA_SCRATCH:], T=T)
    nsa = _nsa_body(*nsa_in, o_nsa, *scratch[:N_NSA_SCRATCH], T=T)
    chains = next(moba) + next(nsa)
    n_steps = pl.program_id(2) * ATTN_TQ // ATTN_TKS
    _pipelined_attention(chains, n_steps, T // ATTN_TKS, ATTN_TKS, CHUNK, ATTN_BODY_PAIRS)
    for rest in (nsa, moba):
        assert next(rest, None) is None


def _attention(yT, ystd, kcvc, vcT, gT, e_slc, e_moba, nsa_gains, moba_gains, casts, B, T):
    G, R, tq = NSA_KV_HEADS, NSA_GROUP, ATTN_TQ
    H, HB = MOBA_HEADS, MOBA_HB
    assert H // HB == G and NSA_TQ == MOBA_TQ == tq and SLC_TKS == MOBA_TKS == ATTN_TKS and tq % ATTN_TKS == 0
    nq = T // tq
    ncp = T // CMP_STRIDE
    nb = T // MOBA_BLOCK
    nbp = max(16, nb)
    grid = (B, G, nq)
    cast_specs, cast_shapes = _cast_specs(casts, grid)
    kern = _carry_casts(functools.partial(_attention_kernel, T=T), N_NSA_IN + N_MOBA_IN, 2, len(casts))
    keys = lambda head0: pl.BlockSpec((1, T, HEAD_DIM), lambda b, g, i: (head0 + g, b, 0))
    values = lambda head0: pl.BlockSpec((1, T // CHUNK, HEAD_DIM, CHUNK), lambda b, g, i: (head0 + g, b, 0, 0))
    onehot = pl.BlockSpec((T, HEAD_DIM), lambda b, g, i: (0, 0))
    nsa_specs = [pl.BlockSpec((R, tq // CHUNK, HEAD_DIM, CHUNK), lambda b, g, i: (T_NSA_Q // R + g, b * nq + i, 0, 0)),
                 pl.BlockSpec((1, 1, ncp, HEAD_DIM), lambda b, g, i: (g, b, 0, 0)),
                 pl.BlockSpec((1, 1, HEAD_DIM, ncp), lambda b, g, i: (g, b, 0, 0)),
                 keys(S_KSLC), values(T_VSLC), keys(S_KWIN), values(T_VWIN),
                 pl.BlockSpec((1, GATE_ROWS, tq), lambda b, g, i: (g, 0, b * nq + i)),
                 onehot,
                 pl.BlockSpec((1, R, HEAD_DIM), lambda b, g, i: (g, 0, 0))]
    moba_specs = [pl.BlockSpec((HB, tq // CHUNK, HEAD_DIM, CHUNK), lambda b, h, i: (T_MOBA_Q // HB + h, b * nq + i, 0, 0)),
                  pl.BlockSpec((HB, T, HEAD_DIM), lambda b, h, i: (S_MOBA_K // HB + h, b, 0)),
                  pl.BlockSpec((HB, T // CHUNK, HEAD_DIM, CHUNK), lambda b, h, i: (T_MOBA_V // HB + h, b, 0, 0)),
                  onehot,
                  pl.BlockSpec((HB, 1, HEAD_DIM), lambda b, h, i: (h, 0, 0))]
    nsa_scratch = [pltpu.VMEM((1, R * tq), F32),
                   pltpu.VMEM((HEAD_DIM + ONES_ROWS, R * tq), F32), pltpu.VMEM((HEAD_DIM, R * tq), F32),
                   pltpu.VMEM((3, 2 * HEAD_DIM, R * tq), BF16),
                   pltpu.VMEM((ATTN_TKS, R * tq), F32), pltpu.VMEM((ATTN_TKS, R * tq), F32),
                   pltpu.VMEM((tq // CHUNK, SUBLANES + ncp, CHUNK), F32)]
    moba_scratch = [pltpu.VMEM((HB, 1, tq), F32),
                    pltpu.VMEM((HB, HEAD_DIM + ONES_ROWS, tq), F32), pltpu.VMEM((HB, nbp, HEAD_DIM), F32),
                    pltpu.VMEM((HB, 2, 2 * HEAD_DIM, tq), BF16),
                    pltpu.VMEM((HB, 2, ATTN_TKS, tq), F32)]
    assert len(nsa_specs) == N_NSA_IN and len(moba_specs) == N_MOBA_IN and len(nsa_scratch) == N_NSA_SCRATCH
    return pl.pallas_call(
        kern,
        grid=grid,
        in_specs=nsa_specs + moba_specs + cast_specs,
        out_specs=[pl.BlockSpec((tq, R * HEAD_DIM), lambda b, g, i: (b * nq + i, g)),
                   pl.BlockSpec((tq, HB * HEAD_DIM), lambda b, h, i: (b * nq + i, h))] + cast_specs,
        out_shape=[jax.ShapeDtypeStruct((B * T, NSA_HEADS * HEAD_DIM), BF16),
                   jax.ShapeDtypeStruct((B * T, H * HEAD_DIM), BF16)] + cast_shapes,
        scratch_shapes=nsa_scratch + moba_scratch,
        compiler_params=_cparams(("parallel", "parallel", "arbitrary")),
        name="attention",
    )(yT, kcvc, vcT, ystd, yT, ystd, yT, gT, e_slc, nsa_gains,
      yT, ystd, yT, e_moba, moba_gains, *[w for w, _ in casts])


def _outproj_kernel(on_ref, om_ref, w_ref, x_ref, mod_ref, o_ref):
    half = on_ref.shape[1]
    acc = jnp.dot(on_ref[...], w_ref[0:half, :], preferred_element_type=F32)
    acc = acc + jnp.dot(om_ref[...], w_ref[half:, :], preferred_element_type=F32)
    o_ref[...] = x_ref[...] + mod_ref[0, 2:3, :] * acc


def _out_proj(o_nsa, o_moba, w_out, x2, mod3, T):
    BT, D = x2.shape
    tm = min(512, T)
    tpb = T // tm
    half = o_nsa.shape[1]
    return pl.pallas_call(
        _outproj_kernel,
        grid=(BT // tm,),
        in_specs=[pl.BlockSpec((tm, half), lambda i: (i, 0)),
                  pl.BlockSpec((tm, half), lambda i: (i, 0)),
                  pl.BlockSpec((D, D), lambda i: (0, 0)),
                  pl.BlockSpec((tm, D), lambda i: (i, 0)),
                  pl.BlockSpec((1, 6, D), lambda i: (i // tpb, 0, 0))],
        out_specs=pl.BlockSpec((tm, D), lambda i: (i, 0)),
        out_shape=jax.ShapeDtypeStruct((BT, D), F32),
        compiler_params=_cparams(("parallel",)),
        name="out_proj",
    )(o_nsa, o_moba, w_out, x2, mod3)


def _ffn_kernel(x_ref, mod_ref, wg_ref, wu_ref, wo_ref, o_ref, h_scr):
    j = pl.program_id(1)

    @pl.when(j == 0)
    def _():
        x = x_ref[...]
        ms = jnp.mean(x * x, axis=-1, keepdims=True)
        h = x * lax.rsqrt(ms + EPS) * (1.0 + mod_ref[0, 4:5, :]) + mod_ref[0, 3:4, :]
        h_scr[...] = h.astype(BF16)
        o_ref[...] = jnp.zeros(o_ref.shape, F32)

    hb = h_scr[...]
    half = wg_ref.shape[1] // 2
    down = None
    for c in range(2):
        cols = slice(c * half, (c + 1) * half)
        gate = jnp.dot(hb, wg_ref[:, cols], preferred_element_type=F32)
        up = jnp.dot(hb, wu_ref[:, cols], preferred_element_type=F32)
        act = (gate * jax.nn.sigmoid(gate) * up).astype(BF16)
        part = jnp.dot(act, wo_ref[cols, :], preferred_element_type=F32)
        down = part if down is None else down + part
    o_ref[...] += down

    @pl.when(j == pl.num_programs(1) - 1)
    def _():
        o_ref[...] = x_ref[...] + mod_ref[0, 5:6, :] * o_ref[...]


def _ffn(x1, mod3, w_in, w_out, T):
    BT, D = x1.shape
    Fh = w_out.shape[0]
    tm = min(512, T)
    tf = 512
    tpb = T // tm
    nf = Fh // tf
    return pl.pallas_call(
        _ffn_kernel,
        grid=(BT // tm, nf),
        in_specs=[pl.BlockSpec((tm, D), lambda i, j: (i, 0)),
                  pl.BlockSpec((1, 6, D), lambda i, j: (i // tpb, 0, 0)),
                  pl.BlockSpec((D, tf), lambda i, j: (0, j)),
                  pl.BlockSpec((D, tf), lambda i, j: (0, nf + j)),
                  pl.BlockSpec((tf, D), lambda i, j: (j, 0))],
        out_specs=pl.BlockSpec((tm, D), lambda i, j: (i, 0)),
        out_shape=jax.ShapeDtypeStruct((BT, D), F32),
        scratch_shapes=[pltpu.VMEM((tm, D), BF16)],
        compiler_params=_cparams(("parallel", "arbitrary")),
        name="ffn",
    )(x1, mod3, w_in, w_in, w_out)


def _rope_cos_sin(pos):
    inv = ROPE_THETA ** (-jnp.arange(0, ROPE_DIMS, 2, dtype=F32) / ROPE_DIMS)
    ang = pos.astype(F32)[:, None] * inv[None, :]
    return jnp.cos(ang), jnp.sin(ang)


def _rope_tables(T):
    c_hi, s_hi = _rope_cos_sin(jnp.arange(0, T, ROPE_SPLIT))
    c_lo, s_lo = _rope_cos_sin(jnp.arange(ROPE_SPLIT))
    cos = (c_hi[:, None] * c_lo[None] - s_hi[:, None] * s_lo[None]).reshape(T, ROPE_HALF)
    sin = (s_hi[:, None] * c_lo[None] + c_hi[:, None] * s_lo[None]).reshape(T, ROPE_HALF)
    cos_c, sin_c = _rope_cos_sin(jnp.arange(CMP_BLOCK - 1, T, CMP_STRIDE))
    n, rest = cos_c.shape[0], HEAD_DIM - ROPE_DIMS
    pad = lambda t: jnp.pad(t, ((0, 1), (0, 0)))
    cc = pad(jnp.concatenate([cos_c, cos_c, jnp.ones((n, rest), F32)], axis=1))
    sa = pad(jnp.concatenate([-sin_c, jnp.zeros((n, HEAD_DIM - ROPE_HALF), F32)], axis=1))
    sb = pad(jnp.concatenate([jnp.zeros((n, ROPE_HALF), F32), sin_c, jnp.zeros((n, rest), F32)], axis=1))
    return cc, sa, sb, cos.T, sin.T


def _block_onehot(T, block):
    return (jnp.arange(T)[:, None] // block == jnp.arange(HEAD_DIM)[None, :]).astype(BF16)


def _layer(x2, c, B, T, w_ada, b_ada, w_in, nsa_q_norm, nsa_k_norm, moba_q_norm, moba_k_norm,
           cmp_pe_k, cmp_w1_k, cmp_w2_k, cmp_pe_v, cmp_w1_v, cmp_w2_v, out_norm, w_out,
           w_ffn_in, w_ffn_out):
    D = x2.shape[1]
    G = NSA_KV_HEADS
    scale = HEAD_DIM ** -0.5 * LOG2E
    assert T % MOBA_BLOCK == 0 and T % SLC_TKS == 0 and T % MOBA_TKS == 0 and T % NSA_TQ == 0 and T >= WIN_SPAN
    assert T // SLC_BLOCK <= HEAD_DIM and T // MOBA_BLOCK <= HEAD_DIM

    nsa_w = NSA_HEADS * HEAD_DIM + 6 * G * HEAD_DIM
    gw = NSA_HEADS * 3
    mod3 = _adaln(c, w_ada, b_ada)
    w_nsa = w_in.astype(BF16)
    w_moba = w_nsa[:, nsa_w + gw:]

    wg = w_in[:, nsa_w:nsa_w + gw].reshape(D, G, NSA_GROUP * 3)
    w_gate = jnp.pad(wg, ((0, 0), (0, 0), (0, GATE_LANES - NSA_GROUP * 3))).reshape(D, G * GATE_LANES)
    w_gate = w_gate.astype(BF16)
    across = lambda g_: jnp.broadcast_to(g_[:, None], (HEAD_DIM, CHUNK))
    gainsT = jnp.stack([across(nsa_q_norm * scale), across(moba_q_norm * scale),
                        across(nsa_k_norm[1]), across(nsa_k_norm[2]), across(moba_k_norm)])
    cc, sa, sb, cosT, sinT = _rope_tables(T)

    yT, ystd, hc, gT = _in_proj(x2, mod3, w_nsa, w_moba, w_gate, gainsT, cosT, sinT, T)

    ncp = T // CMP_STRIDE
    half = CMP_STRIDE * HEAD_DIM
    w1 = jnp.stack([cmp_w1_k, cmp_w1_v]).astype(BF16)
    w1cat = jnp.concatenate([w1[:, :half], w1[:, half:]], axis=2)
    pe8 = jnp.broadcast_to(jnp.stack([cmp_pe_k, cmp_pe_v]).reshape(2, 1, 2 * half), (2, 8, 2 * half))
    w2 = jnp.stack([cmp_w2_k, cmp_w2_v]).astype(BF16)
    kcvc = _compress(hc.reshape(2 * G, B, ncp, half), w1cat, pe8.astype(BF16), w1, w2,
                     nsa_k_norm[0].reshape(1, HEAD_DIM), cc, sa, sb)
    vcT = kcvc[G:].transpose(0, 1, 3, 2)

    on = out_norm.reshape(N_HEADS, HEAD_DIM)
    o_nsa, o_moba, w_ffn_out_b, w_out_b, w_ffn_in_b = _attention(
        yT, ystd, kcvc, vcT, gT, _block_onehot(T, SLC_BLOCK), _block_onehot(T, MOBA_BLOCK),
        on[:NSA_HEADS].reshape(G, NSA_GROUP, HEAD_DIM), on[NSA_HEADS:].reshape(MOBA_HEADS, 1, HEAD_DIM),
        [(w_ffn_out, 128), (w_out, 128), (w_ffn_in, 32)], B, T)

    x1 = _out_proj(o_nsa, o_moba, w_out_b, x2, mod3, T)
    return _ffn(x1, mod3, w_ffn_in_b, w_ffn_out_b, T)


def kernel(x, c, w_ada, b_ada, w_in, nsa_q_norm, nsa_k_norm, moba_q_norm, moba_k_norm, cmp_pe_k, cmp_w1_k, cmp_w2_k, cmp_pe_v, cmp_w1_v, cmp_w2_v, out_norm, w_out, w_ffn_in, w_ffn_out):
    B, T, D = x.shape
    x2 = x.reshape(B * T, D)
    for l in range(w_ada.shape[0]):
        x2 = _layer(x2, c, B, T, w_ada[l], b_ada[l], w_in[l], nsa_q_norm[l], nsa_k_norm[l],
                    moba_q_norm[l], moba_k_norm[l], cmp_pe_k[l], cmp_w1_k[l], cmp_w2_k[l],
                    cmp_pe_v[l], cmp_w1_v[l], cmp_w2_v[l], out_norm[l], w_out[l],
                    w_ffn_in[l], w_ffn_out[l])
    return x2.reshape(B, T, D)
```

```python
import functools

import jax
import jax.numpy as jnp
from jax import lax
from jax.experimental import pallas as pl
from jax.experimental.pallas import tpu as pltpu

F32 = jnp.float32
BF16 = jnp.bfloat16

HEAD_DIM = 128
SUBLANES = 8
NSA_HEADS = 8
NSA_KV_HEADS = 2
NSA_GROUP = NSA_HEADS // NSA_KV_HEADS
MOBA_HEADS = 8
N_HEADS = NSA_HEADS + MOBA_HEADS
CMP_BLOCK = 32
CMP_STRIDE = 16
SLC_BLOCK = 64
SLC_TOPK = 16
WINDOW = 512
FORCE_BONUS = 1e4
MOBA_BLOCK = 256
MOBA_TOPK = 3
ROPE_THETA = 500000.0
ROPE_DIMS = HEAD_DIM // 4
ROPE_HALF = ROPE_DIMS // 2
ROPE_SPLIT = 64
EPS = 1e-6
LOG2E = 1.4426950408889634
ONES_ROWS = 16
NEG = -1e30
M_INIT = -1e29

V7X_VMEM_BYTES = 64 * 1024 * 1024
VMEM_LIMIT = V7X_VMEM_BYTES - 8 * 1024 * 1024

PROJ_STEPS = ("nsa_q", "kv_cmp_slc", "kv_win", "moba_q", "moba_k", "moba_v")
WIDE_HEADS, NARROW_HEADS = 8, 4
OUT_BLOCK_HEADS = 8
YT_BLOCK = (0, 3, 3, 1, 1, 2)
YSTD_BLOCK = (1, 1, 1, 0, 0, 0)
GAIN_KINDS = ("nsa_q", "moba_q", "k_slc", "k_win", "moba_k")
PROJ_GAIN = {"nsa_q": "nsa_q", "moba_q": "moba_q", "kv_cmp_slc": "k_slc", "kv_win": "k_win", "moba_k": "moba_k"}
T_NSA_Q, T_MOBA_Q, T_MOBA_V, T_VSLC, T_VWIN = 0, 8, 16, 24, 26
S_MOBA_K, S_KSLC, S_KWIN = 0, 8, 10
N_T_HEADS, N_STD_HEADS, N_CMP_HEADS = 32, 16, 4
CHUNK = 128
GATE_LANES = 128
GATE_ROWS = 16

ATTN_TQ = 256
ATTN_TKS = 256
ATTN_BODY_PAIRS = (4, 2, 1)
NSA_TQ = ATTN_TQ
SLC_TKS = ATTN_TKS
WIN_SPAN = WINDOW + NSA_TQ
MOBA_TQ = ATTN_TQ
MOBA_TK = MOBA_BLOCK
MOBA_TKS = ATTN_TKS
MOBA_HB = 4

ADALN_TN = 1024
PROJ_TM = 1024
OUT_TM = 512
FFN_TM, FFN_TF = 512, 512


def _cparams(sem):
    return pltpu.CompilerParams(dimension_semantics=sem, vmem_limit_bytes=VMEM_LIMIT)


def _split3(a):
    hi = a.astype(BF16)
    r1 = a - hi.astype(F32)
    mid = r1.astype(BF16)
    lo = (r1 - mid.astype(F32)).astype(BF16)
    return hi, mid, lo


def _adaln_kernel(c_ref, w_ref, b_ref, o_ref):
    cv = c_ref[...]
    s = cv * jax.nn.sigmoid(cv)
    w = w_ref[...].astype(BF16)
    acc = b_ref[...] + jnp.zeros(o_ref.shape, F32)
    for part in _split3(s)[:2]:
        acc = acc + jnp.dot(part, w, preferred_element_type=F32)
    o_ref[...] = acc


def _adaln(c, w_ada, b_ada):
    B, D = c.shape
    N = w_ada.shape[1]
    tn = ADALN_TN
    c8 = jnp.zeros((8, D), F32).at[:B].set(c)
    out = pl.pallas_call(
        _adaln_kernel,
        grid=(N // tn,),
        in_specs=[pl.BlockSpec((8, D), lambda j: (0, 0)),
                  pl.BlockSpec((D, tn), lambda j: (0, j)),
                  pl.BlockSpec((1, tn), lambda j: (0, j))],
        out_specs=pl.BlockSpec((8, tn), lambda j: (0, j)),
        out_shape=jax.ShapeDtypeStruct((8, N), F32),
        compiler_params=_cparams(("arbitrary",)),
        name="adaln",
    )(c8, w_ada, b_ada.reshape(1, N))
    return out[:B].reshape(B, 6, D)


def _rope(y, cc, sa, sb):
    return (y * cc + pltpu.roll(y, HEAD_DIM - ROPE_HALF, 1) * sa + pltpu.roll(y, ROPE_HALF, 1) * sb)


def _inproj_kernel(x_ref, mod_ref, wq_ref, wkv_ref, wm_ref, wg_ref, gainT_ref, cosT_ref, sinT_ref,
                   yT_ref, ystd_ref, hc_ref, gT_ref, h_scr, rows_scr):
    j = pl.program_id(1)
    tm = x_ref.shape[0]
    n_chunks = tm // CHUNK
    pair = 2 * HEAD_DIM

    @pl.when(j == 0)
    def _():
        x = x_ref[...]
        ms = jnp.mean(x * x, axis=-1, keepdims=True)
        h = x * lax.rsqrt(ms + EPS) * (1.0 + mod_ref[0, 1:2, :]) + mod_ref[0, 0:1, :]
        hb = h.astype(BF16)
        h_scr[...] = hb
        g = jax.nn.sigmoid(jnp.dot(hb, wg_ref[...], preferred_element_type=F32))
        for grp in range(NSA_KV_HEADS):
            for c in range(n_chunks):
                blk = g[c * CHUNK:(c + 1) * CHUNK, grp * GATE_LANES:(grp + 1) * GATE_LANES].T
                gT_ref[grp, :, c * CHUNK:(c + 1) * CHUNK] = blk[0:GATE_ROWS, :]

    def head_pairs(w_ref, n_heads):
        for half in range(n_heads // 2):
            acc = jnp.dot(h_scr[...], w_ref[:, half * pair:(half + 1) * pair], preferred_element_type=F32)
            for h2 in range(2):
                yield 2 * half + h2, acc[:, h2 * HEAD_DIM:(h2 + 1) * HEAD_DIM]

    def qk_chunk(t, c):
        ms = jnp.mean(t * t, axis=0, keepdims=True)
        tn = t * lax.rsqrt(ms + EPS) * gainT_ref[0]
        cs = cosT_ref[:, c * CHUNK:(c + 1) * CHUNK]
        sn = sinT_ref[:, c * CHUNK:(c + 1) * CHUNK]
        a, b = tn[0:ROPE_HALF], tn[ROPE_HALF:ROPE_DIMS]
        return jnp.concatenate([a * cs - b * sn, b * cs + a * sn, tn[ROPE_DIMS:]], axis=0)

    def store_T(yh, slot, treated):
        for c in range(n_chunks):
            chunk = yh[c * CHUNK:(c + 1) * CHUNK, :]
            if treated:
                yT_ref[slot, c] = qk_chunk(chunk.T, c).astype(BF16)
            else:
                yT_ref[slot, c] = chunk.astype(BF16).T

    def store_rows(yh, slot):
        for c in range(n_chunks):
            rows = slice(c * CHUNK, (c + 1) * CHUNK)
            ystd_ref[slot, rows, :] = qk_chunk(yh[rows, :].T, c).astype(BF16).T

    def when_step(kind):
        return pl.when(j == PROJ_STEPS.index(kind))

    for kind, w_ref in (("nsa_q", wq_ref), ("moba_q", wm_ref)):
        @when_step(kind)
        def _(w_ref=w_ref):
            for hh, yh in head_pairs(w_ref, WIDE_HEADS):
                store_T(yh, hh, True)

    @when_step("moba_v")
    def _():
        for hh, yh in head_pairs(wm_ref, WIDE_HEADS):
            store_T(yh, hh, False)

    @when_step("moba_k")
    def _():
        for hh, yh in head_pairs(wm_ref, WIDE_HEADS):
            store_rows(yh, hh)

    def flatten_rows(yh, slot):
        rows_scr[...] = yh
        flat = [rows_scr[pl.ds(l, tm // CMP_STRIDE, stride=CMP_STRIDE), :] for l in range(CMP_STRIDE)]
        hc_ref[slot] = jnp.concatenate(flat, axis=1).astype(BF16)

    @when_step("kv_cmp_slc")
    def _():
        for slot in range(2 * NSA_KV_HEADS, OUT_BLOCK_HEADS):
            yT_ref[slot] = jnp.zeros(yT_ref.shape[1:], BF16)
            ystd_ref[slot] = jnp.zeros(ystd_ref.shape[1:], BF16)
        for hh, yh in head_pairs(wq_ref, WIDE_HEADS):
            if hh < N_CMP_HEADS:
                flatten_rows(yh, hh)
            elif hh < N_CMP_HEADS + NSA_KV_HEADS:
                store_rows(yh, hh - N_CMP_HEADS)
            else:
                store_T(yh, hh - N_CMP_HEADS - NSA_KV_HEADS, False)

    @when_step("kv_win")
    def _():
        for hh, yh in head_pairs(wkv_ref, NARROW_HEADS):
            if hh < NSA_KV_HEADS:
                store_rows(yh, NSA_KV_HEADS + hh)
            else:
                store_T(yh, hh, False)


def _in_proj(x2, mod3, w_nsa, w_moba, w_gate, gainsT, cosT, sinT, T):
    BT, D = x2.shape
    G = NSA_KV_HEADS
    tm = min(PROJ_TM, T)
    wide, narrow = WIDE_HEADS * HEAD_DIM, NARROW_HEADS * HEAD_DIM
    tpb = T // tm
    pick = lambda j, table: sum((j == t).astype(jnp.int32) * v for t, v in enumerate(table))
    gain_kind = lambda j: pick(j, [GAIN_KINDS.index(PROJ_GAIN[s]) if s in PROJ_GAIN else 0 for s in PROJ_STEPS])
    win_block = (NSA_HEADS + N_CMP_HEADS + 2 * NSA_KV_HEADS) * HEAD_DIM // narrow
    first_moba = PROJ_STEPS.index("moba_q")
    return pl.pallas_call(
        _inproj_kernel,
        grid=(BT // tm, len(PROJ_STEPS)),
        in_specs=[pl.BlockSpec((tm, D), lambda i, j: (i, 0)),
                  pl.BlockSpec((1, 6, D), lambda i, j: (i // tpb, 0, 0)),
                  pl.BlockSpec((D, wide), lambda i, j: (0, jnp.minimum(j, 1))),
                  pl.BlockSpec((D, narrow), lambda i, j: (0, win_block)),
                  pl.BlockSpec((D, wide), lambda i, j: (0, jnp.clip(j - first_moba, 0, 2))),
                  pl.BlockSpec((D, G * GATE_LANES), lambda i, j: (0, 0)),
                  pl.BlockSpec((1, HEAD_DIM, CHUNK), lambda i, j: (gain_kind(j), 0, 0)),
                  pl.BlockSpec((ROPE_HALF, tm), lambda i, j: (0, i % tpb)),
                  pl.BlockSpec((ROPE_HALF, tm), lambda i, j: (0, i % tpb))],
        out_specs=[pl.BlockSpec((OUT_BLOCK_HEADS, tm // CHUNK, HEAD_DIM, CHUNK), lambda i, j: (pick(j, YT_BLOCK), i, 0, 0)),
                   pl.BlockSpec((OUT_BLOCK_HEADS, tm, HEAD_DIM), lambda i, j: (pick(j, YSTD_BLOCK), i, 0)),
                   pl.BlockSpec((N_CMP_HEADS, tm // CMP_STRIDE, CMP_STRIDE * HEAD_DIM), lambda i, j: (0, i, 0)),
                   pl.BlockSpec((G, GATE_ROWS, tm), lambda i, j: (0, 0, i))],
        out_shape=[jax.ShapeDtypeStruct((N_T_HEADS, BT // CHUNK, HEAD_DIM, CHUNK), BF16),
                   jax.ShapeDtypeStruct((N_STD_HEADS, BT, HEAD_DIM), BF16),
                   jax.ShapeDtypeStruct((N_CMP_HEADS, BT // CMP_STRIDE, CMP_STRIDE * HEAD_DIM), BF16),
                   jax.ShapeDtypeStruct((G, GATE_ROWS, BT), F32)],
        scratch_shapes=[pltpu.VMEM((tm, D), BF16), pltpu.VMEM((tm, HEAD_DIM), F32)],
        compiler_params=_cparams(("parallel", "arbitrary")),
        name="in_proj",
    )(x2, mod3, w_nsa, w_nsa, w_moba, w_gate, gainsT, cosT, sinT)


def _compress_kernel(h_ref, w1c_ref, pe_ref, w1_ref, w2_ref, gain_ref, cc_ref, sa_ref, sb_ref, o_ref):
    a = pl.program_id(0)
    ncp = h_ref.shape[2]
    z = jnp.dot(h_ref[0, 0], w1c_ref[0], preferred_element_type=F32)
    top = z[:, :HEAD_DIM]
    bot = pltpu.roll(z[:, HEAD_DIM:], ncp - 1, 0)
    pe_term = jnp.dot(pe_ref[0], w1_ref[0], preferred_element_type=F32)[0:1, :]
    pre = top + bot + pe_term
    act = pre * jax.nn.sigmoid(pre)
    out = jnp.dot(act.astype(BF16), w2_ref[0], preferred_element_type=F32)
    live = lax.broadcasted_iota(jnp.int32, out.shape, 0) < ncp - 1
    out = jnp.where(live, out, 0.0)

    @pl.when(a < NSA_KV_HEADS)
    def _():
        ms = jnp.mean(out * out, axis=-1, keepdims=True)
        yn = out * lax.rsqrt(ms + EPS) * gain_ref[...]
        o_ref[0, 0] = _rope(yn, cc_ref[...], sa_ref[...], sb_ref[...]).astype(BF16)

    @pl.when(a >= NSA_KV_HEADS)
    def _():
        o_ref[0, 0] = out.astype(BF16)


def _compress(hc, w1cat, pe8, w1, w2, gain, cc, sa, sb):
    A, B, ncp, K = hc.shape
    G = NSA_KV_HEADS
    return pl.pallas_call(
        _compress_kernel,
        grid=(A, B),
        in_specs=[pl.BlockSpec((1, 1, ncp, K), lambda a, b: (a, b, 0, 0)),
                  pl.BlockSpec((1, K, 2 * HEAD_DIM), lambda a, b: (a // G, 0, 0)),
                  pl.BlockSpec((1, 8, 2 * K), lambda a, b: (a // G, 0, 0)),
                  pl.BlockSpec((1, 2 * K, HEAD_DIM), lambda a, b: (a // G, 0, 0)),
                  pl.BlockSpec((1, HEAD_DIM, HEAD_DIM), lambda a, b: (a // G, 0, 0)),
                  pl.BlockSpec((1, HEAD_DIM), lambda a, b: (0, 0)),
                  pl.BlockSpec((ncp, HEAD_DIM), lambda a, b: (0, 0)),
                  pl.BlockSpec((ncp, HEAD_DIM), lambda a, b: (0, 0)),
                  pl.BlockSpec((ncp, HEAD_DIM), lambda a, b: (0, 0))],
        out_specs=pl.BlockSpec((1, 1, ncp, HEAD_DIM), lambda a, b: (a, b, 0, 0)),
        out_shape=jax.ShapeDtypeStruct((A, B, ncp, HEAD_DIM), BF16),
        compiler_params=_cparams(("arbitrary", "arbitrary")),
        name="compress",
    )(hc, w1cat, pe8, w1, w2, gain, cc, sa, sb)


def _first_step(s, vT, m_scr, acc_scr):
    m = jnp.max(s, axis=0, keepdims=True)
    p = jnp.exp2(s - m)
    acc_scr[...] = jnp.dot(_with_ones(vT), p.astype(BF16), preferred_element_type=F32)
    m_scr[...] = m


def _with_ones(vT):
    return jnp.concatenate([vT, jnp.ones((ONES_ROWS, vT.shape[1]), vT.dtype)], axis=0)


def _normalized(acc):
    return acc[0:HEAD_DIM] / acc[HEAD_DIM:HEAD_DIM + 1]


def _online_step(s, vT, m_scr, acc_scr):
    m_prev = m_scr[...]
    m_new = jnp.maximum(m_prev, jnp.max(s, axis=0, keepdims=True))
    alpha = jnp.exp2(m_prev - m_new)
    p = jnp.exp2(s - m_new)
    acc_scr[...] = alpha * acc_scr[...] + jnp.dot(_with_ones(vT), p.astype(BF16), preferred_element_type=F32)
    m_scr[...] = m_new


def _rank_select(score, k):
    n = score.shape[0]
    rank = jnp.zeros(score.shape, F32)
    for m in range(n):
        sm = score[m:m + 1, :]
        lo = (m // SUBLANES) * SUBLANES
        hi = min(lo + SUBLANES, n)
        parts = []
        if lo > 0:
            parts.append(jnp.where(sm > score[:lo], 1.0, 0.0))
        gt = jnp.where(sm > score[lo:hi], 1.0, 0.0)
        ge = jnp.where(sm >= score[lo:hi], 1.0, 0.0)
        below = lax.broadcasted_iota(jnp.int32, gt.shape, 0) > m - lo
        parts.append(jnp.where(below, ge, gt))
        if hi < n:
            parts.append(jnp.where(sm >= score[hi:], 1.0, 0.0))
        rank = rank + (jnp.concatenate(parts, axis=0) if len(parts) > 1 else parts[0])
    return rank < k


def _pad_rows(a, rows):
    return jnp.concatenate([a, jnp.zeros((rows - a.shape[0], a.shape[1]), a.dtype)], axis=0)


def _masked_scores(k_ref, e_ref, start, size, q_aug):
    rows = pl.ds(pl.multiple_of(start, 128), size)
    k_aug = jnp.concatenate([k_ref[rows, :], e_ref[rows, :]], axis=1)
    return jnp.dot(k_aug, q_aug, preferred_element_type=F32)


def _pipelined_attention(chains, n_steps, n_max, tks, tk, body_pairs):
    per = tks // tk

    def scores(step, buf):
        st = jnp.minimum(step, n_max - 1)
        dead = jnp.where(step < n_steps, 0, 1)
        for k_ref, e_ref, _, qaug_ref, bufs, _, _ in chains:
            bufs[buf][...] = _masked_scores(k_ref, e_ref, st * tks, tks, qaug_ref[dead])

    def consume(step, buf):
        st = jnp.minimum(step, n_max - 1)
        for _, _, vT_ref, _, bufs, m_scr, acc_scr in chains:
            vT = jnp.concatenate([vT_ref[st * per + a] for a in range(per)], axis=1)
            _online_step(bufs[buf][...], vT, m_scr, acc_scr)

    def pairs(first_step, n_pairs_in_body):
        def body(it, carry):
            base = first_step + it * 2 * n_pairs_in_body
            for p in range(n_pairs_in_body):
                scores(base + 2 * p + 1, 1)
                consume(base + 2 * p, 0)
                scores(base + 2 * p + 2, 0)
                consume(base + 2 * p + 1, 1)
            return carry
        return body

    scores(0, 0)
    done = 0
    for n_body in body_pairs[:-1]:
        n_iter = (n_steps - done) // (2 * n_body)
        lax.fori_loop(0, n_iter, pairs(done, n_body), 0)
        done = done + n_iter * 2 * n_body
    lax.fori_loop(0, (n_steps - done + 1) // 2, pairs(done, 1), 0)


def _carry_casts(kernel_fn, n_in, n_out, n_cast):
    def wrapped(*refs):
        ins, cast_in = refs[:n_in], refs[n_in:n_in + n_cast]
        outs = refs[n_in + n_cast:n_in + n_cast + n_out]
        cast_out = refs[n_in + n_cast + n_out:n_in + n_cast + n_out + n_cast]
        for src, dst in zip(cast_in, cast_out):
            dst[...] = src[...].astype(BF16)
        kernel_fn(*ins, *outs, *refs[n_in + n_cast + n_out + n_cast:])
    return wrapped


def _cast_specs(weights, grid):
    n_steps = grid[0] * grid[1] * grid[2]
    specs, shapes = [], []
    for w, rows in weights:
        while w.shape[0] // rows > n_steps:
            rows *= 2
        n_blk = w.shape[0] // rows
        assert w.shape[0] % rows == 0
        index = lambda a, b, c, n_blk=n_blk: (jnp.minimum((a * grid[1] + b) * grid[2] + c, n_blk - 1), 0)
        specs.append(pl.BlockSpec((rows, w.shape[1]), index))
        shapes.append(jax.ShapeDtypeStruct(w.shape, BF16))
    return specs, shapes


def _nsa_body(qT_ref, kc_ref, vcT_ref, ks_ref, vsT_ref, kw_ref, vwT_ref, gT_ref, e_ref,
              gain_ref, o_ref, m_scr, acc_scr, comb_scr, qaug_scr, s0_scr, s1_scr, psum_scr, *, T):
    R, tq = NSA_GROUP, NSA_TQ
    i = pl.program_id(2)
    t0 = i * tq
    ncp = T // CMP_STRIDE
    nsb = T // SLC_BLOCK
    qT = jnp.concatenate([qT_ref[r, c] for r in range(R) for c in range(tq // CHUNK)], axis=1)
    gT = gT_ref[0]

    def per_head(row0):
        return jnp.concatenate([gT[row0 + 3 * r:row0 + 3 * r + 1, :] for r in range(R)], axis=1)

    def lanes_x_heads(a):
        return jnp.concatenate([a] * R, axis=1)

    s = jnp.dot(kc_ref[0, 0], qT, preferred_element_type=F32)
    c_idx = lax.broadcasted_iota(jnp.int32, (ncp, tq), 0)
    t_c = t0 + lax.broadcasted_iota(jnp.int32, (ncp, tq), 1)
    vis = lanes_x_heads((c_idx * CMP_STRIDE + (CMP_BLOCK - 1) <= t_c) & (c_idx < ncp - 1))
    s = jnp.where(vis, s, NEG)
    m = jnp.maximum(jnp.max(s, axis=0, keepdims=True), M_INIT)
    p = jnp.exp2(s - m)
    l = jnp.sum(p, axis=0, keepdims=True)
    p = p / jnp.where(l > 0.0, l, 1.0)
    o_cmp = jnp.dot(vcT_ref[0, 0], p.astype(BF16), preferred_element_type=F32)
    comb_scr[...] = o_cmp * per_head(0)

    psum = p[:, 0:tq]
    for r in range(1, R):
        psum = psum + p[:, r * tq:(r + 1) * tq]
    per = SLC_BLOCK // CMP_STRIDE
    imp_chunks = []
    for c in range(tq // CHUNK):
        psum_scr[c, 0:SUBLANES, :] = jnp.zeros((SUBLANES, CHUNK), F32)
        psum_scr[c, SUBLANES:, :] = psum[:, c * CHUNK:(c + 1) * CHUNK]
        acc = jnp.zeros((nsb, CHUNK), F32)
        for k in range(1 - CMP_BLOCK // CMP_STRIDE, per):
            acc = acc + psum_scr[c, pl.ds(SUBLANES + k, nsb, stride=per), :]
        imp_chunks.append(acc)
    imp = jnp.concatenate(imp_chunks, axis=1)
    n_idx = lax.broadcasted_iota(jnp.int32, (nsb, tq), 0)
    cur = (t0 + lax.broadcasted_iota(jnp.int32, (nsb, tq), 1)) // SLC_BLOCK
    forced = (n_idx == 0) | (n_idx == cur) | (n_idx == cur - 1)
    valid = n_idx <= cur
    score = jnp.where(valid, imp + jnp.where(forced, FORCE_BONUS, 0.0), NEG)
    sel = _rank_select(score, min(SLC_TOPK, nsb)) & valid
    sel_past = sel & (n_idx * SLC_BLOCK < t0)
    bias = _pad_rows(jnp.where(sel_past, 0.0, NEG), HEAD_DIM).astype(BF16)
    bias_own = _pad_rows(jnp.where(sel, 0.0, NEG), HEAD_DIM).astype(BF16)
    for d in range(3):
        qaug_scr[d, 0:HEAD_DIM, :] = qT
    qaug_scr[0, HEAD_DIM:, :] = lanes_x_heads(bias)
    qaug_scr[1, HEAD_DIM:, :] = jnp.full((HEAD_DIM, R * tq), NEG, BF16)
    qaug_scr[2, HEAD_DIM:, :] = lanes_x_heads(bias_own)

    w0 = jnp.maximum(t0 + tq - WIN_SPAN, 0)
    k_w = kw_ref[0, pl.ds(pl.multiple_of(w0, CHUNK), WIN_SPAN), :]
    s_w = jnp.dot(k_w, qT, preferred_element_type=F32)
    kpos = w0 + lax.broadcasted_iota(jnp.int32, (WIN_SPAN, tq), 0)
    tpos = t0 + lax.broadcasted_iota(jnp.int32, (WIN_SPAN, tq), 1)
    ok = lanes_x_heads((kpos <= tpos) & (tpos - kpos < WINDOW))
    s_w = jnp.where(ok, s_w, NEG)
    m_w = jnp.max(s_w, axis=0, keepdims=True)
    p_w = jnp.exp2(s_w - m_w)
    jw = w0 // CHUNK
    v_w = jnp.concatenate([vwT_ref[0, jw + a] for a in range(WIN_SPAN // CHUNK)], axis=1)
    o_w = jnp.dot(_with_ones(v_w), p_w.astype(BF16), preferred_element_type=F32)
    comb_scr[...] += _normalized(o_w) * per_head(2)

    tri =lax.broadcasted_iota(jnp.int32, (tq, tq), 0) <= lax.broadcasted_iota(jnp.int32, (tq, tq), 1)
    s_d = jnp.where(lanes_x_heads(tri), _masked_scores(ks_ref.at[0], e_ref, t0, tq, qaug_scr[2]), NEG)
    own = tq // CHUNK
    v_d = jnp.concatenate([vsT_ref[0, i * own + a] for a in range(own)], axis=1)
    _first_step(s_d, v_d, m_scr, acc_scr)
    halves = [slice(h * (R // 2) * tq, (h + 1) * (R // 2) * tq) for h in range(2)]
    yield [(ks_ref.at[0], e_ref, vsT_ref.at[0], qaug_scr.at[:, :, ln], (s0_scr.at[:, ln], s1_scr.at[:, ln]),
            m_scr.at[:, ln], acc_scr.at[:, ln]) for ln in halves]
    comb = comb_scr[...] + _normalized(acc_scr[...]) * per_head(1)

    for r in range(R):
        oT = comb[:, r * tq:(r + 1) * tq]
        ms = jnp.mean(oT * oT, axis=0, keepdims=True)
        on = (oT * lax.rsqrt(ms + EPS)).T * gain_ref[0, r:r + 1, :]
        o_ref[:, r * HEAD_DIM:(r + 1) * HEAD_DIM] = on.astype(BF16)


def _moba_body(qT_ref, k_ref, vT_ref, e_ref, gain_ref, o_ref, m_scr, acc_scr, kmean_scr, qaug_scr,
               s_scr, *, T):
    tq, tk = MOBA_TQ, MOBA_TK
    nb = T // MOBA_BLOCK
    nbp = kmean_scr.shape[1]
    i = pl.program_id(2)
    t0 = i * tq

    @pl.when(i == 0)
    def _():
        kmean_scr[...] = jnp.zeros(kmean_scr.shape, F32)
        for a in range(MOBA_HB):
            kb = k_ref[a].astype(F32).reshape(nb, MOBA_BLOCK, HEAD_DIM)
            kmean_scr[a, 0:nb, :] = jnp.mean(kb, axis=1)

    n_idx = lax.broadcasted_iota(jnp.int32, (nbp, tq), 0)
    cur = (t0 + lax.broadcasted_iota(jnp.int32, (nbp, tq), 1)) // MOBA_BLOCK
    past = n_idx < cur
    causal = lax.broadcasted_iota(jnp.int32, (tk, tq), 0) <= lax.broadcasted_iota(jnp.int32, (tk, tq), 1)

    qTs = [jnp.concatenate([qT_ref[a, c] for c in range(tq // CHUNK)], axis=1) for a in range(MOBA_HB)]
    for a in range(MOBA_HB):
        qT = qTs[a]
        gate = jnp.zeros((nbp, tq), F32)
        for part in _split3(kmean_scr[a]):
            gate = gate + jnp.dot(part, qT, preferred_element_type=F32)
        sel = _rank_select(jnp.where(past, gate, NEG), min(MOBA_TOPK, nb)) & past
        for d in range(2):
            qaug_scr[a, d, 0:HEAD_DIM, :] = qT
        qaug_scr[a, 0, HEAD_DIM:, :] = _pad_rows(jnp.where(sel, 0.0, NEG), HEAD_DIM).astype(BF16)
        qaug_scr[a, 1, HEAD_DIM:, :] = jnp.full((HEAD_DIM, tq), NEG, BF16)

    s_own = []
    for a in range(MOBA_HB):
        m_scr[a] = jnp.full(m_scr.shape[1:], M_INIT, F32)
        acc_scr[a] = jnp.zeros(acc_scr.shape[1:], F32)
        k_own = k_ref[a, pl.ds(pl.multiple_of(t0, tk), tk), :]
        s_own.append(jnp.where(causal, jnp.dot(k_own, qTs[a], preferred_element_type=F32), NEG))

    yield [(k_ref.at[a], e_ref, vT_ref.at[a], qaug_scr.at[a], (s_scr.at[a, 0], s_scr.at[a, 1]),
            m_scr.at[a], acc_scr.at[a]) for a in range(MOBA_HB)]

    own = tq // CHUNK
    for a in range(MOBA_HB):
        v_own = jnp.concatenate([vT_ref[a, i * own + c] for c in range(own)], axis=1)
        _online_step(s_own[a], v_own, m_scr.at[a], acc_scr.at[a])
        oT = _normalized(acc_scr[a])
        ms = jnp.mean(oT * oT, axis=0, keepdims=True)
        on = (oT * lax.rsqrt(ms + EPS)).T * gain_ref[a]
        o_ref[:, a * HEAD_DIM:(a + 1) * HEAD_DIM] = on.astype(BF16)


N_NSA_IN, N_NSA_SCRATCH, N_MOBA_IN = 10, 7, 5


def _attention_kernel(*refs, T):
    nsa_in, moba_in = refs[:N_NSA_IN], refs[N_NSA_IN:N_NSA_IN + N_MOBA_IN]
    o_nsa, o_moba = refs[N_NSA_IN + N_MOBA_IN:N_NSA_IN + N_MOBA_IN + 2]
    scratch = refs[N_NSA_IN + N_MOBA_IN + 2:]
    moba = _moba_body(*moba_in, o_moba, *scratch[N_NSA_SCRATCH:], T=T)
    nsa = _nsa_body(*nsa_in, o_nsa, *scratch[:N_NSA_SCRATCH], T=T)
    chains = next(moba) + next(nsa)
    n_steps = pl.program_id(2) * ATTN_TQ // ATTN_TKS
    _pipelined_attention(chains, n_steps, T // ATTN_TKS, ATTN_TKS, CHUNK, ATTN_BODY_PAIRS)
    for rest in (nsa, moba):
        assert next(rest, None) is None


def _attention(yT, ystd, kcvc, vcT, gT, e_slc, e_moba, nsa_gains, moba_gains, casts, B, T):
    G, R, tq = NSA_KV_HEADS, NSA_GROUP, ATTN_TQ
    H, HB = MOBA_HEADS, MOBA_HB
    assert H // HB == G and NSA_TQ == MOBA_TQ == tq and SLC_TKS == MOBA_TKS == ATTN_TKS and tq % ATTN_TKS == 0
    nq = T // tq
    ncp = T // CMP_STRIDE
    nb = T // MOBA_BLOCK
    nbp = max(16, nb)
    grid = (B, G, nq)
    cast_specs, cast_shapes = _cast_specs(casts, grid)
    kern = _carry_casts(functools.partial(_attention_kernel, T=T), N_NSA_IN + N_MOBA_IN, 2, len(casts))
    keys = lambda head0: pl.BlockSpec((1, T, HEAD_DIM), lambda b, g, i: (head0 + g, b, 0))
    values = lambda head0: pl.BlockSpec((1, T // CHUNK, HEAD_DIM, CHUNK), lambda b, g, i: (head0 + g, b, 0, 0))
    onehot = pl.BlockSpec((T, HEAD_DIM), lambda b, g, i: (0, 0))
    nsa_specs = [pl.BlockSpec((R, tq // CHUNK, HEAD_DIM, CHUNK), lambda b, g, i: (T_NSA_Q // R + g, b * nq + i, 0, 0)),
                 pl.BlockSpec((1, 1, ncp, HEAD_DIM), lambda b, g, i: (g, b, 0, 0)),
                 pl.BlockSpec((1, 1, HEAD_DIM, ncp), lambda b, g, i: (g, b, 0, 0)),
                 keys(S_KSLC), values(T_VSLC), keys(S_KWIN), values(T_VWIN),
                 pl.BlockSpec((1, GATE_ROWS, tq), lambda b, g, i: (g, 0, b * nq + i)),
                 onehot,
                 pl.BlockSpec((1, R, HEAD_DIM), lambda b, g, i: (g, 0, 0))]
    moba_specs = [pl.BlockSpec((HB, tq // CHUNK, HEAD_DIM, CHUNK), lambda b, h, i: (T_MOBA_Q // HB + h, b * nq + i, 0, 0)),
                  pl.BlockSpec((HB, T, HEAD_DIM), lambda b, h, i: (S_MOBA_K // HB + h, b, 0)),
                  pl.BlockSpec((HB, T // CHUNK, HEAD_DIM, CHUNK), lambda b, h, i: (T_MOBA_V // HB + h, b, 0, 0)),
                  onehot,
                  pl.BlockSpec((HB, 1, HEAD_DIM), lambda b, h, i: (h, 0, 0))]
    nsa_scratch = [pltpu.VMEM((1, R * tq), F32),
                   pltpu.VMEM((HEAD_DIM + ONES_ROWS, R * tq), F32), pltpu.VMEM((HEAD_DIM, R * tq), F32),
                   pltpu.VMEM((3, 2 * HEAD_DIM, R * tq), BF16),
                   pltpu.VMEM((ATTN_TKS, R * tq), F32), pltpu.VMEM((ATTN_TKS, R * tq), F32),
                   pltpu.VMEM((tq // CHUNK, SUBLANES + ncp, CHUNK), F32)]
    moba_scratch = [pltpu.VMEM((HB, 1, tq), F32),
                    pltpu.VMEM((HB, HEAD_DIM + ONES_ROWS, tq), F32), pltpu.VMEM((HB, nbp, HEAD_DIM), F32),
                    pltpu.VMEM((HB, 2, 2 * HEAD_DIM, tq), BF16),
                    pltpu.VMEM((HB, 2, ATTN_TKS, tq), F32)]
    assert len(nsa_specs) == N_NSA_IN and len(moba_specs) == N_MOBA_IN and len(nsa_scratch) == N_NSA_SCRATCH
    return pl.pallas_call(
        kern,
        grid=grid,
        in_specs=nsa_specs + moba_specs + cast_specs,
        out_specs=[pl.BlockSpec((tq, R * HEAD_DIM), lambda b, g, i: (b * nq + i, g)),
                   pl.BlockSpec((tq, HB * HEAD_DIM), lambda b, h, i: (b * nq + i, h))] + cast_specs,
        out_shape=[jax.ShapeDtypeStruct((B * T, NSA_HEADS * HEAD_DIM), BF16),
                   jax.ShapeDtypeStruct((B * T, H * HEAD_DIM), BF16)] + cast_shapes,
        scratch_shapes=nsa_scratch + moba_scratch,
        compiler_params=_cparams(("parallel", "parallel", "arbitrary")),
        name="attention",
    )(yT, kcvc, vcT, ystd, yT, ystd, yT, gT, e_slc, nsa_gains,
      yT, ystd, yT, e_moba, moba_gains, *[w for w, _ in casts])


def _outproj_kernel(on_ref, om_ref, w_ref, x_ref, mod_ref, o_ref):
    half = on_ref.shape[1]
    acc = jnp.dot(on_ref[...], w_ref[0:half, :], preferred_element_type=F32)
    acc = acc + jnp.dot(om_ref[...], w_ref[half:, :], preferred_element_type=F32)
    o_ref[...] = x_ref[...] + mod_ref[0, 2:3, :] * acc


def _out_proj(o_nsa, o_moba, w_out, x2, mod3, T):
    BT, D = x2.shape
    tm = min(OUT_TM, T)
    tpb = T // tm
    half = o_nsa.shape[1]
    return pl.pallas_call(
        _outproj_kernel,
        grid=(BT // tm,),
        in_specs=[pl.BlockSpec((tm, half), lambda i: (i, 0)),
                  pl.BlockSpec((tm, half), lambda i: (i, 0)),
                  pl.BlockSpec((D, D), lambda i: (0, 0)),
                  pl.BlockSpec((tm, D), lambda i: (i, 0)),
                  pl.BlockSpec((1, 6, D), lambda i: (i // tpb, 0, 0))],
        out_specs=pl.BlockSpec((tm, D), lambda i: (i, 0)),
        out_shape=jax.ShapeDtypeStruct((BT, D), F32),
        compiler_params=_cparams(("parallel",)),
        name="out_proj",
    )(o_nsa, o_moba, w_out, x2, mod3)


def _ffn_kernel(x_ref, mod_ref, wg_ref, wu_ref, wo_ref, o_ref, h_scr):
    j = pl.program_id(1)

    @pl.when(j == 0)
    def _():
        x = x_ref[...]
        ms = jnp.mean(x * x, axis=-1, keepdims=True)
        h = x * lax.rsqrt(ms + EPS) * (1.0 + mod_ref[0, 4:5, :]) + mod_ref[0, 3:4, :]
        h_scr[...] = h.astype(BF16)
        o_ref[...] = jnp.zeros(o_ref.shape, F32)

    hb = h_scr[...]
    half = wg_ref.shape[1] // 2
    down = None
    for c in range(2):
        cols = slice(c * half, (c + 1) * half)
        gate = jnp.dot(hb, wg_ref[:, cols], preferred_element_type=F32)
        up = jnp.dot(hb, wu_ref[:, cols], preferred_element_type=F32)
        act = (gate * jax.nn.sigmoid(gate) * up).astype(BF16)
        part = jnp.dot(act, wo_ref[cols, :], preferred_element_type=F32)
        down = part if down is None else down + part
    o_ref[...] += down

    @pl.when(j == pl.num_programs(1) - 1)
    def _():
        o_ref[...] = x_ref[...] + mod_ref[0, 5:6, :] * o_ref[...]


def _ffn(x1, mod3, w_in, w_out, T):
    BT, D = x1.shape
    Fh = w_out.shape[0]
    tm = min(FFN_TM, T)
    tf = FFN_TF
    tpb = T // tm
    nf = Fh // tf
    return pl.pallas_call(
        _ffn_kernel,
        grid=(BT // tm, nf),
        in_specs=[pl.BlockSpec((tm, D), lambda i, j: (i, 0)),
                  pl.BlockSpec((1, 6, D), lambda i, j: (i // tpb, 0, 0)),
                  pl.BlockSpec((D, tf), lambda i, j: (0, j)),
                  pl.BlockSpec((D, tf), lambda i, j: (0, nf + j)),
                  pl.BlockSpec((tf, D), lambda i, j: (j, 0))],
        out_specs=pl.BlockSpec((tm, D), lambda i, j: (i, 0)),
        out_shape=jax.ShapeDtypeStruct((BT, D), F32),
        scratch_shapes=[pltpu.VMEM((tm, D), BF16)],
        compiler_params=_cparams(("parallel", "arbitrary")),
        name="ffn",
    )(x1, mod3, w_in, w_in, w_out)


def _rope_cos_sin(pos):
    inv = ROPE_THETA ** (-jnp.arange(0, ROPE_DIMS, 2, dtype=F32) / ROPE_DIMS)
    ang = pos.astype(F32)[:, None] * inv[None, :]
    return jnp.cos(ang), jnp.sin(ang)


def _rope_tables(T):
    c_hi, s_hi = _rope_cos_sin(jnp.arange(0, T, ROPE_SPLIT))
    c_lo, s_lo = _rope_cos_sin(jnp.arange(ROPE_SPLIT))
    cos = (c_hi[:, None] * c_lo[None] - s_hi[:, None] * s_lo[None]).reshape(T, ROPE_HALF)
    sin = (s_hi[:, None] * c_lo[None] + c_hi[:, None] * s_lo[None]).reshape(T, ROPE_HALF)
    cos_c, sin_c = _rope_cos_sin(jnp.arange(CMP_BLOCK - 1, T, CMP_STRIDE))
    n, rest = cos_c.shape[0], HEAD_DIM - ROPE_DIMS
    pad = lambda t: jnp.pad(t, ((0, 1), (0, 0)))
    cc = pad(jnp.concatenate([cos_c, cos_c, jnp.ones((n, rest), F32)], axis=1))
    sa = pad(jnp.concatenate([-sin_c, jnp.zeros((n, HEAD_DIM - ROPE_HALF), F32)], axis=1))
    sb = pad(jnp.concatenate([jnp.zeros((n, ROPE_HALF), F32), sin_c, jnp.zeros((n, rest), F32)], axis=1))
    return cc, sa, sb, cos.T, sin.T


def _block_onehot(T, block):
    return (jnp.arange(T)[:, None] // block == jnp.arange(HEAD_DIM)[None, :]).astype(BF16)


def _layer(x2, c, B, T, w_ada, b_ada, w_in, nsa_q_norm, nsa_k_norm, moba_q_norm, moba_k_norm,
           cmp_pe_k, cmp_w1_k, cmp_w2_k, cmp_pe_v, cmp_w1_v, cmp_w2_v, out_norm, w_out,
           w_ffn_in, w_ffn_out):
    D = x2.shape[1]
    G = NSA_KV_HEADS
    scale = HEAD_DIM ** -0.5 * LOG2E
    assert T % MOBA_BLOCK == 0 and T % SLC_TKS == 0 and T % MOBA_TKS == 0 and T % NSA_TQ == 0 and T >= WIN_SPAN
    assert T // SLC_BLOCK <= HEAD_DIM and T // MOBA_BLOCK <= HEAD_DIM

    nsa_w = NSA_HEADS * HEAD_DIM + 6 * G * HEAD_DIM
    gw = NSA_HEADS * 3
    mod3 = _adaln(c, w_ada, b_ada)
    w_nsa = w_in.astype(BF16)
    w_moba = w_nsa[:, nsa_w + gw:]

    wg = w_in[:, nsa_w:nsa_w + gw].reshape(D, G, NSA_GROUP * 3)
    w_gate = jnp.pad(wg, ((0, 0), (0, 0), (0, GATE_LANES - NSA_GROUP * 3))).reshape(D, G * GATE_LANES)
    w_gate = w_gate.astype(BF16)
    across = lambda g_: jnp.broadcast_to(g_[:, None], (HEAD_DIM, CHUNK))
    gainsT = jnp.stack([across(nsa_q_norm * scale), across(moba_q_norm * scale),
                        across(nsa_k_norm[1]), across(nsa_k_norm[2]), across(moba_k_norm)])
    cc, sa, sb, cosT, sinT = _rope_tables(T)

    yT, ystd, hc, gT = _in_proj(x2, mod3, w_nsa, w_moba, w_gate, gainsT, cosT, sinT, T)

    ncp = T // CMP_STRIDE
    half = CMP_STRIDE * HEAD_DIM
    w1 = jnp.stack([cmp_w1_k, cmp_w1_v]).astype(BF16)
    w1cat = jnp.concatenate([w1[:, :half], w1[:, half:]], axis=2)
    pe8 = jnp.broadcast_to(jnp.stack([cmp_pe_k, cmp_pe_v]).reshape(2, 1, 2 * half), (2, 8, 2 * half))
    w2 = jnp.stack([cmp_w2_k, cmp_w2_v]).astype(BF16)
    kcvc = _compress(hc.reshape(2 * G, B, ncp, half), w1cat, pe8.astype(BF16), w1, w2,
                     nsa_k_norm[0].reshape(1, HEAD_DIM), cc, sa, sb)
    vcT = kcvc[G:].transpose(0, 1, 3, 2)

    on = out_norm.reshape(N_HEADS, HEAD_DIM)
    o_nsa, o_moba, w_ffn_out_b, w_out_b, w_ffn_in_b = _attention(
        yT, ystd, kcvc, vcT, gT, _block_onehot(T, SLC_BLOCK), _block_onehot(T, MOBA_BLOCK),
        on[:NSA_HEADS].reshape(G, NSA_GROUP, HEAD_DIM), on[NSA_HEADS:].reshape(MOBA_HEADS, 1, HEAD_DIM),
        [(w_ffn_out, 128), (w_out, 128), (w_ffn_in, 32)], B, T)

    x1 = _out_proj(o_nsa, o_moba, w_out_b, x2, mod3, T)
    return _ffn(x1, mod3, w_ffn_in_b, w_ffn_out_b, T)


def kernel(x, c, w_ada, b_ada, w_in, nsa_q_norm, nsa_k_norm, moba_q_norm, moba_k_norm, cmp_pe_k, cmp_w1_k, cmp_w2_k, cmp_pe_v, cmp_w1_v, cmp_w2_v, out_norm, w_out, w_ffn_in, w_ffn_out):
    B, T, D = x.shape
    x2 = x.reshape(B * T, D)
    for l in range(w_ada.shape[0]):
        x2 = _layer(x2, c, B, T, w_ada[l], b_ada[l], w_in[l], nsa_q_norm[l], nsa_k_norm[l],
                    moba_q_norm[l], moba_k_norm[l], cmp_pe_k[l], cmp_w1_k[l], cmp_w2_k[l],
                    cmp_pe_v[l], cmp_w1_v[l], cmp_w2_v[l], out_norm[l], w_out[l],
                    w_ffn_in[l], w_ffn_out[l])
    return x2.reshape(B, T, D)
```

```python
import functools

import jax
import jax.numpy as jnp
from jax import lax
from jax.experimental import pallas as pl
from jax.experimental.pallas import tpu as pltpu

F32 = jnp.float32
BF16 = jnp.bfloat16

HEAD_DIM = 128
SUBLANES = 8
NSA_HEADS = 8
NSA_KV_HEADS = 2
NSA_GROUP = NSA_HEADS // NSA_KV_HEADS
MOBA_HEADS = 8
N_HEADS = NSA_HEADS + MOBA_HEADS
CMP_BLOCK = 32
CMP_STRIDE = 16
SLC_BLOCK = 64
SLC_TOPK = 16
WINDOW = 512
FORCE_BONUS = 1e4
MOBA_BLOCK = 256
MOBA_TOPK = 3
ROPE_THETA = 500000.0
ROPE_DIMS = HEAD_DIM // 4
ROPE_HALF = ROPE_DIMS // 2
ROPE_SPLIT = 64
EPS = 1e-6
LOG2E = 1.4426950408889634
ONES_ROWS = 16
NEG = -1e30
M_INIT = -1e29

V7X_VMEM_BYTES = 64 * 1024 * 1024
VMEM_LIMIT = V7X_VMEM_BYTES - 8 * 1024 * 1024

PROJ_STEPS = ("nsa_q", "kv_cmp_slc", "kv_win", "moba_q", "moba_k", "moba_v")
WIDE_HEADS, NARROW_HEADS = 8, 4
OUT_BLOCK_HEADS = 8
YT_BLOCK = (0, 3, 3, 1, 1, 2)
YSTD_BLOCK = (1, 1, 1, 0, 0, 0)
GAIN_KINDS = ("nsa_q", "moba_q", "k_slc", "k_win", "moba_k")
PROJ_GAIN = {"nsa_q": "nsa_q", "moba_q": "moba_q", "kv_cmp_slc": "k_slc", "kv_win": "k_win", "moba_k": "moba_k"}
T_NSA_Q, T_MOBA_Q, T_MOBA_V, T_VSLC, T_VWIN = 0, 8, 16, 24, 26
S_MOBA_K, S_KSLC, S_KWIN = 0, 8, 10
N_T_HEADS, N_STD_HEADS, N_CMP_HEADS = 32, 16, 4
CHUNK = 128
GATE_LANES = 128
GATE_ROWS = 16

ATTN_TQ = 256
ATTN_TKS = 256
ATTN_BODY_PAIRS = (4, 2, 1)
NSA_TQ = ATTN_TQ
SLC_TKS = ATTN_TKS
WIN_SPAN = WINDOW + NSA_TQ
MOBA_TQ = ATTN_TQ
MOBA_TK = MOBA_BLOCK
MOBA_TKS = ATTN_TKS
MOBA_HB = 4

ADALN_TN = 1024
PROJ_TM = 1024
FFN_TM, FFN_TF = 512, 512


def _cparams(sem):
    return pltpu.CompilerParams(dimension_semantics=sem, vmem_limit_bytes=VMEM_LIMIT)


def _split3(a):
    hi = a.astype(BF16)
    r1 = a - hi.astype(F32)
    mid = r1.astype(BF16)
    lo = (r1 - mid.astype(F32)).astype(BF16)
    return hi, mid, lo


def _adaln_kernel(c_ref, w_ref, b_ref, o_ref):
    cv = c_ref[...]
    s = cv * jax.nn.sigmoid(cv)
    w = w_ref[...].astype(BF16)
    acc = b_ref[...] + jnp.zeros(o_ref.shape, F32)
    for part in _split3(s)[:2]:
        acc = acc + jnp.dot(part, w, preferred_element_type=F32)
    o_ref[...] = acc


def _adaln(c, w_ada, b_ada):
    B, D = c.shape
    N = w_ada.shape[1]
    tn = ADALN_TN
    c8 = jnp.zeros((8, D), F32).at[:B].set(c)
    out = pl.pallas_call(
        _adaln_kernel,
        grid=(N // tn,),
        in_specs=[pl.BlockSpec((8, D), lambda j: (0, 0)),
                  pl.BlockSpec((D, tn), lambda j: (0, j)),
                  pl.BlockSpec((1, tn), lambda j: (0, j))],
        out_specs=pl.BlockSpec((8, tn), lambda j: (0, j)),
        out_shape=jax.ShapeDtypeStruct((8, N), F32),
        compiler_params=_cparams(("arbitrary",)),
        name="adaln",
    )(c8, w_ada, b_ada.reshape(1, N))
    return out[:B].reshape(B, 6, D)


def _rope(y, cc, sa, sb):
    return (y * cc + pltpu.roll(y, HEAD_DIM - ROPE_HALF, 1) * sa + pltpu.roll(y, ROPE_HALF, 1) * sb)


def _inproj_kernel(x_ref, mod_ref, wq_ref, wkv_ref, wm_ref, wg_ref, gainT_ref, cosT_ref, sinT_ref,
                   yT_ref, ystd_ref, hc_ref, gT_ref, h_scr, rows_scr):
    j = pl.program_id(1)
    tm = x_ref.shape[0]
    n_chunks = tm // CHUNK
    pair = 2 * HEAD_DIM

    @pl.when(j == 0)
    def _():
        x = x_ref[...]
        ms = jnp.mean(x * x, axis=-1, keepdims=True)
        h = x * lax.rsqrt(ms + EPS) * (1.0 + mod_ref[0, 1:2, :]) + mod_ref[0, 0:1, :]
        hb = h.astype(BF16)
        h_scr[...] = hb
        g = jax.nn.sigmoid(jnp.dot(hb, wg_ref[...], preferred_element_type=F32))
        for grp in range(NSA_KV_HEADS):
            for c in range(n_chunks):
                blk = g[c * CHUNK:(c + 1) * CHUNK, grp * GATE_LANES:(grp + 1) * GATE_LANES].T
                gT_ref[grp, :, c * CHUNK:(c + 1) * CHUNK] = blk[0:GATE_ROWS, :]

    def head_pairs(w_ref, n_heads):
        for half in range(n_heads // 2):
            acc = jnp.dot(h_scr[...], w_ref[:, half * pair:(half + 1) * pair], preferred_element_type=F32)
            for h2 in range(2):
                yield 2 * half + h2, acc[:, h2 * HEAD_DIM:(h2 + 1) * HEAD_DIM]

    def qk_chunk(t, c):
        ms = jnp.mean(t * t, axis=0, keepdims=True)
        tn = t * lax.rsqrt(ms + EPS) * gainT_ref[0]
        cs = cosT_ref[:, c * CHUNK:(c + 1) * CHUNK]
        sn = sinT_ref[:, c * CHUNK:(c + 1) * CHUNK]
        a, b = tn[0:ROPE_HALF], tn[ROPE_HALF:ROPE_DIMS]
        return jnp.concatenate([a * cs - b * sn, b * cs + a * sn, tn[ROPE_DIMS:]], axis=0)

    def store_T(yh, slot, treated):
        for c in range(n_chunks):
            chunk = yh[c * CHUNK:(c + 1) * CHUNK, :]
            if treated:
                yT_ref[slot, c] = qk_chunk(chunk.T, c).astype(BF16)
            else:
                yT_ref[slot, c] = chunk.astype(BF16).T

    def store_rows(yh, slot):
        for c in range(n_chunks):
            rows = slice(c * CHUNK, (c + 1) * CHUNK)
            ystd_ref[slot, rows, :] = qk_chunk(yh[rows, :].T, c).astype(BF16).T

    def when_step(kind):
        return pl.when(j == PROJ_STEPS.index(kind))

    for kind, w_ref in (("nsa_q", wq_ref), ("moba_q", wm_ref)):
        @when_step(kind)
        def _(w_ref=w_ref):
            for hh, yh in head_pairs(w_ref, WIDE_HEADS):
                store_T(yh, hh, True)

    @when_step("moba_v")
    def _():
        for hh, yh in head_pairs(wm_ref, WIDE_HEADS):
            store_T(yh, hh, False)

    @when_step("moba_k")
    def _():
        for hh, yh in head_pairs(wm_ref, WIDE_HEADS):
            store_rows(yh, hh)

    def flatten_rows(yh, slot):
        rows_scr[...] = yh
        flat = [rows_scr[pl.ds(l, tm // CMP_STRIDE, stride=CMP_STRIDE), :] for l in range(CMP_STRIDE)]
        hc_ref[slot] = jnp.concatenate(flat, axis=1).astype(BF16)

    @when_step("kv_cmp_slc")
    def _():
        for slot in range(2 * NSA_KV_HEADS, OUT_BLOCK_HEADS):
            yT_ref[slot] = jnp.zeros(yT_ref.shape[1:], BF16)
            ystd_ref[slot] = jnp.zeros(ystd_ref.shape[1:], BF16)
        for hh, yh in head_pairs(wq_ref, WIDE_HEADS):
            if hh < N_CMP_HEADS:
                flatten_rows(yh, hh)
            elif hh < N_CMP_HEADS + NSA_KV_HEADS:
                store_rows(yh, hh - N_CMP_HEADS)
            else:
                store_T(yh, hh - N_CMP_HEADS - NSA_KV_HEADS, False)

    @when_step("kv_win")
    def _():
        for hh, yh in head_pairs(wkv_ref, NARROW_HEADS):
            if hh < NSA_KV_HEADS:
                store_rows(yh, NSA_KV_HEADS + hh)
            else:
                store_T(yh, hh, False)


def _in_proj(x2, mod3, w_nsa, w_moba, w_gate, gainsT, cosT, sinT, T):
    BT, D = x2.shape
    G = NSA_KV_HEADS
    tm = min(PROJ_TM, T)
    wide, narrow = WIDE_HEADS * HEAD_DIM, NARROW_HEADS * HEAD_DIM
    tpb = T // tm
    pick = lambda j, table: sum((j == t).astype(jnp.int32) * v for t, v in enumerate(table))
    gain_kind = lambda j: pick(j, [GAIN_KINDS.index(PROJ_GAIN[s]) if s in PROJ_GAIN else 0 for s in PROJ_STEPS])
    win_block = (NSA_HEADS + N_CMP_HEADS + 2 * NSA_KV_HEADS) * HEAD_DIM // narrow
    first_moba = PROJ_STEPS.index("moba_q")
    return pl.pallas_call(
        _inproj_kernel,
        grid=(BT // tm, len(PROJ_STEPS)),
        in_specs=[pl.BlockSpec((tm, D), lambda i, j: (i, 0)),
                  pl.BlockSpec((1, 6, D), lambda i, j: (i // tpb, 0, 0)),
                  pl.BlockSpec((D, wide), lambda i, j: (0, jnp.minimum(j, 1))),
                  pl.BlockSpec((D, narrow), lambda i, j: (0, win_block)),
                  pl.BlockSpec((D, wide), lambda i, j: (0, jnp.clip(j - first_moba, 0, 2))),
                  pl.BlockSpec((D, G * GATE_LANES), lambda i, j: (0, 0)),
                  pl.BlockSpec((1, HEAD_DIM, CHUNK), lambda i, j: (gain_kind(j), 0, 0)),
                  pl.BlockSpec((ROPE_HALF, tm), lambda i, j: (0, i % tpb)),
                  pl.BlockSpec((ROPE_HALF, tm), lambda i, j: (0, i % tpb))],
        out_specs=[pl.BlockSpec((OUT_BLOCK_HEADS, tm // CHUNK, HEAD_DIM, CHUNK), lambda i, j: (pick(j, YT_BLOCK), i, 0, 0)),
                   pl.BlockSpec((OUT_BLOCK_HEADS, tm, HEAD_DIM), lambda i, j: (pick(j, YSTD_BLOCK), i, 0)),
                   pl.BlockSpec((N_CMP_HEADS, tm // CMP_STRIDE, CMP_STRIDE * HEAD_DIM), lambda i, j: (0, i, 0)),
                   pl.BlockSpec((G, GATE_ROWS, tm), lambda i, j: (0, 0, i))],
        out_shape=[jax.ShapeDtypeStruct((N_T_HEADS, BT // CHUNK, HEAD_DIM, CHUNK), BF16),
                   jax.ShapeDtypeStruct((N_STD_HEADS, BT, HEAD_DIM), BF16),
                   jax.ShapeDtypeStruct((N_CMP_HEADS, BT // CMP_STRIDE, CMP_STRIDE * HEAD_DIM), BF16),
                   jax.ShapeDtypeStruct((G, GATE_ROWS, BT), F32)],
        scratch_shapes=[pltpu.VMEM((tm, D), BF16), pltpu.VMEM((tm, HEAD_DIM), F32)],
        compiler_params=_cparams(("parallel", "arbitrary")),
        name="in_proj",
    )(x2, mod3, w_nsa, w_nsa, w_moba, w_gate, gainsT, cosT, sinT)


def _compress_kernel(h_ref, w1c_ref, pe_ref, w1_ref, w2_ref, gain_ref, cc_ref, sa_ref, sb_ref, o_ref):
    a = pl.program_id(0)
    ncp = h_ref.shape[2]
    z = jnp.dot(h_ref[0, 0], w1c_ref[0], preferred_element_type=F32)
    top = z[:, :HEAD_DIM]
    bot = pltpu.roll(z[:, HEAD_DIM:], ncp - 1, 0)
    pe_term = jnp.dot(pe_ref[0], w1_ref[0], preferred_element_type=F32)[0:1, :]
    pre = top + bot + pe_term
    act = pre * jax.nn.sigmoid(pre)
    out = jnp.dot(act.astype(BF16), w2_ref[0], preferred_element_type=F32)
    live = lax.broadcasted_iota(jnp.int32, out.shape, 0) < ncp - 1
    out = jnp.where(live, out, 0.0)

    @pl.when(a < NSA_KV_HEADS)
    def _():
        ms = jnp.mean(out * out, axis=-1, keepdims=True)
        yn = out * lax.rsqrt(ms + EPS) * gain_ref[...]
        o_ref[0, 0] = _rope(yn, cc_ref[...], sa_ref[...], sb_ref[...]).astype(BF16)

    @pl.when(a >= NSA_KV_HEADS)
    def _():
        o_ref[0, 0] = out.astype(BF16)


def _compress(hc, w1cat, pe8, w1, w2, gain, cc, sa, sb):
    A, B, ncp, K = hc.shape
    G = NSA_KV_HEADS
    return pl.pallas_call(
        _compress_kernel,
        grid=(A, B),
        in_specs=[pl.BlockSpec((1, 1, ncp, K), lambda a, b: (a, b, 0, 0)),
                  pl.BlockSpec((1, K, 2 * HEAD_DIM), lambda a, b: (a // G, 0, 0)),
                  pl.BlockSpec((1, 8, 2 * K), lambda a, b: (a // G, 0, 0)),
                  pl.BlockSpec((1, 2 * K, HEAD_DIM), lambda a, b: (a // G, 0, 0)),
                  pl.BlockSpec((1, HEAD_DIM, HEAD_DIM), lambda a, b: (a // G, 0, 0)),
                  pl.BlockSpec((1, HEAD_DIM), lambda a, b: (0, 0)),
                  pl.BlockSpec((ncp, HEAD_DIM), lambda a, b: (0, 0)),
                  pl.BlockSpec((ncp, HEAD_DIM), lambda a, b: (0, 0)),
                  pl.BlockSpec((ncp, HEAD_DIM), lambda a, b: (0, 0))],
        out_specs=pl.BlockSpec((1, 1, ncp, HEAD_DIM), lambda a, b: (a, b, 0, 0)),
        out_shape=jax.ShapeDtypeStruct((A, B, ncp, HEAD_DIM), BF16),
        compiler_params=_cparams(("arbitrary", "arbitrary")),
        name="compress",
    )(hc, w1cat, pe8, w1, w2, gain, cc, sa, sb)


def _first_step(s, vT, m_scr, acc_scr):
    m = jnp.max(s, axis=0, keepdims=True)
    p = jnp.exp2(s - m)
    acc_scr[...] = jnp.dot(_with_ones(vT), p.astype(BF16), preferred_element_type=F32)
    m_scr[...] = m


def _with_ones(vT):
    return jnp.concatenate([vT, jnp.ones((ONES_ROWS, vT.shape[1]), vT.dtype)], axis=0)


def _normalized(acc):
    return acc[0:HEAD_DIM] / acc[HEAD_DIM:HEAD_DIM + 1]


def _online_step(s, vT, m_scr, acc_scr):
    m_prev = m_scr[...]
    m_new = jnp.maximum(m_prev, jnp.max(s, axis=0, keepdims=True))
    alpha = jnp.exp2(m_prev - m_new)
    p = jnp.exp2(s - m_new)
    acc_scr[...] = alpha * acc_scr[...] + jnp.dot(_with_ones(vT), p.astype(BF16), preferred_element_type=F32)
    m_scr[...] = m_new


def _rank_select(score, k):
    n = score.shape[0]
    rank = jnp.zeros(score.shape, F32)
    for m in range(n):
        sm = score[m:m + 1, :]
        lo = (m // SUBLANES) * SUBLANES
        hi = min(lo + SUBLANES, n)
        parts = []
        if lo > 0:
            parts.append(jnp.where(sm > score[:lo], 1.0, 0.0))
        gt = jnp.where(sm > score[lo:hi], 1.0, 0.0)
        ge = jnp.where(sm >= score[lo:hi], 1.0, 0.0)
        below = lax.broadcasted_iota(jnp.int32, gt.shape, 0) > m - lo
        parts.append(jnp.where(below, ge, gt))
        if hi < n:
            parts.append(jnp.where(sm >= score[hi:], 1.0, 0.0))
        rank = rank + (jnp.concatenate(parts, axis=0) if len(parts) > 1 else parts[0])
    return rank < k


def _pad_rows(a, rows):
    return jnp.concatenate([a, jnp.zeros((rows - a.shape[0], a.shape[1]), a.dtype)], axis=0)


def _masked_scores(k_ref, e_ref, start, size, q_aug):
    rows = pl.ds(pl.multiple_of(start, 128), size)
    k_aug = jnp.concatenate([k_ref[rows, :], e_ref[rows, :]], axis=1)
    return jnp.dot(k_aug, q_aug, preferred_element_type=F32)


def _pipelined_attention(chains, n_steps, n_max, tks, tk, body_pairs):
    per = tks // tk

    def scores(step, buf):
        st = jnp.minimum(step, n_max - 1)
        dead = jnp.where(step < n_steps, 0, 1)
        for k_ref, e_ref, _, qaug_ref, bufs, _, _ in chains:
            bufs[buf][...] = _masked_scores(k_ref, e_ref, st * tks, tks, qaug_ref[dead])

    def consume(step, buf):
        st = jnp.minimum(step, n_max - 1)
        for _, _, vT_ref, _, bufs, m_scr, acc_scr in chains:
            vT = jnp.concatenate([vT_ref[st * per + a] for a in range(per)], axis=1)
            _online_step(bufs[buf][...], vT, m_scr, acc_scr)

    def pairs(first_step, n_pairs_in_body):
        def body(it, carry):
            base = first_step + it * 2 * n_pairs_in_body
            for p in range(n_pairs_in_body):
                scores(base + 2 * p + 1, 1)
                consume(base + 2 * p, 0)
                scores(base + 2 * p + 2, 0)
                consume(base + 2 * p + 1, 1)
            return carry
        return body

    scores(0, 0)
    done = 0
    for n_body in body_pairs[:-1]:
        n_iter = (n_steps - done) // (2 * n_body)
        lax.fori_loop(0, n_iter, pairs(done, n_body), 0)
        done = done + n_iter * 2 * n_body
    lax.fori_loop(0, (n_steps - done + 1) // 2, pairs(done, 1), 0)


def _carry_casts(kernel_fn, n_in, n_out, n_cast):
    def wrapped(*refs):
        ins, cast_in = refs[:n_in], refs[n_in:n_in + n_cast]
        outs = refs[n_in + n_cast:n_in + n_cast + n_out]
        cast_out = refs[n_in + n_cast + n_out:n_in + n_cast + n_out + n_cast]
        for src, dst in zip(cast_in, cast_out):
            dst[...] = src[...].astype(BF16)
        kernel_fn(*ins, *outs, *refs[n_in + n_cast + n_out + n_cast:])
    return wrapped


def _cast_specs(weights, grid):
    n_steps = grid[0] * grid[1] * grid[2]
    specs, shapes = [], []
    for w, rows in weights:
        while w.shape[0] // rows > n_steps:
            rows *= 2
        n_blk = w.shape[0] // rows
        assert w.shape[0] % rows == 0
        index = lambda a, b, c, n_blk=n_blk: (jnp.minimum((a * grid[1] + b) * grid[2] + c, n_blk - 1), 0)
        specs.append(pl.BlockSpec((rows, w.shape[1]), index))
        shapes.append(jax.ShapeDtypeStruct(w.shape, BF16))
    return specs, shapes


def _nsa_body(qT_ref, kc_ref, vcT_ref, ks_ref, vsT_ref, kw_ref, vwT_ref, gT_ref, e_ref,
              gain_ref, o_ref, m_scr, acc_scr, comb_scr, qaug_scr, s0_scr, s1_scr, psum_scr, *, T):
    R, tq = NSA_GROUP, NSA_TQ
    i = pl.program_id(2)
    t0 = i * tq
    ncp = T // CMP_STRIDE
    nsb = T // SLC_BLOCK
    qT = jnp.concatenate([qT_ref[r, c] for r in range(R) for c in range(tq // CHUNK)], axis=1)
    gT = gT_ref[0]

    def per_head(row0):
        return jnp.concatenate([gT[row0 + 3 * r:row0 + 3 * r + 1, :] for r in range(R)], axis=1)

    def lanes_x_heads(a):
        return jnp.concatenate([a] * R, axis=1)

    s = jnp.dot(kc_ref[0, 0], qT, preferred_element_type=F32)
    c_idx = lax.broadcasted_iota(jnp.int32, (ncp, tq), 0)
    t_c = t0 + lax.broadcasted_iota(jnp.int32, (ncp, tq), 1)
    vis = lanes_x_heads((c_idx * CMP_STRIDE + (CMP_BLOCK - 1) <= t_c) & (c_idx < ncp - 1))
    s = jnp.where(vis, s, NEG)
    m = jnp.maximum(jnp.max(s, axis=0, keepdims=True), M_INIT)
    p = jnp.exp2(s - m)
    l = jnp.sum(p, axis=0, keepdims=True)
    p = p / jnp.where(l > 0.0, l, 1.0)
    o_cmp = jnp.dot(vcT_ref[0, 0], p.astype(BF16), preferred_element_type=F32)
    comb_scr[...] = o_cmp * per_head(0)

    psum = p[:, 0:tq]
    for r in range(1, R):
        psum = psum + p[:, r * tq:(r + 1) * tq]
    per = SLC_BLOCK // CMP_STRIDE
    imp_chunks = []
    for c in range(tq // CHUNK):
        psum_scr[c, 0:SUBLANES, :] = jnp.zeros((SUBLANES, CHUNK), F32)
        psum_scr[c, SUBLANES:, :] = psum[:, c * CHUNK:(c + 1) * CHUNK]
        acc = jnp.zeros((nsb, CHUNK), F32)
        for k in range(1 - CMP_BLOCK // CMP_STRIDE, per):
            acc = acc + psum_scr[c, pl.ds(SUBLANES + k, nsb, stride=per), :]
        imp_chunks.append(acc)
    imp = jnp.concatenate(imp_chunks, axis=1)
    n_idx = lax.broadcasted_iota(jnp.int32, (nsb, tq), 0)
    cur = (t0 + lax.broadcasted_iota(jnp.int32, (nsb, tq), 1)) // SLC_BLOCK
    forced = (n_idx == 0) | (n_idx == cur) | (n_idx == cur - 1)
    valid = n_idx <= cur
    score = jnp.where(valid, imp + jnp.where(forced, FORCE_BONUS, 0.0), NEG)
    sel = _rank_select(score, min(SLC_TOPK, nsb)) & valid
    sel_past = sel & (n_idx * SLC_BLOCK < t0)
    bias = _pad_rows(jnp.where(sel_past, 0.0, NEG), HEAD_DIM).astype(BF16)
    bias_own = _pad_rows(jnp.where(sel, 0.0, NEG), HEAD_DIM).astype(BF16)
    for d in range(3):
        qaug_scr[d, 0:HEAD_DIM, :] = qT
    qaug_scr[0, HEAD_DIM:, :] = lanes_x_heads(bias)
    qaug_scr[1, HEAD_DIM:, :] = jnp.full((HEAD_DIM, R * tq), NEG, BF16)
    qaug_scr[2, HEAD_DIM:, :] = lanes_x_heads(bias_own)

    w0 = jnp.maximum(t0 + tq - WIN_SPAN, 0)
    k_w = kw_ref[0, pl.ds(pl.multiple_of(w0, CHUNK), WIN_SPAN), :]
    s_w = jnp.dot(k_w, qT, preferred_element_type=F32)
    kpos = w0 + lax.broadcasted_iota(jnp.int32, (WIN_SPAN, tq), 0)
    tpos = t0 + lax.broadcasted_iota(jnp.int32, (WIN_SPAN, tq), 1)
    ok = lanes_x_heads((kpos <= tpos) & (tpos - kpos < WINDOW))
    s_w = jnp.where(ok, s_w, NEG)
    m_w = jnp.max(s_w, axis=0, keepdims=True)
    p_w = jnp.exp2(s_w - m_w)
    jw = w0 // CHUNK
    v_w = jnp.concatenate([vwT_ref[0, jw + a] for a in range(WIN_SPAN // CHUNK)], axis=1)
    o_w = jnp.dot(_with_ones(v_w), p_w.astype(BF16), preferred_element_type=F32)
    comb_scr[...] += _normalized(o_w) * per_head(2)

    tri =lax.broadcasted_iota(jnp.int32, (tq, tq), 0) <= lax.broadcasted_iota(jnp.int32, (tq, tq), 1)
    s_d = jnp.where(lanes_x_heads(tri), _masked_scores(ks_ref.at[0], e_ref, t0, tq, qaug_scr[2]), NEG)
    own = tq // CHUNK
    v_d = jnp.concatenate([vsT_ref[0, i * own + a] for a in range(own)], axis=1)
    _first_step(s_d, v_d, m_scr, acc_scr)
    halves = [slice(h * (R // 2) * tq, (h + 1) * (R // 2) * tq) for h in range(2)]
    yield [(ks_ref.at[0], e_ref, vsT_ref.at[0], qaug_scr.at[:, :, ln], (s0_scr.at[:, ln], s1_scr.at[:, ln]),
            m_scr.at[:, ln], acc_scr.at[:, ln]) for ln in halves]
    comb = comb_scr[...] + _normalized(acc_scr[...]) * per_head(1)

    for r in range(R):
        oT = comb[:, r * tq:(r + 1) * tq]
        ms = jnp.mean(oT * oT, axis=0, keepdims=True)
        on = (oT * lax.rsqrt(ms + EPS)).T * gain_ref[0, r:r + 1, :]
        o_ref[:, r * HEAD_DIM:(r + 1) * HEAD_DIM] = on.astype(BF16)


def _moba_body(qT_ref, k_ref, vT_ref, e_ref, gain_ref, o_ref, m_scr, acc_scr, kmean_scr, qaug_scr,
               s_scr, *, T):
    tq, tk = MOBA_TQ, MOBA_TK
    nb = T // MOBA_BLOCK
    nbp = kmean_scr.shape[1]
    i = pl.program_id(2)
    t0 = i * tq

    @pl.when(i == 0)
    def _():
        kmean_scr[...] = jnp.zeros(kmean_scr.shape, F32)
        for a in range(MOBA_HB):
            kb = k_ref[a].astype(F32).reshape(nb, MOBA_BLOCK, HEAD_DIM)
            kmean_scr[a, 0:nb, :] = jnp.mean(kb, axis=1)

    n_idx = lax.broadcasted_iota(jnp.int32, (nbp, tq), 0)
    cur = (t0 + lax.broadcasted_iota(jnp.int32, (nbp, tq), 1)) // MOBA_BLOCK
    past = n_idx < cur
    causal = lax.broadcasted_iota(jnp.int32, (tk, tq), 0) <= lax.broadcasted_iota(jnp.int32, (tk, tq), 1)

    qTs = [jnp.concatenate([qT_ref[a, c] for c in range(tq // CHUNK)], axis=1) for a in range(MOBA_HB)]
    for a in range(MOBA_HB):
        qT = qTs[a]
        gate = jnp.zeros((nbp, tq), F32)
        for part in _split3(kmean_scr[a]):
            gate = gate + jnp.dot(part, qT, preferred_element_type=F32)
        sel = _rank_select(jnp.where(past, gate, NEG), min(MOBA_TOPK, nb)) & past
        for d in range(2):
            qaug_scr[a, d, 0:HEAD_DIM, :] = qT
        qaug_scr[a, 0, HEAD_DIM:, :] = _pad_rows(jnp.where(sel, 0.0, NEG), HEAD_DIM).astype(BF16)
        qaug_scr[a, 1, HEAD_DIM:, :] = jnp.full((HEAD_DIM, tq), NEG, BF16)

    s_own = []
    for a in range(MOBA_HB):
        m_scr[a] = jnp.full(m_scr.shape[1:], M_INIT, F32)
        acc_scr[a] = jnp.zeros(acc_scr.shape[1:], F32)
        k_own = k_ref[a, pl.ds(pl.multiple_of(t0, tk), tk), :]
        s_own.append(jnp.where(causal, jnp.dot(k_own, qTs[a], preferred_element_type=F32), NEG))

    yield [(k_ref.at[a], e_ref, vT_ref.at[a], qaug_scr.at[a], (s_scr.at[a, 0], s_scr.at[a, 1]),
            m_scr.at[a], acc_scr.at[a]) for a in range(MOBA_HB)]

    own = tq // CHUNK
    for a in range(MOBA_HB):
        v_own = jnp.concatenate([vT_ref[a, i * own + c] for c in range(own)], axis=1)
        _online_step(s_own[a], v_own, m_scr.at[a], acc_scr.at[a])
        oT = _normalized(acc_scr[a])
        ms = jnp.mean(oT * oT, axis=0, keepdims=True)
        on = (oT * lax.rsqrt(ms + EPS)).T * gain_ref[a]
        o_ref[:, a * HEAD_DIM:(a + 1) * HEAD_DIM] = on.astype(BF16)


N_NSA_IN, N_NSA_SCRATCH, N_MOBA_IN = 10, 7, 5


def _attention_kernel(*refs, T):
    nsa_in, moba_in = refs[:N_NSA_IN], refs[N_NSA_IN:N_NSA_IN + N_MOBA_IN]
    o_nsa, o_moba = refs[N_NSA_IN + N_MOBA_IN:N_NSA_IN + N_MOBA_IN + 2]
    scratch = refs[N_NSA_IN + N_MOBA_IN + 2:]
    moba = _moba_body(*moba_in, o_moba, *scratch[N_NSA_SCRATCH:], T=T)
    nsa = _nsa_body(*nsa_in, o_nsa, *scratch[:N_NSA_SCRATCH], T=T)
    chains = next(moba) + next(nsa)
    n_steps = pl.program_id(2) * ATTN_TQ // ATTN_TKS
    _pipelined_attention(chains, n_steps, T // ATTN_TKS, ATTN_TKS, CHUNK, ATTN_BODY_PAIRS)
    for rest in (nsa, moba):
        assert next(rest, None) is None


def _attention(yT, ystd, kcvc, vcT, gT, e_slc, e_moba, nsa_gains, moba_gains, casts, B, T):
    G, R, tq = NSA_KV_HEADS, NSA_GROUP, ATTN_TQ
    H, HB = MOBA_HEADS, MOBA_HB
    assert H // HB == G and NSA_TQ == MOBA_TQ == tq and SLC_TKS == MOBA_TKS == ATTN_TKS and tq % ATTN_TKS == 0
    nq = T // tq
    ncp = T // CMP_STRIDE
    nb = T // MOBA_BLOCK
    nbp = max(16, nb)
    grid = (B, G, nq)
    cast_specs, cast_shapes = _cast_specs(casts, grid)
    kern = _carry_casts(functools.partial(_attention_kernel, T=T), N_NSA_IN + N_MOBA_IN, 2, len(casts))
    keys = lambda head0: pl.BlockSpec((1, T, HEAD_DIM), lambda b, g, i: (head0 + g, b, 0))
    values = lambda head0: pl.BlockSpec((1, T // CHUNK, HEAD_DIM, CHUNK), lambda b, g, i: (head0 + g, b, 0, 0))
    onehot = pl.BlockSpec((T, HEAD_DIM), lambda b, g, i: (0, 0))
    nsa_specs = [pl.BlockSpec((R, tq // CHUNK, HEAD_DIM, CHUNK), lambda b, g, i: (T_NSA_Q // R + g, b * nq + i, 0, 0)),
                 pl.BlockSpec((1, 1, ncp, HEAD_DIM), lambda b, g, i: (g, b, 0, 0)),
                 pl.BlockSpec((1, 1, HEAD_DIM, ncp), lambda b, g, i: (g, b, 0, 0)),
                 keys(S_KSLC), values(T_VSLC), keys(S_KWIN), values(T_VWIN),
                 pl.BlockSpec((1, GATE_ROWS, tq), lambda b, g, i: (g, 0, b * nq + i)),
                 onehot,
                 pl.BlockSpec((1, R, HEAD_DIM), lambda b, g, i: (g, 0, 0))]
    moba_specs = [pl.BlockSpec((HB, tq // CHUNK, HEAD_DIM, CHUNK), lambda b, h, i: (T_MOBA_Q // HB + h, b * nq + i, 0, 0)),
                  pl.BlockSpec((HB, T, HEAD_DIM), lambda b, h, i: (S_MOBA_K // HB + h, b, 0)),
                  pl.BlockSpec((HB, T // CHUNK, HEAD_DIM, CHUNK), lambda b, h, i: (T_MOBA_V // HB + h, b, 0, 0)),
                  onehot,
                  pl.BlockSpec((HB, 1, HEAD_DIM), lambda b, h, i: (h, 0, 0))]
    nsa_scratch = [pltpu.VMEM((1, R * tq), F32),
                   pltpu.VMEM((HEAD_DIM + ONES_ROWS, R * tq), F32), pltpu.VMEM((HEAD_DIM, R * tq), F32),
                   pltpu.VMEM((3, 2 * HEAD_DIM, R * tq), BF16),
                   pltpu.VMEM((ATTN_TKS, R * tq), F32), pltpu.VMEM((ATTN_TKS, R * tq), F32),
                   pltpu.VMEM((tq // CHUNK, SUBLANES + ncp, CHUNK), F32)]
    moba_scratch = [pltpu.VMEM((HB, 1, tq), F32),
                    pltpu.VMEM((HB, HEAD_DIM + ONES_ROWS, tq), F32), pltpu.VMEM((HB, nbp, HEAD_DIM), F32),
                    pltpu.VMEM((HB, 2, 2 * HEAD_DIM, tq), BF16),
                    pltpu.VMEM((HB, 2, ATTN_TKS, tq), F32)]
    assert len(nsa_specs) == N_NSA_IN and len(moba_specs) == N_MOBA_IN and len(nsa_scratch) == N_NSA_SCRATCH
    return pl.pallas_call(
        kern,
        grid=grid,
        in_specs=nsa_specs + moba_specs + cast_specs,
        out_specs=[pl.BlockSpec((tq, R * HEAD_DIM), lambda b, g, i: (b * nq + i, g)),
                   pl.BlockSpec((tq, HB * HEAD_DIM), lambda b, h, i: (b * nq + i, h))] + cast_specs,
        out_shape=[jax.ShapeDtypeStruct((B * T, NSA_HEADS * HEAD_DIM), BF16),
                   jax.ShapeDtypeStruct((B * T, H * HEAD_DIM), BF16)] + cast_shapes,
        scratch_shapes=nsa_scratch + moba_scratch,
        compiler_params=_cparams(("parallel", "parallel", "arbitrary")),
        name="attention",
    )(yT, kcvc, vcT, ystd, yT, ystd, yT, gT, e_slc, nsa_gains,
      yT, ystd, yT, e_moba, moba_gains, *[w for w, _ in casts])


def _outffn_kernel(on_ref, om_ref, wout_ref, x_ref, mod_ref, wg_ref, wu_ref, wo_ref, o_ref, h_scr, x1_scr):
    j = pl.program_id(1)

    @pl.when(j == 0)
    def _():
        half = on_ref.shape[1]
        attn = jnp.dot(on_ref[...], wout_ref[0:half, :], preferred_element_type=F32)
        attn = attn + jnp.dot(om_ref[...], wout_ref[half:, :], preferred_element_type=F32)
        x1 = x_ref[...] + mod_ref[0, 2:3, :] * attn
        x1_scr[...] = x1
        ms = jnp.mean(x1 * x1, axis=-1, keepdims=True)
        h = x1 * lax.rsqrt(ms + EPS) * (1.0 + mod_ref[0, 4:5, :]) + mod_ref[0, 3:4, :]
        h_scr[...] = h.astype(BF16)
        o_ref[...] = jnp.zeros(o_ref.shape, F32)

    hb = h_scr[...]
    half = wg_ref.shape[1] // 2
    down = None
    for c in range(2):
        cols = slice(c * half, (c + 1) * half)
        gate = jnp.dot(hb, wg_ref[:, cols], preferred_element_type=F32)
        up = jnp.dot(hb, wu_ref[:, cols], preferred_element_type=F32)
        act = (gate * jax.nn.sigmoid(gate) * up).astype(BF16)
        part = jnp.dot(act, wo_ref[cols, :], preferred_element_type=F32)
        down = part if down is None else down + part
    o_ref[...] += down

    @pl.when(j == pl.num_programs(1) - 1)
    def _():
        o_ref[...] = x1_scr[...] + mod_ref[0, 5:6, :] * o_ref[...]


def _out_ffn(o_nsa, o_moba, w_out, x2, mod3, w_in, w_down, T):
    BT, D = x2.shape
    Fh = w_down.shape[0]
    tm = min(FFN_TM, T)
    tf = FFN_TF
    tpb = T // tm
    nf = Fh // tf
    half = o_nsa.shape[1]
    return pl.pallas_call(
        _outffn_kernel,
        grid=(BT // tm, nf),
        in_specs=[pl.BlockSpec((tm, half), lambda i, j: (i, 0)),
                  pl.BlockSpec((tm, half), lambda i, j: (i, 0)),
                  pl.BlockSpec((D, D), lambda i, j: (0, 0), pipeline_mode=pl.Buffered(1)),
                  pl.BlockSpec((tm, D), lambda i, j: (i, 0)),
                  pl.BlockSpec((1, 6, D), lambda i, j: (i // tpb, 0, 0)),
                  pl.BlockSpec((D, tf), lambda i, j: (0, j)),
                  pl.BlockSpec((D, tf), lambda i, j: (0, nf + j)),
                  pl.BlockSpec((tf, D), lambda i, j: (j, 0))],
        out_specs=pl.BlockSpec((tm, D), lambda i, j: (i, 0)),
        out_shape=jax.ShapeDtypeStruct((BT, D), F32),
        scratch_shapes=[pltpu.VMEM((tm, D), BF16), pltpu.VMEM((tm, D), F32)],
        compiler_params=_cparams(("parallel", "arbitrary")),
        name="out_ffn",
    )(o_nsa, o_moba, w_out, x2, mod3, w_in, w_in, w_down)


def _rope_cos_sin(pos):
    inv = ROPE_THETA ** (-jnp.arange(0, ROPE_DIMS, 2, dtype=F32) / ROPE_DIMS)
    ang = pos.astype(F32)[:, None] * inv[None, :]
    return jnp.cos(ang), jnp.sin(ang)


def _rope_tables(T):
    c_hi, s_hi = _rope_cos_sin(jnp.arange(0, T, ROPE_SPLIT))
    c_lo, s_lo = _rope_cos_sin(jnp.arange(ROPE_SPLIT))
    cos = (c_hi[:, None] * c_lo[None] - s_hi[:, None] * s_lo[None]).reshape(T, ROPE_HALF)
    sin = (s_hi[:, None] * c_lo[None] + c_hi[:, None] * s_lo[None]).reshape(T, ROPE_HALF)
    cos_c, sin_c = _rope_cos_sin(jnp.arange(CMP_BLOCK - 1, T, CMP_STRIDE))
    n, rest = cos_c.shape[0], HEAD_DIM - ROPE_DIMS
    pad = lambda t: jnp.pad(t, ((0, 1), (0, 0)))
    cc = pad(jnp.concatenate([cos_c, cos_c, jnp.ones((n, rest), F32)], axis=1))
    sa = pad(jnp.concatenate([-sin_c, jnp.zeros((n, HEAD_DIM - ROPE_HALF), F32)], axis=1))
    sb = pad(jnp.concatenate([jnp.zeros((n, ROPE_HALF), F32), sin_c, jnp.zeros((n, rest), F32)], axis=1))
    return cc, sa, sb, cos.T, sin.T


def _block_onehot(T, block):
    return (jnp.arange(T)[:, None] // block == jnp.arange(HEAD_DIM)[None, :]).astype(BF16)


def _layer(x2, c, B, T, w_ada, b_ada, w_in, nsa_q_norm, nsa_k_norm, moba_q_norm, moba_k_norm,
           cmp_pe_k, cmp_w1_k, cmp_w2_k, cmp_pe_v, cmp_w1_v, cmp_w2_v, out_norm, w_out,
           w_ffn_in, w_ffn_out):
    D = x2.shape[1]
    G = NSA_KV_HEADS
    scale = HEAD_DIM ** -0.5 * LOG2E
    assert T % MOBA_BLOCK == 0 and T % SLC_TKS == 0 and T % MOBA_TKS == 0 and T % NSA_TQ == 0 and T >= WIN_SPAN
    assert T // SLC_BLOCK <= HEAD_DIM and T // MOBA_BLOCK <= HEAD_DIM

    nsa_w = NSA_HEADS * HEAD_DIM + 6 * G * HEAD_DIM
    gw = NSA_HEADS * 3
    mod3 = _adaln(c, w_ada, b_ada)
    w_nsa = w_in.astype(BF16)
    w_moba = w_nsa[:, nsa_w + gw:]

    wg = w_in[:, nsa_w:nsa_w + gw].reshape(D, G, NSA_GROUP * 3)
    w_gate = jnp.pad(wg, ((0, 0), (0, 0), (0, GATE_LANES - NSA_GROUP * 3))).reshape(D, G * GATE_LANES)
    w_gate = w_gate.astype(BF16)
    across = lambda g_: jnp.broadcast_to(g_[:, None], (HEAD_DIM, CHUNK))
    gainsT = jnp.stack([across(nsa_q_norm * scale), across(moba_q_norm * scale),
                        across(nsa_k_norm[1]), across(nsa_k_norm[2]), across(moba_k_norm)])
    cc, sa, sb, cosT, sinT = _rope_tables(T)

    yT, ystd, hc, gT = _in_proj(x2, mod3, w_nsa, w_moba, w_gate, gainsT, cosT, sinT, T)

    ncp = T // CMP_STRIDE
    half = CMP_STRIDE * HEAD_DIM
    w1 = jnp.stack([cmp_w1_k, cmp_w1_v]).astype(BF16)
    w1cat = jnp.concatenate([w1[:, :half], w1[:, half:]], axis=2)
    pe8 = jnp.broadcast_to(jnp.stack([cmp_pe_k, cmp_pe_v]).reshape(2, 1, 2 * half), (2, 8, 2 * half))
    w2 = jnp.stack([cmp_w2_k, cmp_w2_v]).astype(BF16)
    kcvc = _compress(hc.reshape(2 * G, B, ncp, half), w1cat, pe8.astype(BF16), w1, w2,
                     nsa_k_norm[0].reshape(1, HEAD_DIM), cc, sa, sb)
    vcT = kcvc[G:].transpose(0, 1, 3, 2)

    on = out_norm.reshape(N_HEADS, HEAD_DIM)
    o_nsa, o_moba, w_ffn_out_b, w_out_b, w_ffn_in_b = _attention(
        yT, ystd, kcvc, vcT, gT, _block_onehot(T, SLC_BLOCK), _block_onehot(T, MOBA_BLOCK),
        on[:NSA_HEADS].reshape(G, NSA_GROUP, HEAD_DIM), on[NSA_HEADS:].reshape(MOBA_HEADS, 1, HEAD_DIM),
        [(w_ffn_out, 128), (w_out, 128), (w_ffn_in, 32)], B, T)

    return _out_ffn(o_nsa, o_moba, w_out_b, x2, mod3, w_ffn_in_b, w_ffn_out_b, T)


def kernel(x, c, w_ada, b_ada, w_in, nsa_q_norm, nsa_k_norm, moba_q_norm, moba_k_norm, cmp_pe_k, cmp_w1_k, cmp_w2_k, cmp_pe_v, cmp_w1_v, cmp_w2_v, out_norm, w_out, w_ffn_in, w_ffn_out):
    B, T, D = x.shape
    x2 = x.reshape(B * T, D)
    for l in range(w_ada.shape[0]):
        x2 = _layer(x2, c, B, T, w_ada[l], b_ada[l], w_in[l], nsa_q_norm[l], nsa_k_norm[l],
                    moba_q_norm[l], moba_k_norm[l], cmp_pe_k[l], cmp_w1_k[l], cmp_w2_k[l],
                    cmp_pe_v[l], cmp_w1_v[l], cmp_w2_v[l], out_norm[l], w_out[l],
                    w_ffn_in[l], w_ffn_out[l])
    return x2.reshape(B, T, D)
```

```python
import functools

import jax
import jax.numpy as jnp
from jax import lax
from jax.experimental import pallas as pl
from jax.experimental.pallas import tpu as pltpu

F32 = jnp.float32
BF16 = jnp.bfloat16

HEAD_DIM = 128
SUBLANES = 8
NSA_HEADS = 8
NSA_KV_HEADS = 2
NSA_GROUP = NSA_HEADS // NSA_KV_HEADS
MOBA_HEADS = 8
N_HEADS = NSA_HEADS + MOBA_HEADS
CMP_BLOCK = 32
CMP_STRIDE = 16
SLC_BLOCK = 64
SLC_TOPK = 16
WINDOW = 512
FORCE_BONUS = 1e4
MOBA_BLOCK = 256
MOBA_TOPK = 3
ROPE_THETA = 500000.0
ROPE_DIMS = HEAD_DIM // 4
ROPE_HALF = ROPE_DIMS // 2
ROPE_SPLIT = 64
EPS = 1e-6
LOG2E = 1.4426950408889634
ONES_ROWS = 16
NEG = -1e30
M_INIT = -1e29

V7X_VMEM_BYTES = 64 * 1024 * 1024
VMEM_LIMIT = V7X_VMEM_BYTES - 8 * 1024 * 1024

PROJ_STEPS = ("nsa_q", "kv_cmp_slc", "kv_win", "moba_q", "moba_k", "moba_v")
WIDE_HEADS, NARROW_HEADS = 8, 4
OUT_BLOCK_HEADS = 8
YT_BLOCK = (0, 3, 3, 1, 1, 2)
YSTD_BLOCK = (1, 1, 1, 0, 0, 0)
GAIN_KINDS = ("nsa_q", "moba_q", "k_slc", "k_win", "moba_k")
PROJ_GAIN = {"nsa_q": "nsa_q", "moba_q": "moba_q", "kv_cmp_slc": "k_slc", "kv_win": "k_win", "moba_k": "moba_k"}
T_NSA_Q, T_MOBA_Q, T_MOBA_V, T_VSLC, T_VWIN = 0, 8, 16, 24, 26
S_MOBA_K, S_KSLC, S_KWIN = 0, 8, 10
N_T_HEADS, N_STD_HEADS, N_CMP_HEADS = 32, 16, 4
CHUNK = 128
GATE_LANES = 128
GATE_ROWS = 16

ATTN_TQ = 256
ATTN_TKS = 256
ATTN_BODY_PAIRS = (4, 2, 1)
NSA_TQ = ATTN_TQ
SLC_TKS = ATTN_TKS
WIN_SPAN = WINDOW + NSA_TQ
MOBA_TQ = ATTN_TQ
MOBA_TK = MOBA_BLOCK
MOBA_TKS = ATTN_TKS
MOBA_HB = 4

ADALN_TN = 1024
PROJ_TM = 1024
OUT_TM = 512
FFN_TM, FFN_TF = 512, 512


def _cparams(sem):
    return pltpu.CompilerParams(dimension_semantics=sem, vmem_limit_bytes=VMEM_LIMIT)


def _split3(a):
    hi = a.astype(BF16)
    r1 = a - hi.astype(F32)
    mid = r1.astype(BF16)
    lo = (r1 - mid.astype(F32)).astype(BF16)
    return hi, mid, lo


def _adaln_kernel(c_ref, w_ref, b_ref, o_ref):
    cv = c_ref[...]
    s = cv * jax.nn.sigmoid(cv)
    w = w_ref[...].astype(BF16)
    acc = b_ref[...] + jnp.zeros(o_ref.shape, F32)
    for part in _split3(s)[:2]:
        acc = acc + jnp.dot(part, w, preferred_element_type=F32)
    o_ref[...] = acc


def _adaln(c, w_ada, b_ada):
    B, D = c.shape
    N = w_ada.shape[1]
    tn = ADALN_TN
    c8 = jnp.zeros((8, D), F32).at[:B].set(c)
    out = pl.pallas_call(
        _adaln_kernel,
        grid=(N // tn,),
        in_specs=[pl.BlockSpec((8, D), lambda j: (0, 0)),
                  pl.BlockSpec((D, tn), lambda j: (0, j)),
                  pl.BlockSpec((1, tn), lambda j: (0, j))],
        out_specs=pl.BlockSpec((8, tn), lambda j: (0, j)),
        out_shape=jax.ShapeDtypeStruct((8, N), F32),
        compiler_params=_cparams(("arbitrary",)),
        name="adaln",
    )(c8, w_ada, b_ada.reshape(1, N))
    return out[:B].reshape(B, 6, D)


def _rope(y, cc, sa, sb):
    return (y * cc + pltpu.roll(y, HEAD_DIM - ROPE_HALF, 1) * sa + pltpu.roll(y, ROPE_HALF, 1) * sb)


def _inproj_kernel(x_ref, mod_ref, wq_ref, wkv_ref, wm_ref, wg_ref, gainT_ref, cosT_ref, sinT_ref,
                   yT_ref, ystd_ref, hc_ref, gT_ref, h_scr, rows_scr):
    j = pl.program_id(1)
    tm = x_ref.shape[0]
    n_chunks = tm // CHUNK
    pair = 2 * HEAD_DIM

    @pl.when(j == 0)
    def _():
        x = x_ref[...]
        ms = jnp.mean(x * x, axis=-1, keepdims=True)
        h = x * lax.rsqrt(ms + EPS) * (1.0 + mod_ref[0, 1:2, :]) + mod_ref[0, 0:1, :]
        hb = h.astype(BF16)
        h_scr[...] = hb
        g = jax.nn.sigmoid(jnp.dot(hb, wg_ref[...], preferred_element_type=F32))
        for grp in range(NSA_KV_HEADS):
            for c in range(n_chunks):
                blk = g[c * CHUNK:(c + 1) * CHUNK, grp * GATE_LANES:(grp + 1) * GATE_LANES].T
                gT_ref[grp, :, c * CHUNK:(c + 1) * CHUNK] = blk[0:GATE_ROWS, :]

    def head_pairs(w_ref, n_heads):
        for half in range(n_heads // 2):
            acc = jnp.dot(h_scr[...], w_ref[:, half * pair:(half + 1) * pair], preferred_element_type=F32)
            for h2 in range(2):
                yield 2 * half + h2, acc[:, h2 * HEAD_DIM:(h2 + 1) * HEAD_DIM]

    def qk_chunk(t, c):
        ms = jnp.mean(t * t, axis=0, keepdims=True)
        tn = t * lax.rsqrt(ms + EPS) * gainT_ref[0]
        cs = cosT_ref[:, c * CHUNK:(c + 1) * CHUNK]
        sn = sinT_ref[:, c * CHUNK:(c + 1) * CHUNK]
        a, b = tn[0:ROPE_HALF], tn[ROPE_HALF:ROPE_DIMS]
        return jnp.concatenate([a * cs - b * sn, b * cs + a * sn, tn[ROPE_DIMS:]], axis=0)

    def store_T(yh, slot, treated):
        for c in range(n_chunks):
            chunk = yh[c * CHUNK:(c + 1) * CHUNK, :]
            if treated:
                yT_ref[slot, c] = qk_chunk(chunk.T, c).astype(BF16)
            else:
                yT_ref[slot, c] = chunk.astype(BF16).T

    def store_rows(yh, slot):
        for c in range(n_chunks):
            rows = slice(c * CHUNK, (c + 1) * CHUNK)
            ystd_ref[slot, rows, :] = qk_chunk(yh[rows, :].T, c).astype(BF16).T

    def when_step(kind):
        return pl.when(j == PROJ_STEPS.index(kind))

    for kind, w_ref in (("nsa_q", wq_ref), ("moba_q", wm_ref)):
        @when_step(kind)
        def _(w_ref=w_ref):
            for hh, yh in head_pairs(w_ref, WIDE_HEADS):
                store_T(yh, hh, True)

    @when_step("moba_v")
    def _():
        for hh, yh in head_pairs(wm_ref, WIDE_HEADS):
            store_T(yh, hh, False)

    @when_step("moba_k")
    def _():
        for hh, yh in head_pairs(wm_ref, WIDE_HEADS):
            store_rows(yh, hh)

    def flatten_rows(yh, slot):
        rows_scr[...] = yh
        flat = [rows_scr[pl.ds(l, tm // CMP_STRIDE, stride=CMP_STRIDE), :] for l in range(CMP_STRIDE)]
        hc_ref[slot] = jnp.concatenate(flat, axis=1).astype(BF16)

    @when_step("kv_cmp_slc")
    def _():
        for slot in range(2 * NSA_KV_HEADS, OUT_BLOCK_HEADS):
            yT_ref[slot] = jnp.zeros(yT_ref.shape[1:], BF16)
            ystd_ref[slot] = jnp.zeros(ystd_ref.shape[1:], BF16)
        for hh, yh in head_pairs(wq_ref, WIDE_HEADS):
            if hh < N_CMP_HEADS:
                flatten_rows(yh, hh)
            elif hh < N_CMP_HEADS + NSA_KV_HEADS:
                store_rows(yh, hh - N_CMP_HEADS)
            else:
                store_T(yh, hh - N_CMP_HEADS - NSA_KV_HEADS, False)

    @when_step("kv_win")
    def _():
        for hh, yh in head_pairs(wkv_ref, NARROW_HEADS):
            if hh < NSA_KV_HEADS:
                store_rows(yh, NSA_KV_HEADS + hh)
            else:
                store_T(yh, hh, False)


def _in_proj(x2, mod3, w_nsa, w_moba, w_gate, gainsT, cosT, sinT, T):
    BT, D = x2.shape
    G = NSA_KV_HEADS
    tm = min(PROJ_TM, T)
    wide, narrow = WIDE_HEADS * HEAD_DIM, NARROW_HEADS * HEAD_DIM
    tpb = T // tm
    pick = lambda j, table: sum((j == t).astype(jnp.int32) * v for t, v in enumerate(table))
    gain_kind = lambda j: pick(j, [GAIN_KINDS.index(PROJ_GAIN[s]) if s in PROJ_GAIN else 0 for s in PROJ_STEPS])
    win_block = (NSA_HEADS + N_CMP_HEADS + 2 * NSA_KV_HEADS) * HEAD_DIM // narrow
    first_moba = PROJ_STEPS.index("moba_q")
    return pl.pallas_call(
        _inproj_kernel,
        grid=(BT // tm, len(PROJ_STEPS)),
        in_specs=[pl.BlockSpec((tm, D), lambda i, j: (i, 0)),
                  pl.BlockSpec((1, 6, D), lambda i, j: (i // tpb, 0, 0)),
                  pl.BlockSpec((D, wide), lambda i, j: (0, jnp.minimum(j, 1))),
                  pl.BlockSpec((D, narrow), lambda i, j: (0, win_block)),
                  pl.BlockSpec((D, wide), lambda i, j: (0, jnp.clip(j - first_moba, 0, 2))),
                  pl.BlockSpec((D, G * GATE_LANES), lambda i, j: (0, 0)),
                  pl.BlockSpec((1, HEAD_DIM, CHUNK), lambda i, j: (gain_kind(j), 0, 0)),
                  pl.BlockSpec((ROPE_HALF, tm), lambda i, j: (0, i % tpb)),
                  pl.BlockSpec((ROPE_HALF, tm), lambda i, j: (0, i % tpb))],
        out_specs=[pl.BlockSpec((OUT_BLOCK_HEADS, tm // CHUNK, HEAD_DIM, CHUNK), lambda i, j: (pick(j, YT_BLOCK), i, 0, 0)),
                   pl.BlockSpec((OUT_BLOCK_HEADS, tm, HEAD_DIM), lambda i, j: (pick(j, YSTD_BLOCK), i, 0)),
                   pl.BlockSpec((N_CMP_HEADS, tm // CMP_STRIDE, CMP_STRIDE * HEAD_DIM), lambda i, j: (0, i, 0)),
                   pl.BlockSpec((G, GATE_ROWS, tm), lambda i, j: (0, 0, i))],
        out_shape=[jax.ShapeDtypeStruct((N_T_HEADS, BT // CHUNK, HEAD_DIM, CHUNK), BF16),
                   jax.ShapeDtypeStruct((N_STD_HEADS, BT, HEAD_DIM), BF16),
                   jax.ShapeDtypeStruct((N_CMP_HEADS, BT // CMP_STRIDE, CMP_STRIDE * HEAD_DIM), BF16),
                   jax.ShapeDtypeStruct((G, GATE_ROWS, BT), F32)],
        scratch_shapes=[pltpu.VMEM((tm, D), BF16), pltpu.VMEM((tm, HEAD_DIM), F32)],
        compiler_params=_cparams(("parallel", "arbitrary")),
        name="in_proj",
    )(x2, mod3, w_nsa, w_nsa, w_moba, w_gate, gainsT, cosT, sinT)


def _compress_kernel(h_ref, w1c_ref, pe_ref, w1_ref, w2_ref, gain_ref, cc_ref, sa_ref, sb_ref, o_ref):
    a = pl.program_id(0)
    ncp = h_ref.shape[2]
    z = jnp.dot(h_ref[0, 0], w1c_ref[0], preferred_element_type=F32)
    top = z[:, :HEAD_DIM]
    bot = pltpu.roll(z[:, HEAD_DIM:], ncp - 1, 0)
    pe_term = jnp.dot(pe_ref[0], w1_ref[0], preferred_element_type=F32)[0:1, :]
    pre = top + bot + pe_term
    act = pre * jax.nn.sigmoid(pre)
    out = jnp.dot(act.astype(BF16), w2_ref[0], preferred_element_type=F32)
    live = lax.broadcasted_iota(jnp.int32, out.shape, 0) < ncp - 1
    out = jnp.where(live, out, 0.0)

    @pl.when(a < NSA_KV_HEADS)
    def _():
        ms = jnp.mean(out * out, axis=-1, keepdims=True)
        yn = out * lax.rsqrt(ms + EPS) * gain_ref[...]
        o_ref[0, 0] = _rope(yn, cc_ref[...], sa_ref[...], sb_ref[...]).astype(BF16)

    @pl.when(a >= NSA_KV_HEADS)
    def _():
        o_ref[0, 0] = out.astype(BF16)


def _compress(hc, w1cat, pe8, w1, w2, gain, cc, sa, sb):
    A, B, ncp, K = hc.shape
    G = NSA_KV_HEADS
    return pl.pallas_call(
        _compress_kernel,
        grid=(A, B),
        in_specs=[pl.BlockSpec((1, 1, ncp, K), lambda a, b: (a, b, 0, 0)),
                  pl.BlockSpec((1, K, 2 * HEAD_DIM), lambda a, b: (a // G, 0, 0)),
                  pl.BlockSpec((1, 8, 2 * K), lambda a, b: (a // G, 0, 0)),
                  pl.BlockSpec((1, 2 * K, HEAD_DIM), lambda a, b: (a // G, 0, 0)),
                  pl.BlockSpec((1, HEAD_DIM, HEAD_DIM), lambda a, b: (a // G, 0, 0)),
                  pl.BlockSpec((1, HEAD_DIM), lambda a, b: (0, 0)),
                  pl.BlockSpec((ncp, HEAD_DIM), lambda a, b: (0, 0)),
                  pl.BlockSpec((ncp, HEAD_DIM), lambda a, b: (0, 0)),
                  pl.BlockSpec((ncp, HEAD_DIM), lambda a, b: (0, 0))],
        out_specs=pl.BlockSpec((1, 1, ncp, HEAD_DIM), lambda a, b: (a, b, 0, 0)),
        out_shape=jax.ShapeDtypeStruct((A, B, ncp, HEAD_DIM), BF16),
        compiler_params=_cparams(("arbitrary", "arbitrary")),
        name="compress",
    )(hc, w1cat, pe8, w1, w2, gain, cc, sa, sb)


def _first_step(s, vT, m_scr, acc_scr):
    m = jnp.max(s, axis=0, keepdims=True)
    p = jnp.exp2(s - m)
    acc_scr[...] = jnp.dot(_with_ones(vT), p.astype(BF16), preferred_element_type=F32)
    m_scr[...] = m


def _with_ones(vT):
    return jnp.concatenate([vT, jnp.ones((ONES_ROWS, vT.shape[1]), vT.dtype)], axis=0)


def _normalized(acc):
    return acc[0:HEAD_DIM] / acc[HEAD_DIM:HEAD_DIM + 1]


def _online_step(s, vT, m_scr, acc_scr):
    m_prev = m_scr[...]
    m_new = jnp.maximum(m_prev, jnp.max(s, axis=0, keepdims=True))
    alpha = jnp.exp2(m_prev - m_new)
    p = jnp.exp2(s - m_new)
    acc_scr[...] = alpha * acc_scr[...] + jnp.dot(_with_ones(vT), p.astype(BF16), preferred_element_type=F32)
    m_scr[...] = m_new


def _rank_select(score, k):
    n = score.shape[0]
    rank = jnp.zeros(score.shape, F32)
    for m in range(n):
        sm = score[m:m + 1, :]
        lo = (m // SUBLANES) * SUBLANES
        hi = min(lo + SUBLANES, n)
        parts = []
        if lo > 0:
            parts.append(jnp.where(sm > score[:lo], 1.0, 0.0))
        gt = jnp.where(sm > score[lo:hi], 1.0, 0.0)
        ge = jnp.where(sm >= score[lo:hi], 1.0, 0.0)
        below = lax.broadcasted_iota(jnp.int32, gt.shape, 0) > m - lo
        parts.append(jnp.where(below, ge, gt))
        if hi < n:
            parts.append(jnp.where(sm >= score[hi:], 1.0, 0.0))
        rank = rank + (jnp.concatenate(parts, axis=0) if len(parts) > 1 else parts[0])
    return rank < k


def _pad_rows(a, rows):
    return jnp.concatenate([a, jnp.zeros((rows - a.shape[0], a.shape[1]), a.dtype)], axis=0)


def _masked_scores(k_ref, e_ref, start, size, q_aug):
    rows = pl.ds(pl.multiple_of(start, 128), size)
    k_aug = jnp.concatenate([k_ref[rows, :], e_ref[rows, :]], axis=1)
    return jnp.dot(k_aug, q_aug, preferred_element_type=F32)


def _pipelined_attention(chains, n_steps, n_max, tks, tk, body_pairs):
    per = tks // tk

    def scores(step, buf):
        st = jnp.minimum(step, n_max - 1)
        dead = jnp.where(step < n_steps, 0, 1)
        for k_ref, e_ref, _, qaug_ref, bufs, _, _ in chains:
            bufs[buf][...] = _masked_scores(k_ref, e_ref, st * tks, tks, qaug_ref[dead])

    def consume(step, buf):
        st = jnp.minimum(step, n_max - 1)
        for _, _, vT_ref, _, bufs, m_scr, acc_scr in chains:
            vT = jnp.concatenate([vT_ref[st * per + a] for a in range(per)], axis=1)
            _online_step(bufs[buf][...], vT, m_scr, acc_scr)

    def pairs(first_step, n_pairs_in_body):
        def body(it, carry):
            base = first_step + it * 2 * n_pairs_in_body
            for p in range(n_pairs_in_body):
                scores(base + 2 * p + 1, 1)
                consume(base + 2 * p, 0)
                scores(base + 2 * p + 2, 0)
                consume(base + 2 * p + 1, 1)
            return carry
        return body

    scores(0, 0)
    done = 0
    for n_body in body_pairs[:-1]:
        n_iter = (n_steps - done) // (2 * n_body)
        lax.fori_loop(0, n_iter, pairs(done, n_body), 0)
        done = done + n_iter * 2 * n_body
    rem = n_steps - done
    lax.fori_loop(0, rem // 2, pairs(done, 1), 0)

    @pl.when(rem % 2 == 1)
    def _():
        consume(done + (rem // 2) * 2, 0)


def _carry_casts(kernel_fn, n_in, n_out, n_cast):
    def wrapped(*refs):
        ins, cast_in = refs[:n_in], refs[n_in:n_in + n_cast]
        outs = refs[n_in + n_cast:n_in + n_cast + n_out]
        cast_out = refs[n_in + n_cast + n_out:n_in + n_cast + n_out + n_cast]
        for src, dst in zip(cast_in, cast_out):
            dst[...] = src[...].astype(BF16)
        kernel_fn(*ins, *outs, *refs[n_in + n_cast + n_out + n_cast:])
    return wrapped


def _cast_specs(weights, grid):
    n_steps = grid[0] * grid[1] * grid[2]
    specs, shapes = [], []
    for w, rows in weights:
        while w.shape[0] // rows > n_steps:
            rows *= 2
        n_blk = w.shape[0] // rows
        assert w.shape[0] % rows == 0
        index = lambda a, b, c, n_blk=n_blk: (jnp.minimum((a * grid[1] + b) * grid[2] + c, n_blk - 1), 0)
        specs.append(pl.BlockSpec((rows, w.shape[1]), index))
        shapes.append(jax.ShapeDtypeStruct(w.shape, BF16))
    return specs, shapes


def _nsa_body(qT_ref, kc_ref, vcT_ref, ks_ref, vsT_ref, kw_ref, vwT_ref, gT_ref, e_ref,
              gain_ref, o_ref, m_scr, acc_scr, comb_scr, qaug_scr, s0_scr, s1_scr, psum_scr, *, T):
    R, tq = NSA_GROUP, NSA_TQ
    i = pl.program_id(2)
    t0 = i * tq
    ncp = T // CMP_STRIDE
    nsb = T // SLC_BLOCK
    qT = jnp.concatenate([qT_ref[r, c] for r in range(R) for c in range(tq // CHUNK)], axis=1)
    gT = gT_ref[0]

    def per_head(row0):
        return jnp.concatenate([gT[row0 + 3 * r:row0 + 3 * r + 1, :] for r in range(R)], axis=1)

    def lanes_x_heads(a):
        return jnp.concatenate([a] * R, axis=1)

    s = jnp.dot(kc_ref[0, 0], qT, preferred_element_type=F32)
    c_idx = lax.broadcasted_iota(jnp.int32, (ncp, tq), 0)
    t_c = t0 + lax.broadcasted_iota(jnp.int32, (ncp, tq), 1)
    vis = lanes_x_heads((c_idx * CMP_STRIDE + (CMP_BLOCK - 1) <= t_c) & (c_idx < ncp - 1))
    s = jnp.where(vis, s, NEG)
    m = jnp.maximum(jnp.max(s, axis=0, keepdims=True), M_INIT)
    p = jnp.exp2(s - m)
    l = jnp.sum(p, axis=0, keepdims=True)
    p = p / jnp.where(l > 0.0, l, 1.0)
    o_cmp = jnp.dot(vcT_ref[0, 0], p.astype(BF16), preferred_element_type=F32)
    comb_scr[...] = o_cmp * per_head(0)

    psum = p[:, 0:tq]
    for r in range(1, R):
        psum = psum + p[:, r * tq:(r + 1) * tq]
    per = SLC_BLOCK // CMP_STRIDE
    imp_chunks = []
    for c in range(tq // CHUNK):
        psum_scr[c, 0:SUBLANES, :] = jnp.zeros((SUBLANES, CHUNK), F32)
        psum_scr[c, SUBLANES:, :] = psum[:, c * CHUNK:(c + 1) * CHUNK]
        acc = jnp.zeros((nsb, CHUNK), F32)
        for k in range(1 - CMP_BLOCK // CMP_STRIDE, per):
            acc = acc + psum_scr[c, pl.ds(SUBLANES + k, nsb, stride=per), :]
        imp_chunks.append(acc)
    imp = jnp.concatenate(imp_chunks, axis=1)
    n_idx = lax.broadcasted_iota(jnp.int32, (nsb, tq), 0)
    cur = (t0 + lax.broadcasted_iota(jnp.int32, (nsb, tq), 1)) // SLC_BLOCK
    forced = (n_idx == 0) | (n_idx == cur) | (n_idx == cur - 1)
    valid = n_idx <= cur
    score = jnp.where(valid, imp + jnp.where(forced, FORCE_BONUS, 0.0), NEG)
    sel = _rank_select(score, min(SLC_TOPK, nsb)) & valid
    sel_past = sel & (n_idx * SLC_BLOCK < t0)
    bias = _pad_rows(jnp.where(sel_past, 0.0, NEG), HEAD_DIM).astype(BF16)
    bias_own = _pad_rows(jnp.where(sel, 0.0, NEG), HEAD_DIM).astype(BF16)
    for d in range(3):
        qaug_scr[d, 0:HEAD_DIM, :] = qT
    qaug_scr[0, HEAD_DIM:, :] = lanes_x_heads(bias)
    qaug_scr[1, HEAD_DIM:, :] = jnp.full((HEAD_DIM, R * tq), NEG, BF16)
    qaug_scr[2, HEAD_DIM:, :] = lanes_x_heads(bias_own)

    w0 = jnp.maximum(t0 + tq - WIN_SPAN, 0)
    k_w = kw_ref[0, pl.ds(pl.multiple_of(w0, CHUNK), WIN_SPAN), :]
    s_w = jnp.dot(k_w, qT, preferred_element_type=F32)
    kpos = w0 + lax.broadcasted_iota(jnp.int32, (WIN_SPAN, tq), 0)
    tpos = t0 + lax.broadcasted_iota(jnp.int32, (WIN_SPAN, tq), 1)
    ok = lanes_x_heads((kpos <= tpos) & (tpos - kpos < WINDOW))
    s_w = jnp.where(ok, s_w, NEG)
    m_w = jnp.max(s_w, axis=0, keepdims=True)
    p_w = jnp.exp2(s_w - m_w)
    jw = w0 // CHUNK
    v_w = jnp.concatenate([vwT_ref[0, jw + a] for a in range(WIN_SPAN // CHUNK)], axis=1)
    o_w = jnp.dot(_with_ones(v_w), p_w.astype(BF16), preferred_element_type=F32)
    comb_scr[...] += _normalized(o_w) * per_head(2)

    tri =lax.broadcasted_iota(jnp.int32, (tq, tq), 0) <= lax.broadcasted_iota(jnp.int32, (tq, tq), 1)
    s_d = jnp.where(lanes_x_heads(tri), _masked_scores(ks_ref.at[0], e_ref, t0, tq, qaug_scr[2]), NEG)
    own = tq // CHUNK
    v_d = jnp.concatenate([vsT_ref[0, i * own + a] for a in range(own)], axis=1)
    _first_step(s_d, v_d, m_scr, acc_scr)
    halves = [slice(h * (R // 2) * tq, (h + 1) * (R // 2) * tq) for h in range(2)]
    yield [(ks_ref.at[0], e_ref, vsT_ref.at[0], qaug_scr.at[:, :, ln], (s0_scr.at[:, ln], s1_scr.at[:, ln]),
            m_scr.at[:, ln], acc_scr.at[:, ln]) for ln in halves]
    comb = comb_scr[...] + _normalized(acc_scr[...]) * per_head(1)

    for r in range(R):
        oT = comb[:, r * tq:(r + 1) * tq]
        ms = jnp.mean(oT * oT, axis=0, keepdims=True)
        on = (oT * lax.rsqrt(ms + EPS)).T * gain_ref[0, r:r + 1, :]
        o_ref[:, r * HEAD_DIM:(r + 1) * HEAD_DIM] = on.astype(BF16)


def _moba_body(qT_ref, k_ref, vT_ref, e_ref, gain_ref, o_ref, m_scr, acc_scr, kmean_scr, qaug_scr,
               s_scr, *, T):
    tq, tk = MOBA_TQ, MOBA_TK
    nb = T // MOBA_BLOCK
    nbp = kmean_scr.shape[1]
    i = pl.program_id(2)
    t0 = i * tq

    @pl.when(i == 0)
    def _():
        kmean_scr[...] = jnp.zeros(kmean_scr.shape, F32)
        for a in range(MOBA_HB):
            kb = k_ref[a].astype(F32).reshape(nb, MOBA_BLOCK, HEAD_DIM)
            kmean_scr[a, 0:nb, :] = jnp.mean(kb, axis=1)

    n_idx = lax.broadcasted_iota(jnp.int32, (nbp, tq), 0)
    cur = (t0 + lax.broadcasted_iota(jnp.int32, (nbp, tq), 1)) // MOBA_BLOCK
    past = n_idx < cur
    causal = lax.broadcasted_iota(jnp.int32, (tk, tq), 0) <= lax.broadcasted_iota(jnp.int32, (tk, tq), 1)

    qTs = [jnp.concatenate([qT_ref[a, c] for c in range(tq // CHUNK)], axis=1) for a in range(MOBA_HB)]
    for a in range(MOBA_HB):
        qT = qTs[a]
        gate = jnp.zeros((nbp, tq), F32)
        for part in _split3(kmean_scr[a]):
            gate = gate + jnp.dot(part, qT, preferred_element_type=F32)
        sel = _rank_select(jnp.where(past, gate, NEG), min(MOBA_TOPK, nb)) & past
        for d in range(2):
            qaug_scr[a, d, 0:HEAD_DIM, :] = qT
        qaug_scr[a, 0, HEAD_DIM:, :] = _pad_rows(jnp.where(sel, 0.0, NEG), HEAD_DIM).astype(BF16)
        qaug_scr[a, 1, HEAD_DIM:, :] = jnp.full((HEAD_DIM, tq), NEG, BF16)

    s_own = []
    for a in range(MOBA_HB):
        m_scr[a] = jnp.full(m_scr.shape[1:], M_INIT, F32)
        acc_scr[a] = jnp.zeros(acc_scr.shape[1:], F32)
        k_own = k_ref[a, pl.ds(pl.multiple_of(t0, tk), tk), :]
        s_own.append(jnp.where(causal, jnp.dot(k_own, qTs[a], preferred_element_type=F32), NEG))

    yield [(k_ref.at[a], e_ref, vT_ref.at[a], qaug_scr.at[a], (s_scr.at[a, 0], s_scr.at[a, 1]),
            m_scr.at[a], acc_scr.at[a]) for a in range(MOBA_HB)]

    own = tq // CHUNK
    for a in range(MOBA_HB):
        v_own = jnp.concatenate([vT_ref[a, i * own + c] for c in range(own)], axis=1)
        _online_step(s_own[a], v_own, m_scr.at[a], acc_scr.at[a])
        oT = _normalized(acc_scr[a])
        ms = jnp.mean(oT * oT, axis=0, keepdims=True)
        on = (oT * lax.rsqrt(ms + EPS)).T * gain_ref[a]
        o_ref[:, a * HEAD_DIM:(a + 1) * HEAD_DIM] = on.astype(BF16)


N_NSA_IN, N_NSA_SCRATCH, N_MOBA_IN = 10, 7, 5


def _attention_kernel(*refs, T):
    nsa_in, moba_in = refs[:N_NSA_IN], refs[N_NSA_IN:N_NSA_IN + N_MOBA_IN]
    o_nsa, o_moba = refs[N_NSA_IN + N_MOBA_IN:N_NSA_IN + N_MOBA_IN + 2]
    scratch = refs[N_NSA_IN + N_MOBA_IN + 2:]
    moba = _moba_body(*moba_in, o_moba, *scratch[N_NSA_SCRATCH:], T=T)
    nsa = _nsa_body(*nsa_in, o_nsa, *scratch[:N_NSA_SCRATCH], T=T)
    chains = next(moba) + next(nsa)
    n_steps = pl.program_id(2) * ATTN_TQ // ATTN_TKS
    _pipelined_attention(chains, n_steps, T // ATTN_TKS, ATTN_TKS, CHUNK, ATTN_BODY_PAIRS)
    for rest in (nsa, moba):
        assert next(rest, None) is None


def _attention(yT, ystd, kcvc, vcT, gT, e_slc, e_moba, nsa_gains, moba_gains, casts, B, T):
    G, R, tq = NSA_KV_HEADS, NSA_GROUP, ATTN_TQ
    H, HB = MOBA_HEADS, MOBA_HB
    assert H // HB == G and NSA_TQ == MOBA_TQ == tq and SLC_TKS == MOBA_TKS == ATTN_TKS and tq % ATTN_TKS == 0
    nq = T // tq
    ncp = T // CMP_STRIDE
    nb = T // MOBA_BLOCK
    nbp = max(16, nb)
    grid = (B, G, nq)
    cast_specs, cast_shapes = _cast_specs(casts, grid)
    kern = _carry_casts(functools.partial(_attention_kernel, T=T), N_NSA_IN + N_MOBA_IN, 2, len(casts))
    keys = lambda head0: pl.BlockSpec((1, T, HEAD_DIM), lambda b, g, i: (head0 + g, b, 0))
    values = lambda head0: pl.BlockSpec((1, T // CHUNK, HEAD_DIM, CHUNK), lambda b, g, i: (head0 + g, b, 0, 0))
    onehot = pl.BlockSpec((T, HEAD_DIM), lambda b, g, i: (0, 0))
    nsa_specs = [pl.BlockSpec((R, tq // CHUNK, HEAD_DIM, CHUNK), lambda b, g, i: (T_NSA_Q // R + g, b * nq + i, 0, 0)),
                 pl.BlockSpec((1, 1, ncp, HEAD_DIM), lambda b, g, i: (g, b, 0, 0)),
                 pl.BlockSpec((1, 1, HEAD_DIM, ncp), lambda b, g, i: (g, b, 0, 0)),
                 keys(S_KSLC), values(T_VSLC), keys(S_KWIN), values(T_VWIN),
                 pl.BlockSpec((1, GATE_ROWS, tq), lambda b, g, i: (g, 0, b * nq + i)),
                 onehot,
                 pl.BlockSpec((1, R, HEAD_DIM), lambda b, g, i: (g, 0, 0))]
    moba_specs = [pl.BlockSpec((HB, tq // CHUNK, HEAD_DIM, CHUNK), lambda b, h, i: (T_MOBA_Q // HB + h, b * nq + i, 0, 0)),
                  pl.BlockSpec((HB, T, HEAD_DIM), lambda b, h, i: (S_MOBA_K // HB + h, b, 0)),
                  pl.BlockSpec((HB, T // CHUNK, HEAD_DIM, CHUNK), lambda b, h, i: (T_MOBA_V // HB + h, b, 0, 0)),
                  onehot,
                  pl.BlockSpec((HB, 1, HEAD_DIM), lambda b, h, i: (h, 0, 0))]
    nsa_scratch = [pltpu.VMEM((1, R * tq), F32),
                   pltpu.VMEM((HEAD_DIM + ONES_ROWS, R * tq), F32), pltpu.VMEM((HEAD_DIM, R * tq), F32),
                   pltpu.VMEM((3, 2 * HEAD_DIM, R * tq), BF16),
                   pltpu.VMEM((ATTN_TKS, R * tq), F32), pltpu.VMEM((ATTN_TKS, R * tq), F32),
                   pltpu.VMEM((tq // CHUNK, SUBLANES + ncp, CHUNK), F32)]
    moba_scratch = [pltpu.VMEM((HB, 1, tq), F32),
                    pltpu.VMEM((HB, HEAD_DIM + ONES_ROWS, tq), F32), pltpu.VMEM((HB, nbp, HEAD_DIM), F32),
                    pltpu.VMEM((HB, 2, 2 * HEAD_DIM, tq), BF16),
                    pltpu.VMEM((HB, 2, ATTN_TKS, tq), F32)]
    assert len(nsa_specs) == N_NSA_IN and len(moba_specs) == N_MOBA_IN and len(nsa_scratch) == N_NSA_SCRATCH
    return pl.pallas_call(
        kern,
        grid=grid,
        in_specs=nsa_specs + moba_specs + cast_specs,
        out_specs=[pl.BlockSpec((tq, R * HEAD_DIM), lambda b, g, i: (b * nq + i, g)),
                   pl.BlockSpec((tq, HB * HEAD_DIM), lambda b, h, i: (b * nq + i, h))] + cast_specs,
        out_shape=[jax.ShapeDtypeStruct((B * T, NSA_HEADS * HEAD_DIM), BF16),
                   jax.ShapeDtypeStruct((B * T, H * HEAD_DIM), BF16)] + cast_shapes,
        scratch_shapes=nsa_scratch + moba_scratch,
        compiler_params=_cparams(("parallel", "parallel", "arbitrary")),
        name="attention",
    )(yT, kcvc, vcT, ystd, yT, ystd, yT, gT, e_slc, nsa_gains,
      yT, ystd, yT, e_moba, moba_gains, *[w for w, _ in casts])


def _outproj_kernel(on_ref, om_ref, w_ref, x_ref, mod_ref, o_ref):
    half = on_ref.shape[1]
    acc = jnp.dot(on_ref[...], w_ref[0:half, :], preferred_element_type=F32)
    acc = acc + jnp.dot(om_ref[...], w_ref[half:, :], preferred_element_type=F32)
    o_ref[...] = x_ref[...] + mod_ref[0, 2:3, :] * acc


def _out_proj(o_nsa, o_moba, w_out, x2, mod3, T):
    BT, D = x2.shape
    tm = min(OUT_TM, T)
    tpb = T // tm
    half = o_nsa.shape[1]
    return pl.pallas_call(
        _outproj_kernel,
        grid=(BT // tm,),
        in_specs=[pl.BlockSpec((tm, half), lambda i: (i, 0)),
                  pl.BlockSpec((tm, half), lambda i: (i, 0)),
                  pl.BlockSpec((D, D), lambda i: (0, 0)),
                  pl.BlockSpec((tm, D), lambda i: (i, 0)),
                  pl.BlockSpec((1, 6, D), lambda i: (i // tpb, 0, 0))],
        out_specs=pl.BlockSpec((tm, D), lambda i: (i, 0)),
        out_shape=jax.ShapeDtypeStruct((BT, D), F32),
        compiler_params=_cparams(("parallel",)),
        name="out_proj",
    )(o_nsa, o_moba, w_out, x2, mod3)


def _ffn_kernel(x_ref, mod_ref, wg_ref, wu_ref, wo_ref, o_ref, h_scr):
    j = pl.program_id(1)

    @pl.when(j == 0)
    def _():
        x = x_ref[...]
        ms = jnp.mean(x * x, axis=-1, keepdims=True)
        h = x * lax.rsqrt(ms + EPS) * (1.0 + mod_ref[0, 4:5, :]) + mod_ref[0, 3:4, :]
        h_scr[...] = h.astype(BF16)
        o_ref[...] = jnp.zeros(o_ref.shape, F32)

    hb = h_scr[...]
    half = wg_ref.shape[1] // 2
    down = None
    for c in range(2):
        cols = slice(c * half, (c + 1) * half)
        gate = jnp.dot(hb, wg_ref[:, cols], preferred_element_type=F32)
        up = jnp.dot(hb, wu_ref[:, cols], preferred_element_type=F32)
        act = (gate * jax.nn.sigmoid(gate) * up).astype(BF16)
        part = jnp.dot(act, wo_ref[cols, :], preferred_element_type=F32)
        down = part if down is None else down + part
    o_ref[...] += down

    @pl.when(j == pl.num_programs(1) - 1)
    def _():
        o_ref[...] = x_ref[...] + mod_ref[0, 5:6, :] * o_ref[...]


def _ffn(x1, mod3, w_in, w_out, T):
    BT, D = x1.shape
    Fh = w_out.shape[0]
    tm = min(FFN_TM, T)
    tf = FFN_TF
    tpb = T // tm
    nf = Fh // tf
    return pl.pallas_call(
        _ffn_kernel,
        grid=(BT // tm, nf),
        in_specs=[pl.BlockSpec((tm, D), lambda i, j: (i, 0)),
                  pl.BlockSpec((1, 6, D), lambda i, j: (i // tpb, 0, 0)),
                  pl.BlockSpec((D, tf), lambda i, j: (0, j)),
                  pl.BlockSpec((D, tf), lambda i, j: (0, nf + j)),
                  pl.BlockSpec((tf, D), lambda i, j: (j, 0))],
        out_specs=pl.BlockSpec((tm, D), lambda i, j: (i, 0)),
        out_shape=jax.ShapeDtypeStruct((BT, D), F32),
        scratch_shapes=[pltpu.VMEM((tm, D), BF16)],
        compiler_params=_cparams(("parallel", "arbitrary")),
        name="ffn",
    )(x1, mod3, w_in, w_in, w_out)


def _rope_cos_sin(pos):
    inv = ROPE_THETA ** (-jnp.arange(0, ROPE_DIMS, 2, dtype=F32) / ROPE_DIMS)
    ang = pos.astype(F32)[:, None] * inv[None, :]
    return jnp.cos(ang), jnp.sin(ang)


def _rope_tables(T):
    c_hi, s_hi = _rope_cos_sin(jnp.arange(0, T, ROPE_SPLIT))
    c_lo, s_lo = _rope_cos_sin(jnp.arange(ROPE_SPLIT))
    cos = (c_hi[:, None] * c_lo[None] - s_hi[:, None] * s_lo[None]).reshape(T, ROPE_HALF)
    sin = (s_hi[:, None] * c_lo[None] + c_hi[:, None] * s_lo[None]).reshape(T, ROPE_HALF)
    cos_c, sin_c = _rope_cos_sin(jnp.arange(CMP_BLOCK - 1, T, CMP_STRIDE))
    n, rest = cos_c.shape[0], HEAD_DIM - ROPE_DIMS
    pad = lambda t: jnp.pad(t, ((0, 1), (0, 0)))
    cc = pad(jnp.concatenate([cos_c, cos_c, jnp.ones((n, rest), F32)], axis=1))
    sa = pad(jnp.concatenate([-sin_c, jnp.zeros((n, HEAD_DIM - ROPE_HALF), F32)], axis=1))
    sb = pad(jnp.concatenate([jnp.zeros((n, ROPE_HALF), F32), sin_c, jnp.zeros((n, rest), F32)], axis=1))
    return cc, sa, sb, cos.T, sin.T


def _block_onehot(T, block):
    return (jnp.arange(T)[:, None] // block == jnp.arange(HEAD_DIM)[None, :]).astype(BF16)


def _layer(x2, c, B, T, w_ada, b_ada, w_in, nsa_q_norm, nsa_k_norm, moba_q_norm, moba_k_norm,
           cmp_pe_k, cmp_w1_k, cmp_w2_k, cmp_pe_v, cmp_w1_v, cmp_w2_v, out_norm, w_out,
           w_ffn_in, w_ffn_out):
    D = x2.shape[1]
    G = NSA_KV_HEADS
    scale = HEAD_DIM ** -0.5 * LOG2E
    assert T % MOBA_BLOCK == 0 and T % SLC_TKS == 0 and T % MOBA_TKS == 0 and T % NSA_TQ == 0 and T >= WIN_SPAN
    assert T // SLC_BLOCK <= HEAD_DIM and T // MOBA_BLOCK <= HEAD_DIM

    nsa_w = NSA_HEADS * HEAD_DIM + 6 * G * HEAD_DIM
    gw = NSA_HEADS * 3
    mod3 = _adaln(c, w_ada, b_ada)
    w_nsa = w_in.astype(BF16)
    w_moba = w_nsa[:, nsa_w + gw:]

    wg = w_in[:, nsa_w:nsa_w + gw].reshape(D, G, NSA_GROUP * 3)
    w_gate = jnp.pad(wg, ((0, 0), (0, 0), (0, GATE_LANES - NSA_GROUP * 3))).reshape(D, G * GATE_LANES)
    w_gate = w_gate.astype(BF16)
    across = lambda g_: jnp.broadcast_to(g_[:, None], (HEAD_DIM, CHUNK))
    gainsT = jnp.stack([across(nsa_q_norm * scale), across(moba_q_norm * scale),
                        across(nsa_k_norm[1]), across(nsa_k_norm[2]), across(moba_k_norm)])
    cc, sa, sb, cosT, sinT = _rope_tables(T)

    yT, ystd, hc, gT = _in_proj(x2, mod3, w_nsa, w_moba, w_gate, gainsT, cosT, sinT, T)

    ncp = T // CMP_STRIDE
    half = CMP_STRIDE * HEAD_DIM
    w1 = jnp.stack([cmp_w1_k, cmp_w1_v]).astype(BF16)
    w1cat = jnp.concatenate([w1[:, :half], w1[:, half:]], axis=2)
    pe8 = jnp.broadcast_to(jnp.stack([cmp_pe_k, cmp_pe_v]).reshape(2, 1, 2 * half), (2, 8, 2 * half))
    w2 = jnp.stack([cmp_w2_k, cmp_w2_v]).astype(BF16)
    kcvc = _compress(hc.reshape(2 * G, B, ncp, half), w1cat, pe8.astype(BF16), w1, w2,
                     nsa_k_norm[0].reshape(1, HEAD_DIM), cc, sa, sb)
    vcT = kcvc[G:].transpose(0, 1, 3, 2)

    on = out_norm.reshape(N_HEADS, HEAD_DIM)
    o_nsa, o_moba, w_ffn_out_b, w_out_b, w_ffn_in_b = _attention(
        yT, ystd, kcvc, vcT, gT, _block_onehot(T, SLC_BLOCK), _block_onehot(T, MOBA_BLOCK),
        on[:NSA_HEADS].reshape(G, NSA_GROUP, HEAD_DIM), on[NSA_HEADS:].reshape(MOBA_HEADS, 1, HEAD_DIM),
        [(w_ffn_out, 128), (w_out, 128), (w_ffn_in, 32)], B, T)

    x1 = _out_proj(o_nsa, o_moba, w_out_b, x2, mod3, T)
    return _ffn(x1, mod3, w_ffn_in_b, w_ffn_out_b, T)


def kernel(x, c, w_ada, b_ada, w_in, nsa_q_norm, nsa_k_norm, moba_q_norm, moba_k_norm, cmp_pe_k, cmp_w1_k, cmp_w2_k, cmp_pe_v, cmp_w1_v, cmp_w2_v, out_norm, w_out, w_ffn_in, w_ffn_out):
    B, T, D = x.shape
    x2 = x.reshape(B * T, D)
    for l in range(w_ada.shape[0]):
        x2 = _layer(x2, c, B, T, w_ada[l], b_ada[l], w_in[l], nsa_q_norm[l], nsa_k_norm[l],
                    moba_q_norm[l], moba_k_norm[l], cmp_pe_k[l], cmp_w1_k[l], cmp_w2_k[l],
                    cmp_pe_v[l], cmp_w1_v[l], cmp_w2_v[l], out_norm[l], w_out[l],
                    w_ffn_in[l], w_ffn_out[l])
    return x2.reshape(B, T, D)
```
